```python
import math
import jax, jax.numpy as jnp
from jax import lax
import numpy as np

D_MODEL = 1024
BATCH = 4
SEQ = 8192
DEPTH = 2

CHUNK = 64
Q_BLOCK = 128
HEAD_DIM = 64
H_FOX = 8
H_CHK = 8
W_FOX = H_FOX * HEAD_DIM
W_CHK = H_CHK * HEAD_DIM
LEFT_CHUNKS = 8
MAX_REL = 128
N_REL = 2 * MAX_REL + 1
N_BRANCH = 2
D_FF = 4 * D_MODEL
EPS = 1e-6
NEG = -1e30
N_IN = 3 * W_FOX + H_FOX + 3 * W_CHK + N_BRANCH * D_MODEL
SPLITS = tuple(np.cumsum([W_FOX, W_FOX, W_FOX, H_FOX, W_CHK, W_CHK, W_CHK]).tolist())

kernel_name = "hybrid_fox_chunkattn_gated_block"


def rms_norm(x, g):
    xf = x.astype(jnp.float32)
    y = xf * lax.rsqrt(jnp.mean(xf * xf, axis=-1, keepdims=True) + EPS)
    return (y * g.astype(jnp.float32)).astype(x.dtype)


def to_heads(t, h):
    b, s, _ = t.shape
    return t.reshape(b, s, h, HEAD_DIM).transpose(0, 2, 1, 3)


def from_heads(t):
    b, h, s, d = t.shape
    return t.transpose(0, 2, 1, 3).reshape(b, s, h * d)


def forgetting_attention(q, k, v, log_f):
    b, h, s, d = q.shape
    nblk = s // Q_BLOCK
    scale = 1.0 / math.sqrt(d)
    c = jnp.cumsum(log_f, axis=-1)
    qb = q.reshape(b, h, nblk, Q_BLOCK, d).transpose(2, 0, 1, 3, 4)
    cb = c.reshape(b, h, nblk, Q_BLOCK).transpose(2, 0, 1, 3)
    key_pos = jnp.arange(s, dtype=jnp.int32)

    def block(args):
        qi, ci, i = args
        sc = jnp.einsum('bhqd,bhkd->bhqk', qi, k, preferred_element_type=jnp.float32)
        sc = sc * scale + ci[..., None] - c[:, :, None, :]
        tq = i * Q_BLOCK + jnp.arange(Q_BLOCK, dtype=jnp.int32)
        mask = key_pos[None, :] <= tq[:, None]
        sc = jnp.where(mask, sc, NEG)
        p = jax.nn.softmax(sc, axis=-1).astype(v.dtype)
        return jnp.einsum('bhqk,bhkd->bhqd', p, v)

    out = lax.map(block, (qb, cb, jnp.arange(nblk, dtype=jnp.int32)))
    return out.transpose(1, 2, 0, 3, 4).reshape(b, h, s, d)


def chunk_band_attention(q, k, v, rel_bias):
    b, h, s, d = q.shape
    nblk = s // Q_BLOCK
    scale = 1.0 / math.sqrt(d)
    pad = LEFT_CHUNKS * CHUNK
    band = pad + Q_BLOCK
    kp = jnp.pad(k, ((0, 0), (0, 0), (pad, 0), (0, 0)))
    vp = jnp.pad(v, ((0, 0), (0, 0), (pad, 0), (0, 0)))
    qb = q.reshape(b, h, nblk, Q_BLOCK, d).transpose(2, 0, 1, 3, 4)

    def block(args):
        qi, i = args
        p0 = i * Q_BLOCK
        kb = lax.dynamic_slice_in_dim(kp, p0, band, axis=2)
        vb = lax.dynamic_slice_in_dim(vp, p0, band, axis=2)
        tq = p0 + jnp.arange(Q_BLOCK, dtype=jnp.int32)
        sk = p0 - pad + jnp.arange(band, dtype=jnp.int32)
        cq = tq // CHUNK
        ck = sk // CHUNK
        valid = (sk[None, :] >= 0) & (ck[None, :] <= cq[:, None]) & (ck[None, :] >= cq[:, None] - LEFT_CHUNKS)
        rel = jnp.clip(tq[:, None] - sk[None, :], -MAX_REL, MAX_REL) + MAX_REL
        bias = rel_bias[:, rel].astype(jnp.float32)
        sc = jnp.einsum('bhqd,bhkd->bhqk', qi, kb, preferred_element_type=jnp.float32)
        sc = jnp.where(valid, sc * scale + bias[None], NEG)
        p = jax.nn.softmax(sc, axis=-1).astype(vb.dtype)
        return jnp.einsum('bhqk,bhkd->bhqd', p, vb)

    out = lax.map(block, (qb, jnp.arange(nblk, dtype=jnp.int32)))
    return out.transpose(1, 2, 0, 3, 4).reshape(b, h, s, d)


def setup_inputs(seed: int = 0) -> dict:
    key = jax.random.key(seed)
    ks = jax.random.split(key, 14)
    nrm = lambda k, shape, fan_in: jax.random.normal(k, shape, jnp.float32) * (fan_in ** -0.5)
    x = jax.random.normal(ks[0], (BATCH, SEQ, D_MODEL), jnp.float32)
    norm1 = 1.0 + 0.02 * jax.random.normal(ks[1], (DEPTH, D_MODEL), jnp.float32)
    w_in = nrm(ks[2], (DEPTH, D_MODEL, N_IN), D_MODEL)
    forget_bias = jnp.linspace(1.0, 6.0, H_FOX, dtype=jnp.float32)[None, :] + 0.1 * jax.random.normal(ks[3], (DEPTH, H_FOX), jnp.float32)
    rel_bias = 0.5 * jax.random.normal(ks[4], (DEPTH, H_CHK, N_REL), jnp.float32)
    w_branch_a = nrm(ks[5], (DEPTH, W_FOX, D_MODEL), W_FOX)
    w_branch_b = nrm(ks[6], (DEPTH, W_CHK, D_MODEL), W_CHK)
    w_out = nrm(ks[7], (DEPTH, D_MODEL, D_MODEL), D_MODEL)
    norm2 = 1.0 + 0.02 * jax.random.normal(ks[8], (DEPTH, D_MODEL), jnp.float32)
    w_up = nrm(ks[9], (DEPTH, D_MODEL, D_FF), D_MODEL)
    w_down = nrm(ks[10], (DEPTH, D_FF, D_MODEL), D_FF)
    final_norm = 1.0 + 0.02 * jax.random.normal(ks[11], (D_MODEL,), jnp.float32)
    return {"x": x, "norm1": norm1, "w_in": w_in, "forget_bias": forget_bias,
            "rel_bias": rel_bias, "w_branch_a": w_branch_a, "w_branch_b": w_branch_b,
            "w_out": w_out, "norm2": norm2, "w_up": w_up, "w_down": w_down,
            "final_norm": final_norm}


def reference(x, norm1, w_in, forget_bias, rel_bias, w_branch_a, w_branch_b,
              w_out, norm2, w_up, w_down, final_norm):
    for l in range(DEPTH):
        h = rms_norm(x, norm1[l])
        proj = h @ w_in[l]
        qa, ka, va, fa, qc, kc, vc, gates = jnp.split(proj, SPLITS, axis=-1)
        log_f = jax.nn.log_sigmoid((fa + forget_bias[l]).astype(jnp.float32))
        log_f = log_f.transpose(0, 2, 1)
        o_a = from_heads(forgetting_attention(to_heads(qa, H_FOX), to_heads(ka, H_FOX),
                                              to_heads(va, H_FOX), log_f))
        o_b = from_heads(chunk_band_attention(to_heads(qc, H_CHK), to_heads(kc, H_CHK),
                                              to_heads(vc, H_CHK), rel_bias[l]))
        g_a, g_b = jnp.split(jax.nn.sigmoid(gates), N_BRANCH, axis=-1)
        merged = g_a * (o_a @ w_branch_a[l]) + g_b * (o_b @ w_branch_b[l])
        x = x + merged @ w_out[l]
        h2 = rms_norm(x, norm2[l])
        x = x + jnp.square(jax.nn.relu(h2 @ w_up[l])) @ w_down[l]
    return rms_norm(x, final_norm)
```

```python
import functools
import math

import jax
import jax.numpy as jnp
from jax import lax
from jax.experimental import pallas as pl
from jax.experimental.pallas import tpu as pltpu

HEAD_DIM = 64
H_FOX = 8
H_CHK = 8
N_PAIR = 4
W_ATT = H_FOX * HEAD_DIM
CHUNK = 64
Q_BLOCK = 128
LEFT_CHUNKS = 8
PAD = LEFT_CHUNKS * CHUNK
BAND = PAD + Q_BLOCK
MAX_REL = 128
EPS = 1e-6
NEG = -1e30
LANES = 128
SCALE = 1.0 / math.sqrt(HEAD_DIM)
VMEM_LIMIT = 60 * 1024 * 1024

_NT = (((1,), (1,)), ((), ()))


def _dot(a, b):
    return jnp.dot(a, b, preferred_element_type=jnp.float32)


def _dot_nt(a, b):
    return lax.dot_general(a, b, _NT, preferred_element_type=jnp.float32)


def _rms(x, g):
    ms = jnp.mean(x * x, axis=-1, keepdims=True)
    return x * lax.rsqrt(ms + EPS) * g


def _const_spec(shape):
    nd = len(shape)
    return pl.BlockSpec(shape, lambda *_: (0,) * nd, pipeline_mode=pl.Buffered(1))


def _in_proj_body(x_ref, g_ref, wq_ref, wk_ref, wv_ref, wc_ref, wg_ref, wf_ref, bf_ref, eq_ref, ek_ref,
                  qf_ref, kf_ref, vf_ref, qc_ref, kc_ref, vc_ref, gate_ref, carry_ref):
    tm = x_ref.shape[1]

    @pl.when(pl.program_id(1) == 0)
    def _():
        carry_ref[...] = jnp.zeros_like(carry_ref)

    h = _rms(x_ref[0], g_ref[...]).astype(jnp.bfloat16)

    lane = lax.broadcasted_iota(jnp.int32, (tm, LANES), 1)
    row = lax.broadcasted_iota(jnp.int32, (tm, LANES), 0)
    logf = jax.nn.log_sigmoid(_dot(h, wf_ref[...]) + bf_ref[...])
    c = jnp.where(lane < H_FOX, logf, 0.0)
    k = 1
    while k < tm:
        c = c + jnp.where(row >= k, pltpu.roll(c, k, 0), 0.0)
        k *= 2
    c = c + carry_ref[...]
    carry_ref[...] = c[tm - 1:tm, :]

    hi = c.astype(jnp.bfloat16).astype(jnp.float32)
    r1 = c - hi
    mid = r1.astype(jnp.bfloat16).astype(jnp.float32)
    lo = (r1 - mid).astype(jnp.bfloat16).astype(jnp.float32)
    pieces = hi + pltpu.roll(mid, H_FOX, 1) + pltpu.roll(lo, 2 * H_FOX, 1)
    pieces = jnp.where(lane == 3 * H_FOX, 1.0, pieces).astype(jnp.bfloat16)
    aug_q = _dot(pieces, eq_ref[...]).astype(jnp.bfloat16)
    aug_k = _dot(pieces, ek_ref[...]).astype(jnp.bfloat16)

    q = _dot(h, wq_ref[...]).astype(jnp.bfloat16)
    kk = _dot(h, wk_ref[...]).astype(jnp.bfloat16)
    for p in range(N_PAIR):
        src = slice(p * LANES, (p + 1) * LANES)
        qf_ref[0, :, 2 * p * LANES:(2 * p + 1) * LANES] = q[:, src]
        qf_ref[0, :, (2 * p + 1) * LANES:(2 * p + 2) * LANES] = aug_q[:, src]
        kf_ref[0, :, 2 * p * LANES:(2 * p + 1) * LANES] = kk[:, src]
        kf_ref[0, :, (2 * p + 1) * LANES:(2 * p + 2) * LANES] = aug_k[:, src]
    vf_ref[0] = _dot(h, wv_ref[...]).astype(jnp.bfloat16)

    pc = _dot(h, wc_ref[...])
    qc_ref[0] = pc[:, :W_ATT].astype(jnp.bfloat16)
    kc_ref[0] = pc[:, W_ATT:2 * W_ATT].astype(jnp.bfloat16)
    vc_ref[0] = pc[:, 2 * W_ATT:].astype(jnp.bfloat16)
    gate_ref[0] = jax.nn.sigmoid(_dot(h, wg_ref[...])).astype(jnp.bfloat16)


def _in_proj(x, g, wq, wk, wv, wc, wg, wf, bf, eq, ek, tm):
    b, s, d = x.shape
    tok = lambda w: pl.BlockSpec((1, tm, w), lambda i, j: (i, j, 0))
    bf16 = jnp.bfloat16
    out_shape = [jax.ShapeDtypeStruct((b, s, w), bf16)
                 for w in (2 * W_ATT, 2 * W_ATT, W_ATT, W_ATT, W_ATT, W_ATT, 2 * d)]
    return pl.pallas_call(
        _in_proj_body,
        grid=(b, s // tm),
        in_specs=[tok(d)] + [_const_spec(a.shape) for a in (g, wq, wk, wv, wc, wg, wf, bf, eq, ek)],
        out_specs=[tok(sh.shape[-1]) for sh in out_shape],
        out_shape=out_shape,
        scratch_shapes=[pltpu.VMEM((1, LANES), jnp.float32)],
        compiler_params=pltpu.CompilerParams(
            dimension_semantics=("arbitrary", "arbitrary"), vmem_limit_bytes=VMEM_LIMIT),
        name="in_proj",
    )(x, g, wq, wk, wv, wc, wg, wf, bf, eq, ek)


def _fox_body(q_ref, k_ref, v_ref, o_ref, *, tq):
    qi = pl.program_id(2)
    qcat = q_ref[0]
    lane2 = lax.broadcasted_iota(jnp.int32, (1, 2 * LANES), 1) % LANES
    lane1 = lax.broadcasted_iota(jnp.int32, (1, LANES), 1)
    row = lax.broadcasted_iota(jnp.int32, (tq, tq), 0)
    col = lax.broadcasted_iota(jnp.int32, (tq, tq), 1)

    outs = []
    for head in range(2):
        sel = (lane2 < HEAD_DIM) if head == 0 else (lane2 >= HEAD_DIM)
        qh = jnp.where(sel, qcat, jnp.zeros_like(qcat))

        def scores(j):
            kt = k_ref[0, pl.ds(pl.multiple_of(j * tq, tq), tq), :]
            return _dot_nt(qh, kt)

        def values(j):
            return v_ref[0, pl.ds(pl.multiple_of(j * tq, tq), tq), :]

        s = jnp.where(col <= row, scores(qi), NEG)
        m = jnp.max(s, axis=-1, keepdims=True)
        p = jnp.exp(s - m)
        l = jnp.sum(p, axis=-1, keepdims=True)
        acc = _dot(p.astype(jnp.bfloat16), values(qi))

        def step(t, carry):
            m, l, acc = carry
            j = qi - 1 - t
            s = scores(j)
            m_new = jnp.maximum(m, jnp.max(s, axis=-1, keepdims=True))
            alpha = jnp.exp(m - m_new)
            p = jnp.exp(s - m_new)
            l = alpha * l + jnp.sum(p, axis=-1, keepdims=True)
            acc = alpha * acc + _dot(p.astype(jnp.bfloat16), values(j))
            return m_new, l, acc

        m, l, acc = lax.fori_loop(0, qi, step, (m, l, acc))
        outs.append(acc / l)

    o_ref[0] = jnp.where(lane1 < HEAD_DIM, outs[0], outs[1]).astype(o_ref.dtype)


def _fox(qf, kf, vf, tq):
    b, s, _ = qf.shape
    return pl.pallas_call(
        functools.partial(_fox_body, tq=tq),
        grid=(b, N_PAIR, s // tq),
        in_specs=[pl.BlockSpec((1, tq, 2 * LANES), lambda i, p, j: (i, j, p)),
                  pl.BlockSpec((1, s, 2 * LANES), lambda i, p, j: (i, 0, p)),
                  pl.BlockSpec((1, s, LANES), lambda i, p, j: (i, 0, p))],
        out_specs=pl.BlockSpec((1, tq, LANES), lambda i, p, j: (i, j, p)),
        out_shape=jax.ShapeDtypeStruct((b, s, W_ATT), jnp.bfloat16),
        compiler_params=pltpu.CompilerParams(
            dimension_semantics=("arbitrary", "arbitrary", "arbitrary"), vmem_limit_bytes=VMEM_LIMIT),
        name="fox_attention",
    )(qf, kf, vf)


def _band_bias_body(g_ref, o_ref):
    width = g_ref.shape[-1]
    base = jnp.broadcast_to(g_ref[0], (Q_BLOCK, width))
    toeplitz = pltpu.roll(base, 0, 1, stride=1, stride_axis=0)[:, :BAND]
    qrow = lax.broadcasted_iota(jnp.int32, (Q_BLOCK, BAND), 0)
    kcol = lax.broadcasted_iota(jnp.int32, (Q_BLOCK, BAND), 1)
    cq = qrow // CHUNK
    ck = kcol // CHUNK - LEFT_CHUNKS
    valid = (ck <= cq) & (ck >= cq - LEFT_CHUNKS)
    o_ref[0] = jnp.where(valid, toeplitz, NEG)


def _band_bias(g_ext):
    h, _, width = g_ext.shape
    return pl.pallas_call(
        _band_bias_body,
        grid=(h,),
        in_specs=[pl.BlockSpec((1, 1, width), lambda i: (i, 0, 0))],
        out_specs=pl.BlockSpec((1, Q_BLOCK, BAND), lambda i: (i, 0, 0)),
        out_shape=jax.ShapeDtypeStruct((h, Q_BLOCK, BAND), jnp.float32),
        name="band_bias",
    )(g_ext)


def _chunk_body(q_ref, k_ref, v_ref, bias_ref, o_ref, kpad_ref, vpad_ref, *, tq):
    qi = pl.program_id(2)
    s_len = k_ref.shape[1]

    @pl.when(qi == 0)
    def _():
        zeros = jnp.zeros((PAD, LANES), kpad_ref.dtype)
        kpad_ref[:PAD, :] = zeros
        vpad_ref[:PAD, :] = zeros
        kpad_ref[PAD:PAD + s_len, :] = k_ref[0]
        vpad_ref[PAD:PAD + s_len, :] = v_ref[0]

    lane1 = lax.broadcasted_iota(jnp.int32, (1, LANES), 1)
    kcol = lax.broadcasted_iota(jnp.int32, (1, BAND), 1)

    def block(blk, carry):
        r0 = pl.multiple_of(blk * Q_BLOCK, Q_BLOCK)
        p0 = pl.multiple_of(qi * tq + r0, Q_BLOCK)
        q2 = q_ref[0, pl.ds(r0, Q_BLOCK), :]
        kb = kpad_ref[pl.ds(p0, BAND), :]
        vb = vpad_ref[pl.ds(p0, BAND), :]
        in_seq = kcol + p0 >= PAD
        outs = []
        for head in range(2):
            sel = (lane1 < HEAD_DIM) if head == 0 else (lane1 >= HEAD_DIM)
            qh = jnp.where(sel, q2, jnp.zeros_like(q2))
            s = jnp.where(in_seq, _dot_nt(qh, kb) + bias_ref[head], NEG)
            m = jnp.max(s, axis=-1, keepdims=True)
            p = jnp.exp(s - m)
            l = jnp.sum(p, axis=-1, keepdims=True)
            outs.append(_dot(p.astype(jnp.bfloat16), vb) / l)
        o_ref[0, pl.ds(r0, Q_BLOCK), :] = jnp.where(lane1 < HEAD_DIM, outs[0], outs[1]).astype(o_ref.dtype)
        return carry

    lax.fori_loop(0, tq // Q_BLOCK, block, 0)


def _chunk_attention(qc, kc, vc, bias, tq):
    b, s, _ = qc.shape
    return pl.pallas_call(
        functools.partial(_chunk_body, tq=tq),
        grid=(b, N_PAIR, s // tq),
        in_specs=[pl.BlockSpec((1, tq, LANES), lambda i, p, j: (i, j, p)),
                  pl.BlockSpec((1, s, LANES), lambda i, p, j: (i, 0, p)),
                  pl.BlockSpec((1, s, LANES), lambda i, p, j: (i, 0, p)),
                  pl.BlockSpec((2, Q_BLOCK, BAND), lambda i, p, j: (p, 0, 0))],
        out_specs=pl.BlockSpec((1, tq, LANES), lambda i, p, j: (i, j, p)),
        out_shape=jax.ShapeDtypeStruct((b, s, W_ATT), jnp.bfloat16),
        scratch_shapes=[pltpu.VMEM((PAD + s, LANES), jnp.bfloat16),
                        pltpu.VMEM((PAD + s, LANES), jnp.bfloat16)],
        compiler_params=pltpu.CompilerParams(
            dimension_semantics=("arbitrary", "arbitrary", "arbitrary"), vmem_limit_bytes=VMEM_LIMIT),
        name="chunk_attention",
    )(qc, kc, vc, bias)


def _out_ffn_body(x_ref, oa_ref, ob_ref, gate_ref, wa_ref, wb_ref, wo_ref, g2_ref, wu_ref, wd_ref, gf_ref,
                  o_ref, *, ff_chunk, final):
    d = x_ref.shape[-1]
    ya = _dot(oa_ref[0], wa_ref[...])
    yb = _dot(ob_ref[0], wb_ref[...])
    ga = gate_ref[0, :, :d].astype(jnp.float32)
    gb = gate_ref[0, :, d:].astype(jnp.float32)
    merged = (ga * ya + gb * yb).astype(jnp.bfloat16)
    x1 = x_ref[0] + _dot(merged, wo_ref[...])
    h2 = _rms(x1, g2_ref[...]).astype(jnp.bfloat16)
    acc = x1
    for c0 in range(0, wu_ref.shape[1], ff_chunk):
        u = jnp.maximum(_dot(h2, wu_ref[:, c0:c0 + ff_chunk]), 0.0)
        acc = acc + _dot((u * u).astype(jnp.bfloat16), wd_ref[c0:c0 + ff_chunk, :])
    if final:
        acc = _rms(acc, gf_ref[...])
    o_ref[0] = acc


def _out_ffn(x, oa, ob, gates, wa, wb, wo, g2, wu, wd, gf, tm, final):
    b, s, d = x.shape
    tok = lambda w: pl.BlockSpec((1, tm, w), lambda i, j: (i, j, 0))
    return pl.pallas_call(
        functools.partial(_out_ffn_body, ff_chunk=min(1024, wu.shape[1]), final=final),
        grid=(b, s // tm),
        in_specs=[tok(d), tok(W_ATT), tok(W_ATT), tok(2 * d)]
                 + [_const_spec(a.shape) for a in (wa, wb, wo, g2, wu, wd, gf)],
        out_specs=tok(d),
        out_shape=jax.ShapeDtypeStruct((b, s, d), jnp.float32),
        compiler_params=pltpu.CompilerParams(
            dimension_semantics=("arbitrary", "arbitrary"), vmem_limit_bytes=VMEM_LIMIT),
        name="out_ffn",
    )(x, oa, ob, gates, wa, wb, wo, g2, wu, wd, gf)


def _aug_placement():
    eq = [[0.0] * W_ATT for _ in range(LANES)]
    ek = [[0.0] * W_ATT for _ in range(LANES)]
    one = 3 * H_FOX
    for h in range(H_FOX):
        for piece in range(3):
            eq[piece * H_FOX + h][HEAD_DIM * h + piece] = 1.0
            eq[one][HEAD_DIM * h + 3 + piece] = 1.0
            ek[one][HEAD_DIM * h + piece] = 1.0
            ek[piece * H_FOX + h][HEAD_DIM * h + 3 + piece] = -1.0
    return jnp.array(eq, jnp.bfloat16), jnp.array(ek, jnp.bfloat16)


def kernel(x, norm1, w_in, forget_bias, rel_bias, w_branch_a, w_branch_b, w_out, norm2, w_up, w_down, final_norm):
    b, s, d = x.shape
    depth = w_in.shape[0]
    bf16 = jnp.bfloat16
    tm = min(512, s)
    tq = min(512, s)
    tqc = min(1024, s)
    eq, ek = _aug_placement()
    gf = final_norm.reshape(1, d)
    o = 3 * W_ATT + H_FOX
    for l in range(depth):
        w = w_in[l]
        wq = (w[:, :W_ATT] * SCALE).astype(bf16)
        wk = w[:, W_ATT:2 * W_ATT].astype(bf16)
        wv = w[:, 2 * W_ATT:3 * W_ATT].astype(bf16)
        wf = jnp.pad(w[:, 3 * W_ATT:o], ((0, 0), (0, LANES - H_FOX))).astype(bf16)
        wc = jnp.concatenate([w[:, o:o + W_ATT] * SCALE, w[:, o + W_ATT:o + 3 * W_ATT]], axis=1).astype(bf16)
        wg = w[:, o + 3 * W_ATT:].astype(bf16)
        bf = jnp.pad(forget_bias[l], (0, LANES - H_FOX)).reshape(1, LANES)
        qf, kf, vf, qc, kc, vc, gates = _in_proj(
            x, norm1[l].reshape(1, d), wq, wk, wv, wc, wg, wf, bf, eq, ek, tm)

        o_a = _fox(qf, kf, vf, tq)

        far = rel_bias[l][:, 2 * MAX_REL:]
        g_ext = jnp.concatenate([jnp.broadcast_to(far, (H_CHK, PAD - MAX_REL + 1)),
                                 rel_bias[l][:, 2 * MAX_REL - 1:0:-1],
                                 jnp.broadcast_to(far, (H_CHK, Q_BLOCK))], axis=1)
        bias = _band_bias(g_ext.reshape(H_CHK, 1, -1))
        o_b = _chunk_attention(qc, kc, vc, bias, tqc)

        x = _out_ffn(x, o_a, o_b, gates, w_branch_a[l].astype(bf16), w_branch_b[l].astype(bf16),
                     w_out[l].astype(bf16), norm2[l].reshape(1, d), w_up[l].astype(bf16),
                     w_down[l].astype(bf16), gf, tm, final=(l == depth - 1))
    return x
```

```python
import functools
import math

import jax
import jax.numpy as jnp
from jax import lax
from jax.experimental import pallas as pl
from jax.experimental.pallas import tpu as pltpu

HEAD_DIM = 64
H_FOX = 8
H_CHK = 8
N_PAIR = 4
W_ATT = H_FOX * HEAD_DIM
CHUNK = 64
Q_BLOCK = 128
LEFT_CHUNKS = 8
PAD = LEFT_CHUNKS * CHUNK
BAND = PAD + Q_BLOCK
MAX_REL = 128
EPS = 1e-6
NEG = -1e30
LANES = 128
SCALE = 1.0 / math.sqrt(HEAD_DIM)
F32_BIG = 3.0e38
FOX_ROWS = 32
VMEM_LIMIT = 60 * 1024 * 1024

_NT = (((1,), (1,)), ((), ()))


def _dot(a, b):
    return jnp.dot(a, b, preferred_element_type=jnp.float32)


def _dot_nt(a, b):
    return lax.dot_general(a, b, _NT, preferred_element_type=jnp.float32)


def _rms(x, g):
    ms = jnp.mean(x * x, axis=-1, keepdims=True)
    return x * lax.rsqrt(ms + EPS) * g


def _const_spec(shape):
    nd = len(shape)
    return pl.BlockSpec(shape, lambda *_: (0,) * nd, pipeline_mode=pl.Buffered(1))


def _in_proj_body(x_ref, g_ref, wq_ref, wk_ref, wv_ref, wc_ref, wg_ref, wf_ref, bf_ref, eq_ref, ek_ref,
                  qf_ref, kf_ref, vf_ref, qc_ref, kc_ref, vc_ref, gate_ref, carry_ref):
    tm = x_ref.shape[1]

    @pl.when(pl.program_id(1) == 0)
    def _():
        carry_ref[...] = jnp.zeros_like(carry_ref)

    h = _rms(x_ref[0], g_ref[...]).astype(jnp.bfloat16)

    lane = lax.broadcasted_iota(jnp.int32, (tm, LANES), 1)
    row = lax.broadcasted_iota(jnp.int32, (tm, LANES), 0)
    logf = jax.nn.log_sigmoid(_dot(h, wf_ref[...]) + bf_ref[...])
    c = jnp.where(lane < H_FOX, logf, 0.0)
    k = 1
    while k < tm:
        c = c + jnp.where(row >= k, pltpu.roll(c, k, 0), 0.0)
        k *= 2
    c = c + carry_ref[...]
    carry_ref[...] = c[tm - 1:tm, :]

    hi = c.astype(jnp.bfloat16).astype(jnp.float32)
    r1 = c - hi
    mid = r1.astype(jnp.bfloat16).astype(jnp.float32)
    lo = (r1 - mid).astype(jnp.bfloat16).astype(jnp.float32)
    pieces = hi + pltpu.roll(mid, H_FOX, 1) + pltpu.roll(lo, 2 * H_FOX, 1)
    pieces = jnp.where(lane == 3 * H_FOX, 1.0, pieces).astype(jnp.bfloat16)
    aug_q = _dot(pieces, eq_ref[...]).astype(jnp.bfloat16)
    aug_k = _dot(pieces, ek_ref[...]).astype(jnp.bfloat16)

    q = _dot(h, wq_ref[...]).astype(jnp.bfloat16)
    kk = _dot(h, wk_ref[...]).astype(jnp.bfloat16)
    for p in range(N_PAIR):
        src = slice(p * LANES, (p + 1) * LANES)
        qf_ref[0, :, 2 * p * LANES:(2 * p + 1) * LANES] = q[:, src]
        qf_ref[0, :, (2 * p + 1) * LANES:(2 * p + 2) * LANES] = aug_q[:, src]
        kf_ref[0, :, 2 * p * LANES:(2 * p + 1) * LANES] = kk[:, src]
        kf_ref[0, :, (2 * p + 1) * LANES:(2 * p + 2) * LANES] = aug_k[:, src]
    vf_ref[0] = _dot(h, wv_ref[...]).astype(jnp.bfloat16)

    pc = _dot(h, wc_ref[...])
    qc_ref[0] = pc[:, :W_ATT].astype(jnp.bfloat16)
    kc_ref[0] = pc[:, W_ATT:2 * W_ATT].astype(jnp.bfloat16)
    vc_ref[0] = pc[:, 2 * W_ATT:].astype(jnp.bfloat16)
    gate_ref[0] = jax.nn.sigmoid(_dot(h, wg_ref[...])).astype(jnp.bfloat16)


def _in_proj(x, g, wq, wk, wv, wc, wg, wf, bf, eq, ek, tm):
    b, s, d = x.shape
    tok = lambda w: pl.BlockSpec((1, tm, w), lambda i, j: (i, j, 0))
    bf16 = jnp.bfloat16
    out_shape = [jax.ShapeDtypeStruct((b, s, w), bf16)
                 for w in (2 * W_ATT, 2 * W_ATT, W_ATT, W_ATT, W_ATT, W_ATT, 2 * d)]
    return pl.pallas_call(
        _in_proj_body,
        grid=(b, s // tm),
        in_specs=[tok(d)] + [_const_spec(a.shape) for a in (g, wq, wk, wv, wc, wg, wf, bf, eq, ek)],
        out_specs=[tok(sh.shape[-1]) for sh in out_shape],
        out_shape=out_shape,
        scratch_shapes=[pltpu.VMEM((1, LANES), jnp.float32)],
        compiler_params=pltpu.CompilerParams(
            dimension_semantics=("arbitrary", "arbitrary"), vmem_limit_bytes=VMEM_LIMIT),
        name="in_proj",
    )(x, g, wq, wk, wv, wc, wg, wf, bf, eq, ek)


def _fox_body(q_ref, k_ref, v_ref, o_ref, s_ref, p_ref, m_ref, l_ref, acc_ref, *, tq):
    qi = pl.program_id(2)
    qcat = q_ref[0]
    lane2 = lax.broadcasted_iota(jnp.int32, (1, 2 * LANES), 1) % LANES
    lane1 = lax.broadcasted_iota(jnp.int32, (1, LANES), 1)
    zero = jnp.zeros_like(qcat)
    qh = (jnp.where(lane2 < HEAD_DIM, qcat, zero), jnp.where(lane2 >= HEAD_DIM, qcat, zero))

    def tile(j, diagonal, exact):
        start = pl.multiple_of(j * tq, tq)
        kt = k_ref[0, pl.ds(start, tq), :]
        vt = v_ref[0, pl.ds(start, tq), :]
        for head in range(2):
            s_ref[head] = _dot_nt(qh[head], kt)
        for head in range(2):
            for r in range(0, tq, FOX_ROWS):
                rows = slice(r, r + FOX_ROWS)
                s = s_ref[head, rows, :]
                alpha = None
                if diagonal:
                    row = lax.broadcasted_iota(jnp.int32, (FOX_ROWS, tq), 0) + r
                    col = lax.broadcasted_iota(jnp.int32, (FOX_ROWS, tq), 1)
                    s = jnp.where(col <= row, s, NEG)
                    m = jnp.broadcast_to(jnp.max(s, axis=-1, keepdims=True), (FOX_ROWS, LANES))
                    m_ref[head, rows, :] = m
                elif exact:
                    m_old = m_ref[head, rows, :]
                    m = jnp.maximum(m_old, jnp.max(s, axis=-1, keepdims=True))
                    alpha = jnp.exp(m_old - m)
                    m_ref[head, rows, :] = m
                else:
                    m = m_ref[head, rows, :]
                lsum = None
                for g in range(tq // LANES):
                    cols = slice(g * LANES, (g + 1) * LANES)
                    pg = jnp.exp(s[:, cols] - m)
                    p_ref[head, rows, cols] = pg.astype(p_ref.dtype)
                    lsum = pg if lsum is None else lsum + pg
                if diagonal:
                    l_ref[head, rows, :] = lsum
                elif exact:
                    l_ref[head, rows, :] = l_ref[head, rows, :] * alpha + lsum
                    acc_ref[head, rows, :] = acc_ref[head, rows, :] * alpha
                else:
                    l_ref[head, rows, :] = l_ref[head, rows, :] + lsum
            pv = _dot(p_ref[head], vt)
            if diagonal:
                acc_ref[head] = pv
            else:
                acc_ref[head] = acc_ref[head] + pv

    def sweep(exact):
        tile(qi, True, exact)

        def step(t, carry):
            tile(qi - 1 - t, False, exact)
            return carry

        lax.fori_loop(0, qi, step, 0)
        outs, check = [], None
        for head in range(2):
            acc = acc_ref[head]
            l = jnp.sum(l_ref[head], axis=-1, keepdims=True)
            outs.append(acc / l)
            c = jnp.sum(jnp.abs(acc), axis=-1, keepdims=True) + l
            check = c if check is None else check + c
        o_ref[0] = jnp.where(lane1 < HEAD_DIM, outs[0], outs[1]).astype(o_ref.dtype)
        return jnp.max(jnp.where(check <= F32_BIG, 0.0, 1.0))

    overflowed = sweep(exact=False)

    @pl.when(overflowed > 0.0)
    def _():
        sweep(exact=True)


def _fox(qf, kf, vf, tq):
    b, s, _ = qf.shape
    return pl.pallas_call(
        functools.partial(_fox_body, tq=tq),
        grid=(b, N_PAIR, s // tq),
        in_specs=[pl.BlockSpec((1, tq, 2 * LANES), lambda i, p, j: (i, j, p)),
                  pl.BlockSpec((1, s, 2 * LANES), lambda i, p, j: (i, 0, p)),
                  pl.BlockSpec((1, s, LANES), lambda i, p, j: (i, 0, p))],
        out_specs=pl.BlockSpec((1, tq, LANES), lambda i, p, j: (i, j, p)),
        out_shape=jax.ShapeDtypeStruct((b, s, W_ATT), jnp.bfloat16),
        scratch_shapes=[pltpu.VMEM((2, tq, tq), jnp.float32),
                        pltpu.VMEM((2, tq, tq), jnp.bfloat16),
                        pltpu.VMEM((2, tq, LANES), jnp.float32),
                        pltpu.VMEM((2, tq, LANES), jnp.float32),
                        pltpu.VMEM((2, tq, LANES), jnp.float32)],
        compiler_params=pltpu.CompilerParams(
            dimension_semantics=("arbitrary", "arbitrary", "arbitrary"), vmem_limit_bytes=VMEM_LIMIT),
        name="fox_attention",
    )(qf, kf, vf)


def _band_bias_body(g_ref, o_ref):
    width = g_ref.shape[-1]
    base = jnp.broadcast_to(g_ref[0], (Q_BLOCK, width))
    toeplitz = pltpu.roll(base, 0, 1, stride=1, stride_axis=0)[:, :BAND]
    qrow = lax.broadcasted_iota(jnp.int32, (Q_BLOCK, BAND), 0)
    kcol = lax.broadcasted_iota(jnp.int32, (Q_BLOCK, BAND), 1)
    cq = qrow // CHUNK
    ck = kcol // CHUNK - LEFT_CHUNKS
    valid = (ck <= cq) & (ck >= cq - LEFT_CHUNKS)
    o_ref[0] = jnp.where(valid, toeplitz, NEG)


def _band_bias(g_ext):
    h, _, width = g_ext.shape
    return pl.pallas_call(
        _band_bias_body,
        grid=(h,),
        in_specs=[pl.BlockSpec((1, 1, width), lambda i: (i, 0, 0))],
        out_specs=pl.BlockSpec((1, Q_BLOCK, BAND), lambda i: (i, 0, 0)),
        out_shape=jax.ShapeDtypeStruct((h, Q_BLOCK, BAND), jnp.float32),
        name="band_bias",
    )(g_ext)


def _chunk_body(q_ref, k_ref, v_ref, bias_ref, o_ref, kpad_ref, vpad_ref, *, tq):
    qi = pl.program_id(2)
    s_len = k_ref.shape[1]

    @pl.when(qi == 0)
    def _():
        zeros = jnp.zeros((PAD, LANES), kpad_ref.dtype)
        kpad_ref[:PAD, :] = zeros
        vpad_ref[:PAD, :] = zeros
        kpad_ref[PAD:PAD + s_len, :] = k_ref[0]
        vpad_ref[PAD:PAD + s_len, :] = v_ref[0]

    lane1 = lax.broadcasted_iota(jnp.int32, (1, LANES), 1)
    kcol = lax.broadcasted_iota(jnp.int32, (1, BAND), 1)

    def block(blk, carry):
        r0 = pl.multiple_of(blk * Q_BLOCK, Q_BLOCK)
        p0 = pl.multiple_of(qi * tq + r0, Q_BLOCK)
        q2 = q_ref[0, pl.ds(r0, Q_BLOCK), :]
        kb = kpad_ref[pl.ds(p0, BAND), :]
        vb = vpad_ref[pl.ds(p0, BAND), :]
        in_seq = kcol + p0 >= PAD
        outs = []
        for head in range(2):
            sel = (lane1 < HEAD_DIM) if head == 0 else (lane1 >= HEAD_DIM)
            qh = jnp.where(sel, q2, jnp.zeros_like(q2))
            s = jnp.where(in_seq, _dot_nt(qh, kb) + bias_ref[head], NEG)
            m = jnp.max(s, axis=-1, keepdims=True)
            p = jnp.exp(s - m)
            l = jnp.sum(p, axis=-1, keepdims=True)
            outs.append(_dot(p.astype(jnp.bfloat16), vb) / l)
        o_ref[0, pl.ds(r0, Q_BLOCK), :] = jnp.where(lane1 < HEAD_DIM, outs[0], outs[1]).astype(o_ref.dtype)
        return carry

    lax.fori_loop(0, tq // Q_BLOCK, block, 0)


def _chunk_attention(qc, kc, vc, bias, tq):
    b, s, _ = qc.shape
    return pl.pallas_call(
        functools.partial(_chunk_body, tq=tq),
        grid=(b, N_PAIR, s // tq),
        in_specs=[pl.BlockSpec((1, tq, LANES), lambda i, p, j: (i, j, p)),
                  pl.BlockSpec((1, s, LANES), lambda i, p, j: (i, 0, p)),
                  pl.BlockSpec((1, s, LANES), lambda i, p, j: (i, 0, p)),
                  pl.BlockSpec((2, Q_BLOCK, BAND), lambda i, p, j: (p, 0, 0))],
        out_specs=pl.BlockSpec((1, tq, LANES), lambda i, p, j: (i, j, p)),
        out_shape=jax.ShapeDtypeStruct((b, s, W_ATT), jnp.bfloat16),
        scratch_shapes=[pltpu.VMEM((PAD + s, LANES), jnp.bfloat16),
                        pltpu.VMEM((PAD + s, LANES), jnp.bfloat16)],
        compiler_params=pltpu.CompilerParams(
            dimension_semantics=("arbitrary", "arbitrary", "arbitrary"), vmem_limit_bytes=VMEM_LIMIT),
        name="chunk_attention",
    )(qc, kc, vc, bias)


def _out_ffn_body(x_ref, oa_ref, ob_ref, gate_ref, wa_ref, wb_ref, wo_ref, g2_ref, wu_ref, wd_ref, gf_ref,
                  o_ref, *, ff_chunk, final):
    d = x_ref.shape[-1]
    ya = _dot(oa_ref[0], wa_ref[...])
    yb = _dot(ob_ref[0], wb_ref[...])
    ga = gate_ref[0, :, :d].astype(jnp.float32)
    gb = gate_ref[0, :, d:].astype(jnp.float32)
    merged = (ga * ya + gb * yb).astype(jnp.bfloat16)
    x1 = x_ref[0] + _dot(merged, wo_ref[...])
    h2 = _rms(x1, g2_ref[...]).astype(jnp.bfloat16)
    acc = x1
    for c0 in range(0, wu_ref.shape[1], ff_chunk):
        u = jnp.maximum(_dot(h2, wu_ref[:, c0:c0 + ff_chunk]), 0.0)
        acc = acc + _dot((u * u).astype(jnp.bfloat16), wd_ref[c0:c0 + ff_chunk, :])
    if final:
        acc = _rms(acc, gf_ref[...])
    o_ref[0] = acc


def _out_ffn(x, oa, ob, gates, wa, wb, wo, g2, wu, wd, gf, tm, final):
    b, s, d = x.shape
    tok = lambda w: pl.BlockSpec((1, tm, w), lambda i, j: (i, j, 0))
    return pl.pallas_call(
        functools.partial(_out_ffn_body, ff_chunk=min(1024, wu.shape[1]), final=final),
        grid=(b, s // tm),
        in_specs=[tok(d), tok(W_ATT), tok(W_ATT), tok(2 * d)]
                 + [_const_spec(a.shape) for a in (wa, wb, wo, g2, wu, wd, gf)],
        out_specs=tok(d),
        out_shape=jax.ShapeDtypeStruct((b, s, d), jnp.float32),
        compiler_params=pltpu.CompilerParams(
            dimension_semantics=("arbitrary", "arbitrary"), vmem_limit_bytes=VMEM_LIMIT),
        name="out_ffn",
    )(x, oa, ob, gates, wa, wb, wo, g2, wu, wd, gf)


def _aug_placement():
    eq = [[0.0] * W_ATT for _ in range(LANES)]
    ek = [[0.0] * W_ATT for _ in range(LANES)]
    one = 3 * H_FOX
    for h in range(H_FOX):
        for piece in range(3):
            eq[piece * H_FOX + h][HEAD_DIM * h + piece] = 1.0
            eq[one][HEAD_DIM * h + 3 + piece] = 1.0
            ek[one][HEAD_DIM * h + piece] = 1.0
            ek[piece * H_FOX + h][HEAD_DIM * h + 3 + piece] = -1.0
    return jnp.array(eq, jnp.bfloat16), jnp.array(ek, jnp.bfloat16)


def kernel(x, norm1, w_in, forget_bias, rel_bias, w_branch_a, w_branch_b, w_out, norm2, w_up, w_down, final_norm):
    b, s, d = x.shape
    depth = w_in.shape[0]
    bf16 = jnp.bfloat16
    tm = min(512, s)
    tq = min(512, s)
    tqc = min(1024, s)
    eq, ek = _aug_placement()
    gf = final_norm.reshape(1, d)
    o = 3 * W_ATT + H_FOX
    for l in range(depth):
        w = w_in[l]
        wq = (w[:, :W_ATT] * SCALE).astype(bf16)
        wk = w[:, W_ATT:2 * W_ATT].astype(bf16)
        wv = w[:, 2 * W_ATT:3 * W_ATT].astype(bf16)
        wf = jnp.pad(w[:, 3 * W_ATT:o], ((0, 0), (0, LANES - H_FOX))).astype(bf16)
        wc = jnp.concatenate([w[:, o:o + W_ATT] * SCALE, w[:, o + W_ATT:o + 3 * W_ATT]], axis=1).astype(bf16)
        wg = w[:, o + 3 * W_ATT:].astype(bf16)
        bf = jnp.pad(forget_bias[l], (0, LANES - H_FOX)).reshape(1, LANES)
        qf, kf, vf, qc, kc, vc, gates = _in_proj(
            x, norm1[l].reshape(1, d), wq, wk, wv, wc, wg, wf, bf, eq, ek, tm)

        o_a = _fox(qf, kf, vf, tq)

        far = rel_bias[l][:, 2 * MAX_REL:]
        g_ext = jnp.concatenate([jnp.broadcast_to(far, (H_CHK, PAD - MAX_REL + 1)),
                                 rel_bias[l][:, 2 * MAX_REL - 1:0:-1],
                                 jnp.broadcast_to(far, (H_CHK, Q_BLOCK))], axis=1)
        bias = _band_bias(g_ext.reshape(H_CHK, 1, -1))
        o_b = _chunk_attention(qc, kc, vc, bias, tqc)

        x = _out_ffn(x, o_a, o_b, gates, w_branch_a[l].astype(bf16), w_branch_b[l].astype(bf16),
                     w_out[l].astype(bf16), norm2[l].reshape(1, d), w_up[l].astype(bf16),
                     w_down[l].astype(bf16), gf, tm, final=(l == depth - 1))
    return x
```

```python
import functools
import math

import jax
import jax.numpy as jnp
from jax import lax
from jax.experimental import pallas as pl
from jax.experimental.pallas import tpu as pltpu

HEAD_DIM = 64
H_FOX = 8
H_CHK = 8
N_PAIR = 4
W_ATT = H_FOX * HEAD_DIM
CHUNK = 64
Q_BLOCK = 128
LEFT_CHUNKS = 8
PAD = LEFT_CHUNKS * CHUNK
BAND = PAD + Q_BLOCK
MAX_REL = 128
EPS = 1e-6
NEG = -1e30
LANES = 128
SCALE = 1.0 / math.sqrt(HEAD_DIM)
F32_BIG = 3.0e38
FOX_ROWS = 32
SKIP_MARGIN = 30.0
NORM_SLACK = 1.02
VMEM_LIMIT = 60 * 1024 * 1024

_NT = (((1,), (1,)), ((), ()))


def _dot(a, b):
    return jnp.dot(a, b, preferred_element_type=jnp.float32)


def _dot_nt(a, b):
    return lax.dot_general(a, b, _NT, preferred_element_type=jnp.float32)


def _rms(x, g):
    ms = jnp.mean(x * x, axis=-1, keepdims=True)
    return x * lax.rsqrt(ms + EPS) * g


def _const_spec(shape):
    nd = len(shape)
    return pl.BlockSpec(shape, lambda *_: (0,) * nd, pipeline_mode=pl.Buffered(1))


def _in_proj_body(x_ref, g_ref, wq_ref, wk_ref, wv_ref, wc_ref, wg_ref, wf_ref, bf_ref, eq_ref, ek_ref, hsum_ref,
                  qf_ref, kf_ref, vf_ref, qc_ref, kc_ref, vc_ref, gate_ref, stat_ref, carry_ref):
    tm = x_ref.shape[1]

    @pl.when(pl.program_id(1) == 0)
    def _():
        carry_ref[...] = jnp.zeros_like(carry_ref)

    h = _rms(x_ref[0], g_ref[...]).astype(jnp.bfloat16)

    lane = lax.broadcasted_iota(jnp.int32, (tm, LANES), 1)
    row = lax.broadcasted_iota(jnp.int32, (tm, LANES), 0)
    logf = jax.nn.log_sigmoid(_dot(h, wf_ref[...]) + bf_ref[...])
    c = jnp.where(lane < H_FOX, logf, 0.0)
    k = 1
    while k < tm:
        c = c + jnp.where(row >= k, pltpu.roll(c, k, 0), 0.0)
        k *= 2
    c = c + carry_ref[...]
    carry_ref[...] = c[tm - 1:tm, :]

    hi = c.astype(jnp.bfloat16).astype(jnp.float32)
    r1 = c - hi
    mid = r1.astype(jnp.bfloat16).astype(jnp.float32)
    lo = (r1 - mid).astype(jnp.bfloat16).astype(jnp.float32)
    pieces = hi + pltpu.roll(mid, H_FOX, 1) + pltpu.roll(lo, 2 * H_FOX, 1)
    pieces = jnp.where(lane == 3 * H_FOX, 1.0, pieces).astype(jnp.bfloat16)
    aug_q = _dot(pieces, eq_ref[...]).astype(jnp.bfloat16)
    aug_k = _dot(pieces, ek_ref[...]).astype(jnp.bfloat16)

    q = _dot(h, wq_ref[...]).astype(jnp.bfloat16)
    kk = _dot(h, wk_ref[...]).astype(jnp.bfloat16)
    for p in range(N_PAIR):
        src = slice(p * LANES, (p + 1) * LANES)
        qf_ref[0, :, 2 * p * LANES:(2 * p + 1) * LANES] = q[:, src]
        qf_ref[0, :, (2 * p + 1) * LANES:(2 * p + 2) * LANES] = aug_q[:, src]
        kf_ref[0, :, 2 * p * LANES:(2 * p + 1) * LANES] = kk[:, src]
        kf_ref[0, :, (2 * p + 1) * LANES:(2 * p + 2) * LANES] = aug_k[:, src]
    vf_ref[0] = _dot(h, wv_ref[...]).astype(jnp.bfloat16)

    def max_norm(t):
        t = t.astype(jnp.float32)
        sq = _dot((t * t).astype(jnp.bfloat16), hsum_ref[...])
        return jnp.sqrt(jnp.max(sq, axis=0, keepdims=True))

    sub = lax.broadcasted_iota(jnp.int32, (8, LANES), 0)
    stat = jnp.where(sub == 0, c[0:1, :], 0.0)
    stat = jnp.where(sub == 1, c[tm - 1:tm, :], stat)
    stat = jnp.where(sub == 2, max_norm(q), stat)
    stat_ref[0, 0] = jnp.where(sub == 3, max_norm(kk), stat)

    pc = _dot(h, wc_ref[...])
    qc_ref[0] = pc[:, :W_ATT].astype(jnp.bfloat16)
    kc_ref[0] = pc[:, W_ATT:2 * W_ATT].astype(jnp.bfloat16)
    vc_ref[0] = pc[:, 2 * W_ATT:].astype(jnp.bfloat16)
    gate_ref[0] = jax.nn.sigmoid(_dot(h, wg_ref[...])).astype(jnp.bfloat16)


def _in_proj(x, g, wq, wk, wv, wc, wg, wf, bf, eq, ek, hsum, tm):
    b, s, d = x.shape
    tok = lambda w: pl.BlockSpec((1, tm, w), lambda i, j: (i, j, 0))
    bf16 = jnp.bfloat16
    out_shape = [jax.ShapeDtypeStruct((b, s, w), bf16)
                 for w in (2 * W_ATT, 2 * W_ATT, W_ATT, W_ATT, W_ATT, W_ATT, 2 * d)]
    out_shape.append(jax.ShapeDtypeStruct((b, s // tm, 8, LANES), jnp.float32))
    consts = (g, wq, wk, wv, wc, wg, wf, bf, eq, ek, hsum)
    return pl.pallas_call(
        _in_proj_body,
        grid=(b, s // tm),
        in_specs=[tok(d)] + [_const_spec(a.shape) for a in consts],
        out_specs=[tok(sh.shape[-1]) for sh in out_shape[:-1]]
                  + [pl.BlockSpec((1, 1, 8, LANES), lambda i, j: (i, j, 0, 0))],
        out_shape=out_shape,
        scratch_shapes=[pltpu.VMEM((1, LANES), jnp.float32)],
        compiler_params=pltpu.CompilerParams(
            dimension_semantics=("arbitrary", "arbitrary"), vmem_limit_bytes=VMEM_LIMIT),
        name="in_proj",
    )(x, *consts)


def _fox_body(cfirst_ref, clast_ref, qnorm_ref, knorm_ref, q_ref, k_ref, v_ref, o_ref,
              s_ref, p_ref, m_ref, l_ref, acc_ref, *, tq):
    qi = pl.program_id(2)
    n_tiles = pl.num_programs(2)

    def reach(head):
        base = (pl.program_id(0) * H_FOX + 2 * pl.program_id(1) + head) * n_tiles
        kmax = lax.fori_loop(0, qi + 1, lambda t, mx: jnp.maximum(mx, knorm_ref[base + t]), 0.0)
        bound = cfirst_ref[base + qi] + 2.0 * NORM_SLACK * qnorm_ref[base + qi] * kmax + SKIP_MARGIN
        return lax.fori_loop(
            0, qi, lambda t, n: n + jnp.where(bound - clast_ref[base + t] > 0.0, 1, 0), 0)

    n_visit = jnp.maximum(reach(0), reach(1))
    qcat = q_ref[0]
    lane2 = lax.broadcasted_iota(jnp.int32, (1, 2 * LANES), 1) % LANES
    lane1 = lax.broadcasted_iota(jnp.int32, (1, LANES), 1)
    zero = jnp.zeros_like(qcat)
    qh = (jnp.where(lane2 < HEAD_DIM, qcat, zero), jnp.where(lane2 >= HEAD_DIM, qcat, zero))

    def tile(j, diagonal, exact):
        start = pl.multiple_of(j * tq, tq)
        kt = k_ref[0, pl.ds(start, tq), :]
        vt = v_ref[0, pl.ds(start, tq), :]
        for head in range(2):
            s_ref[head] = _dot_nt(qh[head], kt)
        for head in range(2):
            for r in range(0, tq, FOX_ROWS):
                rows = slice(r, r + FOX_ROWS)
                s = s_ref[head, rows, :]
                alpha = None
                if diagonal:
                    row = lax.broadcasted_iota(jnp.int32, (FOX_ROWS, tq), 0) + r
                    col = lax.broadcasted_iota(jnp.int32, (FOX_ROWS, tq), 1)
                    s = jnp.where(col <= row, s, NEG)
                    m = jnp.broadcast_to(jnp.max(s, axis=-1, keepdims=True), (FOX_ROWS, LANES))
                    m_ref[head, rows, :] = m
                elif exact:
                    m_old = m_ref[head, rows, :]
                    m = jnp.maximum(m_old, jnp.max(s, axis=-1, keepdims=True))
                    alpha = jnp.exp(m_old - m)
                    m_ref[head, rows, :] = m
                else:
                    m = m_ref[head, rows, :]
                lsum = None
                for g in range(tq // LANES):
                    cols = slice(g * LANES, (g + 1) * LANES)
                    pg = jnp.exp(s[:, cols] - m)
                    p_ref[head, rows, cols] = pg.astype(p_ref.dtype)
                    lsum = pg if lsum is None else lsum + pg
                if diagonal:
                    l_ref[head, rows, :] = lsum
                elif exact:
                    l_ref[head, rows, :] = l_ref[head, rows, :] * alpha + lsum
                    acc_ref[head, rows, :] = acc_ref[head, rows, :] * alpha
                else:
                    l_ref[head, rows, :] = l_ref[head, rows, :] + lsum
            pv = _dot(p_ref[head], vt)
            if diagonal:
                acc_ref[head] = pv
            else:
                acc_ref[head] = acc_ref[head] + pv

    def sweep(exact):
        tile(qi, True, exact)

        def step(t, carry):
            tile(qi - 1 - t, False, exact)
            return carry

        lax.fori_loop(0, n_visit, step, 0)
        outs, check = [], None
        for head in range(2):
            acc = acc_ref[head]
            l = jnp.sum(l_ref[head], axis=-1, keepdims=True)
            outs.append(acc / l)
            c = jnp.sum(jnp.abs(acc), axis=-1, keepdims=True) + l
            check = c if check is None else check + c
        o_ref[0] = jnp.where(lane1 < HEAD_DIM, outs[0], outs[1]).astype(o_ref.dtype)
        return jnp.max(jnp.where(check <= F32_BIG, 0.0, 1.0))

    overflowed = sweep(exact=False)

    @pl.when(overflowed > 0.0)
    def _():
        sweep(exact=True)


def _fox(stats, qf, kf, vf, tq):
    b, s, _ = qf.shape
    smem = pl.BlockSpec(memory_space=pltpu.SMEM)
    return pl.pallas_call(
        functools.partial(_fox_body, tq=tq),
        grid=(b, N_PAIR, s // tq),
        in_specs=[smem, smem, smem, smem,
                  pl.BlockSpec((1, tq, 2 * LANES), lambda i, p, j: (i, j, p)),
                  pl.BlockSpec((1, s, 2 * LANES), lambda i, p, j: (i, 0, p)),
                  pl.BlockSpec((1, s, LANES), lambda i, p, j: (i, 0, p))],
        out_specs=pl.BlockSpec((1, tq, LANES), lambda i, p, j: (i, j, p)),
        out_shape=jax.ShapeDtypeStruct((b, s, W_ATT), jnp.bfloat16),
        scratch_shapes=[pltpu.VMEM((2, tq, tq), jnp.float32),
                        pltpu.VMEM((2, tq, tq), jnp.bfloat16),
                        pltpu.VMEM((2, tq, LANES), jnp.float32),
                        pltpu.VMEM((2, tq, LANES), jnp.float32),
                        pltpu.VMEM((2, tq, LANES), jnp.float32)],
        compiler_params=pltpu.CompilerParams(
            dimension_semantics=("arbitrary", "arbitrary", "arbitrary"), vmem_limit_bytes=VMEM_LIMIT),
        name="fox_attention",
    )(*stats, qf, kf, vf)


def _band_bias_body(g_ref, o_ref):
    width = g_ref.shape[-1]
    base = jnp.broadcast_to(g_ref[0], (Q_BLOCK, width))
    toeplitz = pltpu.roll(base, 0, 1, stride=1, stride_axis=0)[:, :BAND]
    qrow = lax.broadcasted_iota(jnp.int32, (Q_BLOCK, BAND), 0)
    kcol = lax.broadcasted_iota(jnp.int32, (Q_BLOCK, BAND), 1)
    cq = qrow // CHUNK
    ck = kcol // CHUNK - LEFT_CHUNKS
    valid = (ck <= cq) & (ck >= cq - LEFT_CHUNKS)
    o_ref[0] = jnp.where(valid, toeplitz, NEG)


def _band_bias(g_ext):
    h, _, width = g_ext.shape
    return pl.pallas_call(
        _band_bias_body,
        grid=(h,),
        in_specs=[pl.BlockSpec((1, 1, width), lambda i: (i, 0, 0))],
        out_specs=pl.BlockSpec((1, Q_BLOCK, BAND), lambda i: (i, 0, 0)),
        out_shape=jax.ShapeDtypeStruct((h, Q_BLOCK, BAND), jnp.float32),
        name="band_bias",
    )(g_ext)


def _chunk_body(q_ref, k_ref, v_ref, bias_ref, o_ref, kpad_ref, vpad_ref, *, tq):
    qi = pl.program_id(2)
    s_len = k_ref.shape[1]

    @pl.when(qi == 0)
    def _():
        zeros = jnp.zeros((PAD, LANES), kpad_ref.dtype)
        kpad_ref[:PAD, :] = zeros
        vpad_ref[:PAD, :] = zeros
        kpad_ref[PAD:PAD + s_len, :] = k_ref[0]
        vpad_ref[PAD:PAD + s_len, :] = v_ref[0]

    lane1 = lax.broadcasted_iota(jnp.int32, (1, LANES), 1)
    kcol = lax.broadcasted_iota(jnp.int32, (1, BAND), 1)

    def block(blk, carry):
        r0 = pl.multiple_of(blk * Q_BLOCK, Q_BLOCK)
        p0 = pl.multiple_of(qi * tq + r0, Q_BLOCK)
        q2 = q_ref[0, pl.ds(r0, Q_BLOCK), :]
        kb = kpad_ref[pl.ds(p0, BAND), :]
        vb = vpad_ref[pl.ds(p0, BAND), :]
        in_seq = kcol + p0 >= PAD
        outs = []
        for head in range(2):
            sel = (lane1 < HEAD_DIM) if head == 0 else (lane1 >= HEAD_DIM)
            qh = jnp.where(sel, q2, jnp.zeros_like(q2))
            s = jnp.where(in_seq, _dot_nt(qh, kb) + bias_ref[head], NEG)
            m = jnp.max(s, axis=-1, keepdims=True)
            p = jnp.exp(s - m)
            l = jnp.sum(p, axis=-1, keepdims=True)
            outs.append(_dot(p.astype(jnp.bfloat16), vb) / l)
        o_ref[0, pl.ds(r0, Q_BLOCK), :] = jnp.where(lane1 < HEAD_DIM, outs[0], outs[1]).astype(o_ref.dtype)
        return carry

    lax.fori_loop(0, tq // Q_BLOCK, block, 0)


def _chunk_attention(qc, kc, vc, bias, tq):
    b, s, _ = qc.shape
    return pl.pallas_call(
        functools.partial(_chunk_body, tq=tq),
        grid=(b, N_PAIR, s // tq),
        in_specs=[pl.BlockSpec((1, tq, LANES), lambda i, p, j: (i, j, p)),
                  pl.BlockSpec((1, s, LANES), lambda i, p, j: (i, 0, p)),
                  pl.BlockSpec((1, s, LANES), lambda i, p, j: (i, 0, p)),
                  pl.BlockSpec((2, Q_BLOCK, BAND), lambda i, p, j: (p, 0, 0))],
        out_specs=pl.BlockSpec((1, tq, LANES), lambda i, p, j: (i, j, p)),
        out_shape=jax.ShapeDtypeStruct((b, s, W_ATT), jnp.bfloat16),
        scratch_shapes=[pltpu.VMEM((PAD + s, LANES), jnp.bfloat16),
                        pltpu.VMEM((PAD + s, LANES), jnp.bfloat16)],
        compiler_params=pltpu.CompilerParams(
            dimension_semantics=("arbitrary", "arbitrary", "arbitrary"), vmem_limit_bytes=VMEM_LIMIT),
        name="chunk_attention",
    )(qc, kc, vc, bias)


def _out_ffn_body(x_ref, oa_ref, ob_ref, gate_ref, wa_ref, wb_ref, wo_ref, g2_ref, wu_ref, wd_ref, gf_ref,
                  o_ref, *, ff_chunk, final):
    d = x_ref.shape[-1]
    ya = _dot(oa_ref[0], wa_ref[...])
    yb = _dot(ob_ref[0], wb_ref[...])
    ga = gate_ref[0, :, :d].astype(jnp.float32)
    gb = gate_ref[0, :, d:].astype(jnp.float32)
    merged = (ga * ya + gb * yb).astype(jnp.bfloat16)
    x1 = x_ref[0] + _dot(merged, wo_ref[...])
    h2 = _rms(x1, g2_ref[...]).astype(jnp.bfloat16)
    acc = x1
    for c0 in range(0, wu_ref.shape[1], ff_chunk):
        u = jnp.maximum(_dot(h2, wu_ref[:, c0:c0 + ff_chunk]), 0.0)
        acc = acc + _dot((u * u).astype(jnp.bfloat16), wd_ref[c0:c0 + ff_chunk, :])
    if final:
        acc = _rms(acc, gf_ref[...])
    o_ref[0] = acc


def _out_ffn(x, oa, ob, gates, wa, wb, wo, g2, wu, wd, gf, tm, final):
    b, s, d = x.shape
    tok = lambda w: pl.BlockSpec((1, tm, w), lambda i, j: (i, j, 0))
    return pl.pallas_call(
        functools.partial(_out_ffn_body, ff_chunk=min(1024, wu.shape[1]), final=final),
        grid=(b, s // tm),
        in_specs=[tok(d), tok(W_ATT), tok(W_ATT), tok(2 * d)]
                 + [_const_spec(a.shape) for a in (wa, wb, wo, g2, wu, wd, gf)],
        out_specs=tok(d),
        out_shape=jax.ShapeDtypeStruct((b, s, d), jnp.float32),
        compiler_params=pltpu.CompilerParams(
            dimension_semantics=("arbitrary", "arbitrary"), vmem_limit_bytes=VMEM_LIMIT),
        name="out_ffn",
    )(x, oa, ob, gates, wa, wb, wo, g2, wu, wd, gf)


def _aug_placement():
    eq = [[0.0] * W_ATT for _ in range(LANES)]
    ek = [[0.0] * W_ATT for _ in range(LANES)]
    one = 3 * H_FOX
    for h in range(H_FOX):
        for piece in range(3):
            eq[piece * H_FOX + h][HEAD_DIM * h + piece] = 1.0
            eq[one][HEAD_DIM * h + 3 + piece] = 1.0
            ek[one][HEAD_DIM * h + piece] = 1.0
            ek[piece * H_FOX + h][HEAD_DIM * h + 3 + piece] = -1.0
    return jnp.array(eq, jnp.bfloat16), jnp.array(ek, jnp.bfloat16)


def kernel(x, norm1, w_in, forget_bias, rel_bias, w_branch_a, w_branch_b, w_out, norm2, w_up, w_down, final_norm):
    b, s, d = x.shape
    depth = w_in.shape[0]
    bf16 = jnp.bfloat16
    tm = min(512, s)
    tqc = min(1024, s)
    eq, ek = _aug_placement()
    head_of_col = lax.broadcasted_iota(jnp.int32, (W_ATT, LANES), 0) // HEAD_DIM
    hsum = (head_of_col == lax.broadcasted_iota(jnp.int32, (W_ATT, LANES), 1)).astype(bf16)
    gf = final_norm.reshape(1, d)
    o = 3 * W_ATT + H_FOX
    for l in range(depth):
        w = w_in[l]
        wq = (w[:, :W_ATT] * SCALE).astype(bf16)
        wk = w[:, W_ATT:2 * W_ATT].astype(bf16)
        wv = w[:, 2 * W_ATT:3 * W_ATT].astype(bf16)
        wf = jnp.pad(w[:, 3 * W_ATT:o], ((0, 0), (0, LANES - H_FOX))).astype(bf16)
        wc = jnp.concatenate([w[:, o:o + W_ATT] * SCALE, w[:, o + W_ATT:o + 3 * W_ATT]], axis=1).astype(bf16)
        wg = w[:, o + 3 * W_ATT:].astype(bf16)
        bf = jnp.pad(forget_bias[l], (0, LANES - H_FOX)).reshape(1, LANES)
        qf, kf, vf, qc, kc, vc, gates, stat = _in_proj(
            x, norm1[l].reshape(1, d), wq, wk, wv, wc, wg, wf, bf, eq, ek, hsum, tm)

        stats = stat[:, :, :4, :H_FOX].transpose(2, 0, 3, 1).reshape(4, -1)
        o_a = _fox(tuple(stats), qf, kf, vf, tm)

        far = rel_bias[l][:, 2 * MAX_REL:]
        g_ext = jnp.concatenate([jnp.broadcast_to(far, (H_CHK, PAD - MAX_REL + 1)),
                                 rel_bias[l][:, 2 * MAX_REL - 1:0:-1],
                                 jnp.broadcast_to(far, (H_CHK, Q_BLOCK))], axis=1)
        bias = _band_bias(g_ext.reshape(H_CHK, 1, -1))
        o_b = _chunk_attention(qc, kc, vc, bias, tqc)

        x = _out_ffn(x, o_a, o_b, gates, w_branch_a[l].astype(bf16), w_branch_b[l].astype(bf16),
                     w_out[l].astype(bf16), norm2[l].reshape(1, d), w_up[l].astype(bf16),
                     w_down[l].astype(bf16), gf, tm, final=(l == depth - 1))
    return x
```

```python
import functools
import math

import jax
import jax.numpy as jnp
from jax import lax
from jax.experimental import pallas as pl
from jax.experimental.pallas import tpu as pltpu

HEAD_DIM = 64
H_FOX = 8
H_CHK = 8
N_PAIR = 4
W_ATT = H_FOX * HEAD_DIM
CHUNK = 64
Q_BLOCK = 128
LEFT_CHUNKS = 8
PAD = LEFT_CHUNKS * CHUNK
BAND = PAD + Q_BLOCK
MAX_REL = 128
EPS = 1e-6
NEG = -1e30
LANES = 128
SCALE = 1.0 / math.sqrt(HEAD_DIM)
F32_BIG = 3.0e38
FOX_ROWS = 32
CHUNK_ROWS = 32
CHUNK_GROUP = 4
SKIP_MARGIN = 30.0
NORM_SLACK = 1.02
VMEM_LIMIT = 60 * 1024 * 1024

_NT = (((1,), (1,)), ((), ()))


def _dot(a, b):
    return jnp.dot(a, b, preferred_element_type=jnp.float32)


def _dot_nt(a, b):
    return lax.dot_general(a, b, _NT, preferred_element_type=jnp.float32)


def _rms(x, g):
    ms = jnp.mean(x * x, axis=-1, keepdims=True)
    return x * lax.rsqrt(ms + EPS) * g


def _const_spec(shape):
    nd = len(shape)
    return pl.BlockSpec(shape, lambda *_: (0,) * nd, pipeline_mode=pl.Buffered(1))


def _in_proj_body(x_ref, g_ref, wq_ref, wk_ref, wv_ref, wc_ref, wg_ref, wf_ref, bf_ref, eq_ref, ek_ref, hsum_ref,
                  qf_ref, kf_ref, vf_ref, qc_ref, kc_ref, vc_ref, gate_ref, stat_ref, carry_ref):
    tm = x_ref.shape[1]

    @pl.when(pl.program_id(1) == 0)
    def _():
        carry_ref[...] = jnp.zeros_like(carry_ref)

    h = _rms(x_ref[0], g_ref[...]).astype(jnp.bfloat16)

    lane = lax.broadcasted_iota(jnp.int32, (tm, LANES), 1)
    row = lax.broadcasted_iota(jnp.int32, (tm, LANES), 0)
    logf = jax.nn.log_sigmoid(_dot(h, wf_ref[...]) + bf_ref[...])
    c = jnp.where(lane < H_FOX, logf, 0.0)
    k = 1
    while k < tm:
        c = c + jnp.where(row >= k, pltpu.roll(c, k, 0), 0.0)
        k *= 2
    c = c + carry_ref[...]
    carry_ref[...] = c[tm - 1:tm, :]

    hi = c.astype(jnp.bfloat16).astype(jnp.float32)
    r1 = c - hi
    mid = r1.astype(jnp.bfloat16).astype(jnp.float32)
    lo = (r1 - mid).astype(jnp.bfloat16).astype(jnp.float32)
    pieces = hi + pltpu.roll(mid, H_FOX, 1) + pltpu.roll(lo, 2 * H_FOX, 1)
    pieces = jnp.where(lane == 3 * H_FOX, 1.0, pieces).astype(jnp.bfloat16)
    aug_q = _dot(pieces, eq_ref[...]).astype(jnp.bfloat16)
    aug_k = _dot(pieces, ek_ref[...]).astype(jnp.bfloat16)

    q = _dot(h, wq_ref[...]).astype(jnp.bfloat16)
    kk = _dot(h, wk_ref[...]).astype(jnp.bfloat16)
    for p in range(N_PAIR):
        src = slice(p * LANES, (p + 1) * LANES)
        qf_ref[0, :, 2 * p * LANES:(2 * p + 1) * LANES] = q[:, src]
        qf_ref[0, :, (2 * p + 1) * LANES:(2 * p + 2) * LANES] = aug_q[:, src]
        kf_ref[0, :, 2 * p * LANES:(2 * p + 1) * LANES] = kk[:, src]
        kf_ref[0, :, (2 * p + 1) * LANES:(2 * p + 2) * LANES] = aug_k[:, src]
    vf_ref[0] = _dot(h, wv_ref[...]).astype(jnp.bfloat16)

    def max_norm(t):
        t = t.astype(jnp.float32)
        sq = _dot((t * t).astype(jnp.bfloat16), hsum_ref[...])
        return jnp.sqrt(jnp.max(sq, axis=0, keepdims=True))

    sub = lax.broadcasted_iota(jnp.int32, (8, LANES), 0)
    stat = jnp.where(sub == 0, c[0:1, :], 0.0)
    stat = jnp.where(sub == 1, c[tm - 1:tm, :], stat)
    stat = jnp.where(sub == 2, max_norm(q), stat)
    stat_ref[0, 0] = jnp.where(sub == 3, max_norm(kk), stat)

    pc = _dot(h, wc_ref[...])
    qc_ref[0] = pc[:, :W_ATT].astype(jnp.bfloat16)
    kc_ref[0] = pc[:, W_ATT:2 * W_ATT].astype(jnp.bfloat16)
    vc_ref[0] = pc[:, 2 * W_ATT:].astype(jnp.bfloat16)
    gate_ref[0] = jax.nn.sigmoid(_dot(h, wg_ref[...])).astype(jnp.bfloat16)


def _in_proj(x, g, wq, wk, wv, wc, wg, wf, bf, eq, ek, hsum, tm):
    b, s, d = x.shape
    tok = lambda w: pl.BlockSpec((1, tm, w), lambda i, j: (i, j, 0))
    bf16 = jnp.bfloat16
    out_shape = [jax.ShapeDtypeStruct((b, s, w), bf16)
                 for w in (2 * W_ATT, 2 * W_ATT, W_ATT, W_ATT, W_ATT, W_ATT, 2 * d)]
    out_shape.append(jax.ShapeDtypeStruct((b, s // tm, 8, LANES), jnp.float32))
    consts = (g, wq, wk, wv, wc, wg, wf, bf, eq, ek, hsum)
    return pl.pallas_call(
        _in_proj_body,
        grid=(b, s // tm),
        in_specs=[tok(d)] + [_const_spec(a.shape) for a in consts],
        out_specs=[tok(sh.shape[-1]) for sh in out_shape[:-1]]
                  + [pl.BlockSpec((1, 1, 8, LANES), lambda i, j: (i, j, 0, 0))],
        out_shape=out_shape,
        scratch_shapes=[pltpu.VMEM((1, LANES), jnp.float32)],
        compiler_params=pltpu.CompilerParams(
            dimension_semantics=("arbitrary", "arbitrary"), vmem_limit_bytes=VMEM_LIMIT),
        name="in_proj",
    )(x, *consts)


def _fox_body(cfirst_ref, clast_ref, qnorm_ref, knorm_ref, q_ref, k_ref, v_ref, o_ref,
              s_ref, p_ref, m_ref, l_ref, acc_ref, *, tq):
    qi = pl.program_id(2)
    n_tiles = pl.num_programs(2)

    def reach(head):
        base = (pl.program_id(0) * H_FOX + 2 * pl.program_id(1) + head) * n_tiles
        kmax = lax.fori_loop(0, qi + 1, lambda t, mx: jnp.maximum(mx, knorm_ref[base + t]), 0.0)
        bound = cfirst_ref[base + qi] + 2.0 * NORM_SLACK * qnorm_ref[base + qi] * kmax + SKIP_MARGIN
        return lax.fori_loop(
            0, qi, lambda t, n: n + jnp.where(bound - clast_ref[base + t] > 0.0, 1, 0), 0)

    n_visit = jnp.maximum(reach(0), reach(1))
    qcat = q_ref[0]
    lane2 = lax.broadcasted_iota(jnp.int32, (1, 2 * LANES), 1) % LANES
    lane1 = lax.broadcasted_iota(jnp.int32, (1, LANES), 1)
    zero = jnp.zeros_like(qcat)
    qh = (jnp.where(lane2 < HEAD_DIM, qcat, zero), jnp.where(lane2 >= HEAD_DIM, qcat, zero))

    def tile(j, diagonal, exact):
        start = pl.multiple_of(j * tq, tq)
        kt = k_ref[0, pl.ds(start, tq), :]
        vt = v_ref[0, pl.ds(start, tq), :]
        for head in range(2):
            s_ref[head] = _dot_nt(qh[head], kt)
        for head in range(2):
            for r in range(0, tq, FOX_ROWS):
                rows = slice(r, r + FOX_ROWS)
                s = s_ref[head, rows, :]
                alpha = None
                if diagonal:
                    row = lax.broadcasted_iota(jnp.int32, (FOX_ROWS, tq), 0) + r
                    col = lax.broadcasted_iota(jnp.int32, (FOX_ROWS, tq), 1)
                    s = jnp.where(col <= row, s, NEG)
                    m = jnp.broadcast_to(jnp.max(s, axis=-1, keepdims=True), (FOX_ROWS, LANES))
                    m_ref[head, rows, :] = m
                elif exact:
                    m_old = m_ref[head, rows, :]
                    m = jnp.maximum(m_old, jnp.max(s, axis=-1, keepdims=True))
                    alpha = jnp.exp(m_old - m)
                    m_ref[head, rows, :] = m
                else:
                    m = m_ref[head, rows, :]
                lsum = None
                for g in range(tq // LANES):
                    cols = slice(g * LANES, (g + 1) * LANES)
                    pg = jnp.exp(s[:, cols] - m)
                    p_ref[head, rows, cols] = pg.astype(p_ref.dtype)
                    lsum = pg if lsum is None else lsum + pg
                if diagonal:
                    l_ref[head, rows, :] = lsum
                elif exact:
                    l_ref[head, rows, :] = l_ref[head, rows, :] * alpha + lsum
                    acc_ref[head, rows, :] = acc_ref[head, rows, :] * alpha
                else:
                    l_ref[head, rows, :] = l_ref[head, rows, :] + lsum
            pv = _dot(p_ref[head], vt)
            if diagonal:
                acc_ref[head] = pv
            else:
                acc_ref[head] = acc_ref[head] + pv

    def sweep(exact):
        tile(qi, True, exact)

        def step(t, carry):
            tile(qi - 1 - t, False, exact)
            return carry

        lax.fori_loop(0, n_visit, step, 0)
        outs, check = [], None
        for head in range(2):
            acc = acc_ref[head]
            l = jnp.sum(l_ref[head], axis=-1, keepdims=True)
            outs.append(acc / l)
            c = jnp.sum(jnp.abs(acc), axis=-1, keepdims=True) + l
            check = c if check is None else check + c
        o_ref[0] = jnp.where(lane1 < HEAD_DIM, outs[0], outs[1]).astype(o_ref.dtype)
        return jnp.max(jnp.where(check <= F32_BIG, 0.0, 1.0))

    overflowed = sweep(exact=False)

    @pl.when(overflowed > 0.0)
    def _():
        sweep(exact=True)


def _fox(stats, qf, kf, vf, tq):
    b, s, _ = qf.shape
    smem = pl.BlockSpec(memory_space=pltpu.SMEM)
    return pl.pallas_call(
        functools.partial(_fox_body, tq=tq),
        grid=(b, N_PAIR, s // tq),
        in_specs=[smem, smem, smem, smem,
                  pl.BlockSpec((1, tq, 2 * LANES), lambda i, p, j: (i, j, p)),
                  pl.BlockSpec((1, s, 2 * LANES), lambda i, p, j: (i, 0, p)),
                  pl.BlockSpec((1, s, LANES), lambda i, p, j: (i, 0, p))],
        out_specs=pl.BlockSpec((1, tq, LANES), lambda i, p, j: (i, j, p)),
        out_shape=jax.ShapeDtypeStruct((b, s, W_ATT), jnp.bfloat16),
        scratch_shapes=[pltpu.VMEM((2, tq, tq), jnp.float32),
                        pltpu.VMEM((2, tq, tq), jnp.bfloat16),
                        pltpu.VMEM((2, tq, LANES), jnp.float32),
                        pltpu.VMEM((2, tq, LANES), jnp.float32),
                        pltpu.VMEM((2, tq, LANES), jnp.float32)],
        compiler_params=pltpu.CompilerParams(
            dimension_semantics=("arbitrary", "arbitrary", "arbitrary"), vmem_limit_bytes=VMEM_LIMIT),
        name="fox_attention",
    )(*stats, qf, kf, vf)


def _band_bias_body(g_ref, o_ref):
    width = g_ref.shape[-1]
    base = jnp.broadcast_to(g_ref[0], (Q_BLOCK, width))
    toeplitz = pltpu.roll(base, 0, 1, stride=1, stride_axis=0)[:, :BAND]
    qrow = lax.broadcasted_iota(jnp.int32, (Q_BLOCK, BAND), 0)
    kcol = lax.broadcasted_iota(jnp.int32, (Q_BLOCK, BAND), 1)
    cq = qrow // CHUNK
    ck = kcol // CHUNK - LEFT_CHUNKS
    valid = (ck <= cq) & (ck >= cq - LEFT_CHUNKS)
    o_ref[0] = jnp.where(valid, toeplitz, NEG)


def _band_bias(g_ext):
    h, _, width = g_ext.shape
    return pl.pallas_call(
        _band_bias_body,
        grid=(h,),
        in_specs=[pl.BlockSpec((1, 1, width), lambda i: (i, 0, 0))],
        out_specs=pl.BlockSpec((1, Q_BLOCK, BAND), lambda i: (i, 0, 0)),
        out_shape=jax.ShapeDtypeStruct((h, Q_BLOCK, BAND), jnp.float32),
        name="band_bias",
    )(g_ext)


def _chunk_body(q_ref, k_ref, v_ref, bias_ref, o_ref, kpad_ref, vpad_ref, s_ref, p_ref, l_ref, *, tq):
    qi = pl.program_id(2)
    s_len = k_ref.shape[1]

    @pl.when(qi == 0)
    def _():
        zeros = jnp.zeros((PAD, LANES), kpad_ref.dtype)
        kpad_ref[:PAD, :] = zeros
        vpad_ref[:PAD, :] = zeros
        kpad_ref[PAD:PAD + s_len, :] = k_ref[0]
        vpad_ref[PAD:PAD + s_len, :] = v_ref[0]

    head_a = lax.broadcasted_iota(jnp.int32, (1, LANES), 1) < HEAD_DIM
    kcol = lax.broadcasted_iota(jnp.int32, (1, BAND), 1)

    def offsets(blk):
        r0 = pl.multiple_of(blk * Q_BLOCK, Q_BLOCK)
        return r0, pl.multiple_of(qi * tq + r0, Q_BLOCK)

    def scores(slot, blk):
        r0, p0 = offsets(blk)
        q2 = q_ref[0, pl.ds(r0, Q_BLOCK), :]
        zero = jnp.zeros_like(q2)
        qs = jnp.concatenate([jnp.where(head_a, q2, zero), jnp.where(head_a, zero, q2)], axis=0)
        s_ref[slot] = _dot_nt(qs, kpad_ref[pl.ds(p0, BAND), :])

    def softmax(slot, blk, masked):
        _, p0 = offsets(blk)
        for r in range(0, 2 * Q_BLOCK, CHUNK_ROWS):
            rows = slice(r, r + CHUNK_ROWS)
            s = s_ref[slot, rows, :] + bias_ref[0, rows, :]
            if masked:
                s = jnp.where(kcol + p0 >= PAD, s, NEG)
            p = jnp.exp(s - jnp.max(s, axis=-1, keepdims=True))
            l_ref[slot, rows, :] = jnp.broadcast_to(jnp.sum(p, axis=-1, keepdims=True), (CHUNK_ROWS, LANES))
            p_ref[slot, rows, :] = p.astype(p_ref.dtype)

    def values(slot, blk):
        r0, p0 = offsets(blk)
        o = _dot(p_ref[slot], vpad_ref[pl.ds(p0, BAND), :]) / l_ref[slot]
        o_ref[0, pl.ds(r0, Q_BLOCK), :] = jnp.where(head_a, o[:Q_BLOCK], o[Q_BLOCK:]).astype(o_ref.dtype)

    def group(i, masked):
        first = i * CHUNK_GROUP
        scores(0, first)
        for n in range(CHUNK_GROUP):
            if n + 1 < CHUNK_GROUP:
                scores((n + 1) % 2, first + n + 1)
            softmax(n % 2, first + n, masked)
            values(n % 2, first + n)

    def masked_group(i, carry):
        group(i, True)
        return carry

    def plain_group(i, carry):
        group(i, False)
        return carry

    n_groups = tq // (Q_BLOCK * CHUNK_GROUP)
    n_masked = jnp.clip((PAD - qi * tq) // (Q_BLOCK * CHUNK_GROUP), 0, n_groups)
    lax.fori_loop(0, n_masked, masked_group, 0)
    lax.fori_loop(n_masked, n_groups, plain_group, 0)


def _chunk_attention(qc, kc, vc, bias, tq):
    b, s, _ = qc.shape
    assert PAD % (CHUNK_GROUP * Q_BLOCK) == 0 and tq % (CHUNK_GROUP * Q_BLOCK) == 0
    return pl.pallas_call(
        functools.partial(_chunk_body, tq=tq),
        grid=(b, N_PAIR, s // tq),
        in_specs=[pl.BlockSpec((1, tq, LANES), lambda i, p, j: (i, j, p)),
                  pl.BlockSpec((1, s, LANES), lambda i, p, j: (i, 0, p)),
                  pl.BlockSpec((1, s, LANES), lambda i, p, j: (i, 0, p)),
                  pl.BlockSpec((1, 2 * Q_BLOCK, BAND), lambda i, p, j: (p, 0, 0))],
        out_specs=pl.BlockSpec((1, tq, LANES), lambda i, p, j: (i, j, p)),
        out_shape=jax.ShapeDtypeStruct((b, s, W_ATT), jnp.bfloat16),
        scratch_shapes=[pltpu.VMEM((PAD + s, LANES), jnp.bfloat16),
                        pltpu.VMEM((PAD + s, LANES), jnp.bfloat16),
                        pltpu.VMEM((2, 2 * Q_BLOCK, BAND), jnp.float32),
                        pltpu.VMEM((2, 2 * Q_BLOCK, BAND), jnp.bfloat16),
                        pltpu.VMEM((2, 2 * Q_BLOCK, LANES), jnp.float32)],
        compiler_params=pltpu.CompilerParams(
            dimension_semantics=("arbitrary", "arbitrary", "arbitrary"), vmem_limit_bytes=VMEM_LIMIT),
        name="chunk_attention",
    )(qc, kc, vc, bias)


def _out_ffn_body(x_ref, oa_ref, ob_ref, gate_ref, wa_ref, wb_ref, wo_ref, g2_ref, wu_ref, wd_ref, gf_ref,
                  o_ref, *, ff_chunk, final):
    d = x_ref.shape[-1]
    ya = _dot(oa_ref[0], wa_ref[...])
    yb = _dot(ob_ref[0], wb_ref[...])
    ga = gate_ref[0, :, :d].astype(jnp.float32)
    gb = gate_ref[0, :, d:].astype(jnp.float32)
    merged = (ga * ya + gb * yb).astype(jnp.bfloat16)
    x1 = x_ref[0] + _dot(merged, wo_ref[...])
    h2 = _rms(x1, g2_ref[...]).astype(jnp.bfloat16)
    acc = x1
    for c0 in range(0, wu_ref.shape[1], ff_chunk):
        u = jnp.maximum(_dot(h2, wu_ref[:, c0:c0 + ff_chunk]), 0.0)
        acc = acc + _dot((u * u).astype(jnp.bfloat16), wd_ref[c0:c0 + ff_chunk, :])
    if final:
        acc = _rms(acc, gf_ref[...])
    o_ref[0] = acc


def _out_ffn(x, oa, ob, gates, wa, wb, wo, g2, wu, wd, gf, tm, final):
    b, s, d = x.shape
    tok = lambda w: pl.BlockSpec((1, tm, w), lambda i, j: (i, j, 0))
    return pl.pallas_call(
        functools.partial(_out_ffn_body, ff_chunk=min(1024, wu.shape[1]), final=final),
        grid=(b, s // tm),
        in_specs=[tok(d), tok(W_ATT), tok(W_ATT), tok(2 * d)]
                 + [_const_spec(a.shape) for a in (wa, wb, wo, g2, wu, wd, gf)],
        out_specs=tok(d),
        out_shape=jax.ShapeDtypeStruct((b, s, d), jnp.float32),
        compiler_params=pltpu.CompilerParams(
            dimension_semantics=("arbitrary", "arbitrary"), vmem_limit_bytes=VMEM_LIMIT),
        name="out_ffn",
    )(x, oa, ob, gates, wa, wb, wo, g2, wu, wd, gf)


def _aug_placement():
    eq = [[0.0] * W_ATT for _ in range(LANES)]
    ek = [[0.0] * W_ATT for _ in range(LANES)]
    one = 3 * H_FOX
    for h in range(H_FOX):
        for piece in range(3):
            eq[piece * H_FOX + h][HEAD_DIM * h + piece] = 1.0
            eq[one][HEAD_DIM * h + 3 + piece] = 1.0
            ek[one][HEAD_DIM * h + piece] = 1.0
            ek[piece * H_FOX + h][HEAD_DIM * h + 3 + piece] = -1.0
    return jnp.array(eq, jnp.bfloat16), jnp.array(ek, jnp.bfloat16)


def kernel(x, norm1, w_in, forget_bias, rel_bias, w_branch_a, w_branch_b, w_out, norm2, w_up, w_down, final_norm):
    b, s, d = x.shape
    depth = w_in.shape[0]
    bf16 = jnp.bfloat16
    tm = min(512, s)
    tqc = min(1024, s)
    eq, ek = _aug_placement()
    head_of_col = lax.broadcasted_iota(jnp.int32, (W_ATT, LANES), 0) // HEAD_DIM
    hsum = (head_of_col == lax.broadcasted_iota(jnp.int32, (W_ATT, LANES), 1)).astype(bf16)
    gf = final_norm.reshape(1, d)
    o = 3 * W_ATT + H_FOX
    for l in range(depth):
        w = w_in[l]
        wq = (w[:, :W_ATT] * SCALE).astype(bf16)
        wk = w[:, W_ATT:2 * W_ATT].astype(bf16)
        wv = w[:, 2 * W_ATT:3 * W_ATT].astype(bf16)
        wf = jnp.pad(w[:, 3 * W_ATT:o], ((0, 0), (0, LANES - H_FOX))).astype(bf16)
        wc = jnp.concatenate([w[:, o:o + W_ATT] * SCALE, w[:, o + W_ATT:o + 3 * W_ATT]], axis=1).astype(bf16)
        wg = w[:, o + 3 * W_ATT:].astype(bf16)
        bf = jnp.pad(forget_bias[l], (0, LANES - H_FOX)).reshape(1, LANES)
        qf, kf, vf, qc, kc, vc, gates, stat = _in_proj(
            x, norm1[l].reshape(1, d), wq, wk, wv, wc, wg, wf, bf, eq, ek, hsum, tm)

        stats = stat[:, :, :4, :H_FOX].transpose(2, 0, 3, 1).reshape(4, -1)
        o_a = _fox(tuple(stats), qf, kf, vf, tm)

        far = rel_bias[l][:, 2 * MAX_REL:]
        g_ext = jnp.concatenate([jnp.broadcast_to(far, (H_CHK, PAD - MAX_REL + 1)),
                                 rel_bias[l][:, 2 * MAX_REL - 1:0:-1],
                                 jnp.broadcast_to(far, (H_CHK, Q_BLOCK))], axis=1)
        bias = _band_bias(g_ext.reshape(H_CHK, 1, -1)).reshape(N_PAIR, 2 * Q_BLOCK, BAND)
        o_b = _chunk_attention(qc, kc, vc, bias, tqc)

        x = _out_ffn(x, o_a, o_b, gates, w_branch_a[l].astype(bf16), w_branch_b[l].astype(bf16),
                     w_out[l].astype(bf16), norm2[l].reshape(1, d), w_up[l].astype(bf16),
                     w_down[l].astype(bf16), gf, tm, final=(l == depth - 1))
    return x
```

```python
import functools
import math

import jax
import jax.numpy as jnp
from jax import lax
from jax.experimental import pallas as pl
from jax.experimental.pallas import tpu as pltpu

HEAD_DIM = 64
H_FOX = 8
H_CHK = 8
N_PAIR = 4
W_ATT = H_FOX * HEAD_DIM
CHUNK = 64
Q_BLOCK = 128
LEFT_CHUNKS = 8
PAD = LEFT_CHUNKS * CHUNK
BAND = PAD + Q_BLOCK
MAX_REL = 128
EPS = 1e-6
NEG = -1e30
LANES = 128
SCALE = 1.0 / math.sqrt(HEAD_DIM)
F32_BIG = 3.0e38
FOX_ROWS = 32
CHUNK_ROWS = 32
CHUNK_GROUP = 4
SKIP_MARGIN = 30.0
NORM_SLACK = 1.02
VMEM_LIMIT = 60 * 1024 * 1024

_NT = (((1,), (1,)), ((), ()))


def _dot(a, b):
    return jnp.dot(a, b, preferred_element_type=jnp.float32)


def _dot_nt(a, b):
    return lax.dot_general(a, b, _NT, preferred_element_type=jnp.float32)


def _rms(x, g):
    ms = jnp.mean(x * x, axis=-1, keepdims=True)
    return x * lax.rsqrt(ms + EPS) * g


def _const_spec(shape):
    nd = len(shape)
    return pl.BlockSpec(shape, lambda *_: (0,) * nd, pipeline_mode=pl.Buffered(1))


def _in_proj_body(x_ref, g_ref, wq_ref, wk_ref, wv_ref, wc_ref, wg_ref, wf_ref, bf_ref, eq_ref, ek_ref, hsum_ref,
                  qf_ref, kf_ref, vf_ref, qc_ref, kc_ref, vc_ref, gate_ref, stat_ref, carry_ref):
    tm = x_ref.shape[1]

    @pl.when(pl.program_id(1) == 0)
    def _():
        carry_ref[...] = jnp.zeros_like(carry_ref)

    h = _rms(x_ref[0], g_ref[...]).astype(jnp.bfloat16)

    lane = lax.broadcasted_iota(jnp.int32, (tm, LANES), 1)
    row = lax.broadcasted_iota(jnp.int32, (tm, LANES), 0)
    logf = jax.nn.log_sigmoid(_dot(h, wf_ref[...]) + bf_ref[...])
    c = jnp.where(lane < H_FOX, logf, 0.0)
    k = 1
    while k < tm:
        c = c + jnp.where(row >= k, pltpu.roll(c, k, 0), 0.0)
        k *= 2
    c = c + carry_ref[...]
    carry_ref[...] = c[tm - 1:tm, :]

    hi = c.astype(jnp.bfloat16).astype(jnp.float32)
    r1 = c - hi
    mid = r1.astype(jnp.bfloat16).astype(jnp.float32)
    lo = (r1 - mid).astype(jnp.bfloat16).astype(jnp.float32)
    pieces = hi + pltpu.roll(mid, H_FOX, 1) + pltpu.roll(lo, 2 * H_FOX, 1)
    pieces = jnp.where(lane == 3 * H_FOX, 1.0, pieces).astype(jnp.bfloat16)
    aug_q = _dot(pieces, eq_ref[...]).astype(jnp.bfloat16)
    aug_k = _dot(pieces, ek_ref[...]).astype(jnp.bfloat16)

    q = _dot(h, wq_ref[...]).astype(jnp.bfloat16)
    kk = _dot(h, wk_ref[...]).astype(jnp.bfloat16)
    for p in range(N_PAIR):
        src = slice(p * LANES, (p + 1) * LANES)
        qf_ref[0, :, 2 * p * LANES:(2 * p + 1) * LANES] = q[:, src]
        qf_ref[0, :, (2 * p + 1) * LANES:(2 * p + 2) * LANES] = aug_q[:, src]
        kf_ref[0, :, 2 * p * LANES:(2 * p + 1) * LANES] = kk[:, src]
        kf_ref[0, :, (2 * p + 1) * LANES:(2 * p + 2) * LANES] = aug_k[:, src]
    vf_ref[0] = _dot(h, wv_ref[...]).astype(jnp.bfloat16)

    def max_norm(t):
        t = t.astype(jnp.float32)
        sq = _dot((t * t).astype(jnp.bfloat16), hsum_ref[...])
        return jnp.sqrt(jnp.max(sq, axis=0, keepdims=True))

    sub = lax.broadcasted_iota(jnp.int32, (8, LANES), 0)
    stat = jnp.where(sub == 0, c[0:1, :], 0.0)
    stat = jnp.where(sub == 1, c[tm - 1:tm, :], stat)
    stat = jnp.where(sub == 2, max_norm(q), stat)
    stat_ref[0, 0] = jnp.where(sub == 3, max_norm(kk), stat)

    pc = _dot(h, wc_ref[...])
    qc_ref[0] = pc[:, :W_ATT].astype(jnp.bfloat16)
    kc_ref[0] = pc[:, W_ATT:2 * W_ATT].astype(jnp.bfloat16)
    vc_ref[0] = pc[:, 2 * W_ATT:].astype(jnp.bfloat16)
    gate_ref[0] = jax.nn.sigmoid(_dot(h, wg_ref[...])).astype(jnp.bfloat16)


def _in_proj(x, g, wq, wk, wv, wc, wg, wf, bf, eq, ek, hsum, tm):
    b, s, d = x.shape
    tok = lambda w: pl.BlockSpec((1, tm, w), lambda i, j: (i, j, 0))
    bf16 = jnp.bfloat16
    out_shape = [jax.ShapeDtypeStruct((b, s, w), bf16)
                 for w in (2 * W_ATT, 2 * W_ATT, W_ATT, W_ATT, W_ATT, W_ATT, 2 * d)]
    out_shape.append(jax.ShapeDtypeStruct((b, s // tm, 8, LANES), jnp.float32))
    consts = (g, wq, wk, wv, wc, wg, wf, bf, eq, ek, hsum)
    return pl.pallas_call(
        _in_proj_body,
        grid=(b, s // tm),
        in_specs=[tok(d)] + [_const_spec(a.shape) for a in consts],
        out_specs=[tok(sh.shape[-1]) for sh in out_shape[:-1]]
                  + [pl.BlockSpec((1, 1, 8, LANES), lambda i, j: (i, j, 0, 0))],
        out_shape=out_shape,
        scratch_shapes=[pltpu.VMEM((1, LANES), jnp.float32)],
        compiler_params=pltpu.CompilerParams(
            dimension_semantics=("arbitrary", "arbitrary"), vmem_limit_bytes=VMEM_LIMIT),
        name="in_proj",
    )(x, *consts)


def _fox_body(cfirst_ref, clast_ref, qnorm_ref, knorm_ref, q_ref, k_ref, v_ref, o_ref,
              s_ref, p_ref, m_ref, l_ref, acc_ref, bad_ref, *, tq):
    n_tiles = q_ref.shape[1] // tq
    lane2 = lax.broadcasted_iota(jnp.int32, (1, 2 * LANES), 1) % LANES
    head_a = lax.broadcasted_iota(jnp.int32, (1, LANES), 1) < HEAD_DIM

    def reach(qi, head):
        base = (pl.program_id(0) * H_FOX + 2 * pl.program_id(1) + head) * n_tiles
        kmax = lax.fori_loop(0, qi + 1, lambda t, mx: jnp.maximum(mx, knorm_ref[base + t]), 0.0)
        bound = cfirst_ref[base + qi] + 2.0 * NORM_SLACK * qnorm_ref[base + qi] * kmax + SKIP_MARGIN
        return lax.fori_loop(
            0, qi, lambda t, n: n + jnp.where(bound - clast_ref[base + t] > 0.0, 1, 0), 0)

    def query_tile(qi, exact):
        q_rows = pl.ds(pl.multiple_of(qi * tq, tq), tq)
        qcat = q_ref[0, q_rows, :]
        zero = jnp.zeros_like(qcat)
        qh = (jnp.where(lane2 < HEAD_DIM, qcat, zero), jnp.where(lane2 >= HEAD_DIM, qcat, zero))
        n_visit = jnp.maximum(reach(qi, 0), reach(qi, 1))
        tile = functools.partial(key_tile, qh, exact=exact)
        tile(qi, diagonal=True)

        def step(t, carry):
            tile(qi - 1 - t, diagonal=False)
            return carry

        lax.fori_loop(0, n_visit, step, 0)
        outs = []
        for head in range(2):
            acc, lp = acc_ref[head], l_ref[head]
            outs.append(acc / jnp.sum(lp, axis=-1, keepdims=True))
            if not exact:
                finite = (jnp.abs(acc) <= F32_BIG) & (lp <= F32_BIG)
                bad_ref[...] = jnp.maximum(bad_ref[...], jnp.where(finite, 0.0, 1.0))
        o_ref[0, q_rows, :] = jnp.where(head_a, outs[0], outs[1]).astype(o_ref.dtype)

    def key_tile(qh, j, *, diagonal, exact):
        start = pl.multiple_of(j * tq, tq)
        kt = k_ref[0, pl.ds(start, tq), :]
        vt = v_ref[0, pl.ds(start, tq), :]
        for head in range(2):
            s_ref[head] = _dot_nt(qh[head], kt)
        for head in range(2):
            for r in range(0, tq, FOX_ROWS):
                rows = slice(r, r + FOX_ROWS)
                s = s_ref[head, rows, :]
                alpha = None
                if diagonal:
                    row = lax.broadcasted_iota(jnp.int32, (FOX_ROWS, tq), 0) + r
                    col = lax.broadcasted_iota(jnp.int32, (FOX_ROWS, tq), 1)
                    s = jnp.where(col <= row, s, NEG)
                    m = jnp.broadcast_to(jnp.max(s, axis=-1, keepdims=True), (FOX_ROWS, LANES))
                    m_ref[head, rows, :] = m
                elif exact:
                    m_old = m_ref[head, rows, :]
                    m = jnp.maximum(m_old, jnp.max(s, axis=-1, keepdims=True))
                    alpha = jnp.exp(m_old - m)
                    m_ref[head, rows, :] = m
                else:
                    m = m_ref[head, rows, :]
                lsum = None
                for g in range(tq // LANES):
                    cols = slice(g * LANES, (g + 1) * LANES)
                    pg = jnp.exp(s[:, cols] - m)
                    p_ref[head, rows, cols] = pg.astype(p_ref.dtype)
                    lsum = pg if lsum is None else lsum + pg
                if diagonal:
                    l_ref[head, rows, :] = lsum
                elif exact:
                    l_ref[head, rows, :] = l_ref[head, rows, :] * alpha + lsum
                    acc_ref[head, rows, :] = acc_ref[head, rows, :] * alpha
                else:
                    l_ref[head, rows, :] = l_ref[head, rows, :] + lsum
            pv = _dot(p_ref[head], vt)
            if diagonal:
                acc_ref[head] = pv
            else:
                acc_ref[head] = acc_ref[head] + pv

    def sweep(exact):
        def body(qi, carry):
            query_tile(qi, exact)
            return carry

        lax.fori_loop(0, n_tiles, body, 0)

    bad_ref[...] = jnp.zeros_like(bad_ref)
    sweep(exact=False)

    @pl.when(jnp.max(bad_ref[...]) > 0.0)
    def _():
        sweep(exact=True)


def _fox(stats, qf, kf, vf, tq):
    b, s, _ = qf.shape
    smem = pl.BlockSpec(memory_space=pltpu.SMEM)
    return pl.pallas_call(
        functools.partial(_fox_body, tq=tq),
        grid=(b, N_PAIR),
        in_specs=[smem, smem, smem, smem,
                  pl.BlockSpec((1, s, 2 * LANES), lambda i, p: (i, 0, p)),
                  pl.BlockSpec((1, s, 2 * LANES), lambda i, p: (i, 0, p)),
                  pl.BlockSpec((1, s, LANES), lambda i, p: (i, 0, p))],
        out_specs=pl.BlockSpec((1, s, LANES), lambda i, p: (i, 0, p)),
        out_shape=jax.ShapeDtypeStruct((b, s, W_ATT), jnp.bfloat16),
        scratch_shapes=[pltpu.VMEM((2, tq, tq), jnp.float32),
                        pltpu.VMEM((2, tq, tq), jnp.bfloat16),
                        pltpu.VMEM((2, tq, LANES), jnp.float32),
                        pltpu.VMEM((2, tq, LANES), jnp.float32),
                        pltpu.VMEM((2, tq, LANES), jnp.float32),
                        pltpu.VMEM((tq, LANES), jnp.float32)],
        compiler_params=pltpu.CompilerParams(
            dimension_semantics=("arbitrary", "arbitrary"), vmem_limit_bytes=VMEM_LIMIT),
        name="fox_attention",
    )(*stats, qf, kf, vf)


def _band_bias_body(g_ref, o_ref):
    width = g_ref.shape[-1]
    base = jnp.broadcast_to(g_ref[0], (Q_BLOCK, width))
    toeplitz = pltpu.roll(base, 0, 1, stride=1, stride_axis=0)[:, :BAND]
    qrow = lax.broadcasted_iota(jnp.int32, (Q_BLOCK, BAND), 0)
    kcol = lax.broadcasted_iota(jnp.int32, (Q_BLOCK, BAND), 1)
    cq = qrow // CHUNK
    ck = kcol // CHUNK - LEFT_CHUNKS
    valid = (ck <= cq) & (ck >= cq - LEFT_CHUNKS)
    o_ref[0] = jnp.where(valid, toeplitz, NEG)


def _band_bias(g_ext):
    h, _, width = g_ext.shape
    return pl.pallas_call(
        _band_bias_body,
        grid=(h,),
        in_specs=[pl.BlockSpec((1, 1, width), lambda i: (i, 0, 0))],
        out_specs=pl.BlockSpec((1, Q_BLOCK, BAND), lambda i: (i, 0, 0)),
        out_shape=jax.ShapeDtypeStruct((h, Q_BLOCK, BAND), jnp.float32),
        name="band_bias",
    )(g_ext)


def _chunk_body(q_ref, k_ref, v_ref, bias_ref, o_ref, kpad_ref, vpad_ref, s_ref, p_ref, l_ref, *, tq):
    qi = pl.program_id(2)
    s_len = k_ref.shape[1]

    @pl.when(qi == 0)
    def _():
        zeros = jnp.zeros((PAD, LANES), kpad_ref.dtype)
        kpad_ref[:PAD, :] = zeros
        vpad_ref[:PAD, :] = zeros
        kpad_ref[PAD:PAD + s_len, :] = k_ref[0]
        vpad_ref[PAD:PAD + s_len, :] = v_ref[0]

    head_a = lax.broadcasted_iota(jnp.int32, (1, LANES), 1) < HEAD_DIM
    kcol = lax.broadcasted_iota(jnp.int32, (1, BAND), 1)

    def offsets(blk):
        r0 = pl.multiple_of(blk * Q_BLOCK, Q_BLOCK)
        return r0, pl.multiple_of(qi * tq + r0, Q_BLOCK)

    def scores(slot, blk):
        r0, p0 = offsets(blk)
        q2 = q_ref[0, pl.ds(r0, Q_BLOCK), :]
        zero = jnp.zeros_like(q2)
        qs = jnp.concatenate([jnp.where(head_a, q2, zero), jnp.where(head_a, zero, q2)], axis=0)
        s_ref[slot] = _dot_nt(qs, kpad_ref[pl.ds(p0, BAND), :])

    def softmax(slot, blk, masked):
        _, p0 = offsets(blk)
        for r in range(0, 2 * Q_BLOCK, CHUNK_ROWS):
            rows = slice(r, r + CHUNK_ROWS)
            s = s_ref[slot, rows, :] + bias_ref[0, rows, :]
            if masked:
                s = jnp.where(kcol + p0 >= PAD, s, NEG)
            p = jnp.exp(s - jnp.max(s, axis=-1, keepdims=True))
            l_ref[slot, rows, :] = jnp.broadcast_to(jnp.sum(p, axis=-1, keepdims=True), (CHUNK_ROWS, LANES))
            p_ref[slot, rows, :] = p.astype(p_ref.dtype)

    def values(slot, blk):
        r0, p0 = offsets(blk)
        o = _dot(p_ref[slot], vpad_ref[pl.ds(p0, BAND), :]) / l_ref[slot]
        o_ref[0, pl.ds(r0, Q_BLOCK), :] = jnp.where(head_a, o[:Q_BLOCK], o[Q_BLOCK:]).astype(o_ref.dtype)

    def group(i, masked):
        first = i * CHUNK_GROUP
        scores(0, first)
        for n in range(CHUNK_GROUP):
            if n + 1 < CHUNK_GROUP:
                scores((n + 1) % 2, first + n + 1)
            softmax(n % 2, first + n, masked)
            values(n % 2, first + n)

    def masked_group(i, carry):
        group(i, True)
        return carry

    def plain_group(i, carry):
        group(i, False)
        return carry

    n_groups = tq // (Q_BLOCK * CHUNK_GROUP)
    n_masked = jnp.clip((PAD - qi * tq) // (Q_BLOCK * CHUNK_GROUP), 0, n_groups)
    lax.fori_loop(0, n_masked, masked_group, 0)
    lax.fori_loop(n_masked, n_groups, plain_group, 0)


def _chunk_attention(qc, kc, vc, bias, tq):
    b, s, _ = qc.shape
    assert PAD % (CHUNK_GROUP * Q_BLOCK) == 0 and tq % (CHUNK_GROUP * Q_BLOCK) == 0
    return pl.pallas_call(
        functools.partial(_chunk_body, tq=tq),
        grid=(b, N_PAIR, s // tq),
        in_specs=[pl.BlockSpec((1, tq, LANES), lambda i, p, j: (i, j, p)),
                  pl.BlockSpec((1, s, LANES), lambda i, p, j: (i, 0, p)),
                  pl.BlockSpec((1, s, LANES), lambda i, p, j: (i, 0, p)),
                  pl.BlockSpec((1, 2 * Q_BLOCK, BAND), lambda i, p, j: (p, 0, 0))],
        out_specs=pl.BlockSpec((1, tq, LANES), lambda i, p, j: (i, j, p)),
        out_shape=jax.ShapeDtypeStruct((b, s, W_ATT), jnp.bfloat16),
        scratch_shapes=[pltpu.VMEM((PAD + s, LANES), jnp.bfloat16),
                        pltpu.VMEM((PAD + s, LANES), jnp.bfloat16),
                        pltpu.VMEM((2, 2 * Q_BLOCK, BAND), jnp.float32),
                        pltpu.VMEM((2, 2 * Q_BLOCK, BAND), jnp.bfloat16),
                        pltpu.VMEM((2, 2 * Q_BLOCK, LANES), jnp.float32)],
        compiler_params=pltpu.CompilerParams(
            dimension_semantics=("arbitrary", "arbitrary", "arbitrary"), vmem_limit_bytes=VMEM_LIMIT),
        name="chunk_attention",
    )(qc, kc, vc, bias)


def _out_ffn_body(x_ref, oa_ref, ob_ref, gate_ref, wa_ref, wb_ref, wo_ref, g2_ref, wu_ref, wd_ref, gf_ref,
                  o_ref, *, ff_chunk, final):
    d = x_ref.shape[-1]
    ya = _dot(oa_ref[0], wa_ref[...])
    yb = _dot(ob_ref[0], wb_ref[...])
    ga = gate_ref[0, :, :d].astype(jnp.float32)
    gb = gate_ref[0, :, d:].astype(jnp.float32)
    merged = (ga * ya + gb * yb).astype(jnp.bfloat16)
    x1 = x_ref[0] + _dot(merged, wo_ref[...])
    h2 = _rms(x1, g2_ref[...]).astype(jnp.bfloat16)
    acc = x1
    for c0 in range(0, wu_ref.shape[1], ff_chunk):
        u = jnp.maximum(_dot(h2, wu_ref[:, c0:c0 + ff_chunk]), 0.0)
        acc = acc + _dot((u * u).astype(jnp.bfloat16), wd_ref[c0:c0 + ff_chunk, :])
    if final:
        acc = _rms(acc, gf_ref[...])
    o_ref[0] = acc


def _out_ffn(x, oa, ob, gates, wa, wb, wo, g2, wu, wd, gf, tm, final):
    b, s, d = x.shape
    tok = lambda w: pl.BlockSpec((1, tm, w), lambda i, j: (i, j, 0))
    return pl.pallas_call(
        functools.partial(_out_ffn_body, ff_chunk=min(1024, wu.shape[1]), final=final),
        grid=(b, s // tm),
        in_specs=[tok(d), tok(W_ATT), tok(W_ATT), tok(2 * d)]
                 + [_const_spec(a.shape) for a in (wa, wb, wo, g2, wu, wd, gf)],
        out_specs=tok(d),
        out_shape=jax.ShapeDtypeStruct((b, s, d), jnp.float32),
        compiler_params=pltpu.CompilerParams(
            dimension_semantics=("arbitrary", "arbitrary"), vmem_limit_bytes=VMEM_LIMIT),
        name="out_ffn",
    )(x, oa, ob, gates, wa, wb, wo, g2, wu, wd, gf)


def _aug_placement():
    eq = [[0.0] * W_ATT for _ in range(LANES)]
    ek = [[0.0] * W_ATT for _ in range(LANES)]
    one = 3 * H_FOX
    for h in range(H_FOX):
        for piece in range(3):
            eq[piece * H_FOX + h][HEAD_DIM * h + piece] = 1.0
            eq[one][HEAD_DIM * h + 3 + piece] = 1.0
            ek[one][HEAD_DIM * h + piece] = 1.0
            ek[piece * H_FOX + h][HEAD_DIM * h + 3 + piece] = -1.0
    return jnp.array(eq, jnp.bfloat16), jnp.array(ek, jnp.bfloat16)


def kernel(x, norm1, w_in, forget_bias, rel_bias, w_branch_a, w_branch_b, w_out, norm2, w_up, w_down, final_norm):
    b, s, d = x.shape
    depth = w_in.shape[0]
    bf16 = jnp.bfloat16
    tm = min(512, s)
    tqc = min(1024, s)
    eq, ek = _aug_placement()
    head_of_col = lax.broadcasted_iota(jnp.int32, (W_ATT, LANES), 0) // HEAD_DIM
    hsum = (head_of_col == lax.broadcasted_iota(jnp.int32, (W_ATT, LANES), 1)).astype(bf16)
    gf = final_norm.reshape(1, d)
    o = 3 * W_ATT + H_FOX
    for l in range(depth):
        w = w_in[l]
        wq = (w[:, :W_ATT] * SCALE).astype(bf16)
        wk = w[:, W_ATT:2 * W_ATT].astype(bf16)
        wv = w[:, 2 * W_ATT:3 * W_ATT].astype(bf16)
        wf = jnp.pad(w[:, 3 * W_ATT:o], ((0, 0), (0, LANES - H_FOX))).astype(bf16)
        wc = jnp.concatenate([w[:, o:o + W_ATT] * SCALE, w[:, o + W_ATT:o + 3 * W_ATT]], axis=1).astype(bf16)
        wg = w[:, o + 3 * W_ATT:].astype(bf16)
        bf = jnp.pad(forget_bias[l], (0, LANES - H_FOX)).reshape(1, LANES)
        qf, kf, vf, qc, kc, vc, gates, stat = _in_proj(
            x, norm1[l].reshape(1, d), wq, wk, wv, wc, wg, wf, bf, eq, ek, hsum, tm)

        stats = stat[:, :, :4, :H_FOX].transpose(2, 0, 3, 1).reshape(4, -1)
        o_a = _fox(tuple(stats), qf, kf, vf, tm)

        far = rel_bias[l][:, 2 * MAX_REL:]
        g_ext = jnp.concatenate([jnp.broadcast_to(far, (H_CHK, PAD - MAX_REL + 1)),
                                 rel_bias[l][:, 2 * MAX_REL - 1:0:-1],
                                 jnp.broadcast_to(far, (H_CHK, Q_BLOCK))], axis=1)
        bias = _band_bias(g_ext.reshape(H_CHK, 1, -1)).reshape(N_PAIR, 2 * Q_BLOCK, BAND)
        o_b = _chunk_attention(qc, kc, vc, bias, tqc)

        x = _out_ffn(x, o_a, o_b, gates, w_branch_a[l].astype(bf16), w_branch_b[l].astype(bf16),
                     w_out[l].astype(bf16), norm2[l].reshape(1, d), w_up[l].astype(bf16),
                     w_down[l].astype(bf16), gf, tm, final=(l == depth - 1))
    return x
```

```python
import functools
import math

import jax
import jax.numpy as jnp
from jax import lax
from jax.experimental import pallas as pl
from jax.experimental.pallas import tpu as pltpu

HEAD_DIM = 64
H_FOX = 8
H_CHK = 8
N_PAIR = 4
W_ATT = H_FOX * HEAD_DIM
CHUNK = 64
Q_BLOCK = 128
LEFT_CHUNKS = 8
PAD = LEFT_CHUNKS * CHUNK
BAND = PAD + Q_BLOCK
MAX_REL = 128
EPS = 1e-6
NEG = -1e30
LANES = 128
SCALE = 1.0 / math.sqrt(HEAD_DIM)
F32_BIG = 3.0e38
FOX_ROWS = 32
FOX_SPLIT = 2
CHUNK_ROWS = 32
CHUNK_GROUP = 4
SKIP_MARGIN = 30.0
NORM_SLACK = 1.02
VMEM_LIMIT = 60 * 1024 * 1024

_NT = (((1,), (1,)), ((), ()))


def _dot(a, b):
    return jnp.dot(a, b, preferred_element_type=jnp.float32)


def _dot_nt(a, b):
    return lax.dot_general(a, b, _NT, preferred_element_type=jnp.float32)


def _rms(x, g):
    ms = jnp.mean(x * x, axis=-1, keepdims=True)
    return x * lax.rsqrt(ms + EPS) * g


def _const_spec(shape):
    nd = len(shape)
    return pl.BlockSpec(shape, lambda *_: (0,) * nd, pipeline_mode=pl.Buffered(1))


def _in_proj_body(x_ref, g_ref, wq_ref, wk_ref, wv_ref, wc_ref, wg_ref, wf_ref, bf_ref, eq_ref, ek_ref, hsum_ref,
                  qf_ref, kf_ref, vf_ref, qc_ref, kc_ref, vc_ref, gate_ref, stat_ref, carry_ref):
    tm = x_ref.shape[1]

    @pl.when(pl.program_id(1) == 0)
    def _():
        carry_ref[...] = jnp.zeros_like(carry_ref)

    h = _rms(x_ref[0], g_ref[...]).astype(jnp.bfloat16)

    lane = lax.broadcasted_iota(jnp.int32, (tm, LANES), 1)
    row = lax.broadcasted_iota(jnp.int32, (tm, LANES), 0)
    logf = jax.nn.log_sigmoid(_dot(h, wf_ref[...]) + bf_ref[...])
    c = jnp.where(lane < H_FOX, logf, 0.0)
    k = 1
    while k < tm:
        c = c + jnp.where(row >= k, pltpu.roll(c, k, 0), 0.0)
        k *= 2
    c = c + carry_ref[...]
    carry_ref[...] = c[tm - 1:tm, :]

    hi = c.astype(jnp.bfloat16).astype(jnp.float32)
    r1 = c - hi
    mid = r1.astype(jnp.bfloat16).astype(jnp.float32)
    lo = (r1 - mid).astype(jnp.bfloat16).astype(jnp.float32)
    pieces = hi + pltpu.roll(mid, H_FOX, 1) + pltpu.roll(lo, 2 * H_FOX, 1)
    pieces = jnp.where(lane == 3 * H_FOX, 1.0, pieces).astype(jnp.bfloat16)
    aug_q = _dot(pieces, eq_ref[...]).astype(jnp.bfloat16)
    aug_k = _dot(pieces, ek_ref[...]).astype(jnp.bfloat16)

    q = _dot(h, wq_ref[...]).astype(jnp.bfloat16)
    kk = _dot(h, wk_ref[...]).astype(jnp.bfloat16)
    for p in range(N_PAIR):
        src = slice(p * LANES, (p + 1) * LANES)
        qf_ref[0, :, 2 * p * LANES:(2 * p + 1) * LANES] = q[:, src]
        qf_ref[0, :, (2 * p + 1) * LANES:(2 * p + 2) * LANES] = aug_q[:, src]
        kf_ref[0, :, 2 * p * LANES:(2 * p + 1) * LANES] = kk[:, src]
        kf_ref[0, :, (2 * p + 1) * LANES:(2 * p + 2) * LANES] = aug_k[:, src]
    vf_ref[0] = _dot(h, wv_ref[...]).astype(jnp.bfloat16)

    def max_norm(t):
        t = t.astype(jnp.float32)
        sq = _dot((t * t).astype(jnp.bfloat16), hsum_ref[...])
        return jnp.sqrt(jnp.max(sq, axis=0, keepdims=True))

    sub = lax.broadcasted_iota(jnp.int32, (8, LANES), 0)
    stat = jnp.where(sub == 0, c[0:1, :], 0.0)
    stat = jnp.where(sub == 1, c[tm - 1:tm, :], stat)
    stat = jnp.where(sub == 2, max_norm(q), stat)
    stat_ref[0, 0] = jnp.where(sub == 3, max_norm(kk), stat)

    pc = _dot(h, wc_ref[...])
    qc_ref[0] = pc[:, :W_ATT].astype(jnp.bfloat16)
    kc_ref[0] = pc[:, W_ATT:2 * W_ATT].astype(jnp.bfloat16)
    vc_ref[0] = pc[:, 2 * W_ATT:].astype(jnp.bfloat16)
    gate_ref[0] = jax.nn.sigmoid(_dot(h, wg_ref[...])).astype(jnp.bfloat16)


def _in_proj(x, g, wq, wk, wv, wc, wg, wf, bf, eq, ek, hsum, tm):
    b, s, d = x.shape
    tok = lambda w: pl.BlockSpec((1, tm, w), lambda i, j: (i, j, 0))
    bf16 = jnp.bfloat16
    out_shape = [jax.ShapeDtypeStruct((b, s, w), bf16)
                 for w in (2 * W_ATT, 2 * W_ATT, W_ATT, W_ATT, W_ATT, W_ATT, 2 * d)]
    out_shape.append(jax.ShapeDtypeStruct((b, s // tm, 8, LANES), jnp.float32))
    consts = (g, wq, wk, wv, wc, wg, wf, bf, eq, ek, hsum)
    return pl.pallas_call(
        _in_proj_body,
        grid=(b, s // tm),
        in_specs=[tok(d)] + [_const_spec(a.shape) for a in consts],
        out_specs=[tok(sh.shape[-1]) for sh in out_shape[:-1]]
                  + [pl.BlockSpec((1, 1, 8, LANES), lambda i, j: (i, j, 0, 0))],
        out_shape=out_shape,
        scratch_shapes=[pltpu.VMEM((1, LANES), jnp.float32)],
        compiler_params=pltpu.CompilerParams(
            dimension_semantics=("arbitrary", "arbitrary"), vmem_limit_bytes=VMEM_LIMIT),
        name="in_proj",
    )(x, *consts)


def _fox_body(cfirst_ref, clast_ref, qnorm_ref, knorm_ref, q_ref, k_ref, v_ref, o_ref,
              qs_ref, s_ref, p_ref, m_ref, l_ref, acc_ref, bad_ref, *, tq):
    n_tiles = q_ref.shape[1] // tq
    slab = tq // FOX_SPLIT
    lane2 = lax.broadcasted_iota(jnp.int32, (1, 2 * LANES), 1) % LANES
    head_a = lax.broadcasted_iota(jnp.int32, (1, LANES), 1) < HEAD_DIM

    def reach(qi, head):
        base = (pl.program_id(0) * H_FOX + 2 * pl.program_id(1) + head) * n_tiles
        kmax = lax.fori_loop(0, qi + 1, lambda t, mx: jnp.maximum(mx, knorm_ref[base + t]), 0.0)
        bound = cfirst_ref[base + qi] + 2.0 * NORM_SLACK * qnorm_ref[base + qi] * kmax + SKIP_MARGIN
        return lax.fori_loop(
            0, qi, lambda t, n: n + jnp.where(bound - clast_ref[base + t] > 0.0, 1, 0), 0)

    def scores(j, part):
        keys = k_ref[0, pl.ds(pl.multiple_of(j * tq + part * slab, slab), slab), :]
        s_ref[part] = _dot_nt(qs_ref[...], keys)

    def weighted_values(j, part):
        vals = v_ref[0, pl.ds(pl.multiple_of(j * tq + part * slab, slab), slab), :]
        return _dot(p_ref[part], vals)

    def exponentials(part, rows, s, m):
        lsum = None
        for g in range(slab // LANES):
            cols = slice(g * LANES, (g + 1) * LANES)
            pg = jnp.exp(s[:, cols] - m)
            p_ref[part, rows, cols] = pg.astype(p_ref.dtype)
            lsum = pg if lsum is None else lsum + pg
        return lsum

    def lazy_tile(j):
        scores(j, 0)
        for part in range(FOX_SPLIT):
            if part + 1 < FOX_SPLIT:
                scores(j, part + 1)
            for r in range(0, 2 * tq, FOX_ROWS):
                rows = slice(r, r + FOX_ROWS)
                lsum = exponentials(part, rows, s_ref[part, rows, :], m_ref[rows, :])
                l_ref[rows, :] = l_ref[rows, :] + lsum
            acc_ref[...] = acc_ref[...] + weighted_values(j, part)

    def full_tile(j, diagonal):
        for part in range(FOX_SPLIT):
            scores(j, part)
        for r in range(0, 2 * tq, FOX_ROWS):
            rows = slice(r, r + FOX_ROWS)
            s = jnp.concatenate([s_ref[part, rows, :] for part in range(FOX_SPLIT)], axis=1)
            if diagonal:
                row = lax.broadcasted_iota(jnp.int32, (FOX_ROWS, tq), 0) + (r % tq)
                col = lax.broadcasted_iota(jnp.int32, (FOX_ROWS, tq), 1)
                s = jnp.where(col <= row, s, NEG)
                m = jnp.broadcast_to(jnp.max(s, axis=-1, keepdims=True), (FOX_ROWS, LANES))
            else:
                m_old = m_ref[rows, :]
                m = jnp.maximum(m_old, jnp.max(s, axis=-1, keepdims=True))
                alpha = jnp.exp(m_old - m)
            m_ref[rows, :] = m
            lsum = None
            for part in range(FOX_SPLIT):
                part_sum = exponentials(part, rows, s[:, part * slab:(part + 1) * slab], m)
                lsum = part_sum if lsum is None else lsum + part_sum
            if diagonal:
                l_ref[rows, :] = lsum
            else:
                l_ref[rows, :] = l_ref[rows, :] * alpha + lsum
                acc_ref[rows, :] = acc_ref[rows, :] * alpha
        pv = weighted_values(j, 0)
        for part in range(1, FOX_SPLIT):
            pv = pv + weighted_values(j, part)
        acc_ref[...] = pv if diagonal else acc_ref[...] + pv

    def query_tile(qi, exact):
        q_rows = pl.ds(pl.multiple_of(qi * tq, tq), tq)
        qcat = q_ref[0, q_rows, :]
        zero = jnp.zeros_like(qcat)
        qs_ref[:tq, :] = jnp.where(lane2 < HEAD_DIM, qcat, zero)
        qs_ref[tq:, :] = jnp.where(lane2 >= HEAD_DIM, qcat, zero)
        n_visit = jnp.maximum(reach(qi, 0), reach(qi, 1))
        full_tile(qi, diagonal=True)

        def step(t, carry):
            if exact:
                full_tile(qi - 1 - t, diagonal=False)
            else:
                lazy_tile(qi - 1 - t)
            return carry

        lax.fori_loop(0, n_visit, step, 0)
        acc, lp = acc_ref[...], l_ref[...]
        out = acc / jnp.sum(lp, axis=-1, keepdims=True)
        o_ref[0, q_rows, :] = jnp.where(head_a, out[:tq], out[tq:]).astype(o_ref.dtype)
        if not exact:
            finite = (jnp.abs(acc) <= F32_BIG) & (lp <= F32_BIG)
            bad_ref[...] = jnp.maximum(bad_ref[...], jnp.where(finite, 0.0, 1.0))

    def sweep(exact):
        def body(qi, carry):
            query_tile(qi, exact)
            return carry

        lax.fori_loop(0, n_tiles, body, 0)

    bad_ref[...] = jnp.zeros_like(bad_ref)
    sweep(exact=False)

    @pl.when(jnp.max(bad_ref[...]) > 0.0)
    def _():
        sweep(exact=True)


def _fox(stats, qf, kf, vf, tq):
    b, s, _ = qf.shape
    smem = pl.BlockSpec(memory_space=pltpu.SMEM)
    stacked = lambda width, dtype: pltpu.VMEM((2 * tq, width), dtype)
    return pl.pallas_call(
        functools.partial(_fox_body, tq=tq),
        grid=(b, N_PAIR),
        in_specs=[smem, smem, smem, smem,
                  pl.BlockSpec((1, s, 2 * LANES), lambda i, p: (i, 0, p)),
                  pl.BlockSpec((1, s, 2 * LANES), lambda i, p: (i, 0, p)),
                  pl.BlockSpec((1, s, LANES), lambda i, p: (i, 0, p))],
        out_specs=pl.BlockSpec((1, s, LANES), lambda i, p: (i, 0, p)),
        out_shape=jax.ShapeDtypeStruct((b, s, W_ATT), jnp.bfloat16),
        scratch_shapes=[stacked(2 * LANES, jnp.bfloat16),
                        pltpu.VMEM((FOX_SPLIT, 2 * tq, tq // FOX_SPLIT), jnp.float32),
                        pltpu.VMEM((FOX_SPLIT, 2 * tq, tq // FOX_SPLIT), jnp.bfloat16),
                        stacked(LANES, jnp.float32), stacked(LANES, jnp.float32),
                        stacked(LANES, jnp.float32), stacked(LANES, jnp.float32)],
        compiler_params=pltpu.CompilerParams(
            dimension_semantics=("arbitrary", "arbitrary"), vmem_limit_bytes=VMEM_LIMIT),
        name="fox_attention",
    )(*stats, qf, kf, vf)


def _band_bias_body(g_ref, o_ref):
    width = g_ref.shape[-1]
    base = jnp.broadcast_to(g_ref[0], (Q_BLOCK, width))
    toeplitz = pltpu.roll(base, 0, 1, stride=1, stride_axis=0)[:, :BAND]
    qrow = lax.broadcasted_iota(jnp.int32, (Q_BLOCK, BAND), 0)
    kcol = lax.broadcasted_iota(jnp.int32, (Q_BLOCK, BAND), 1)
    cq = qrow // CHUNK
    ck = kcol // CHUNK - LEFT_CHUNKS
    valid = (ck <= cq) & (ck >= cq - LEFT_CHUNKS)
    o_ref[0] = jnp.where(valid, toeplitz, NEG)


def _band_bias(g_ext):
    h, _, width = g_ext.shape
    return pl.pallas_call(
        _band_bias_body,
        grid=(h,),
        in_specs=[pl.BlockSpec((1, 1, width), lambda i: (i, 0, 0))],
        out_specs=pl.BlockSpec((1, Q_BLOCK, BAND), lambda i: (i, 0, 0)),
        out_shape=jax.ShapeDtypeStruct((h, Q_BLOCK, BAND), jnp.float32),
        name="band_bias",
    )(g_ext)


def _chunk_body(q_ref, k_ref, v_ref, bias_ref, o_ref, kpad_ref, vpad_ref, s_ref, p_ref, l_ref, *, tq):
    qi = pl.program_id(2)
    s_len = k_ref.shape[1]

    @pl.when(qi == 0)
    def _():
        zeros = jnp.zeros((PAD, LANES), kpad_ref.dtype)
        kpad_ref[:PAD, :] = zeros
        vpad_ref[:PAD, :] = zeros
        kpad_ref[PAD:PAD + s_len, :] = k_ref[0]
        vpad_ref[PAD:PAD + s_len, :] = v_ref[0]

    head_a = lax.broadcasted_iota(jnp.int32, (1, LANES), 1) < HEAD_DIM
    kcol = lax.broadcasted_iota(jnp.int32, (1, BAND), 1)

    def offsets(blk):
        r0 = pl.multiple_of(blk * Q_BLOCK, Q_BLOCK)
        return r0, pl.multiple_of(qi * tq + r0, Q_BLOCK)

    def scores(slot, blk):
        r0, p0 = offsets(blk)
        q2 = q_ref[0, pl.ds(r0, Q_BLOCK), :]
        zero = jnp.zeros_like(q2)
        qs = jnp.concatenate([jnp.where(head_a, q2, zero), jnp.where(head_a, zero, q2)], axis=0)
        s_ref[slot] = _dot_nt(qs, kpad_ref[pl.ds(p0, BAND), :])

    def softmax(slot, blk, masked):
        _, p0 = offsets(blk)
        for r in range(0, 2 * Q_BLOCK, CHUNK_ROWS):
            rows = slice(r, r + CHUNK_ROWS)
            s = s_ref[slot, rows, :] + bias_ref[0, rows, :]
            if masked:
                s = jnp.where(kcol + p0 >= PAD, s, NEG)
            p = jnp.exp(s - jnp.max(s, axis=-1, keepdims=True))
            l_ref[slot, rows, :] = jnp.broadcast_to(jnp.sum(p, axis=-1, keepdims=True), (CHUNK_ROWS, LANES))
            p_ref[slot, rows, :] = p.astype(p_ref.dtype)

    def values(slot, blk):
        r0, p0 = offsets(blk)
        o = _dot(p_ref[slot], vpad_ref[pl.ds(p0, BAND), :]) / l_ref[slot]
        o_ref[0, pl.ds(r0, Q_BLOCK), :] = jnp.where(head_a, o[:Q_BLOCK], o[Q_BLOCK:]).astype(o_ref.dtype)

    def group(i, masked):
        first = i * CHUNK_GROUP
        scores(0, first)
        for n in range(CHUNK_GROUP):
            if n + 1 < CHUNK_GROUP:
                scores((n + 1) % 2, first + n + 1)
            softmax(n % 2, first + n, masked)
            values(n % 2, first + n)

    def masked_group(i, carry):
        group(i, True)
        return carry

    def plain_group(i, carry):
        group(i, False)
        return carry

    n_groups = tq // (Q_BLOCK * CHUNK_GROUP)
    n_masked = jnp.clip((PAD - qi * tq) // (Q_BLOCK * CHUNK_GROUP), 0, n_groups)
    lax.fori_loop(0, n_masked, masked_group, 0)
    lax.fori_loop(n_masked, n_groups, plain_group, 0)


def _chunk_attention(qc, kc, vc, bias, tq):
    b, s, _ = qc.shape
    assert PAD % (CHUNK_GROUP * Q_BLOCK) == 0 and tq % (CHUNK_GROUP * Q_BLOCK) == 0
    return pl.pallas_call(
        functools.partial(_chunk_body, tq=tq),
        grid=(b, N_PAIR, s // tq),
        in_specs=[pl.BlockSpec((1, tq, LANES), lambda i, p, j: (i, j, p)),
                  pl.BlockSpec((1, s, LANES), lambda i, p, j: (i, 0, p)),
                  pl.BlockSpec((1, s, LANES), lambda i, p, j: (i, 0, p)),
                  pl.BlockSpec((1, 2 * Q_BLOCK, BAND), lambda i, p, j: (p, 0, 0))],
        out_specs=pl.BlockSpec((1, tq, LANES), lambda i, p, j: (i, j, p)),
        out_shape=jax.ShapeDtypeStruct((b, s, W_ATT), jnp.bfloat16),
        scratch_shapes=[pltpu.VMEM((PAD + s, LANES), jnp.bfloat16),
                        pltpu.VMEM((PAD + s, LANES), jnp.bfloat16),
                        pltpu.VMEM((2, 2 * Q_BLOCK, BAND), jnp.float32),
                        pltpu.VMEM((2, 2 * Q_BLOCK, BAND), jnp.bfloat16),
                        pltpu.VMEM((2, 2 * Q_BLOCK, LANES), jnp.float32)],
        compiler_params=pltpu.CompilerParams(
            dimension_semantics=("arbitrary", "arbitrary", "arbitrary"), vmem_limit_bytes=VMEM_LIMIT),
        name="chunk_attention",
    )(qc, kc, vc, bias)


def _out_ffn_body(x_ref, oa_ref, ob_ref, gate_ref, wa_ref, wb_ref, wo_ref, g2_ref, wu_ref, wd_ref, gf_ref,
                  o_ref, *, ff_chunk, final):
    d = x_ref.shape[-1]
    ya = _dot(oa_ref[0], wa_ref[...])
    yb = _dot(ob_ref[0], wb_ref[...])
    ga = gate_ref[0, :, :d].astype(jnp.float32)
    gb = gate_ref[0, :, d:].astype(jnp.float32)
    merged = (ga * ya + gb * yb).astype(jnp.bfloat16)
    x1 = x_ref[0] + _dot(merged, wo_ref[...])
    h2 = _rms(x1, g2_ref[...]).astype(jnp.bfloat16)
    acc = x1
    for c0 in range(0, wu_ref.shape[1], ff_chunk):
        u = jnp.maximum(_dot(h2, wu_ref[:, c0:c0 + ff_chunk]), 0.0)
        acc = acc + _dot((u * u).astype(jnp.bfloat16), wd_ref[c0:c0 + ff_chunk, :])
    if final:
        acc = _rms(acc, gf_ref[...])
    o_ref[0] = acc


def _out_ffn(x, oa, ob, gates, wa, wb, wo, g2, wu, wd, gf, tm, final):
    b, s, d = x.shape
    tok = lambda w: pl.BlockSpec((1, tm, w), lambda i, j: (i, j, 0))
    return pl.pallas_call(
        functools.partial(_out_ffn_body, ff_chunk=min(1024, wu.shape[1]), final=final),
        grid=(b, s // tm),
        in_specs=[tok(d), tok(W_ATT), tok(W_ATT), tok(2 * d)]
                 + [_const_spec(a.shape) for a in (wa, wb, wo, g2, wu, wd, gf)],
        out_specs=tok(d),
        out_shape=jax.ShapeDtypeStruct((b, s, d), jnp.float32),
        compiler_params=pltpu.CompilerParams(
            dimension_semantics=("arbitrary", "arbitrary"), vmem_limit_bytes=VMEM_LIMIT),
        name="out_ffn",
    )(x, oa, ob, gates, wa, wb, wo, g2, wu, wd, gf)


def _aug_placement():
    eq = [[0.0] * W_ATT for _ in range(LANES)]
    ek = [[0.0] * W_ATT for _ in range(LANES)]
    one = 3 * H_FOX
    for h in range(H_FOX):
        for piece in range(3):
            eq[piece * H_FOX + h][HEAD_DIM * h + piece] = 1.0
            eq[one][HEAD_DIM * h + 3 + piece] = 1.0
            ek[one][HEAD_DIM * h + piece] = 1.0
            ek[piece * H_FOX + h][HEAD_DIM * h + 3 + piece] = -1.0
    return jnp.array(eq, jnp.bfloat16), jnp.array(ek, jnp.bfloat16)


def kernel(x, norm1, w_in, forget_bias, rel_bias, w_branch_a, w_branch_b, w_out, norm2, w_up, w_down, final_norm):
    b, s, d = x.shape
    depth = w_in.shape[0]
    bf16 = jnp.bfloat16
    tm = min(512, s)
    tqc = min(1024, s)
    eq, ek = _aug_placement()
    head_of_col = lax.broadcasted_iota(jnp.int32, (W_ATT, LANES), 0) // HEAD_DIM
    hsum = (head_of_col == lax.broadcasted_iota(jnp.int32, (W_ATT, LANES), 1)).astype(bf16)
    gf = final_norm.reshape(1, d)
    o = 3 * W_ATT + H_FOX
    for l in range(depth):
        w = w_in[l]
        wq = (w[:, :W_ATT] * SCALE).astype(bf16)
        wk = w[:, W_ATT:2 * W_ATT].astype(bf16)
        wv = w[:, 2 * W_ATT:3 * W_ATT].astype(bf16)
        wf = jnp.pad(w[:, 3 * W_ATT:o], ((0, 0), (0, LANES - H_FOX))).astype(bf16)
        wc = jnp.concatenate([w[:, o:o + W_ATT] * SCALE, w[:, o + W_ATT:o + 3 * W_ATT]], axis=1).astype(bf16)
        wg = w[:, o + 3 * W_ATT:].astype(bf16)
        bf = jnp.pad(forget_bias[l], (0, LANES - H_FOX)).reshape(1, LANES)
        qf, kf, vf, qc, kc, vc, gates, stat = _in_proj(
            x, norm1[l].reshape(1, d), wq, wk, wv, wc, wg, wf, bf, eq, ek, hsum, tm)

        stats = stat[:, :, :4, :H_FOX].transpose(2, 0, 3, 1).reshape(4, -1)
        o_a = _fox(tuple(stats), qf, kf, vf, tm)

        far = rel_bias[l][:, 2 * MAX_REL:]
        g_ext = jnp.concatenate([jnp.broadcast_to(far, (H_CHK, PAD - MAX_REL + 1)),
                                 rel_bias[l][:, 2 * MAX_REL - 1:0:-1],
                                 jnp.broadcast_to(far, (H_CHK, Q_BLOCK))], axis=1)
        bias = _band_bias(g_ext.reshape(H_CHK, 1, -1)).reshape(N_PAIR, 2 * Q_BLOCK, BAND)
        o_b = _chunk_attention(qc, kc, vc, bias, tqc)

        x = _out_ffn(x, o_a, o_b, gates, w_branch_a[l].astype(bf16), w_branch_b[l].astype(bf16),
                     w_out[l].astype(bf16), norm2[l].reshape(1, d), w_up[l].astype(bf16),
                     w_down[l].astype(bf16), gf, tm, final=(l == depth - 1))
    return x
```

```python
import functools
import math

import jax
import jax.numpy as jnp
from jax import lax
from jax.experimental import pallas as pl
from jax.experimental.pallas import tpu as pltpu

HEAD_DIM = 64
H_FOX = 8
H_CHK = 8
N_PAIR = 4
W_ATT = H_FOX * HEAD_DIM
CHUNK = 64
Q_BLOCK = 128
LEFT_CHUNKS = 8
PAD = LEFT_CHUNKS * CHUNK
BAND = PAD + Q_BLOCK
MAX_REL = 128
EPS = 1e-6
NEG = -1e30
LANES = 128
SCALE = 1.0 / math.sqrt(HEAD_DIM)
F32_BIG = 3.0e38
FOX_ROWS = 32
CHUNK_ROWS = 32
CHUNK_GROUP = 4
SKIP_MARGIN = 30.0
NORM_SLACK = 1.02
VMEM_LIMIT = 60 * 1024 * 1024

_NT = (((1,), (1,)), ((), ()))


def _dot(a, b):
    return jnp.dot(a, b, preferred_element_type=jnp.float32)


def _dot_nt(a, b):
    return lax.dot_general(a, b, _NT, preferred_element_type=jnp.float32)


def _rms(x, g):
    ms = jnp.mean(x * x, axis=-1, keepdims=True)
    return x * lax.rsqrt(ms + EPS) * g


def _const_spec(shape):
    nd = len(shape)
    return pl.BlockSpec(shape, lambda *_: (0,) * nd, pipeline_mode=pl.Buffered(1))


def _in_proj_body(x_ref, g_ref, wq_ref, wk_ref, wv_ref, wc_ref, wg_ref, wf_ref, bf_ref, eq_ref, ek_ref, hsum_ref,
                  qf_ref, kf_ref, vf_ref, qc_ref, kc_ref, vc_ref, gate_ref, stat_ref, carry_ref):
    tm = x_ref.shape[1]

    @pl.when(pl.program_id(1) == 0)
    def _():
        carry_ref[...] = jnp.zeros_like(carry_ref)

    h = _rms(x_ref[0], g_ref[...]).astype(jnp.bfloat16)

    lane = lax.broadcasted_iota(jnp.int32, (tm, LANES), 1)
    row = lax.broadcasted_iota(jnp.int32, (tm, LANES), 0)
    logf = jax.nn.log_sigmoid(_dot(h, wf_ref[...]) + bf_ref[...])
    c = jnp.where(lane < H_FOX, logf, 0.0)
    k = 1
    while k < tm:
        c = c + jnp.where(row >= k, pltpu.roll(c, k, 0), 0.0)
        k *= 2
    c = c + carry_ref[...]
    carry_ref[...] = c[tm - 1:tm, :]

    hi = c.astype(jnp.bfloat16).astype(jnp.float32)
    r1 = c - hi
    mid = r1.astype(jnp.bfloat16).astype(jnp.float32)
    lo = (r1 - mid).astype(jnp.bfloat16).astype(jnp.float32)
    pieces = hi + pltpu.roll(mid, H_FOX, 1) + pltpu.roll(lo, 2 * H_FOX, 1)
    pieces = jnp.where(lane == 3 * H_FOX, 1.0, pieces).astype(jnp.bfloat16)
    aug_q = _dot(pieces, eq_ref[...]).astype(jnp.bfloat16)
    aug_k = _dot(pieces, ek_ref[...]).astype(jnp.bfloat16)

    q = _dot(h, wq_ref[...]).astype(jnp.bfloat16)
    kk = _dot(h, wk_ref[...]).astype(jnp.bfloat16)
    for p in range(N_PAIR):
        src = slice(p * LANES, (p + 1) * LANES)
        qf_ref[0, :, 2 * p * LANES:(2 * p + 1) * LANES] = q[:, src]
        qf_ref[0, :, (2 * p + 1) * LANES:(2 * p + 2) * LANES] = aug_q[:, src]
        kf_ref[0, :, 2 * p * LANES:(2 * p + 1) * LANES] = kk[:, src]
        kf_ref[0, :, (2 * p + 1) * LANES:(2 * p + 2) * LANES] = aug_k[:, src]
    vf_ref[0] = _dot(h, wv_ref[...]).astype(jnp.bfloat16)

    def max_norm(t):
        t = t.astype(jnp.float32)
        sq = _dot((t * t).astype(jnp.bfloat16), hsum_ref[...])
        return jnp.sqrt(jnp.max(sq, axis=0, keepdims=True))

    sub = lax.broadcasted_iota(jnp.int32, (8, LANES), 0)
    stat = jnp.where(sub == 0, c[0:1, :], 0.0)
    stat = jnp.where(sub == 1, c[tm - 1:tm, :], stat)
    stat = jnp.where(sub == 2, max_norm(q), stat)
    stat_ref[0, 0] = jnp.where(sub == 3, max_norm(kk), stat)

    pc = _dot(h, wc_ref[...])
    qc_ref[0] = pc[:, :W_ATT].astype(jnp.bfloat16)
    kc_ref[0] = pc[:, W_ATT:2 * W_ATT].astype(jnp.bfloat16)
    vc_ref[0] = pc[:, 2 * W_ATT:].astype(jnp.bfloat16)
    gate_ref[0] = jax.nn.sigmoid(_dot(h, wg_ref[...])).astype(jnp.bfloat16)


def _in_proj(x, g, wq, wk, wv, wc, wg, wf, bf, eq, ek, hsum, tm):
    b, s, d = x.shape
    tok = lambda w: pl.BlockSpec((1, tm, w), lambda i, j: (i, j, 0))
    bf16 = jnp.bfloat16
    out_shape = [jax.ShapeDtypeStruct((b, s, w), bf16)
                 for w in (2 * W_ATT, 2 * W_ATT, W_ATT, W_ATT, W_ATT, W_ATT, 2 * d)]
    out_shape.append(jax.ShapeDtypeStruct((b, s // tm, 8, LANES), jnp.float32))
    consts = (g, wq, wk, wv, wc, wg, wf, bf, eq, ek, hsum)
    return pl.pallas_call(
        _in_proj_body,
        grid=(b, s // tm),
        in_specs=[tok(d)] + [_const_spec(a.shape) for a in consts],
        out_specs=[tok(sh.shape[-1]) for sh in out_shape[:-1]]
                  + [pl.BlockSpec((1, 1, 8, LANES), lambda i, j: (i, j, 0, 0))],
        out_shape=out_shape,
        scratch_shapes=[pltpu.VMEM((1, LANES), jnp.float32)],
        compiler_params=pltpu.CompilerParams(
            dimension_semantics=("arbitrary", "arbitrary"), vmem_limit_bytes=VMEM_LIMIT),
        name="in_proj",
    )(x, *consts)


def _fox_body(cfirst_ref, clast_ref, qnorm_ref, knorm_ref, q_ref, k_ref, v_ref, o_ref,
              s_ref, p_ref, m_ref, l_ref, acc_ref, bad_ref, *, tq):
    n_tiles = q_ref.shape[1] // tq
    lane2 = lax.broadcasted_iota(jnp.int32, (1, 2 * LANES), 1) % LANES
    head_a = lax.broadcasted_iota(jnp.int32, (1, LANES), 1) < HEAD_DIM

    def reach(qi, head):
        base = (pl.program_id(0) * H_FOX + 2 * pl.program_id(1) + head) * n_tiles
        kmax = lax.fori_loop(0, qi + 1, lambda t, mx: jnp.maximum(mx, knorm_ref[base + t]), 0.0)
        bound = cfirst_ref[base + qi] + 2.0 * NORM_SLACK * qnorm_ref[base + qi] * kmax + SKIP_MARGIN
        return lax.fori_loop(
            0, qi, lambda t, n: n + jnp.where(bound - clast_ref[base + t] > 0.0, 1, 0), 0)

    def key_rows(j):
        return pl.ds(pl.multiple_of(j * tq, tq), tq)

    def scores(slot, q_head, j):
        s_ref[slot] = _dot_nt(q_head, k_ref[0, key_rows(j), :])

    def softmax(slot, head, mode):
        for r in range(0, tq, FOX_ROWS):
            rows = slice(r, r + FOX_ROWS)
            s = s_ref[slot, rows, :]
            if mode == "diagonal":
                row = lax.broadcasted_iota(jnp.int32, (FOX_ROWS, tq), 0) + r
                col = lax.broadcasted_iota(jnp.int32, (FOX_ROWS, tq), 1)
                s = jnp.where(col <= row, s, NEG)
                m = jnp.broadcast_to(jnp.max(s, axis=-1, keepdims=True), (FOX_ROWS, LANES))
                m_ref[head, rows, :] = m
            elif mode == "exact":
                m_old = m_ref[head, rows, :]
                m = jnp.maximum(m_old, jnp.max(s, axis=-1, keepdims=True))
                alpha = jnp.exp(m_old - m)
                m_ref[head, rows, :] = m
            else:
                m = m_ref[head, rows, :]
            lsum = None
            for g in range(tq // LANES):
                cols = slice(g * LANES, (g + 1) * LANES)
                pg = jnp.exp(s[:, cols] - m)
                p_ref[slot, rows, cols] = pg.astype(p_ref.dtype)
                lsum = pg if lsum is None else lsum + pg
            if mode == "diagonal":
                l_ref[head, rows, :] = lsum
            elif mode == "exact":
                l_ref[head, rows, :] = l_ref[head, rows, :] * alpha + lsum
                acc_ref[head, rows, :] = acc_ref[head, rows, :] * alpha
            else:
                l_ref[head, rows, :] = l_ref[head, rows, :] + lsum

    def values(slot, head, j, mode):
        pv = _dot(p_ref[slot], v_ref[0, key_rows(j), :])
        acc_ref[head] = pv if mode == "diagonal" else acc_ref[head] + pv

    def single_tile(qh, j, mode):
        for head in range(2):
            scores(head, qh[head], j)
        for head in range(2):
            softmax(head, head, mode)
            values(head, head, j, mode)

    def double_tile(qh, j):
        units = [(j, 0), (j, 1), (j - 1, 0), (j - 1, 1)]
        for n in range(2):
            scores(n, qh[units[n][1]], units[n][0])
        for n, (tile, head) in enumerate(units):
            softmax(n, head, "lazy")
            if n + 2 < len(units):
                scores(n + 2, qh[units[n + 2][1]], units[n + 2][0])
            values(n, head, tile, "lazy")

    def query_tile(qi, exact):
        q_rows = pl.ds(pl.multiple_of(qi * tq, tq), tq)
        qcat = q_ref[0, q_rows, :]
        zero = jnp.zeros_like(qcat)
        qh = (jnp.where(lane2 < HEAD_DIM, qcat, zero), jnp.where(lane2 >= HEAD_DIM, qcat, zero))
        n_visit = jnp.maximum(reach(qi, 0), reach(qi, 1))
        single_tile(qh, qi, "diagonal")

        if exact:
            def step(t, carry):
                single_tile(qh, qi - 1 - t, "exact")
                return carry

            lax.fori_loop(0, n_visit, step, 0)
        else:
            def step(t, carry):
                double_tile(qh, qi - 1 - 2 * t)
                return carry

            lax.fori_loop(0, n_visit // 2, step, 0)

            @pl.when(n_visit % 2 == 1)
            def _():
                single_tile(qh, qi - n_visit, "lazy")

        outs = []
        for head in range(2):
            acc, lp = acc_ref[head], l_ref[head]
            outs.append(acc / jnp.sum(lp, axis=-1, keepdims=True))
            if not exact:
                finite = (jnp.abs(acc) <= F32_BIG) & (lp <= F32_BIG)
                bad_ref[...] = jnp.maximum(bad_ref[...], jnp.where(finite, 0.0, 1.0))
        o_ref[0, q_rows, :] = jnp.where(head_a, outs[0], outs[1]).astype(o_ref.dtype)

    def sweep(exact):
        def body(qi, carry):
            query_tile(qi, exact)
            return carry

        lax.fori_loop(0, n_tiles, body, 0)

    bad_ref[...] = jnp.zeros_like(bad_ref)
    sweep(exact=False)

    @pl.when(jnp.max(bad_ref[...]) > 0.0)
    def _():
        sweep(exact=True)


def _fox(stats, qf, kf, vf, tq):
    b, s, _ = qf.shape
    smem = pl.BlockSpec(memory_space=pltpu.SMEM)
    return pl.pallas_call(
        functools.partial(_fox_body, tq=tq),
        grid=(b, N_PAIR),
        in_specs=[smem, smem, smem, smem,
                  pl.BlockSpec((1, s, 2 * LANES), lambda i, p: (i, 0, p)),
                  pl.BlockSpec((1, s, 2 * LANES), lambda i, p: (i, 0, p)),
                  pl.BlockSpec((1, s, LANES), lambda i, p: (i, 0, p))],
        out_specs=pl.BlockSpec((1, s, LANES), lambda i, p: (i, 0, p)),
        out_shape=jax.ShapeDtypeStruct((b, s, W_ATT), jnp.bfloat16),
        scratch_shapes=[pltpu.VMEM((4, tq, tq), jnp.float32),
                        pltpu.VMEM((4, tq, tq), jnp.bfloat16),
                        pltpu.VMEM((2, tq, LANES), jnp.float32),
                        pltpu.VMEM((2, tq, LANES), jnp.float32),
                        pltpu.VMEM((2, tq, LANES), jnp.float32),
                        pltpu.VMEM((tq, LANES), jnp.float32)],
        compiler_params=pltpu.CompilerParams(
            dimension_semantics=("arbitrary", "arbitrary"), vmem_limit_bytes=VMEM_LIMIT),
        name="fox_attention",
    )(*stats, qf, kf, vf)


def _band_bias_body(g_ref, o_ref):
    width = g_ref.shape[-1]
    base = jnp.broadcast_to(g_ref[0], (Q_BLOCK, width))
    toeplitz = pltpu.roll(base, 0, 1, stride=1, stride_axis=0)[:, :BAND]
    qrow = lax.broadcasted_iota(jnp.int32, (Q_BLOCK, BAND), 0)
    kcol = lax.broadcasted_iota(jnp.int32, (Q_BLOCK, BAND), 1)
    cq = qrow // CHUNK
    ck = kcol // CHUNK - LEFT_CHUNKS
    valid = (ck <= cq) & (ck >= cq - LEFT_CHUNKS)
    o_ref[0] = jnp.where(valid, toeplitz, NEG)


def _band_bias(g_ext):
    h, _, width = g_ext.shape
    return pl.pallas_call(
        _band_bias_body,
        grid=(h,),
        in_specs=[pl.BlockSpec((1, 1, width), lambda i: (i, 0, 0))],
        out_specs=pl.BlockSpec((1, Q_BLOCK, BAND), lambda i: (i, 0, 0)),
        out_shape=jax.ShapeDtypeStruct((h, Q_BLOCK, BAND), jnp.float32),
        name="band_bias",
    )(g_ext)


def _chunk_body(q_ref, k_ref, v_ref, bias_ref, o_ref, kpad_ref, vpad_ref, s_ref, p_ref, l_ref, *, tq):
    qi = pl.program_id(2)
    s_len = k_ref.shape[1]

    @pl.when(qi == 0)
    def _():
        zeros = jnp.zeros((PAD, LANES), kpad_ref.dtype)
        kpad_ref[:PAD, :] = zeros
        vpad_ref[:PAD, :] = zeros
        kpad_ref[PAD:PAD + s_len, :] = k_ref[0]
        vpad_ref[PAD:PAD + s_len, :] = v_ref[0]

    head_a = lax.broadcasted_iota(jnp.int32, (1, LANES), 1) < HEAD_DIM
    kcol = lax.broadcasted_iota(jnp.int32, (1, BAND), 1)

    def offsets(blk):
        r0 = pl.multiple_of(blk * Q_BLOCK, Q_BLOCK)
        return r0, pl.multiple_of(qi * tq + r0, Q_BLOCK)

    def scores(slot, blk):
        r0, p0 = offsets(blk)
        q2 = q_ref[0, pl.ds(r0, Q_BLOCK), :]
        zero = jnp.zeros_like(q2)
        qs = jnp.concatenate([jnp.where(head_a, q2, zero), jnp.where(head_a, zero, q2)], axis=0)
        s_ref[slot] = _dot_nt(qs, kpad_ref[pl.ds(p0, BAND), :])

    def softmax(slot, blk, masked):
        _, p0 = offsets(blk)
        for r in range(0, 2 * Q_BLOCK, CHUNK_ROWS):
            rows = slice(r, r + CHUNK_ROWS)
            s = s_ref[slot, rows, :] + bias_ref[0, rows, :]
            if masked:
                s = jnp.where(kcol + p0 >= PAD, s, NEG)
            p = jnp.exp(s - jnp.max(s, axis=-1, keepdims=True))
            l_ref[slot, rows, :] = jnp.broadcast_to(jnp.sum(p, axis=-1, keepdims=True), (CHUNK_ROWS, LANES))
            p_ref[slot, rows, :] = p.astype(p_ref.dtype)

    def values(slot, blk):
        r0, p0 = offsets(blk)
        o = _dot(p_ref[slot], vpad_ref[pl.ds(p0, BAND), :]) / l_ref[slot]
        o_ref[0, pl.ds(r0, Q_BLOCK), :] = jnp.where(head_a, o[:Q_BLOCK], o[Q_BLOCK:]).astype(o_ref.dtype)

    def group(i, masked):
        first = i * CHUNK_GROUP
        scores(0, first)
        for n in range(CHUNK_GROUP):
            if n + 1 < CHUNK_GROUP:
                scores((n + 1) % 2, first + n + 1)
            softmax(n % 2, first + n, masked)
            values(n % 2, first + n)

    def masked_group(i, carry):
        group(i, True)
        return carry

    def plain_group(i, carry):
        group(i, False)
        return carry

    n_groups = tq // (Q_BLOCK * CHUNK_GROUP)
    n_masked = jnp.clip((PAD - qi * tq) // (Q_BLOCK * CHUNK_GROUP), 0, n_groups)
    lax.fori_loop(0, n_masked, masked_group, 0)
    lax.fori_loop(n_masked, n_groups, plain_group, 0)


def _chunk_attention(qc, kc, vc, bias, tq):
    b, s, _ = qc.shape
    assert PAD % (CHUNK_GROUP * Q_BLOCK) == 0 and tq % (CHUNK_GROUP * Q_BLOCK) == 0
    return pl.pallas_call(
        functools.partial(_chunk_body, tq=tq),
        grid=(b, N_PAIR, s // tq),
        in_specs=[pl.BlockSpec((1, tq, LANES), lambda i, p, j: (i, j, p)),
                  pl.BlockSpec((1, s, LANES), lambda i, p, j: (i, 0, p)),
                  pl.BlockSpec((1, s, LANES), lambda i, p, j: (i, 0, p)),
                  pl.BlockSpec((1, 2 * Q_BLOCK, BAND), lambda i, p, j: (p, 0, 0))],
        out_specs=pl.BlockSpec((1, tq, LANES), lambda i, p, j: (i, j, p)),
        out_shape=jax.ShapeDtypeStruct((b, s, W_ATT), jnp.bfloat16),
        scratch_shapes=[pltpu.VMEM((PAD + s, LANES), jnp.bfloat16),
                        pltpu.VMEM((PAD + s, LANES), jnp.bfloat16),
                        pltpu.VMEM((2, 2 * Q_BLOCK, BAND), jnp.float32),
                        pltpu.VMEM((2, 2 * Q_BLOCK, BAND), jnp.bfloat16),
                        pltpu.VMEM((2, 2 * Q_BLOCK, LANES), jnp.float32)],
        compiler_params=pltpu.CompilerParams(
            dimension_semantics=("arbitrary", "arbitrary", "arbitrary"), vmem_limit_bytes=VMEM_LIMIT),
        name="chunk_attention",
    )(qc, kc, vc, bias)


def _out_ffn_body(x_ref, oa_ref, ob_ref, gate_ref, wa_ref, wb_ref, wo_ref, g2_ref, wu_ref, wd_ref, gf_ref,
                  o_ref, *, ff_chunk, final):
    d = x_ref.shape[-1]
    ya = _dot(oa_ref[0], wa_ref[...])
    yb = _dot(ob_ref[0], wb_ref[...])
    ga = gate_ref[0, :, :d].astype(jnp.float32)
    gb = gate_ref[0, :, d:].astype(jnp.float32)
    merged = (ga * ya + gb * yb).astype(jnp.bfloat16)
    x1 = x_ref[0] + _dot(merged, wo_ref[...])
    h2 = _rms(x1, g2_ref[...]).astype(jnp.bfloat16)
    acc = x1
    for c0 in range(0, wu_ref.shape[1], ff_chunk):
        u = jnp.maximum(_dot(h2, wu_ref[:, c0:c0 + ff_chunk]), 0.0)
        acc = acc + _dot((u * u).astype(jnp.bfloat16), wd_ref[c0:c0 + ff_chunk, :])
    if final:
        acc = _rms(acc, gf_ref[...])
    o_ref[0] = acc


def _out_ffn(x, oa, ob, gates, wa, wb, wo, g2, wu, wd, gf, tm, final):
    b, s, d = x.shape
    tok = lambda w: pl.BlockSpec((1, tm, w), lambda i, j: (i, j, 0))
    return pl.pallas_call(
        functools.partial(_out_ffn_body, ff_chunk=min(1024, wu.shape[1]), final=final),
        grid=(b, s // tm),
        in_specs=[tok(d), tok(W_ATT), tok(W_ATT), tok(2 * d)]
                 + [_const_spec(a.shape) for a in (wa, wb, wo, g2, wu, wd, gf)],
        out_specs=tok(d),
        out_shape=jax.ShapeDtypeStruct((b, s, d), jnp.float32),
        compiler_params=pltpu.CompilerParams(
            dimension_semantics=("arbitrary", "arbitrary"), vmem_limit_bytes=VMEM_LIMIT),
        name="out_ffn",
    )(x, oa, ob, gates, wa, wb, wo, g2, wu, wd, gf)


def _aug_placement():
    eq = [[0.0] * W_ATT for _ in range(LANES)]
    ek = [[0.0] * W_ATT for _ in range(LANES)]
    one = 3 * H_FOX
    for h in range(H_FOX):
        for piece in range(3):
            eq[piece * H_FOX + h][HEAD_DIM * h + piece] = 1.0
            eq[one][HEAD_DIM * h + 3 + piece] = 1.0
            ek[one][HEAD_DIM * h + piece] = 1.0
            ek[piece * H_FOX + h][HEAD_DIM * h + 3 + piece] = -1.0
    return jnp.array(eq, jnp.bfloat16), jnp.array(ek, jnp.bfloat16)


def kernel(x, norm1, w_in, forget_bias, rel_bias, w_branch_a, w_branch_b, w_out, norm2, w_up, w_down, final_norm):
    b, s, d = x.shape
    depth = w_in.shape[0]
    bf16 = jnp.bfloat16
    tm = min(512, s)
    tqc = min(1024, s)
    eq, ek = _aug_placement()
    head_of_col = lax.broadcasted_iota(jnp.int32, (W_ATT, LANES), 0) // HEAD_DIM
    hsum = (head_of_col == lax.broadcasted_iota(jnp.int32, (W_ATT, LANES), 1)).astype(bf16)
    gf = final_norm.reshape(1, d)
    o = 3 * W_ATT + H_FOX
    for l in range(depth):
        w = w_in[l]
        wq = (w[:, :W_ATT] * SCALE).astype(bf16)
        wk = w[:, W_ATT:2 * W_ATT].astype(bf16)
        wv = w[:, 2 * W_ATT:3 * W_ATT].astype(bf16)
        wf = jnp.pad(w[:, 3 * W_ATT:o], ((0, 0), (0, LANES - H_FOX))).astype(bf16)
        wc = jnp.concatenate([w[:, o:o + W_ATT] * SCALE, w[:, o + W_ATT:o + 3 * W_ATT]], axis=1).astype(bf16)
        wg = w[:, o + 3 * W_ATT:].astype(bf16)
        bf = jnp.pad(forget_bias[l], (0, LANES - H_FOX)).reshape(1, LANES)
        qf, kf, vf, qc, kc, vc, gates, stat = _in_proj(
            x, norm1[l].reshape(1, d), wq, wk, wv, wc, wg, wf, bf, eq, ek, hsum, tm)

        stats = stat[:, :, :4, :H_FOX].transpose(2, 0, 3, 1).reshape(4, -1)
        o_a = _fox(tuple(stats), qf, kf, vf, tm)

        far = rel_bias[l][:, 2 * MAX_REL:]
        g_ext = jnp.concatenate([jnp.broadcast_to(far, (H_CHK, PAD - MAX_REL + 1)),
                                 rel_bias[l][:, 2 * MAX_REL - 1:0:-1],
                                 jnp.broadcast_to(far, (H_CHK, Q_BLOCK))], axis=1)
        bias = _band_bias(g_ext.reshape(H_CHK, 1, -1)).reshape(N_PAIR, 2 * Q_BLOCK, BAND)
        o_b = _chunk_attention(qc, kc, vc, bias, tqc)

        x = _out_ffn(x, o_a, o_b, gates, w_branch_a[l].astype(bf16), w_branch_b[l].astype(bf16),
                     w_out[l].astype(bf16), norm2[l].reshape(1, d), w_up[l].astype(bf16),
                     w_down[l].astype(bf16), gf, tm, final=(l == depth - 1))
    return x
```

```python
import functools
import math

import jax
import jax.numpy as jnp
from jax import lax
from jax.experimental import pallas as pl
from jax.experimental.pallas import tpu as pltpu

HEAD_DIM = 64
H_FOX = 8
H_CHK = 8
N_PAIR = 4
W_ATT = H_FOX * HEAD_DIM
CHUNK = 64
Q_BLOCK = 128
LEFT_CHUNKS = 8
PAD = LEFT_CHUNKS * CHUNK
BAND = PAD + Q_BLOCK
MAX_REL = 128
EPS = 1e-6
NEG = -1e30
LANES = 128
SCALE = 1.0 / math.sqrt(HEAD_DIM)
F32_BIG = 3.0e38
FOX_ROWS = 32
FOX_TRIP = 4
CHUNK_ROWS = 32
CHUNK_GROUP = 8
SKIP_MARGIN = 30.0
NORM_SLACK = 1.02
VMEM_LIMIT = 60 * 1024 * 1024

_NT = (((1,), (1,)), ((), ()))


def _dot(a, b):
    return jnp.dot(a, b, preferred_element_type=jnp.float32)


def _dot_nt(a, b):
    return lax.dot_general(a, b, _NT, preferred_element_type=jnp.float32)


def _rms(x, g):
    ms = jnp.mean(x * x, axis=-1, keepdims=True)
    return x * lax.rsqrt(ms + EPS) * g


def _const_spec(shape):
    nd = len(shape)
    return pl.BlockSpec(shape, lambda *_: (0,) * nd, pipeline_mode=pl.Buffered(1))


def _in_proj_body(x_ref, g_ref, wq_ref, wk_ref, wv_ref, wc_ref, wg_ref, wf_ref, bf_ref, eq_ref, ek_ref, hsum_ref,
                  qf_ref, kf_ref, vf_ref, qc_ref, kc_ref, vc_ref, gate_ref, stat_ref, carry_ref):
    tm = x_ref.shape[1]

    @pl.when(pl.program_id(1) == 0)
    def _():
        carry_ref[...] = jnp.zeros_like(carry_ref)

    h = _rms(x_ref[0], g_ref[...]).astype(jnp.bfloat16)

    lane = lax.broadcasted_iota(jnp.int32, (tm, LANES), 1)
    row = lax.broadcasted_iota(jnp.int32, (tm, LANES), 0)
    logf = jax.nn.log_sigmoid(_dot(h, wf_ref[...]) + bf_ref[...])
    c = jnp.where(lane < H_FOX, logf, 0.0)
    k = 1
    while k < tm:
        c = c + jnp.where(row >= k, pltpu.roll(c, k, 0), 0.0)
        k *= 2
    c = c + carry_ref[...]
    carry_ref[...] = c[tm - 1:tm, :]

    hi = c.astype(jnp.bfloat16).astype(jnp.float32)
    r1 = c - hi
    mid = r1.astype(jnp.bfloat16).astype(jnp.float32)
    lo = (r1 - mid).astype(jnp.bfloat16).astype(jnp.float32)
    pieces = hi + pltpu.roll(mid, H_FOX, 1) + pltpu.roll(lo, 2 * H_FOX, 1)
    pieces = jnp.where(lane == 3 * H_FOX, 1.0, pieces).astype(jnp.bfloat16)
    aug_q = _dot(pieces, eq_ref[...]).astype(jnp.bfloat16)
    aug_k = _dot(pieces, ek_ref[...]).astype(jnp.bfloat16)

    q = _dot(h, wq_ref[...]).astype(jnp.bfloat16)
    kk = _dot(h, wk_ref[...]).astype(jnp.bfloat16)
    for p in range(N_PAIR):
        src = slice(p * LANES, (p + 1) * LANES)
        qf_ref[0, :, 2 * p * LANES:(2 * p + 1) * LANES] = q[:, src]
        qf_ref[0, :, (2 * p + 1) * LANES:(2 * p + 2) * LANES] = aug_q[:, src]
        kf_ref[0, :, 2 * p * LANES:(2 * p + 1) * LANES] = kk[:, src]
        kf_ref[0, :, (2 * p + 1) * LANES:(2 * p + 2) * LANES] = aug_k[:, src]
    vf_ref[0] = _dot(h, wv_ref[...]).astype(jnp.bfloat16)

    def max_norm(t):
        t = t.astype(jnp.float32)
        sq = _dot((t * t).astype(jnp.bfloat16), hsum_ref[...])
        return jnp.sqrt(jnp.max(sq, axis=0, keepdims=True))

    sub = lax.broadcasted_iota(jnp.int32, (8, LANES), 0)
    stat = jnp.where(sub == 0, c[0:1, :], 0.0)
    stat = jnp.where(sub == 1, c[tm - 1:tm, :], stat)
    stat = jnp.where(sub == 2, max_norm(q), stat)
    stat_ref[0, 0] = jnp.where(sub == 3, max_norm(kk), stat)

    pc = _dot(h, wc_ref[...])
    qc_ref[0] = pc[:, :W_ATT].astype(jnp.bfloat16)
    kc_ref[0] = pc[:, W_ATT:2 * W_ATT].astype(jnp.bfloat16)
    vc_ref[0] = pc[:, 2 * W_ATT:].astype(jnp.bfloat16)
    gate_ref[0] = jax.nn.sigmoid(_dot(h, wg_ref[...])).astype(jnp.bfloat16)


def _in_proj(x, g, wq, wk, wv, wc, wg, wf, bf, eq, ek, hsum, tm):
    b, s, d = x.shape
    tok = lambda w: pl.BlockSpec((1, tm, w), lambda i, j: (i, j, 0))
    bf16 = jnp.bfloat16
    out_shape = [jax.ShapeDtypeStruct((b, s, w), bf16)
                 for w in (2 * W_ATT, 2 * W_ATT, W_ATT, W_ATT, W_ATT, W_ATT, 2 * d)]
    out_shape.append(jax.ShapeDtypeStruct((b, s // tm, 8, LANES), jnp.float32))
    consts = (g, wq, wk, wv, wc, wg, wf, bf, eq, ek, hsum)
    return pl.pallas_call(
        _in_proj_body,
        grid=(b, s // tm),
        in_specs=[tok(d)] + [_const_spec(a.shape) for a in consts],
        out_specs=[tok(sh.shape[-1]) for sh in out_shape[:-1]]
                  + [pl.BlockSpec((1, 1, 8, LANES), lambda i, j: (i, j, 0, 0))],
        out_shape=out_shape,
        scratch_shapes=[pltpu.VMEM((1, LANES), jnp.float32)],
        compiler_params=pltpu.CompilerParams(
            dimension_semantics=("arbitrary", "arbitrary"), vmem_limit_bytes=VMEM_LIMIT),
        name="in_proj",
    )(x, *consts)


def _fox_body(cfirst_ref, clast_ref, qnorm_ref, knorm_ref, q_ref, k_ref, v_ref, o_ref,
              s_ref, p_ref, m_ref, l_ref, acc_ref, bad_ref, *, tq):
    n_tiles = q_ref.shape[1] // tq
    lane2 = lax.broadcasted_iota(jnp.int32, (1, 2 * LANES), 1) % LANES
    head_a = lax.broadcasted_iota(jnp.int32, (1, LANES), 1) < HEAD_DIM

    def reach(qi, head):
        base = (pl.program_id(0) * H_FOX + 2 * pl.program_id(1) + head) * n_tiles
        kmax = lax.fori_loop(0, qi + 1, lambda t, mx: jnp.maximum(mx, knorm_ref[base + t]), 0.0)
        bound = cfirst_ref[base + qi] + 2.0 * NORM_SLACK * qnorm_ref[base + qi] * kmax + SKIP_MARGIN
        return lax.fori_loop(
            0, qi, lambda t, n: n + jnp.where(bound - clast_ref[base + t] > 0.0, 1, 0), 0)

    def key_rows(j):
        return pl.ds(pl.multiple_of(j * tq, tq), tq)

    def scores(slot, q_head, j):
        s_ref[slot] = _dot_nt(q_head, k_ref[0, key_rows(j), :])

    def softmax(slot, head, mode):
        for r in range(0, tq, FOX_ROWS):
            rows = slice(r, r + FOX_ROWS)
            s = s_ref[slot, rows, :]
            if mode == "diagonal":
                row = lax.broadcasted_iota(jnp.int32, (FOX_ROWS, tq), 0) + r
                col = lax.broadcasted_iota(jnp.int32, (FOX_ROWS, tq), 1)
                s = jnp.where(col <= row, s, NEG)
                m = jnp.broadcast_to(jnp.max(s, axis=-1, keepdims=True), (FOX_ROWS, LANES))
                m_ref[head, rows, :] = m
            elif mode == "exact":
                m_old = m_ref[head, rows, :]
                m = jnp.maximum(m_old, jnp.max(s, axis=-1, keepdims=True))
                alpha = jnp.exp(m_old - m)
                m_ref[head, rows, :] = m
            else:
                m = m_ref[head, rows, :]
            lsum = None
            for g in range(tq // LANES):
                cols = slice(g * LANES, (g + 1) * LANES)
                pg = jnp.exp(s[:, cols] - m)
                p_ref[slot, rows, cols] = pg.astype(p_ref.dtype)
                lsum = pg if lsum is None else lsum + pg
            if mode == "diagonal":
                l_ref[head, rows, :] = lsum
            elif mode == "exact":
                l_ref[head, rows, :] = l_ref[head, rows, :] * alpha + lsum
                acc_ref[head, rows, :] = acc_ref[head, rows, :] * alpha
            else:
                l_ref[head, rows, :] = l_ref[head, rows, :] + lsum

    def values(slot, head, j, mode):
        pv = _dot(p_ref[slot], v_ref[0, key_rows(j), :])
        acc_ref[head] = pv if mode == "diagonal" else acc_ref[head] + pv

    def single_tile(qh, j, mode):
        for head in range(2):
            scores(head, qh[head], j)
        for head in range(2):
            softmax(head, head, mode)
            values(head, head, j, mode)

    def tile_run(qh, j):
        units = [(j - i, head) for i in range(FOX_TRIP) for head in range(2)]
        n_slots = s_ref.shape[0]
        for n in range(2):
            scores(n, qh[units[n][1]], units[n][0])
        for n, (tile, head) in enumerate(units):
            softmax(n % n_slots, head, "lazy")
            if n + 2 < len(units):
                scores((n + 2) % n_slots, qh[units[n + 2][1]], units[n + 2][0])
            values(n % n_slots, head, tile, "lazy")

    def query_tile(qi, exact):
        q_rows = pl.ds(pl.multiple_of(qi * tq, tq), tq)
        qcat = q_ref[0, q_rows, :]
        zero = jnp.zeros_like(qcat)
        qh = (jnp.where(lane2 < HEAD_DIM, qcat, zero), jnp.where(lane2 >= HEAD_DIM, qcat, zero))
        n_visit = jnp.maximum(reach(qi, 0), reach(qi, 1))
        single_tile(qh, qi, "diagonal")

        if exact:
            def step(t, carry):
                single_tile(qh, qi - 1 - t, "exact")
                return carry

            lax.fori_loop(0, n_visit, step, 0)
        else:
            def run(t, carry):
                tile_run(qh, qi - 1 - FOX_TRIP * t)
                return carry

            def leftover(t, carry):
                single_tile(qh, qi - n_visit + t, "lazy")
                return carry

            lax.fori_loop(0, n_visit // FOX_TRIP, run, 0)
            lax.fori_loop(0, n_visit % FOX_TRIP, leftover, 0)

        outs = []
        for head in range(2):
            acc, lp = acc_ref[head], l_ref[head]
            outs.append(acc / jnp.sum(lp, axis=-1, keepdims=True))
            if not exact:
                finite = (jnp.abs(acc) <= F32_BIG) & (lp <= F32_BIG)
                bad_ref[...] = jnp.maximum(bad_ref[...], jnp.where(finite, 0.0, 1.0))
        o_ref[0, q_rows, :] = jnp.where(head_a, outs[0], outs[1]).astype(o_ref.dtype)

    def sweep(exact):
        def body(qi, carry):
            query_tile(qi, exact)
            return carry

        lax.fori_loop(0, n_tiles, body, 0)

    bad_ref[...] = jnp.zeros_like(bad_ref)
    sweep(exact=False)

    @pl.when(jnp.max(bad_ref[...]) > 0.0)
    def _():
        sweep(exact=True)


def _fox(stats, qf, kf, vf, tq):
    b, s, _ = qf.shape
    smem = pl.BlockSpec(memory_space=pltpu.SMEM)
    return pl.pallas_call(
        functools.partial(_fox_body, tq=tq),
        grid=(b, N_PAIR),
        in_specs=[smem, smem, smem, smem,
                  pl.BlockSpec((1, s, 2 * LANES), lambda i, p: (i, 0, p)),
                  pl.BlockSpec((1, s, 2 * LANES), lambda i, p: (i, 0, p)),
                  pl.BlockSpec((1, s, LANES), lambda i, p: (i, 0, p))],
        out_specs=pl.BlockSpec((1, s, LANES), lambda i, p: (i, 0, p)),
        out_shape=jax.ShapeDtypeStruct((b, s, W_ATT), jnp.bfloat16),
        scratch_shapes=[pltpu.VMEM((4, tq, tq), jnp.float32),
                        pltpu.VMEM((4, tq, tq), jnp.bfloat16),
                        pltpu.VMEM((2, tq, LANES), jnp.float32),
                        pltpu.VMEM((2, tq, LANES), jnp.float32),
                        pltpu.VMEM((2, tq, LANES), jnp.float32),
                        pltpu.VMEM((tq, LANES), jnp.float32)],
        compiler_params=pltpu.CompilerParams(
            dimension_semantics=("arbitrary", "arbitrary"), vmem_limit_bytes=VMEM_LIMIT),
        name="fox_attention",
    )(*stats, qf, kf, vf)


def _band_bias_body(g_ref, o_ref):
    width = g_ref.shape[-1]
    base = jnp.broadcast_to(g_ref[0], (Q_BLOCK, width))
    toeplitz = pltpu.roll(base, 0, 1, stride=1, stride_axis=0)[:, :BAND]
    qrow = lax.broadcasted_iota(jnp.int32, (Q_BLOCK, BAND), 0)
    kcol = lax.broadcasted_iota(jnp.int32, (Q_BLOCK, BAND), 1)
    cq = qrow // CHUNK
    ck = kcol // CHUNK - LEFT_CHUNKS
    valid = (ck <= cq) & (ck >= cq - LEFT_CHUNKS)
    o_ref[0] = jnp.where(valid, toeplitz, NEG)


def _band_bias(g_ext):
    h, _, width = g_ext.shape
    return pl.pallas_call(
        _band_bias_body,
        grid=(h,),
        in_specs=[pl.BlockSpec((1, 1, width), lambda i: (i, 0, 0))],
        out_specs=pl.BlockSpec((1, Q_BLOCK, BAND), lambda i: (i, 0, 0)),
        out_shape=jax.ShapeDtypeStruct((h, Q_BLOCK, BAND), jnp.float32),
        name="band_bias",
    )(g_ext)


def _chunk_body(q_ref, k_ref, v_ref, bias_ref, o_ref, kpad_ref, vpad_ref, s_ref, p_ref, l_ref, *, tq):
    qi = pl.program_id(2)
    s_len = k_ref.shape[1]

    @pl.when(qi == 0)
    def _():
        zeros = jnp.zeros((PAD, LANES), kpad_ref.dtype)
        kpad_ref[:PAD, :] = zeros
        vpad_ref[:PAD, :] = zeros
        kpad_ref[PAD:PAD + s_len, :] = k_ref[0]
        vpad_ref[PAD:PAD + s_len, :] = v_ref[0]

    head_a = lax.broadcasted_iota(jnp.int32, (1, LANES), 1) < HEAD_DIM
    kcol = lax.broadcasted_iota(jnp.int32, (1, BAND), 1)

    def offsets(blk):
        r0 = pl.multiple_of(blk * Q_BLOCK, Q_BLOCK)
        return r0, pl.multiple_of(qi * tq + r0, Q_BLOCK)

    def scores(slot, blk):
        r0, p0 = offsets(blk)
        q2 = q_ref[0, pl.ds(r0, Q_BLOCK), :]
        zero = jnp.zeros_like(q2)
        qs = jnp.concatenate([jnp.where(head_a, q2, zero), jnp.where(head_a, zero, q2)], axis=0)
        s_ref[slot] = _dot_nt(qs, kpad_ref[pl.ds(p0, BAND), :])

    def softmax(slot, blk, masked):
        _, p0 = offsets(blk)
        for r in range(0, 2 * Q_BLOCK, CHUNK_ROWS):
            rows = slice(r, r + CHUNK_ROWS)
            s = s_ref[slot, rows, :] + bias_ref[0, rows, :]
            if masked:
                s = jnp.where(kcol + p0 >= PAD, s, NEG)
            p = jnp.exp(s - jnp.max(s, axis=-1, keepdims=True))
            l_ref[slot, rows, :] = jnp.broadcast_to(jnp.sum(p, axis=-1, keepdims=True), (CHUNK_ROWS, LANES))
            p_ref[slot, rows, :] = p.astype(p_ref.dtype)

    def values(slot, blk):
        r0, p0 = offsets(blk)
        o = _dot(p_ref[slot], vpad_ref[pl.ds(p0, BAND), :]) / l_ref[slot]
        o_ref[0, pl.ds(r0, Q_BLOCK), :] = jnp.where(head_a, o[:Q_BLOCK], o[Q_BLOCK:]).astype(o_ref.dtype)

    def group(i, masked):
        first = i * CHUNK_GROUP
        scores(0, first)
        for n in range(CHUNK_GROUP):
            if n + 1 < CHUNK_GROUP:
                scores((n + 1) % 2, first + n + 1)
            softmax(n % 2, first + n, masked)
            values(n % 2, first + n)

    def masked_group(i, carry):
        group(i, True)
        return carry

    def plain_group(i, carry):
        group(i, False)
        return carry

    group_rows = Q_BLOCK * CHUNK_GROUP
    n_groups = tq // group_rows
    n_masked = jnp.clip((PAD - qi * tq + group_rows - 1) // group_rows, 0, n_groups)
    lax.fori_loop(0, n_masked, masked_group, 0)
    lax.fori_loop(n_masked, n_groups, plain_group, 0)


def _chunk_attention(qc, kc, vc, bias, tq):
    b, s, _ = qc.shape
    assert tq % (CHUNK_GROUP * Q_BLOCK) == 0
    return pl.pallas_call(
        functools.partial(_chunk_body, tq=tq),
        grid=(b, N_PAIR, s // tq),
        in_specs=[pl.BlockSpec((1, tq, LANES), lambda i, p, j: (i, j, p)),
                  pl.BlockSpec((1, s, LANES), lambda i, p, j: (i, 0, p)),
                  pl.BlockSpec((1, s, LANES), lambda i, p, j: (i, 0, p)),
                  pl.BlockSpec((1, 2 * Q_BLOCK, BAND), lambda i, p, j: (p, 0, 0))],
        out_specs=pl.BlockSpec((1, tq, LANES), lambda i, p, j: (i, j, p)),
        out_shape=jax.ShapeDtypeStruct((b, s, W_ATT), jnp.bfloat16),
        scratch_shapes=[pltpu.VMEM((PAD + s, LANES), jnp.bfloat16),
                        pltpu.VMEM((PAD + s, LANES), jnp.bfloat16),
                        pltpu.VMEM((2, 2 * Q_BLOCK, BAND), jnp.float32),
                        pltpu.VMEM((2, 2 * Q_BLOCK, BAND), jnp.bfloat16),
                        pltpu.VMEM((2, 2 * Q_BLOCK, LANES), jnp.float32)],
        compiler_params=pltpu.CompilerParams(
            dimension_semantics=("arbitrary", "arbitrary", "arbitrary"), vmem_limit_bytes=VMEM_LIMIT),
        name="chunk_attention",
    )(qc, kc, vc, bias)


def _out_ffn_body(x_ref, oa_ref, ob_ref, gate_ref, wa_ref, wb_ref, wo_ref, g2_ref, wu_ref, wd_ref, gf_ref,
                  o_ref, *, ff_chunk, final):
    d = x_ref.shape[-1]
    ya = _dot(oa_ref[0], wa_ref[...])
    yb = _dot(ob_ref[0], wb_ref[...])
    ga = gate_ref[0, :, :d].astype(jnp.float32)
    gb = gate_ref[0, :, d:].astype(jnp.float32)
    merged = (ga * ya + gb * yb).astype(jnp.bfloat16)
    x1 = x_ref[0] + _dot(merged, wo_ref[...])
    h2 = _rms(x1, g2_ref[...]).astype(jnp.bfloat16)
    acc = x1
    for c0 in range(0, wu_ref.shape[1], ff_chunk):
        u = jnp.maximum(_dot(h2, wu_ref[:, c0:c0 + ff_chunk]), 0.0)
        acc = acc + _dot((u * u).astype(jnp.bfloat16), wd_ref[c0:c0 + ff_chunk, :])
    if final:
        acc = _rms(acc, gf_ref[...])
    o_ref[0] = acc


def _out_ffn(x, oa, ob, gates, wa, wb, wo, g2, wu, wd, gf, tm, final):
    b, s, d = x.shape
    tok = lambda w: pl.BlockSpec((1, tm, w), lambda i, j: (i, j, 0))
    return pl.pallas_call(
        functools.partial(_out_ffn_body, ff_chunk=min(1024, wu.shape[1]), final=final),
        grid=(b, s // tm),
        in_specs=[tok(d), tok(W_ATT), tok(W_ATT), tok(2 * d)]
                 + [_const_spec(a.shape) for a in (wa, wb, wo, g2, wu, wd, gf)],
        out_specs=tok(d),
        out_shape=jax.ShapeDtypeStruct((b, s, d), jnp.float32),
        compiler_params=pltpu.CompilerParams(
            dimension_semantics=("arbitrary", "arbitrary"), vmem_limit_bytes=VMEM_LIMIT),
        name="out_ffn",
    )(x, oa, ob, gates, wa, wb, wo, g2, wu, wd, gf)


def _aug_placement():
    eq = [[0.0] * W_ATT for _ in range(LANES)]
    ek = [[0.0] * W_ATT for _ in range(LANES)]
    one = 3 * H_FOX
    for h in range(H_FOX):
        for piece in range(3):
            eq[piece * H_FOX + h][HEAD_DIM * h + piece] = 1.0
            eq[one][HEAD_DIM * h + 3 + piece] = 1.0
            ek[one][HEAD_DIM * h + piece] = 1.0
            ek[piece * H_FOX + h][HEAD_DIM * h + 3 + piece] = -1.0
    return jnp.array(eq, jnp.bfloat16), jnp.array(ek, jnp.bfloat16)


def kernel(x, norm1, w_in, forget_bias, rel_bias, w_branch_a, w_branch_b, w_out, norm2, w_up, w_down, final_norm):
    b, s, d = x.shape
    depth = w_in.shape[0]
    bf16 = jnp.bfloat16
    tm = min(512, s)
    tqc = min(1024, s)
    eq, ek = _aug_placement()
    head_of_col = lax.broadcasted_iota(jnp.int32, (W_ATT, LANES), 0) // HEAD_DIM
    hsum = (head_of_col == lax.broadcasted_iota(jnp.int32, (W_ATT, LANES), 1)).astype(bf16)
    gf = final_norm.reshape(1, d)
    o = 3 * W_ATT + H_FOX
    for l in range(depth):
        w = w_in[l]
        wq = (w[:, :W_ATT] * SCALE).astype(bf16)
        wk = w[:, W_ATT:2 * W_ATT].astype(bf16)
        wv = w[:, 2 * W_ATT:3 * W_ATT].astype(bf16)
        wf = jnp.pad(w[:, 3 * W_ATT:o], ((0, 0), (0, LANES - H_FOX))).astype(bf16)
        wc = jnp.concatenate([w[:, o:o + W_ATT] * SCALE, w[:, o + W_ATT:o + 3 * W_ATT]], axis=1).astype(bf16)
        wg = w[:, o + 3 * W_ATT:].astype(bf16)
        bf = jnp.pad(forget_bias[l], (0, LANES - H_FOX)).reshape(1, LANES)
        qf, kf, vf, qc, kc, vc, gates, stat = _in_proj(
            x, norm1[l].reshape(1, d), wq, wk, wv, wc, wg, wf, bf, eq, ek, hsum, tm)

        stats = stat[:, :, :4, :H_FOX].transpose(2, 0, 3, 1).reshape(4, -1)
        o_a = _fox(tuple(stats), qf, kf, vf, tm)

        far = rel_bias[l][:, 2 * MAX_REL:]
        g_ext = jnp.concatenate([jnp.broadcast_to(far, (H_CHK, PAD - MAX_REL + 1)),
                                 rel_bias[l][:, 2 * MAX_REL - 1:0:-1],
                                 jnp.broadcast_to(far, (H_CHK, Q_BLOCK))], axis=1)
        bias = _band_bias(g_ext.reshape(H_CHK, 1, -1)).reshape(N_PAIR, 2 * Q_BLOCK, BAND)
        o_b = _chunk_attention(qc, kc, vc, bias, tqc)

        x = _out_ffn(x, o_a, o_b, gates, w_branch_a[l].astype(bf16), w_branch_b[l].astype(bf16),
                     w_out[l].astype(bf16), norm2[l].reshape(1, d), w_up[l].astype(bf16),
                     w_down[l].astype(bf16), gf, tm, final=(l == depth - 1))
    return x
```

```python
import functools
import math

import jax
import jax.numpy as jnp
from jax import lax
from jax.experimental import pallas as pl
from jax.experimental.pallas import tpu as pltpu

HEAD_DIM = 64
H_FOX = 8
H_CHK = 8
N_PAIR = 4
W_ATT = H_FOX * HEAD_DIM
CHUNK = 64
Q_BLOCK = 128
LEFT_CHUNKS = 8
PAD = LEFT_CHUNKS * CHUNK
BAND = PAD + Q_BLOCK
MAX_REL = 128
EPS = 1e-6
NEG = -1e30
LANES = 128
SCALE = 1.0 / math.sqrt(HEAD_DIM)
F32_BIG = 3.0e38
FOX_ROWS = 32
FOX_TRIP = 4
CHUNK_ROWS = 32
CHUNK_GROUP = 8
SKIP_MARGIN = 30.0
NORM_SLACK = 1.02
VMEM_LIMIT = 60 * 1024 * 1024

_NT = (((1,), (1,)), ((), ()))


def _dot(a, b):
    return jnp.dot(a, b, preferred_element_type=jnp.float32)


def _dot_nt(a, b):
    return lax.dot_general(a, b, _NT, preferred_element_type=jnp.float32)


def _rms(x, g):
    ms = jnp.mean(x * x, axis=-1, keepdims=True)
    return x * lax.rsqrt(ms + EPS) * g


def _const_spec(shape):
    nd = len(shape)
    return pl.BlockSpec(shape, lambda *_: (0,) * nd, pipeline_mode=pl.Buffered(1))


def _in_proj_body(x_ref, g_ref, wq_ref, wk_ref, wv_ref, wc_ref, wg_ref, wf_ref, bf_ref, eq_ref, ek_ref, hsum_ref,
                  qf_ref, kf_ref, vf_ref, qc_ref, kc_ref, vc_ref, gate_ref, stat_ref, carry_ref):
    tm = x_ref.shape[1]

    @pl.when(pl.program_id(1) == 0)
    def _():
        carry_ref[...] = jnp.zeros_like(carry_ref)

    h = _rms(x_ref[0], g_ref[...]).astype(jnp.bfloat16)

    lane = lax.broadcasted_iota(jnp.int32, (tm, LANES), 1)
    row = lax.broadcasted_iota(jnp.int32, (tm, LANES), 0)
    logf = jax.nn.log_sigmoid(_dot(h, wf_ref[...]) + bf_ref[...])
    c = jnp.where(lane < H_FOX, logf, 0.0)
    k = 1
    while k < tm:
        c = c + jnp.where(row >= k, pltpu.roll(c, k, 0), 0.0)
        k *= 2
    c = c + carry_ref[...]
    carry_ref[...] = c[tm - 1:tm, :]

    hi = c.astype(jnp.bfloat16).astype(jnp.float32)
    r1 = c - hi
    mid = r1.astype(jnp.bfloat16).astype(jnp.float32)
    lo = (r1 - mid).astype(jnp.bfloat16).astype(jnp.float32)
    pieces = hi + pltpu.roll(mid, H_FOX, 1) + pltpu.roll(lo, 2 * H_FOX, 1)
    pieces = jnp.where(lane == 3 * H_FOX, 1.0, pieces).astype(jnp.bfloat16)
    aug_q = _dot(pieces, eq_ref[...]).astype(jnp.bfloat16)
    aug_k = _dot(pieces, ek_ref[...]).astype(jnp.bfloat16)

    q = _dot(h, wq_ref[...]).astype(jnp.bfloat16)
    kk = _dot(h, wk_ref[...]).astype(jnp.bfloat16)
    for p in range(N_PAIR):
        src = slice(p * LANES, (p + 1) * LANES)
        qf_ref[0, :, 2 * p * LANES:(2 * p + 1) * LANES] = q[:, src]
        qf_ref[0, :, (2 * p + 1) * LANES:(2 * p + 2) * LANES] = aug_q[:, src]
        kf_ref[0, :, 2 * p * LANES:(2 * p + 1) * LANES] = kk[:, src]
        kf_ref[0, :, (2 * p + 1) * LANES:(2 * p + 2) * LANES] = aug_k[:, src]
    vf_ref[0] = _dot(h, wv_ref[...]).astype(jnp.bfloat16)

    def max_norm(t):
        t = t.astype(jnp.float32)
        sq = _dot((t * t).astype(jnp.bfloat16), hsum_ref[...])
        return jnp.sqrt(jnp.max(sq, axis=0, keepdims=True))

    sub = lax.broadcasted_iota(jnp.int32, (8, LANES), 0)
    stat = jnp.where(sub == 0, c[0:1, :], 0.0)
    stat = jnp.where(sub == 1, c[tm - 1:tm, :], stat)
    stat = jnp.where(sub == 2, max_norm(q), stat)
    stat_ref[0, 0] = jnp.where(sub == 3, max_norm(kk), stat)

    pc = _dot(h, wc_ref[...])
    qc_ref[0] = pc[:, :W_ATT].astype(jnp.bfloat16)
    kc_ref[0] = pc[:, W_ATT:2 * W_ATT].astype(jnp.bfloat16)
    vc_ref[0] = pc[:, 2 * W_ATT:].astype(jnp.bfloat16)
    gate_ref[0] = jax.nn.sigmoid(_dot(h, wg_ref[...])).astype(jnp.bfloat16)


def _in_proj(x, g, wq, wk, wv, wc, wg, wf, bf, eq, ek, hsum, tm):
    b, s, d = x.shape
    tok = lambda w: pl.BlockSpec((1, tm, w), lambda i, j: (i, j, 0))
    bf16 = jnp.bfloat16
    out_shape = [jax.ShapeDtypeStruct((b, s, w), bf16)
                 for w in (2 * W_ATT, 2 * W_ATT, W_ATT, W_ATT, W_ATT, W_ATT, 2 * d)]
    out_shape.append(jax.ShapeDtypeStruct((b, s // tm, 8, LANES), jnp.float32))
    consts = (g, wq, wk, wv, wc, wg, wf, bf, eq, ek, hsum)
    return pl.pallas_call(
        _in_proj_body,
        grid=(b, s // tm),
        in_specs=[tok(d)] + [_const_spec(a.shape) for a in consts],
        out_specs=[tok(sh.shape[-1]) for sh in out_shape[:-1]]
                  + [pl.BlockSpec((1, 1, 8, LANES), lambda i, j: (i, j, 0, 0))],
        out_shape=out_shape,
        scratch_shapes=[pltpu.VMEM((1, LANES), jnp.float32)],
        compiler_params=pltpu.CompilerParams(
            dimension_semantics=("arbitrary", "arbitrary"), vmem_limit_bytes=VMEM_LIMIT),
        name="in_proj",
    )(x, *consts)


def _fox_body(cfirst_ref, clast_ref, qnorm_ref, knorm_ref, q_ref, k_ref, v_ref, o_ref,
              s_ref, p_ref, m_ref, l_ref, acc_ref, bad_ref, *, tq):
    n_tiles = q_ref.shape[1] // tq
    lane2 = lax.broadcasted_iota(jnp.int32, (1, 2 * LANES), 1) % LANES
    head_a = lax.broadcasted_iota(jnp.int32, (1, LANES), 1) < HEAD_DIM

    def reach(qi, head):
        base = (pl.program_id(0) * H_FOX + 2 * pl.program_id(1) + head) * n_tiles
        kmax = lax.fori_loop(0, qi + 1, lambda t, mx: jnp.maximum(mx, knorm_ref[base + t]), 0.0)
        bound = cfirst_ref[base + qi] + 2.0 * NORM_SLACK * qnorm_ref[base + qi] * kmax + SKIP_MARGIN
        return lax.fori_loop(
            0, qi, lambda t, n: n + jnp.where(bound - clast_ref[base + t] > 0.0, 1, 0), 0)

    def key_rows(j):
        return pl.ds(pl.multiple_of(j * tq, tq), tq)

    def scores(slot, q_head, j):
        s_ref[slot] = _dot_nt(q_head, k_ref[0, key_rows(j), :])

    def softmax(slot, head, mode):
        for r in range(0, tq, FOX_ROWS):
            rows = slice(r, r + FOX_ROWS)
            s = s_ref[slot, rows, :]
            if mode == "diagonal":
                row = lax.broadcasted_iota(jnp.int32, (FOX_ROWS, tq), 0) + r
                col = lax.broadcasted_iota(jnp.int32, (FOX_ROWS, tq), 1)
                s = jnp.where(col <= row, s, NEG)
                m = jnp.broadcast_to(jnp.max(s, axis=-1, keepdims=True), (FOX_ROWS, LANES))
                m_ref[head, rows, :] = m
            elif mode == "exact":
                m_old = m_ref[head, rows, :]
                m = jnp.maximum(m_old, jnp.max(s, axis=-1, keepdims=True))
                alpha = jnp.exp(m_old - m)
                m_ref[head, rows, :] = m
            else:
                m = m_ref[head, rows, :]
            lsum = None
            for g in range(tq // LANES):
                cols = slice(g * LANES, (g + 1) * LANES)
                pg = jnp.exp(s[:, cols] - m)
                p_ref[slot, rows, cols] = pg.astype(p_ref.dtype)
                lsum = pg if lsum is None else lsum + pg
            if mode == "diagonal":
                l_ref[head, rows, :] = lsum
            elif mode == "exact":
                l_ref[head, rows, :] = l_ref[head, rows, :] * alpha + lsum
                acc_ref[head, rows, :] = acc_ref[head, rows, :] * alpha
            else:
                l_ref[head, rows, :] = l_ref[head, rows, :] + lsum

    def values(slot, head, j, mode):
        pv = _dot(p_ref[slot], v_ref[0, key_rows(j), :])
        acc_ref[head] = pv if mode == "diagonal" else acc_ref[head] + pv

    def single_tile(qh, j, mode):
        for head in range(2):
            scores(head, qh[head], j)
        for head in range(2):
            softmax(head, head, mode)
            values(head, head, j, mode)

    def tile_run(qh, j, count):
        units = [(j - i, head) for i in range(count) for head in range(2)]
        n_slots = s_ref.shape[0]
        for n in range(2):
            scores(n, qh[units[n][1]], units[n][0])
        for n, (tile, head) in enumerate(units):
            softmax(n % n_slots, head, "lazy")
            if n + 2 < len(units):
                scores((n + 2) % n_slots, qh[units[n + 2][1]], units[n + 2][0])
            values(n % n_slots, head, tile, "lazy")

    def query_tile(qi, exact):
        q_rows = pl.ds(pl.multiple_of(qi * tq, tq), tq)
        qcat = q_ref[0, q_rows, :]
        zero = jnp.zeros_like(qcat)
        qh = (jnp.where(lane2 < HEAD_DIM, qcat, zero), jnp.where(lane2 >= HEAD_DIM, qcat, zero))
        n_visit = jnp.maximum(reach(qi, 0), reach(qi, 1))
        single_tile(qh, qi, "diagonal")

        if exact:
            def step(t, carry):
                single_tile(qh, qi - 1 - t, "exact")
                return carry

            lax.fori_loop(0, n_visit, step, 0)
        else:
            def run(t, carry):
                tile_run(qh, qi - 1 - FOX_TRIP * t, FOX_TRIP)
                return carry

            lax.fori_loop(0, n_visit // FOX_TRIP, run, 0)
            done = (n_visit // FOX_TRIP) * FOX_TRIP
            count = FOX_TRIP // 2
            while count >= 1:
                first = qi - 1 - done

                @pl.when((n_visit - done) >= count)
                def _(first=first, count=count):
                    tile_run(qh, first, count)

                done = jnp.where((n_visit - done) >= count, done + count, done)
                count //= 2

        outs = []
        for head in range(2):
            acc, lp = acc_ref[head], l_ref[head]
            outs.append(acc / jnp.sum(lp, axis=-1, keepdims=True))
            if not exact:
                finite = (jnp.abs(acc) <= F32_BIG) & (lp <= F32_BIG)
                bad_ref[...] = jnp.maximum(bad_ref[...], jnp.where(finite, 0.0, 1.0))
        o_ref[0, q_rows, :] = jnp.where(head_a, outs[0], outs[1]).astype(o_ref.dtype)

    def sweep(exact):
        def body(qi, carry):
            query_tile(qi, exact)
            return carry

        lax.fori_loop(0, n_tiles, body, 0)

    bad_ref[...] = jnp.zeros_like(bad_ref)
    sweep(exact=False)

    @pl.when(jnp.max(bad_ref[...]) > 0.0)
    def _():
        sweep(exact=True)


def _fox(stats, qf, kf, vf, tq):
    b, s, _ = qf.shape
    smem = pl.BlockSpec(memory_space=pltpu.SMEM)
    return pl.pallas_call(
        functools.partial(_fox_body, tq=tq),
        grid=(b, N_PAIR),
        in_specs=[smem, smem, smem, smem,
                  pl.BlockSpec((1, s, 2 * LANES), lambda i, p: (i, 0, p)),
                  pl.BlockSpec((1, s, 2 * LANES), lambda i, p: (i, 0, p)),
                  pl.BlockSpec((1, s, LANES), lambda i, p: (i, 0, p))],
        out_specs=pl.BlockSpec((1, s, LANES), lambda i, p: (i, 0, p)),
        out_shape=jax.ShapeDtypeStruct((b, s, W_ATT), jnp.bfloat16),
        scratch_shapes=[pltpu.VMEM((4, tq, tq), jnp.float32),
                        pltpu.VMEM((4, tq, tq), jnp.bfloat16),
                        pltpu.VMEM((2, tq, LANES), jnp.float32),
                        pltpu.VMEM((2, tq, LANES), jnp.float32),
                        pltpu.VMEM((2, tq, LANES), jnp.float32),
                        pltpu.VMEM((tq, LANES), jnp.float32)],
        compiler_params=pltpu.CompilerParams(
            dimension_semantics=("arbitrary", "arbitrary"), vmem_limit_bytes=VMEM_LIMIT),
        name="fox_attention",
    )(*stats, qf, kf, vf)


def _band_bias_body(g_ref, o_ref):
    width = g_ref.shape[-1]
    base = jnp.broadcast_to(g_ref[0], (Q_BLOCK, width))
    toeplitz = pltpu.roll(base, 0, 1, stride=1, stride_axis=0)[:, :BAND]
    qrow = lax.broadcasted_iota(jnp.int32, (Q_BLOCK, BAND), 0)
    kcol = lax.broadcasted_iota(jnp.int32, (Q_BLOCK, BAND), 1)
    cq = qrow // CHUNK
    ck = kcol // CHUNK - LEFT_CHUNKS
    valid = (ck <= cq) & (ck >= cq - LEFT_CHUNKS)
    o_ref[0] = jnp.where(valid, toeplitz, NEG)


def _band_bias(g_ext):
    h, _, width = g_ext.shape
    return pl.pallas_call(
        _band_bias_body,
        grid=(h,),
        in_specs=[pl.BlockSpec((1, 1, width), lambda i: (i, 0, 0))],
        out_specs=pl.BlockSpec((1, Q_BLOCK, BAND), lambda i: (i, 0, 0)),
        out_shape=jax.ShapeDtypeStruct((h, Q_BLOCK, BAND), jnp.float32),
        name="band_bias",
    )(g_ext)


def _chunk_body(q_ref, k_ref, v_ref, bias_ref, o_ref, kpad_ref, vpad_ref, s_ref, p_ref, l_ref, *, tq):
    qi = pl.program_id(2)
    s_len = k_ref.shape[1]

    @pl.when(qi == 0)
    def _():
        zeros = jnp.zeros((PAD, LANES), kpad_ref.dtype)
        kpad_ref[:PAD, :] = zeros
        vpad_ref[:PAD, :] = zeros
        kpad_ref[PAD:PAD + s_len, :] = k_ref[0]
        vpad_ref[PAD:PAD + s_len, :] = v_ref[0]

    head_a = lax.broadcasted_iota(jnp.int32, (1, LANES), 1) < HEAD_DIM
    kcol = lax.broadcasted_iota(jnp.int32, (1, BAND), 1)

    def offsets(blk):
        r0 = pl.multiple_of(blk * Q_BLOCK, Q_BLOCK)
        return r0, pl.multiple_of(qi * tq + r0, Q_BLOCK)

    def scores(slot, blk):
        r0, p0 = offsets(blk)
        q2 = q_ref[0, pl.ds(r0, Q_BLOCK), :]
        zero = jnp.zeros_like(q2)
        qs = jnp.concatenate([jnp.where(head_a, q2, zero), jnp.where(head_a, zero, q2)], axis=0)
        s_ref[slot] = _dot_nt(qs, kpad_ref[pl.ds(p0, BAND), :])

    def softmax(slot, blk, masked):
        _, p0 = offsets(blk)
        for r in range(0, 2 * Q_BLOCK, CHUNK_ROWS):
            rows = slice(r, r + CHUNK_ROWS)
            s = s_ref[slot, rows, :] + bias_ref[0, rows, :]
            if masked:
                s = jnp.where(kcol + p0 >= PAD, s, NEG)
            p = jnp.exp(s - jnp.max(s, axis=-1, keepdims=True))
            l_ref[slot, rows, :] = jnp.broadcast_to(jnp.sum(p, axis=-1, keepdims=True), (CHUNK_ROWS, LANES))
            p_ref[slot, rows, :] = p.astype(p_ref.dtype)

    def values(slot, blk):
        r0, p0 = offsets(blk)
        o = _dot(p_ref[slot], vpad_ref[pl.ds(p0, BAND), :]) / l_ref[slot]
        o_ref[0, pl.ds(r0, Q_BLOCK), :] = jnp.where(head_a, o[:Q_BLOCK], o[Q_BLOCK:]).astype(o_ref.dtype)

    def group(i, masked):
        first = i * CHUNK_GROUP
        scores(0, first)
        for n in range(CHUNK_GROUP):
            if n + 1 < CHUNK_GROUP:
                scores((n + 1) % 2, first + n + 1)
            softmax(n % 2, first + n, masked)
            values(n % 2, first + n)

    def masked_group(i, carry):
        group(i, True)
        return carry

    def plain_group(i, carry):
        group(i, False)
        return carry

    group_rows = Q_BLOCK * CHUNK_GROUP
    n_groups = tq // group_rows
    n_masked = jnp.clip((PAD - qi * tq + group_rows - 1) // group_rows, 0, n_groups)
    lax.fori_loop(0, n_masked, masked_group, 0)
    lax.fori_loop(n_masked, n_groups, plain_group, 0)


def _chunk_attention(qc, kc, vc, bias, tq):
    b, s, _ = qc.shape
    assert tq % (CHUNK_GROUP * Q_BLOCK) == 0
    return pl.pallas_call(
        functools.partial(_chunk_body, tq=tq),
        grid=(b, N_PAIR, s // tq),
        in_specs=[pl.BlockSpec((1, tq, LANES), lambda i, p, j: (i, j, p)),
                  pl.BlockSpec((1, s, LANES), lambda i, p, j: (i, 0, p)),
                  pl.BlockSpec((1, s, LANES), lambda i, p, j: (i, 0, p)),
                  pl.BlockSpec((1, 2 * Q_BLOCK, BAND), lambda i, p, j: (p, 0, 0))],
        out_specs=pl.BlockSpec((1, tq, LANES), lambda i, p, j: (i, j, p)),
        out_shape=jax.ShapeDtypeStruct((b, s, W_ATT), jnp.bfloat16),
        scratch_shapes=[pltpu.VMEM((PAD + s, LANES), jnp.bfloat16),
                        pltpu.VMEM((PAD + s, LANES), jnp.bfloat16),
                        pltpu.VMEM((2, 2 * Q_BLOCK, BAND), jnp.float32),
                        pltpu.VMEM((2, 2 * Q_BLOCK, BAND), jnp.bfloat16),
                        pltpu.VMEM((2, 2 * Q_BLOCK, LANES), jnp.float32)],
        compiler_params=pltpu.CompilerParams(
            dimension_semantics=("arbitrary", "arbitrary", "arbitrary"), vmem_limit_bytes=VMEM_LIMIT),
        name="chunk_attention",
    )(qc, kc, vc, bias)


def _out_ffn_body(x_ref, oa_ref, ob_ref, gate_ref, wa_ref, wb_ref, wo_ref, g2_ref, wu_ref, wd_ref, gf_ref,
                  o_ref, *, ff_chunk, final):
    d = x_ref.shape[-1]
    ya = _dot(oa_ref[0], wa_ref[...])
    yb = _dot(ob_ref[0], wb_ref[...])
    ga = gate_ref[0, :, :d].astype(jnp.float32)
    gb = gate_ref[0, :, d:].astype(jnp.float32)
    merged = (ga * ya + gb * yb).astype(jnp.bfloat16)
    x1 = x_ref[0] + _dot(merged, wo_ref[...])
    h2 = _rms(x1, g2_ref[...]).astype(jnp.bfloat16)
    acc = x1
    for c0 in range(0, wu_ref.shape[1], ff_chunk):
        u = jnp.maximum(_dot(h2, wu_ref[:, c0:c0 + ff_chunk]), 0.0)
        acc = acc + _dot((u * u).astype(jnp.bfloat16), wd_ref[c0:c0 + ff_chunk, :])
    if final:
        acc = _rms(acc, gf_ref[...])
    o_ref[0] = acc


def _out_ffn(x, oa, ob, gates, wa, wb, wo, g2, wu, wd, gf, tm, final):
    b, s, d = x.shape
    tok = lambda w: pl.BlockSpec((1, tm, w), lambda i, j: (i, j, 0))
    return pl.pallas_call(
        functools.partial(_out_ffn_body, ff_chunk=min(1024, wu.shape[1]), final=final),
        grid=(b, s // tm),
        in_specs=[tok(d), tok(W_ATT), tok(W_ATT), tok(2 * d)]
                 + [_const_spec(a.shape) for a in (wa, wb, wo, g2, wu, wd, gf)],
        out_specs=tok(d),
        out_shape=jax.ShapeDtypeStruct((b, s, d), jnp.float32),
        compiler_params=pltpu.CompilerParams(
            dimension_semantics=("arbitrary", "arbitrary"), vmem_limit_bytes=VMEM_LIMIT),
        name="out_ffn",
    )(x, oa, ob, gates, wa, wb, wo, g2, wu, wd, gf)


def _aug_placement():
    eq = [[0.0] * W_ATT for _ in range(LANES)]
    ek = [[0.0] * W_ATT for _ in range(LANES)]
    one = 3 * H_FOX
    for h in range(H_FOX):
        for piece in range(3):
            eq[piece * H_FOX + h][HEAD_DIM * h + piece] = 1.0
            eq[one][HEAD_DIM * h + 3 + piece] = 1.0
            ek[one][HEAD_DIM * h + piece] = 1.0
            ek[piece * H_FOX + h][HEAD_DIM * h + 3 + piece] = -1.0
    return jnp.array(eq, jnp.bfloat16), jnp.array(ek, jnp.bfloat16)


def kernel(x, norm1, w_in, forget_bias, rel_bias, w_branch_a, w_branch_b, w_out, norm2, w_up, w_down, final_norm):
    b, s, d = x.shape
    depth = w_in.shape[0]
    bf16 = jnp.bfloat16
    tm = min(512, s)
    tqc = min(1024, s)
    eq, ek = _aug_placement()
    head_of_col = lax.broadcasted_iota(jnp.int32, (W_ATT, LANES), 0) // HEAD_DIM
    hsum = (head_of_col == lax.broadcasted_iota(jnp.int32, (W_ATT, LANES), 1)).astype(bf16)
    gf = final_norm.reshape(1, d)
    o = 3 * W_ATT + H_FOX
    for l in range(depth):
        w = w_in[l]
        wq = (w[:, :W_ATT] * SCALE).astype(bf16)
        wk = w[:, W_ATT:2 * W_ATT].astype(bf16)
        wv = w[:, 2 * W_ATT:3 * W_ATT].astype(bf16)
        wf = jnp.pad(w[:, 3 * W_ATT:o], ((0, 0), (0, LANES - H_FOX))).astype(bf16)
        wc = jnp.concatenate([w[:, o:o + W_ATT] * SCALE, w[:, o + W_ATT:o + 3 * W_ATT]], axis=1).astype(bf16)
        wg = w[:, o + 3 * W_ATT:].astype(bf16)
        bf = jnp.pad(forget_bias[l], (0, LANES - H_FOX)).reshape(1, LANES)
        qf, kf, vf, qc, kc, vc, gates, stat = _in_proj(
            x, norm1[l].reshape(1, d), wq, wk, wv, wc, wg, wf, bf, eq, ek, hsum, tm)

        stats = stat[:, :, :4, :H_FOX].transpose(2, 0, 3, 1).reshape(4, -1)
        o_a = _fox(tuple(stats), qf, kf, vf, tm)

        far = rel_bias[l][:, 2 * MAX_REL:]
        g_ext = jnp.concatenate([jnp.broadcast_to(far, (H_CHK, PAD - MAX_REL + 1)),
                                 rel_bias[l][:, 2 * MAX_REL - 1:0:-1],
                                 jnp.broadcast_to(far, (H_CHK, Q_BLOCK))], axis=1)
        bias = _band_bias(g_ext.reshape(H_CHK, 1, -1)).reshape(N_PAIR, 2 * Q_BLOCK, BAND)
        o_b = _chunk_attention(qc, kc, vc, bias, tqc)

        x = _out_ffn(x, o_a, o_b, gates, w_branch_a[l].astype(bf16), w_branch_b[l].astype(bf16),
                     w_out[l].astype(bf16), norm2[l].reshape(1, d), w_up[l].astype(bf16),
                     w_down[l].astype(bf16), gf, tm, final=(l == depth - 1))
    return x
```

```python
import functools
import math

import jax
import jax.numpy as jnp
from jax import lax
from jax.experimental import pallas as pl
from jax.experimental.pallas import tpu as pltpu

HEAD_DIM = 64
H_FOX = 8
H_CHK = 8
N_PAIR = 4
W_ATT = H_FOX * HEAD_DIM
CHUNK = 64
Q_BLOCK = 128
LEFT_CHUNKS = 8
PAD = LEFT_CHUNKS * CHUNK
BAND = PAD + Q_BLOCK
MAX_REL = 128
EPS = 1e-6
NEG = -1e30
LANES = 128
SCALE = 1.0 / math.sqrt(HEAD_DIM)
F32_BIG = 3.0e38
FOX_ROWS = 32
FOX_TRIP = 4
CHUNK_ROWS = 32
CHUNK_GROUP = 8
SKIP_MARGIN = 30.0
NORM_SLACK = 1.02
VMEM_LIMIT = 60 * 1024 * 1024

_NT = (((1,), (1,)), ((), ()))


def _dot(a, b):
    return jnp.dot(a, b, preferred_element_type=jnp.float32)


def _dot_nt(a, b):
    return lax.dot_general(a, b, _NT, preferred_element_type=jnp.float32)


def _rms(x, g):
    ms = jnp.mean(x * x, axis=-1, keepdims=True)
    return x * lax.rsqrt(ms + EPS) * g


def _const_spec(shape):
    nd = len(shape)
    return pl.BlockSpec(shape, lambda *_: (0,) * nd, pipeline_mode=pl.Buffered(1))


def _in_proj_body(x_ref, g_ref, wq_ref, wk_ref, wv_ref, wc_ref, wg_ref, wf_ref, bf_ref, eq_ref, ek_ref, hsum_ref,
                  qf_ref, kf_ref, vf_ref, qc_ref, kc_ref, vc_ref, gate_ref, stat_ref, carry_ref):
    tm = x_ref.shape[1]

    @pl.when(pl.program_id(1) == 0)
    def _():
        carry_ref[...] = jnp.zeros_like(carry_ref)

    h = _rms(x_ref[0], g_ref[...]).astype(jnp.bfloat16)

    lane = lax.broadcasted_iota(jnp.int32, (tm, LANES), 1)
    row = lax.broadcasted_iota(jnp.int32, (tm, LANES), 0)
    logf = jax.nn.log_sigmoid(_dot(h, wf_ref[...]) + bf_ref[...])
    c = jnp.where(lane < H_FOX, logf, 0.0)
    k = 1
    while k < tm:
        c = c + jnp.where(row >= k, pltpu.roll(c, k, 0), 0.0)
        k *= 2
    c = c + carry_ref[...]
    carry_ref[...] = c[tm - 1:tm, :]

    hi = c.astype(jnp.bfloat16).astype(jnp.float32)
    r1 = c - hi
    mid = r1.astype(jnp.bfloat16).astype(jnp.float32)
    lo = (r1 - mid).astype(jnp.bfloat16).astype(jnp.float32)
    pieces = hi + pltpu.roll(mid, H_FOX, 1) + pltpu.roll(lo, 2 * H_FOX, 1)
    pieces = jnp.where(lane == 3 * H_FOX, 1.0, pieces).astype(jnp.bfloat16)
    aug_q = _dot(pieces, eq_ref[...]).astype(jnp.bfloat16)
    aug_k = _dot(pieces, ek_ref[...]).astype(jnp.bfloat16)

    q = _dot(h, wq_ref[...]).astype(jnp.bfloat16)
    kk = _dot(h, wk_ref[...]).astype(jnp.bfloat16)
    for p in range(N_PAIR):
        src = slice(p * LANES, (p + 1) * LANES)
        qf_ref[0, :, 2 * p * LANES:(2 * p + 1) * LANES] = q[:, src]
        qf_ref[0, :, (2 * p + 1) * LANES:(2 * p + 2) * LANES] = aug_q[:, src]
        kf_ref[0, :, 2 * p * LANES:(2 * p + 1) * LANES] = kk[:, src]
        kf_ref[0, :, (2 * p + 1) * LANES:(2 * p + 2) * LANES] = aug_k[:, src]
    vf_ref[0] = _dot(h, wv_ref[...]).astype(jnp.bfloat16)

    def max_norm(t):
        t = t.astype(jnp.float32)
        sq = _dot((t * t).astype(jnp.bfloat16), hsum_ref[...])
        return jnp.sqrt(jnp.max(sq, axis=0, keepdims=True))

    sub = lax.broadcasted_iota(jnp.int32, (8, LANES), 0)
    stat = jnp.where(sub == 0, c[0:1, :], 0.0)
    stat = jnp.where(sub == 1, c[tm - 1:tm, :], stat)
    stat = jnp.where(sub == 2, max_norm(q), stat)
    stat_ref[0, 0] = jnp.where(sub == 3, max_norm(kk), stat)

    pc = _dot(h, wc_ref[...])
    qc_ref[0] = pc[:, :W_ATT].astype(jnp.bfloat16)
    kc_ref[0] = pc[:, W_ATT:2 * W_ATT].astype(jnp.bfloat16)
    vc_ref[0] = pc[:, 2 * W_ATT:].astype(jnp.bfloat16)
    gate_ref[0] = jax.nn.sigmoid(_dot(h, wg_ref[...])).astype(jnp.bfloat16)


def _in_proj(x, g, wq, wk, wv, wc, wg, wf, bf, eq, ek, hsum, tm):
    b, s, d = x.shape
    tok = lambda w: pl.BlockSpec((1, tm, w), lambda i, j: (i, j, 0))
    bf16 = jnp.bfloat16
    out_shape = [jax.ShapeDtypeStruct((b, s, w), bf16)
                 for w in (2 * W_ATT, 2 * W_ATT, W_ATT, W_ATT, W_ATT, W_ATT, 2 * d)]
    out_shape.append(jax.ShapeDtypeStruct((b, s // tm, 8, LANES), jnp.float32))
    consts = (g, wq, wk, wv, wc, wg, wf, bf, eq, ek, hsum)
    return pl.pallas_call(
        _in_proj_body,
        grid=(b, s // tm),
        in_specs=[tok(d)] + [_const_spec(a.shape) for a in consts],
        out_specs=[tok(sh.shape[-1]) for sh in out_shape[:-1]]
                  + [pl.BlockSpec((1, 1, 8, LANES), lambda i, j: (i, j, 0, 0))],
        out_shape=out_shape,
        scratch_shapes=[pltpu.VMEM((1, LANES), jnp.float32)],
        compiler_params=pltpu.CompilerParams(
            dimension_semantics=("arbitrary", "arbitrary"), vmem_limit_bytes=VMEM_LIMIT),
        name="in_proj",
    )(x, *consts)


def _fox_body(cfirst_ref, clast_ref, qnorm_ref, knorm_ref, q_ref, k_ref, v_ref, o_ref,
              s_ref, p_ref, m_ref, l_ref, acc_ref, bad_ref, *, tq):
    n_tiles = q_ref.shape[1] // tq
    lane2 = lax.broadcasted_iota(jnp.int32, (1, 2 * LANES), 1) % LANES
    head_a = lax.broadcasted_iota(jnp.int32, (1, LANES), 1) < HEAD_DIM

    def reach(qi, head):
        base = (pl.program_id(0) * H_FOX + 2 * pl.program_id(1) + head) * n_tiles
        kmax = lax.fori_loop(0, qi + 1, lambda t, mx: jnp.maximum(mx, knorm_ref[base + t]), 0.0)
        bound = cfirst_ref[base + qi] + 2.0 * NORM_SLACK * qnorm_ref[base + qi] * kmax + SKIP_MARGIN
        return lax.fori_loop(
            0, qi, lambda t, n: n + jnp.where(bound - clast_ref[base + t] > 0.0, 1, 0), 0)

    def key_rows(j):
        return pl.ds(pl.multiple_of(j * tq, tq), tq)

    def scores(slot, q_head, j):
        s_ref[slot] = _dot_nt(q_head, k_ref[0, key_rows(j), :])

    def softmax(slot, head, mode):
        for r in range(0, tq, FOX_ROWS):
            rows = slice(r, r + FOX_ROWS)
            s = s_ref[slot, rows, :]
            if mode in ("diagonal", "diagonal_self"):
                row = lax.broadcasted_iota(jnp.int32, (FOX_ROWS, tq), 0) + r
                col = lax.broadcasted_iota(jnp.int32, (FOX_ROWS, tq), 1)
                if mode == "diagonal":
                    s = jnp.where(col <= row, s, NEG)
                    m = jnp.max(s, axis=-1, keepdims=True)
                else:
                    own = slice(r // LANES * LANES, (r // LANES + 1) * LANES)
                    lane = lax.broadcasted_iota(jnp.int32, (FOX_ROWS, LANES), 1)
                    sub = lax.broadcasted_iota(jnp.int32, (FOX_ROWS, LANES), 0)
                    m = jnp.sum(jnp.where(lane == sub + r % LANES, s[:, own], 0.0), axis=-1, keepdims=True)
                    s = jnp.where(col <= row, s, NEG)
                m = jnp.broadcast_to(m, (FOX_ROWS, LANES))
                m_ref[head, rows, :] = m
            elif mode == "exact":
                m_old = m_ref[head, rows, :]
                m = jnp.maximum(m_old, jnp.max(s, axis=-1, keepdims=True))
                alpha = jnp.exp(m_old - m)
                m_ref[head, rows, :] = m
            else:
                m = m_ref[head, rows, :]
            lsum = None
            for g in range(tq // LANES):
                cols = slice(g * LANES, (g + 1) * LANES)
                pg = jnp.exp(s[:, cols] - m)
                p_ref[slot, rows, cols] = pg.astype(p_ref.dtype)
                lsum = pg if lsum is None else lsum + pg
            if mode in ("diagonal", "diagonal_self"):
                l_ref[head, rows, :] = lsum
            elif mode == "exact":
                l_ref[head, rows, :] = l_ref[head, rows, :] * alpha + lsum
                acc_ref[head, rows, :] = acc_ref[head, rows, :] * alpha
            else:
                l_ref[head, rows, :] = l_ref[head, rows, :] + lsum

    def values(slot, head, j, mode):
        pv = _dot(p_ref[slot], v_ref[0, key_rows(j), :])
        acc_ref[head] = pv if mode in ("diagonal", "diagonal_self") else acc_ref[head] + pv

    def single_tile(qh, j, mode):
        for head in range(2):
            scores(head, qh[head], j)
        for head in range(2):
            softmax(head, head, mode)
            values(head, head, j, mode)

    def tile_run(qh, j, count, starts_on_diagonal=False):
        units = [(j - i, head, "diagonal_self" if starts_on_diagonal and i == 0 else "lazy")
                 for i in range(count) for head in range(2)]
        n_slots = s_ref.shape[0]
        for n in range(2):
            scores(n, qh[units[n][1]], units[n][0])
        for n, (tile, head, mode) in enumerate(units):
            softmax(n % n_slots, head, mode)
            if n + 2 < len(units):
                scores((n + 2) % n_slots, qh[units[n + 2][1]], units[n + 2][0])
            values(n % n_slots, head, tile, mode)

    def query_tile(qi, exact):
        q_rows = pl.ds(pl.multiple_of(qi * tq, tq), tq)
        qcat = q_ref[0, q_rows, :]
        zero = jnp.zeros_like(qcat)
        qh = (jnp.where(lane2 < HEAD_DIM, qcat, zero), jnp.where(lane2 >= HEAD_DIM, qcat, zero))
        n_visit = jnp.maximum(reach(qi, 0), reach(qi, 1))

        if exact:
            single_tile(qh, qi, "diagonal")

            def step(t, carry):
                single_tile(qh, qi - 1 - t, "exact")
                return carry

            lax.fori_loop(0, n_visit, step, 0)
        else:
            total = n_visit + 1
            count, first_len = FOX_TRIP, 1
            while count >= 1:
                fits = total >= count
                if count < FOX_TRIP:
                    fits = fits & (total < 2 * count)

                @pl.when(fits)
                def _(count=count):
                    tile_run(qh, qi, count, starts_on_diagonal=True)

                first_len = jnp.where(fits, count, first_len)
                count //= 2

            def run(t, carry):
                tile_run(qh, qi - first_len - FOX_TRIP * t, FOX_TRIP)
                return carry

            n_full = (total - first_len) // FOX_TRIP
            lax.fori_loop(0, n_full, run, 0)
            done = first_len + n_full * FOX_TRIP
            count = FOX_TRIP // 2
            while count >= 1:
                fits = (total - done) >= count

                @pl.when(fits)
                def _(first=qi - done, count=count):
                    tile_run(qh, first, count)

                done = jnp.where(fits, done + count, done)
                count //= 2

        outs = []
        for head in range(2):
            acc, lp = acc_ref[head], l_ref[head]
            outs.append(acc / jnp.sum(lp, axis=-1, keepdims=True))
            if not exact:
                finite = (jnp.abs(acc) <= F32_BIG) & (lp <= F32_BIG)
                bad_ref[...] = jnp.maximum(bad_ref[...], jnp.where(finite, 0.0, 1.0))
        o_ref[0, q_rows, :] = jnp.where(head_a, outs[0], outs[1]).astype(o_ref.dtype)

    def sweep(exact):
        def body(qi, carry):
            query_tile(qi, exact)
            return carry

        lax.fori_loop(0, n_tiles, body, 0)

    bad_ref[...] = jnp.zeros_like(bad_ref)
    sweep(exact=False)

    @pl.when(jnp.max(bad_ref[...]) > 0.0)
    def _():
        sweep(exact=True)


def _fox(stats, qf, kf, vf, tq):
    b, s, _ = qf.shape
    smem = pl.BlockSpec(memory_space=pltpu.SMEM)
    return pl.pallas_call(
        functools.partial(_fox_body, tq=tq),
        grid=(b, N_PAIR),
        in_specs=[smem, smem, smem, smem,
                  pl.BlockSpec((1, s, 2 * LANES), lambda i, p: (i, 0, p)),
                  pl.BlockSpec((1, s, 2 * LANES), lambda i, p: (i, 0, p)),
                  pl.BlockSpec((1, s, LANES), lambda i, p: (i, 0, p))],
        out_specs=pl.BlockSpec((1, s, LANES), lambda i, p: (i, 0, p)),
        out_shape=jax.ShapeDtypeStruct((b, s, W_ATT), jnp.bfloat16),
        scratch_shapes=[pltpu.VMEM((4, tq, tq), jnp.float32),
                        pltpu.VMEM((4, tq, tq), jnp.bfloat16),
                        pltpu.VMEM((2, tq, LANES), jnp.float32),
                        pltpu.VMEM((2, tq, LANES), jnp.float32),
                        pltpu.VMEM((2, tq, LANES), jnp.float32),
                        pltpu.VMEM((tq, LANES), jnp.float32)],
        compiler_params=pltpu.CompilerParams(
            dimension_semantics=("arbitrary", "arbitrary"), vmem_limit_bytes=VMEM_LIMIT),
        name="fox_attention",
    )(*stats, qf, kf, vf)


def _band_bias_body(g_ref, o_ref):
    width = g_ref.shape[-1]
    base = jnp.broadcast_to(g_ref[0], (Q_BLOCK, width))
    toeplitz = pltpu.roll(base, 0, 1, stride=1, stride_axis=0)[:, :BAND]
    qrow = lax.broadcasted_iota(jnp.int32, (Q_BLOCK, BAND), 0)
    kcol = lax.broadcasted_iota(jnp.int32, (Q_BLOCK, BAND), 1)
    cq = qrow // CHUNK
    ck = kcol // CHUNK - LEFT_CHUNKS
    valid = (ck <= cq) & (ck >= cq - LEFT_CHUNKS)
    o_ref[0] = jnp.where(valid, toeplitz, NEG)


def _band_bias(g_ext):
    h, _, width = g_ext.shape
    return pl.pallas_call(
        _band_bias_body,
        grid=(h,),
        in_specs=[pl.BlockSpec((1, 1, width), lambda i: (i, 0, 0))],
        out_specs=pl.BlockSpec((1, Q_BLOCK, BAND), lambda i: (i, 0, 0)),
        out_shape=jax.ShapeDtypeStruct((h, Q_BLOCK, BAND), jnp.float32),
        name="band_bias",
    )(g_ext)


def _chunk_body(q_ref, k_ref, v_ref, bias_ref, o_ref, kpad_ref, vpad_ref, s_ref, p_ref, l_ref, *, tq):
    qi = pl.program_id(2)
    s_len = k_ref.shape[1]

    @pl.when(qi == 0)
    def _():
        zeros = jnp.zeros((PAD, LANES), kpad_ref.dtype)
        kpad_ref[:PAD, :] = zeros
        vpad_ref[:PAD, :] = zeros
        kpad_ref[PAD:PAD + s_len, :] = k_ref[0]
        vpad_ref[PAD:PAD + s_len, :] = v_ref[0]

    head_a = lax.broadcasted_iota(jnp.int32, (1, LANES), 1) < HEAD_DIM
    kcol = lax.broadcasted_iota(jnp.int32, (1, BAND), 1)

    def offsets(blk):
        r0 = pl.multiple_of(blk * Q_BLOCK, Q_BLOCK)
        return r0, pl.multiple_of(qi * tq + r0, Q_BLOCK)

    def scores(slot, blk):
        r0, p0 = offsets(blk)
        q2 = q_ref[0, pl.ds(r0, Q_BLOCK), :]
        zero = jnp.zeros_like(q2)
        qs = jnp.concatenate([jnp.where(head_a, q2, zero), jnp.where(head_a, zero, q2)], axis=0)
        s_ref[slot] = _dot_nt(qs, kpad_ref[pl.ds(p0, BAND), :])

    def softmax(slot, blk, masked):
        _, p0 = offsets(blk)
        for r in range(0, 2 * Q_BLOCK, CHUNK_ROWS):
            rows = slice(r, r + CHUNK_ROWS)
            s = s_ref[slot, rows, :] + bias_ref[0, rows, :]
            if masked:
                s = jnp.where(kcol + p0 >= PAD, s, NEG)
            p = jnp.exp(s - jnp.max(s, axis=-1, keepdims=True))
            l_ref[slot, rows, :] = jnp.broadcast_to(jnp.sum(p, axis=-1, keepdims=True), (CHUNK_ROWS, LANES))
            p_ref[slot, rows, :] = p.astype(p_ref.dtype)

    def values(slot, blk):
        r0, p0 = offsets(blk)
        o = _dot(p_ref[slot], vpad_ref[pl.ds(p0, BAND), :]) / l_ref[slot]
        o_ref[0, pl.ds(r0, Q_BLOCK), :] = jnp.where(head_a, o[:Q_BLOCK], o[Q_BLOCK:]).astype(o_ref.dtype)

    def group(i, masked):
        first = i * CHUNK_GROUP
        scores(0, first)
        for n in range(CHUNK_GROUP):
            if n + 1 < CHUNK_GROUP:
                scores((n + 1) % 2, first + n + 1)
            softmax(n % 2, first + n, masked)
            values(n % 2, first + n)

    def masked_group(i, carry):
        group(i, True)
        return carry

    def plain_group(i, carry):
        group(i, False)
        return carry

    group_rows = Q_BLOCK * CHUNK_GROUP
    n_groups = tq // group_rows
    n_masked = jnp.clip((PAD - qi * tq + group_rows - 1) // group_rows, 0, n_groups)
    lax.fori_loop(0, n_masked, masked_group, 0)
    lax.fori_loop(n_masked, n_groups, plain_group, 0)


def _chunk_attention(qc, kc, vc, bias, tq):
    b, s, _ = qc.shape
    assert tq % (CHUNK_GROUP * Q_BLOCK) == 0
    return pl.pallas_call(
        functools.partial(_chunk_body, tq=tq),
        grid=(b, N_PAIR, s // tq),
        in_specs=[pl.BlockSpec((1, tq, LANES), lambda i, p, j: (i, j, p)),
                  pl.BlockSpec((1, s, LANES), lambda i, p, j: (i, 0, p)),
                  pl.BlockSpec((1, s, LANES), lambda i, p, j: (i, 0, p)),
                  pl.BlockSpec((1, 2 * Q_BLOCK, BAND), lambda i, p, j: (p, 0, 0))],
        out_specs=pl.BlockSpec((1, tq, LANES), lambda i, p, j: (i, j, p)),
        out_shape=jax.ShapeDtypeStruct((b, s, W_ATT), jnp.bfloat16),
        scratch_shapes=[pltpu.VMEM((PAD + s, LANES), jnp.bfloat16),
                        pltpu.VMEM((PAD + s, LANES), jnp.bfloat16),
                        pltpu.VMEM((2, 2 * Q_BLOCK, BAND), jnp.float32),
                        pltpu.VMEM((2, 2 * Q_BLOCK, BAND), jnp.bfloat16),
                        pltpu.VMEM((2, 2 * Q_BLOCK, LANES), jnp.float32)],
        compiler_params=pltpu.CompilerParams(
            dimension_semantics=("arbitrary", "arbitrary", "arbitrary"), vmem_limit_bytes=VMEM_LIMIT),
        name="chunk_attention",
    )(qc, kc, vc, bias)


def _out_ffn_body(x_ref, oa_ref, ob_ref, gate_ref, wa_ref, wb_ref, wo_ref, g2_ref, wu_ref, wd_ref, gf_ref,
                  o_ref, *, ff_chunk, final):
    d = x_ref.shape[-1]
    ya = _dot(oa_ref[0], wa_ref[...])
    yb = _dot(ob_ref[0], wb_ref[...])
    ga = gate_ref[0, :, :d].astype(jnp.float32)
    gb = gate_ref[0, :, d:].astype(jnp.float32)
    merged = (ga * ya + gb * yb).astype(jnp.bfloat16)
    x1 = x_ref[0] + _dot(merged, wo_ref[...])
    h2 = _rms(x1, g2_ref[...]).astype(jnp.bfloat16)
    acc = x1
    for c0 in range(0, wu_ref.shape[1], ff_chunk):
        u = jnp.maximum(_dot(h2, wu_ref[:, c0:c0 + ff_chunk]), 0.0)
        acc = acc + _dot((u * u).astype(jnp.bfloat16), wd_ref[c0:c0 + ff_chunk, :])
    if final:
        acc = _rms(acc, gf_ref[...])
    o_ref[0] = acc


def _out_ffn(x, oa, ob, gates, wa, wb, wo, g2, wu, wd, gf, tm, final):
    b, s, d = x.shape
    tok = lambda w: pl.BlockSpec((1, tm, w), lambda i, j: (i, j, 0))
    return pl.pallas_call(
        functools.partial(_out_ffn_body, ff_chunk=min(1024, wu.shape[1]), final=final),
        grid=(b, s // tm),
        in_specs=[tok(d), tok(W_ATT), tok(W_ATT), tok(2 * d)]
                 + [_const_spec(a.shape) for a in (wa, wb, wo, g2, wu, wd, gf)],
        out_specs=tok(d),
        out_shape=jax.ShapeDtypeStruct((b, s, d), jnp.float32),
        compiler_params=pltpu.CompilerParams(
            dimension_semantics=("arbitrary", "arbitrary"), vmem_limit_bytes=VMEM_LIMIT),
        name="out_ffn",
    )(x, oa, ob, gates, wa, wb, wo, g2, wu, wd, gf)


def _aug_placement():
    eq = [[0.0] * W_ATT for _ in range(LANES)]
    ek = [[0.0] * W_ATT for _ in range(LANES)]
    one = 3 * H_FOX
    for h in range(H_FOX):
        for piece in range(3):
            eq[piece * H_FOX + h][HEAD_DIM * h + piece] = 1.0
            eq[one][HEAD_DIM * h + 3 + piece] = 1.0
            ek[one][HEAD_DIM * h + piece] = 1.0
            ek[piece * H_FOX + h][HEAD_DIM * h + 3 + piece] = -1.0
    return jnp.array(eq, jnp.bfloat16), jnp.array(ek, jnp.bfloat16)


def kernel(x, norm1, w_in, forget_bias, rel_bias, w_branch_a, w_branch_b, w_out, norm2, w_up, w_down, final_norm):
    b, s, d = x.shape
    depth = w_in.shape[0]
    bf16 = jnp.bfloat16
    tm = min(512, s)
    tqc = min(1024, s)
    eq, ek = _aug_placement()
    head_of_col = lax.broadcasted_iota(jnp.int32, (W_ATT, LANES), 0) // HEAD_DIM
    hsum = (head_of_col == lax.broadcasted_iota(jnp.int32, (W_ATT, LANES), 1)).astype(bf16)
    gf = final_norm.reshape(1, d)
    o = 3 * W_ATT + H_FOX
    for l in range(depth):
        w = w_in[l]
        wq = (w[:, :W_ATT] * SCALE).astype(bf16)
        wk = w[:, W_ATT:2 * W_ATT].astype(bf16)
        wv = w[:, 2 * W_ATT:3 * W_ATT].astype(bf16)
        wf = jnp.pad(w[:, 3 * W_ATT:o], ((0, 0), (0, LANES - H_FOX))).astype(bf16)
        wc = jnp.concatenate([w[:, o:o + W_ATT] * SCALE, w[:, o + W_ATT:o + 3 * W_ATT]], axis=1).astype(bf16)
        wg = w[:, o + 3 * W_ATT:].astype(bf16)
        bf = jnp.pad(forget_bias[l], (0, LANES - H_FOX)).reshape(1, LANES)
        qf, kf, vf, qc, kc, vc, gates, stat = _in_proj(
            x, norm1[l].reshape(1, d), wq, wk, wv, wc, wg, wf, bf, eq, ek, hsum, tm)

        stats = stat[:, :, :4, :H_FOX].transpose(2, 0, 3, 1).reshape(4, -1)
        o_a = _fox(tuple(stats), qf, kf, vf, tm)

        far = rel_bias[l][:, 2 * MAX_REL:]
        g_ext = jnp.concatenate([jnp.broadcast_to(far, (H_CHK, PAD - MAX_REL + 1)),
                                 rel_bias[l][:, 2 * MAX_REL - 1:0:-1],
                                 jnp.broadcast_to(far, (H_CHK, Q_BLOCK))], axis=1)
        bias = _band_bias(g_ext.reshape(H_CHK, 1, -1)).reshape(N_PAIR, 2 * Q_BLOCK, BAND)
        o_b = _chunk_attention(qc, kc, vc, bias, tqc)

        x = _out_ffn(x, o_a, o_b, gates, w_branch_a[l].astype(bf16), w_branch_b[l].astype(bf16),
                     w_out[l].astype(bf16), norm2[l].reshape(1, d), w_up[l].astype(bf16),
                     w_down[l].astype(bf16), gf, tm, final=(l == depth - 1))
    return x
```

```python
import functools
import math

import jax
import jax.numpy as jnp
from jax import lax
from jax.experimental import pallas as pl
from jax.experimental.pallas import tpu as pltpu

HEAD_DIM = 64
H_FOX = 8
H_CHK = 8
N_PAIR = 4
W_ATT = H_FOX * HEAD_DIM
CHUNK = 64
Q_BLOCK = 128
LEFT_CHUNKS = 8
PAD = LEFT_CHUNKS * CHUNK
BAND = PAD + Q_BLOCK
MAX_REL = 128
EPS = 1e-6
NEG = -1e30
LANES = 128
SCALE = 1.0 / math.sqrt(HEAD_DIM)
F32_BIG = 3.0e38
PROJ_SPLIT = 2
FOX_ROWS = 32
FOX_TRIP = 4
CHUNK_ROWS = 32
CHUNK_GROUP = 8
SKIP_MARGIN = 30.0
NORM_SLACK = 1.02
VMEM_LIMIT = 60 * 1024 * 1024

_NT = (((1,), (1,)), ((), ()))


def _dot(a, b):
    return jnp.dot(a, b, preferred_element_type=jnp.float32)


def _dot_nt(a, b):
    return lax.dot_general(a, b, _NT, preferred_element_type=jnp.float32)


def _rms(x, g):
    ms = jnp.mean(x * x, axis=-1, keepdims=True)
    return x * lax.rsqrt(ms + EPS) * g


def _const_spec(shape):
    nd = len(shape)
    return pl.BlockSpec(shape, lambda *_: (0,) * nd, pipeline_mode=pl.Buffered(1))


def _in_proj_body(x_ref, g_ref, wq_ref, wk_ref, wv_ref, wc_ref, wg_ref, wf_ref, bf_ref, eq_ref, ek_ref, hsum_ref,
                  qf_ref, kf_ref, vf_ref, qc_ref, kc_ref, vc_ref, gate_ref, stat_ref, carry_ref):
    tm = x_ref.shape[1]
    th = tm // PROJ_SPLIT

    @pl.when(pl.program_id(1) == 0)
    def _():
        carry_ref[...] = jnp.zeros_like(carry_ref)

    lane = lax.broadcasted_iota(jnp.int32, (th, LANES), 1)
    row = lax.broadcasted_iota(jnp.int32, (th, LANES), 0)

    def max_norm(t):
        t = t.astype(jnp.float32)
        sq = _dot((t * t).astype(jnp.bfloat16), hsum_ref[...])
        return jnp.max(sq, axis=0, keepdims=True)

    c_first, c_last, q_sq, k_sq = None, None, None, None
    for part in range(PROJ_SPLIT):
        rows = slice(part * th, (part + 1) * th)
        h = _rms(x_ref[0, rows, :], g_ref[...]).astype(jnp.bfloat16)

        logf = jax.nn.log_sigmoid(_dot(h, wf_ref[...]) + bf_ref[...])
        c = jnp.where(lane < H_FOX, logf, 0.0)
        k = 1
        while k < th:
            c = c + jnp.where(row >= k, pltpu.roll(c, k, 0), 0.0)
            k *= 2
        c = c + carry_ref[...]
        carry_ref[...] = c[th - 1:th, :]
        c_first = c[0:1, :] if part == 0 else c_first
        c_last = c[th - 1:th, :]

        hi = c.astype(jnp.bfloat16).astype(jnp.float32)
        r1 = c - hi
        mid = r1.astype(jnp.bfloat16).astype(jnp.float32)
        lo = (r1 - mid).astype(jnp.bfloat16).astype(jnp.float32)
        pieces = hi + pltpu.roll(mid, H_FOX, 1) + pltpu.roll(lo, 2 * H_FOX, 1)
        pieces = jnp.where(lane == 3 * H_FOX, 1.0, pieces).astype(jnp.bfloat16)
        aug_q = _dot(pieces, eq_ref[...]).astype(jnp.bfloat16)
        aug_k = _dot(pieces, ek_ref[...]).astype(jnp.bfloat16)

        q = _dot(h, wq_ref[...]).astype(jnp.bfloat16)
        kk = _dot(h, wk_ref[...]).astype(jnp.bfloat16)
        for p in range(N_PAIR):
            src = slice(p * LANES, (p + 1) * LANES)
            qf_ref[0, rows, 2 * p * LANES:(2 * p + 1) * LANES] = q[:, src]
            qf_ref[0, rows, (2 * p + 1) * LANES:(2 * p + 2) * LANES] = aug_q[:, src]
            kf_ref[0, rows, 2 * p * LANES:(2 * p + 1) * LANES] = kk[:, src]
            kf_ref[0, rows, (2 * p + 1) * LANES:(2 * p + 2) * LANES] = aug_k[:, src]
        vf_ref[0, rows, :] = _dot(h, wv_ref[...]).astype(jnp.bfloat16)
        q_sq = max_norm(q) if part == 0 else jnp.maximum(q_sq, max_norm(q))
        k_sq = max_norm(kk) if part == 0 else jnp.maximum(k_sq, max_norm(kk))

        pc = _dot(h, wc_ref[...])
        qc_ref[0, rows, :] = pc[:, :W_ATT].astype(jnp.bfloat16)
        kc_ref[0, rows, :] = pc[:, W_ATT:2 * W_ATT].astype(jnp.bfloat16)
        vc_ref[0, rows, :] = pc[:, 2 * W_ATT:].astype(jnp.bfloat16)
        gate_ref[0, rows, :] = jax.nn.sigmoid(_dot(h, wg_ref[...])).astype(jnp.bfloat16)

    sub = lax.broadcasted_iota(jnp.int32, (8, LANES), 0)
    stat = jnp.where(sub == 0, c_first, 0.0)
    stat = jnp.where(sub == 1, c_last, stat)
    stat = jnp.where(sub == 2, jnp.sqrt(q_sq), stat)
    stat_ref[0, 0] = jnp.where(sub == 3, jnp.sqrt(k_sq), stat)


def _in_proj(x, g, wq, wk, wv, wc, wg, wf, bf, eq, ek, hsum, tm):
    b, s, d = x.shape
    tok = lambda w: pl.BlockSpec((1, tm, w), lambda i, j: (i, j, 0))
    bf16 = jnp.bfloat16
    out_shape = [jax.ShapeDtypeStruct((b, s, w), bf16)
                 for w in (2 * W_ATT, 2 * W_ATT, W_ATT, W_ATT, W_ATT, W_ATT, 2 * d)]
    out_shape.append(jax.ShapeDtypeStruct((b, s // tm, 8, LANES), jnp.float32))
    consts = (g, wq, wk, wv, wc, wg, wf, bf, eq, ek, hsum)
    return pl.pallas_call(
        _in_proj_body,
        grid=(b, s // tm),
        in_specs=[tok(d)] + [_const_spec(a.shape) for a in consts],
        out_specs=[tok(sh.shape[-1]) for sh in out_shape[:-1]]
                  + [pl.BlockSpec((1, 1, 8, LANES), lambda i, j: (i, j, 0, 0))],
        out_shape=out_shape,
        scratch_shapes=[pltpu.VMEM((1, LANES), jnp.float32)],
        compiler_params=pltpu.CompilerParams(
            dimension_semantics=("arbitrary", "arbitrary"), vmem_limit_bytes=VMEM_LIMIT),
        name="in_proj",
    )(x, *consts)


def _fox_body(cfirst_ref, clast_ref, qnorm_ref, knorm_ref, q_ref, k_ref, v_ref, o_ref,
              s_ref, p_ref, m_ref, l_ref, acc_ref, bad_ref, *, tq):
    n_tiles = q_ref.shape[1] // tq
    lane2 = lax.broadcasted_iota(jnp.int32, (1, 2 * LANES), 1) % LANES
    head_a = lax.broadcasted_iota(jnp.int32, (1, LANES), 1) < HEAD_DIM

    def reach(qi, head):
        base = (pl.program_id(0) * H_FOX + 2 * pl.program_id(1) + head) * n_tiles
        kmax = lax.fori_loop(0, qi + 1, lambda t, mx: jnp.maximum(mx, knorm_ref[base + t]), 0.0)
        bound = cfirst_ref[base + qi] + 2.0 * NORM_SLACK * qnorm_ref[base + qi] * kmax + SKIP_MARGIN
        return lax.fori_loop(
            0, qi, lambda t, n: n + jnp.where(bound - clast_ref[base + t] > 0.0, 1, 0), 0)

    def key_rows(j):
        return pl.ds(pl.multiple_of(j * tq, tq), tq)

    def scores(slot, q_head, j):
        s_ref[slot] = _dot_nt(q_head, k_ref[0, key_rows(j), :])

    def softmax(slot, head, mode):
        for r in range(0, tq, FOX_ROWS):
            rows = slice(r, r + FOX_ROWS)
            s = s_ref[slot, rows, :]
            if mode in ("diagonal", "diagonal_self"):
                row = lax.broadcasted_iota(jnp.int32, (FOX_ROWS, tq), 0) + r
                col = lax.broadcasted_iota(jnp.int32, (FOX_ROWS, tq), 1)
                if mode == "diagonal":
                    s = jnp.where(col <= row, s, NEG)
                    m = jnp.max(s, axis=-1, keepdims=True)
                else:
                    own = slice(r // LANES * LANES, (r // LANES + 1) * LANES)
                    lane = lax.broadcasted_iota(jnp.int32, (FOX_ROWS, LANES), 1)
                    sub = lax.broadcasted_iota(jnp.int32, (FOX_ROWS, LANES), 0)
                    m = jnp.sum(jnp.where(lane == sub + r % LANES, s[:, own], 0.0), axis=-1, keepdims=True)
                    s = jnp.where(col <= row, s, NEG)
                m = jnp.broadcast_to(m, (FOX_ROWS, LANES))
                m_ref[head, rows, :] = m
            elif mode == "exact":
                m_old = m_ref[head, rows, :]
                m = jnp.maximum(m_old, jnp.max(s, axis=-1, keepdims=True))
                alpha = jnp.exp(m_old - m)
                m_ref[head, rows, :] = m
            else:
                m = m_ref[head, rows, :]
            lsum = None
            for g in range(tq // LANES):
                cols = slice(g * LANES, (g + 1) * LANES)
                pg = jnp.exp(s[:, cols] - m)
                p_ref[slot, rows, cols] = pg.astype(p_ref.dtype)
                lsum = pg if lsum is None else lsum + pg
            if mode in ("diagonal", "diagonal_self"):
                l_ref[head, rows, :] = lsum
            elif mode == "exact":
                l_ref[head, rows, :] = l_ref[head, rows, :] * alpha + lsum
                acc_ref[head, rows, :] = acc_ref[head, rows, :] * alpha
            else:
                l_ref[head, rows, :] = l_ref[head, rows, :] + lsum

    def values(slot, head, j, mode):
        pv = _dot(p_ref[slot], v_ref[0, key_rows(j), :])
        acc_ref[head] = pv if mode in ("diagonal", "diagonal_self") else acc_ref[head] + pv

    def single_tile(qh, j, mode):
        for head in range(2):
            scores(head, qh[head], j)
        for head in range(2):
            softmax(head, head, mode)
            values(head, head, j, mode)

    def tile_run(qh, j, count, starts_on_diagonal=False):
        units = [(j - i, head, "diagonal_self" if starts_on_diagonal and i == 0 else "lazy")
                 for i in range(count) for head in range(2)]
        n_slots = s_ref.shape[0]
        for n in range(2):
            scores(n, qh[units[n][1]], units[n][0])
        for n, (tile, head, mode) in enumerate(units):
            softmax(n % n_slots, head, mode)
            if n + 2 < len(units):
                scores((n + 2) % n_slots, qh[units[n + 2][1]], units[n + 2][0])
            values(n % n_slots, head, tile, mode)

    def query_tile(qi, exact):
        q_rows = pl.ds(pl.multiple_of(qi * tq, tq), tq)
        qcat = q_ref[0, q_rows, :]
        zero = jnp.zeros_like(qcat)
        qh = (jnp.where(lane2 < HEAD_DIM, qcat, zero), jnp.where(lane2 >= HEAD_DIM, qcat, zero))
        n_visit = jnp.maximum(reach(qi, 0), reach(qi, 1))

        if exact:
            single_tile(qh, qi, "diagonal")

            def step(t, carry):
                single_tile(qh, qi - 1 - t, "exact")
                return carry

            lax.fori_loop(0, n_visit, step, 0)
        else:
            total = n_visit + 1
            count, first_len = FOX_TRIP, 1
            while count >= 1:
                fits = total >= count
                if count < FOX_TRIP:
                    fits = fits & (total < 2 * count)

                @pl.when(fits)
                def _(count=count):
                    tile_run(qh, qi, count, starts_on_diagonal=True)

                first_len = jnp.where(fits, count, first_len)
                count //= 2

            def run(t, carry):
                tile_run(qh, qi - first_len - FOX_TRIP * t, FOX_TRIP)
                return carry

            n_full = (total - first_len) // FOX_TRIP
            lax.fori_loop(0, n_full, run, 0)
            done = first_len + n_full * FOX_TRIP
            count = FOX_TRIP // 2
            while count >= 1:
                fits = (total - done) >= count

                @pl.when(fits)
                def _(first=qi - done, count=count):
                    tile_run(qh, first, count)

                done = jnp.where(fits, done + count, done)
                count //= 2

        outs = []
        for head in range(2):
            acc, lp = acc_ref[head], l_ref[head]
            outs.append(acc / jnp.sum(lp, axis=-1, keepdims=True))
            if not exact:
                finite = (jnp.abs(acc) <= F32_BIG) & (lp <= F32_BIG)
                bad_ref[...] = jnp.maximum(bad_ref[...], jnp.where(finite, 0.0, 1.0))
        o_ref[0, q_rows, :] = jnp.where(head_a, outs[0], outs[1]).astype(o_ref.dtype)

    def sweep(exact):
        def body(qi, carry):
            query_tile(qi, exact)
            return carry

        lax.fori_loop(0, n_tiles, body, 0)

    bad_ref[...] = jnp.zeros_like(bad_ref)
    sweep(exact=False)

    @pl.when(jnp.max(bad_ref[...]) > 0.0)
    def _():
        sweep(exact=True)


def _fox(stats, qf, kf, vf, tq):
    b, s, _ = qf.shape
    smem = pl.BlockSpec(memory_space=pltpu.SMEM)
    return pl.pallas_call(
        functools.partial(_fox_body, tq=tq),
        grid=(b, N_PAIR),
        in_specs=[smem, smem, smem, smem,
                  pl.BlockSpec((1, s, 2 * LANES), lambda i, p: (i, 0, p)),
                  pl.BlockSpec((1, s, 2 * LANES), lambda i, p: (i, 0, p)),
                  pl.BlockSpec((1, s, LANES), lambda i, p: (i, 0, p))],
        out_specs=pl.BlockSpec((1, s, LANES), lambda i, p: (i, 0, p)),
        out_shape=jax.ShapeDtypeStruct((b, s, W_ATT), jnp.bfloat16),
        scratch_shapes=[pltpu.VMEM((4, tq, tq), jnp.float32),
                        pltpu.VMEM((4, tq, tq), jnp.bfloat16),
                        pltpu.VMEM((2, tq, LANES), jnp.float32),
                        pltpu.VMEM((2, tq, LANES), jnp.float32),
                        pltpu.VMEM((2, tq, LANES), jnp.float32),
                        pltpu.VMEM((tq, LANES), jnp.float32)],
        compiler_params=pltpu.CompilerParams(
            dimension_semantics=("arbitrary", "arbitrary"), vmem_limit_bytes=VMEM_LIMIT),
        name="fox_attention",
    )(*stats, qf, kf, vf)


def _band_bias_body(g_ref, o_ref):
    width = g_ref.shape[-1]
    base = jnp.broadcast_to(g_ref[0], (Q_BLOCK, width))
    toeplitz = pltpu.roll(base, 0, 1, stride=1, stride_axis=0)[:, :BAND]
    qrow = lax.broadcasted_iota(jnp.int32, (Q_BLOCK, BAND), 0)
    kcol = lax.broadcasted_iota(jnp.int32, (Q_BLOCK, BAND), 1)
    cq = qrow // CHUNK
    ck = kcol // CHUNK - LEFT_CHUNKS
    valid = (ck <= cq) & (ck >= cq - LEFT_CHUNKS)
    o_ref[0] = jnp.where(valid, toeplitz, NEG)


def _band_bias(g_ext):
    h, _, width = g_ext.shape
    return pl.pallas_call(
        _band_bias_body,
        grid=(h,),
        in_specs=[pl.BlockSpec((1, 1, width), lambda i: (i, 0, 0))],
        out_specs=pl.BlockSpec((1, Q_BLOCK, BAND), lambda i: (i, 0, 0)),
        out_shape=jax.ShapeDtypeStruct((h, Q_BLOCK, BAND), jnp.float32),
        name="band_bias",
    )(g_ext)


def _chunk_body(q_ref, k_ref, v_ref, bias_ref, o_ref, kpad_ref, vpad_ref, s_ref, p_ref, l_ref, *, tq):
    qi = pl.program_id(2)
    s_len = k_ref.shape[1]

    @pl.when(qi == 0)
    def _():
        zeros = jnp.zeros((PAD, LANES), kpad_ref.dtype)
        kpad_ref[:PAD, :] = zeros
        vpad_ref[:PAD, :] = zeros
        kpad_ref[PAD:PAD + s_len, :] = k_ref[0]
        vpad_ref[PAD:PAD + s_len, :] = v_ref[0]

    head_a = lax.broadcasted_iota(jnp.int32, (1, LANES), 1) < HEAD_DIM
    kcol = lax.broadcasted_iota(jnp.int32, (1, BAND), 1)

    def offsets(blk):
        r0 = pl.multiple_of(blk * Q_BLOCK, Q_BLOCK)
        return r0, pl.multiple_of(qi * tq + r0, Q_BLOCK)

    def scores(slot, blk):
        r0, p0 = offsets(blk)
        q2 = q_ref[0, pl.ds(r0, Q_BLOCK), :]
        zero = jnp.zeros_like(q2)
        qs = jnp.concatenate([jnp.where(head_a, q2, zero), jnp.where(head_a, zero, q2)], axis=0)
        s_ref[slot] = _dot_nt(qs, kpad_ref[pl.ds(p0, BAND), :])

    def softmax(slot, blk, masked):
        _, p0 = offsets(blk)
        for r in range(0, 2 * Q_BLOCK, CHUNK_ROWS):
            rows = slice(r, r + CHUNK_ROWS)
            s = s_ref[slot, rows, :] + bias_ref[0, rows, :]
            if masked:
                s = jnp.where(kcol + p0 >= PAD, s, NEG)
            p = jnp.exp(s - jnp.max(s, axis=-1, keepdims=True))
            l_ref[slot, rows, :] = jnp.broadcast_to(jnp.sum(p, axis=-1, keepdims=True), (CHUNK_ROWS, LANES))
            p_ref[slot, rows, :] = p.astype(p_ref.dtype)

    def values(slot, blk):
        r0, p0 = offsets(blk)
        o = _dot(p_ref[slot], vpad_ref[pl.ds(p0, BAND), :]) / l_ref[slot]
        o_ref[0, pl.ds(r0, Q_BLOCK), :] = jnp.where(head_a, o[:Q_BLOCK], o[Q_BLOCK:]).astype(o_ref.dtype)

    def group(i, masked):
        first = i * CHUNK_GROUP
        scores(0, first)
        for n in range(CHUNK_GROUP):
            if n + 1 < CHUNK_GROUP:
                scores((n + 1) % 2, first + n + 1)
            softmax(n % 2, first + n, masked)
            values(n % 2, first + n)

    def masked_group(i, carry):
        group(i, True)
        return carry

    def plain_group(i, carry):
        group(i, False)
        return carry

    group_rows = Q_BLOCK * CHUNK_GROUP
    n_groups = tq // group_rows
    n_masked = jnp.clip((PAD - qi * tq + group_rows - 1) // group_rows, 0, n_groups)
    lax.fori_loop(0, n_masked, masked_group, 0)
    lax.fori_loop(n_masked, n_groups, plain_group, 0)


def _chunk_attention(qc, kc, vc, bias, tq):
    b, s, _ = qc.shape
    assert tq % (CHUNK_GROUP * Q_BLOCK) == 0
    return pl.pallas_call(
        functools.partial(_chunk_body, tq=tq),
        grid=(b, N_PAIR, s // tq),
        in_specs=[pl.BlockSpec((1, tq, LANES), lambda i, p, j: (i, j, p)),
                  pl.BlockSpec((1, s, LANES), lambda i, p, j: (i, 0, p)),
                  pl.BlockSpec((1, s, LANES), lambda i, p, j: (i, 0, p)),
                  pl.BlockSpec((1, 2 * Q_BLOCK, BAND), lambda i, p, j: (p, 0, 0))],
        out_specs=pl.BlockSpec((1, tq, LANES), lambda i, p, j: (i, j, p)),
        out_shape=jax.ShapeDtypeStruct((b, s, W_ATT), jnp.bfloat16),
        scratch_shapes=[pltpu.VMEM((PAD + s, LANES), jnp.bfloat16),
                        pltpu.VMEM((PAD + s, LANES), jnp.bfloat16),
                        pltpu.VMEM((2, 2 * Q_BLOCK, BAND), jnp.float32),
                        pltpu.VMEM((2, 2 * Q_BLOCK, BAND), jnp.bfloat16),
                        pltpu.VMEM((2, 2 * Q_BLOCK, LANES), jnp.float32)],
        compiler_params=pltpu.CompilerParams(
            dimension_semantics=("arbitrary", "arbitrary", "arbitrary"), vmem_limit_bytes=VMEM_LIMIT),
        name="chunk_attention",
    )(qc, kc, vc, bias)


def _out_ffn_body(x_ref, oa_ref, ob_ref, gate_ref, wa_ref, wb_ref, wo_ref, g2_ref, wu_ref, wd_ref, gf_ref,
                  o_ref, *, ff_chunk, final):
    d = x_ref.shape[-1]
    ya = _dot(oa_ref[0], wa_ref[...])
    yb = _dot(ob_ref[0], wb_ref[...])
    ga = gate_ref[0, :, :d].astype(jnp.float32)
    gb = gate_ref[0, :, d:].astype(jnp.float32)
    merged = (ga * ya + gb * yb).astype(jnp.bfloat16)
    x1 = x_ref[0] + _dot(merged, wo_ref[...])
    h2 = _rms(x1, g2_ref[...]).astype(jnp.bfloat16)
    acc = x1
    for c0 in range(0, wu_ref.shape[1], ff_chunk):
        u = jnp.maximum(_dot(h2, wu_ref[:, c0:c0 + ff_chunk]), 0.0)
        acc = acc + _dot((u * u).astype(jnp.bfloat16), wd_ref[c0:c0 + ff_chunk, :])
    if final:
        acc = _rms(acc, gf_ref[...])
    o_ref[0] = acc


def _out_ffn(x, oa, ob, gates, wa, wb, wo, g2, wu, wd, gf, tm, final):
    b, s, d = x.shape
    tok = lambda w: pl.BlockSpec((1, tm, w), lambda i, j: (i, j, 0))
    return pl.pallas_call(
        functools.partial(_out_ffn_body, ff_chunk=min(1024, wu.shape[1]), final=final),
        grid=(b, s // tm),
        in_specs=[tok(d), tok(W_ATT), tok(W_ATT), tok(2 * d)]
                 + [_const_spec(a.shape) for a in (wa, wb, wo, g2, wu, wd, gf)],
        out_specs=tok(d),
        out_shape=jax.ShapeDtypeStruct((b, s, d), jnp.float32),
        compiler_params=pltpu.CompilerParams(
            dimension_semantics=("arbitrary", "arbitrary"), vmem_limit_bytes=VMEM_LIMIT),
        name="out_ffn",
    )(x, oa, ob, gates, wa, wb, wo, g2, wu, wd, gf)


def _aug_placement():
    eq = [[0.0] * W_ATT for _ in range(LANES)]
    ek = [[0.0] * W_ATT for _ in range(LANES)]
    one = 3 * H_FOX
    for h in range(H_FOX):
        for piece in range(3):
            eq[piece * H_FOX + h][HEAD_DIM * h + piece] = 1.0
            eq[one][HEAD_DIM * h + 3 + piece] = 1.0
            ek[one][HEAD_DIM * h + piece] = 1.0
            ek[piece * H_FOX + h][HEAD_DIM * h + 3 + piece] = -1.0
    return jnp.array(eq, jnp.bfloat16), jnp.array(ek, jnp.bfloat16)


def kernel(x, norm1, w_in, forget_bias, rel_bias, w_branch_a, w_branch_b, w_out, norm2, w_up, w_down, final_norm):
    b, s, d = x.shape
    depth = w_in.shape[0]
    bf16 = jnp.bfloat16
    tm = min(512, s)
    tqc = min(1024, s)
    eq, ek = _aug_placement()
    head_of_col = lax.broadcasted_iota(jnp.int32, (W_ATT, LANES), 0) // HEAD_DIM
    hsum = (head_of_col == lax.broadcasted_iota(jnp.int32, (W_ATT, LANES), 1)).astype(bf16)
    gf = final_norm.reshape(1, d)
    o = 3 * W_ATT + H_FOX
    for l in range(depth):
        w = w_in[l]
        wq = (w[:, :W_ATT] * SCALE).astype(bf16)
        wk = w[:, W_ATT:2 * W_ATT].astype(bf16)
        wv = w[:, 2 * W_ATT:3 * W_ATT].astype(bf16)
        wf = jnp.pad(w[:, 3 * W_ATT:o], ((0, 0), (0, LANES - H_FOX))).astype(bf16)
        wc = jnp.concatenate([w[:, o:o + W_ATT] * SCALE, w[:, o + W_ATT:o + 3 * W_ATT]], axis=1).astype(bf16)
        wg = w[:, o + 3 * W_ATT:].astype(bf16)
        bf = jnp.pad(forget_bias[l], (0, LANES - H_FOX)).reshape(1, LANES)
        qf, kf, vf, qc, kc, vc, gates, stat = _in_proj(
            x, norm1[l].reshape(1, d), wq, wk, wv, wc, wg, wf, bf, eq, ek, hsum, tm)

        stats = stat[:, :, :4, :H_FOX].transpose(2, 0, 3, 1).reshape(4, -1)
        o_a = _fox(tuple(stats), qf, kf, vf, tm)

        far = rel_bias[l][:, 2 * MAX_REL:]
        g_ext = jnp.concatenate([jnp.broadcast_to(far, (H_CHK, PAD - MAX_REL + 1)),
                                 rel_bias[l][:, 2 * MAX_REL - 1:0:-1],
                                 jnp.broadcast_to(far, (H_CHK, Q_BLOCK))], axis=1)
        bias = _band_bias(g_ext.reshape(H_CHK, 1, -1)).reshape(N_PAIR, 2 * Q_BLOCK, BAND)
        o_b = _chunk_attention(qc, kc, vc, bias, tqc)

        x = _out_ffn(x, o_a, o_b, gates, w_branch_a[l].astype(bf16), w_branch_b[l].astype(bf16),
                     w_out[l].astype(bf16), norm2[l].reshape(1, d), w_up[l].astype(bf16),
                     w_down[l].astype(bf16), gf, tm, final=(l == depth - 1))
    return x
```

```python
import functools
import math

import jax
import jax.numpy as jnp
from jax import lax
from jax.experimental import pallas as pl
from jax.experimental.pallas import tpu as pltpu

HEAD_DIM = 64
H_FOX = 8
H_CHK = 8
N_PAIR = 4
W_ATT = H_FOX * HEAD_DIM
CHUNK = 64
Q_BLOCK = 128
LEFT_CHUNKS = 8
PAD = LEFT_CHUNKS * CHUNK
BAND = PAD + Q_BLOCK
MAX_REL = 128
EPS = 1e-6
NEG = -1e30
LANES = 128
SCALE = 1.0 / math.sqrt(HEAD_DIM)
F32_BIG = 3.0e38
PROJ_SPLIT = 2
FOX_ROWS = 32
FOX_TILE = 256
FOX_TRIP = 8
FOX_AHEAD = 2
CHUNK_ROWS = 32
CHUNK_GROUP = 8
SKIP_MARGIN = 30.0
NORM_SLACK = 1.02
VMEM_LIMIT = 60 * 1024 * 1024

_NT = (((1,), (1,)), ((), ()))


def _dot(a, b):
    return jnp.dot(a, b, preferred_element_type=jnp.float32)


def _dot_nt(a, b):
    return lax.dot_general(a, b, _NT, preferred_element_type=jnp.float32)


def _rms(x, g):
    ms = jnp.mean(x * x, axis=-1, keepdims=True)
    return x * lax.rsqrt(ms + EPS) * g


def _const_spec(shape):
    nd = len(shape)
    return pl.BlockSpec(shape, lambda *_: (0,) * nd, pipeline_mode=pl.Buffered(1))


def _in_proj_body(x_ref, g_ref, wq_ref, wk_ref, wv_ref, wc_ref, wg_ref, wf_ref, bf_ref, eq_ref, ek_ref, hsum_ref,
                  qf_ref, kf_ref, vf_ref, qc_ref, kc_ref, vc_ref, gate_ref, stat_ref, carry_ref):
    tm = x_ref.shape[1]
    th = tm // PROJ_SPLIT

    @pl.when(pl.program_id(1) == 0)
    def _():
        carry_ref[...] = jnp.zeros_like(carry_ref)

    lane = lax.broadcasted_iota(jnp.int32, (th, LANES), 1)
    row = lax.broadcasted_iota(jnp.int32, (th, LANES), 0)

    def max_norm(t):
        t = t.astype(jnp.float32)
        sq = _dot((t * t).astype(jnp.bfloat16), hsum_ref[...])
        return jnp.max(sq, axis=0, keepdims=True)

    sub = lax.broadcasted_iota(jnp.int32, (8, LANES), 0)
    for part in range(PROJ_SPLIT):
        rows = slice(part * th, (part + 1) * th)
        h = _rms(x_ref[0, rows, :], g_ref[...]).astype(jnp.bfloat16)

        logf = jax.nn.log_sigmoid(_dot(h, wf_ref[...]) + bf_ref[...])
        c = jnp.where(lane < H_FOX, logf, 0.0)
        k = 1
        while k < th:
            c = c + jnp.where(row >= k, pltpu.roll(c, k, 0), 0.0)
            k *= 2
        c = c + carry_ref[...]
        carry_ref[...] = c[th - 1:th, :]

        hi = c.astype(jnp.bfloat16).astype(jnp.float32)
        r1 = c - hi
        mid = r1.astype(jnp.bfloat16).astype(jnp.float32)
        lo = (r1 - mid).astype(jnp.bfloat16).astype(jnp.float32)
        pieces = hi + pltpu.roll(mid, H_FOX, 1) + pltpu.roll(lo, 2 * H_FOX, 1)
        pieces = jnp.where(lane == 3 * H_FOX, 1.0, pieces).astype(jnp.bfloat16)
        aug_q = _dot(pieces, eq_ref[...]).astype(jnp.bfloat16)
        aug_k = _dot(pieces, ek_ref[...]).astype(jnp.bfloat16)

        q = _dot(h, wq_ref[...]).astype(jnp.bfloat16)
        kk = _dot(h, wk_ref[...]).astype(jnp.bfloat16)
        for p in range(N_PAIR):
            src = slice(p * LANES, (p + 1) * LANES)
            qf_ref[0, rows, 2 * p * LANES:(2 * p + 1) * LANES] = q[:, src]
            qf_ref[0, rows, (2 * p + 1) * LANES:(2 * p + 2) * LANES] = aug_q[:, src]
            kf_ref[0, rows, 2 * p * LANES:(2 * p + 1) * LANES] = kk[:, src]
            kf_ref[0, rows, (2 * p + 1) * LANES:(2 * p + 2) * LANES] = aug_k[:, src]
        vf_ref[0, rows, :] = _dot(h, wv_ref[...]).astype(jnp.bfloat16)

        stat = jnp.where(sub == 0, c[0:1, :], 0.0)
        stat = jnp.where(sub == 1, c[th - 1:th, :], stat)
        stat = jnp.where(sub == 2, jnp.sqrt(max_norm(q)), stat)
        stat_ref[0, part] = jnp.where(sub == 3, jnp.sqrt(max_norm(kk)), stat)

        pc = _dot(h, wc_ref[...])
        qc_ref[0, rows, :] = pc[:, :W_ATT].astype(jnp.bfloat16)
        kc_ref[0, rows, :] = pc[:, W_ATT:2 * W_ATT].astype(jnp.bfloat16)
        vc_ref[0, rows, :] = pc[:, 2 * W_ATT:].astype(jnp.bfloat16)
        gate_ref[0, rows, :] = jax.nn.sigmoid(_dot(h, wg_ref[...])).astype(jnp.bfloat16)


def _in_proj(x, g, wq, wk, wv, wc, wg, wf, bf, eq, ek, hsum, tm):
    b, s, d = x.shape
    tok = lambda w: pl.BlockSpec((1, tm, w), lambda i, j: (i, j, 0))
    bf16 = jnp.bfloat16
    out_shape = [jax.ShapeDtypeStruct((b, s, w), bf16)
                 for w in (2 * W_ATT, 2 * W_ATT, W_ATT, W_ATT, W_ATT, W_ATT, 2 * d)]
    out_shape.append(jax.ShapeDtypeStruct((b, s // tm * PROJ_SPLIT, 8, LANES), jnp.float32))
    consts = (g, wq, wk, wv, wc, wg, wf, bf, eq, ek, hsum)
    return pl.pallas_call(
        _in_proj_body,
        grid=(b, s // tm),
        in_specs=[tok(d)] + [_const_spec(a.shape) for a in consts],
        out_specs=[tok(sh.shape[-1]) for sh in out_shape[:-1]]
                  + [pl.BlockSpec((1, PROJ_SPLIT, 8, LANES), lambda i, j: (i, j, 0, 0))],
        out_shape=out_shape,
        scratch_shapes=[pltpu.VMEM((1, LANES), jnp.float32)],
        compiler_params=pltpu.CompilerParams(
            dimension_semantics=("arbitrary", "arbitrary"), vmem_limit_bytes=VMEM_LIMIT),
        name="in_proj",
    )(x, *consts)


def _fox_body(cfirst_ref, clast_ref, qnorm_ref, knorm_ref, q_ref, k_ref, v_ref, o_ref,
              qs_ref, s_ref, p_ref, m_ref, l_ref, acc_ref, bad_ref, *, tq):
    n_tiles = q_ref.shape[1] // tq
    n_slots = s_ref.shape[0]
    lane2 = lax.broadcasted_iota(jnp.int32, (1, 2 * LANES), 1) % LANES
    head_a = lax.broadcasted_iota(jnp.int32, (1, LANES), 1) < HEAD_DIM
    base = [(pl.program_id(0) * H_FOX + 2 * pl.program_id(1) + head) * n_tiles for head in range(2)]

    def key_rows(j):
        return pl.ds(pl.multiple_of(j * tq, tq), tq)

    def scores(slot, j):
        s_ref[slot] = _dot_nt(qs_ref[...], k_ref[0, key_rows(j), :])

    def softmax(slot, mode):
        for r in range(0, 2 * tq, FOX_ROWS):
            rows = slice(r, r + FOX_ROWS)
            s = s_ref[slot, rows, :]
            if mode in ("diagonal", "diagonal_self"):
                row = lax.broadcasted_iota(jnp.int32, (FOX_ROWS, tq), 0) + r % tq
                col = lax.broadcasted_iota(jnp.int32, (FOX_ROWS, tq), 1)
                if mode == "diagonal":
                    s = jnp.where(col <= row, s, NEG)
                    m = jnp.max(s, axis=-1, keepdims=True)
                else:
                    own = slice(r % tq // LANES * LANES, (r % tq // LANES + 1) * LANES)
                    lane = lax.broadcasted_iota(jnp.int32, (FOX_ROWS, LANES), 1)
                    sub = lax.broadcasted_iota(jnp.int32, (FOX_ROWS, LANES), 0)
                    m = jnp.sum(jnp.where(lane == sub + r % LANES, s[:, own], 0.0), axis=-1, keepdims=True)
                    s = jnp.where(col <= row, s, NEG)
                m = jnp.broadcast_to(m, (FOX_ROWS, LANES))
                m_ref[rows, :] = m
            elif mode == "exact":
                m_old = m_ref[rows, :]
                m = jnp.maximum(m_old, jnp.max(s, axis=-1, keepdims=True))
                alpha = jnp.exp(m_old - m)
                m_ref[rows, :] = m
            else:
                m = m_ref[rows, :]
            lsum = None
            for g in range(tq // LANES):
                cols = slice(g * LANES, (g + 1) * LANES)
                pg = jnp.exp(s[:, cols] - m)
                p_ref[slot, rows, cols] = pg.astype(p_ref.dtype)
                lsum = pg if lsum is None else lsum + pg
            if mode in ("diagonal", "diagonal_self"):
                l_ref[rows, :] = lsum
            elif mode == "exact":
                l_ref[rows, :] = l_ref[rows, :] * alpha + lsum
                acc_ref[rows, :] = acc_ref[rows, :] * alpha
            else:
                l_ref[rows, :] = l_ref[rows, :] + lsum

    def values(slot, j, mode):
        pv = _dot(p_ref[slot], v_ref[0, key_rows(j), :])
        acc_ref[...] = pv if mode in ("diagonal", "diagonal_self") else acc_ref[...] + pv

    def tile_run(j, count, first_mode="lazy"):
        modes = [first_mode] + ["lazy"] * (count - 1)
        for n in range(min(FOX_AHEAD, count)):
            scores(n % n_slots, j - n)
        for n in range(count):
            softmax(n % n_slots, modes[n])
            if n + FOX_AHEAD < count:
                scores((n + FOX_AHEAD) % n_slots, j - n - FOX_AHEAD)
            values(n % n_slots, j - n, modes[n])

    def query_tile(qi, kmax, exact):
        q_rows = pl.ds(pl.multiple_of(qi * tq, tq), tq)
        qcat = q_ref[0, q_rows, :]
        zero = jnp.zeros_like(qcat)
        qs_ref[:tq, :] = jnp.where(lane2 < HEAD_DIM, qcat, zero)
        qs_ref[tq:, :] = jnp.where(lane2 >= HEAD_DIM, qcat, zero)

        kmax = [jnp.maximum(kmax[head], knorm_ref[base[head] + qi]) for head in range(2)]
        bound = [cfirst_ref[base[head] + qi] + 2.0 * NORM_SLACK * qnorm_ref[base[head] + qi] * kmax[head]
                 + SKIP_MARGIN for head in range(2)]

        def reaches(n):
            t = jnp.maximum(qi - 1 - n, 0)
            hit = (bound[0] - clast_ref[base[0] + t] > 0.0) | (bound[1] - clast_ref[base[1] + t] > 0.0)
            return (n < qi) & hit

        n_visit = lax.while_loop(reaches, lambda n: n + 1, 0)

        if exact:
            tile_run(qi, 1, "diagonal")

            def step(t, carry):
                tile_run(qi - 1 - t, 1, "exact")
                return carry

            lax.fori_loop(0, n_visit, step, 0)
        else:
            total = n_visit + 1
            count, first_len = FOX_TRIP, 1
            while count >= 1:
                fits = total >= count
                if count < FOX_TRIP:
                    fits = fits & (total < 2 * count)

                @pl.when(fits)
                def _(count=count):
                    tile_run(qi, count, "diagonal_self")

                first_len = jnp.where(fits, count, first_len)
                count //= 2

            def run(t, carry):
                tile_run(qi - first_len - FOX_TRIP * t, FOX_TRIP)
                return carry

            n_full = (total - first_len) // FOX_TRIP
            lax.fori_loop(0, n_full, run, 0)
            done = first_len + n_full * FOX_TRIP
            count = FOX_TRIP // 2
            while count >= 1:
                fits = (total - done) >= count

                @pl.when(fits)
                def _(first=qi - done, count=count):
                    tile_run(first, count)

                done = jnp.where(fits, done + count, done)
                count //= 2

        acc, lp = acc_ref[...], l_ref[...]
        out = acc / jnp.sum(lp, axis=-1, keepdims=True)
        o_ref[0, q_rows, :] = jnp.where(head_a, out[:tq], out[tq:]).astype(o_ref.dtype)
        if not exact:
            finite = (jnp.abs(acc) <= F32_BIG) & (lp <= F32_BIG)
            bad_ref[...] = jnp.maximum(bad_ref[...], jnp.where(finite, 0.0, 1.0))
        return kmax

    def sweep(exact):
        def body(qi, kmax):
            return tuple(query_tile(qi, kmax, exact))

        lax.fori_loop(0, n_tiles, body, (jnp.float32(0.0), jnp.float32(0.0)))

    bad_ref[...] = jnp.zeros_like(bad_ref)
    sweep(exact=False)

    @pl.when(jnp.max(bad_ref[...]) > 0.0)
    def _():
        sweep(exact=True)


def _fox(stats, qf, kf, vf, tq):
    b, s, _ = qf.shape
    smem = pl.BlockSpec(memory_space=pltpu.SMEM)
    rows = lambda width, dtype: pltpu.VMEM((2 * tq, width), dtype)
    n_slots = 2 * FOX_AHEAD
    return pl.pallas_call(
        functools.partial(_fox_body, tq=tq),
        grid=(b, N_PAIR),
        in_specs=[smem, smem, smem, smem,
                  pl.BlockSpec((1, s, 2 * LANES), lambda i, p: (i, 0, p)),
                  pl.BlockSpec((1, s, 2 * LANES), lambda i, p: (i, 0, p)),
                  pl.BlockSpec((1, s, LANES), lambda i, p: (i, 0, p))],
        out_specs=pl.BlockSpec((1, s, LANES), lambda i, p: (i, 0, p)),
        out_shape=jax.ShapeDtypeStruct((b, s, W_ATT), jnp.bfloat16),
        scratch_shapes=[rows(2 * LANES, jnp.bfloat16),
                        pltpu.VMEM((n_slots, 2 * tq, tq), jnp.float32),
                        pltpu.VMEM((n_slots, 2 * tq, tq), jnp.bfloat16),
                        rows(LANES, jnp.float32), rows(LANES, jnp.float32),
                        rows(LANES, jnp.float32), rows(LANES, jnp.float32)],
        compiler_params=pltpu.CompilerParams(
            dimension_semantics=("arbitrary", "arbitrary"), vmem_limit_bytes=VMEM_LIMIT),
        name="fox_attention",
    )(*stats, qf, kf, vf)


def _band_bias_body(g_ref, o_ref):
    width = g_ref.shape[-1]
    base = jnp.broadcast_to(g_ref[0], (Q_BLOCK, width))
    toeplitz = pltpu.roll(base, 0, 1, stride=1, stride_axis=0)[:, :BAND]
    qrow = lax.broadcasted_iota(jnp.int32, (Q_BLOCK, BAND), 0)
    kcol = lax.broadcasted_iota(jnp.int32, (Q_BLOCK, BAND), 1)
    cq = qrow // CHUNK
    ck = kcol // CHUNK - LEFT_CHUNKS
    valid = (ck <= cq) & (ck >= cq - LEFT_CHUNKS)
    o_ref[0] = jnp.where(valid, toeplitz, NEG)


def _band_bias(g_ext):
    h, _, width = g_ext.shape
    return pl.pallas_call(
        _band_bias_body,
        grid=(h,),
        in_specs=[pl.BlockSpec((1, 1, width), lambda i: (i, 0, 0))],
        out_specs=pl.BlockSpec((1, Q_BLOCK, BAND), lambda i: (i, 0, 0)),
        out_shape=jax.ShapeDtypeStruct((h, Q_BLOCK, BAND), jnp.float32),
        name="band_bias",
    )(g_ext)


def _chunk_body(q_ref, k_ref, v_ref, bias_ref, o_ref, kpad_ref, vpad_ref, s_ref, p_ref, l_ref, *, tq):
    qi = pl.program_id(2)
    s_len = k_ref.shape[1]

    @pl.when(qi == 0)
    def _():
        zeros = jnp.zeros((PAD, LANES), kpad_ref.dtype)
        kpad_ref[:PAD, :] = zeros
        vpad_ref[:PAD, :] = zeros
        kpad_ref[PAD:PAD + s_len, :] = k_ref[0]
        vpad_ref[PAD:PAD + s_len, :] = v_ref[0]

    head_a = lax.broadcasted_iota(jnp.int32, (1, LANES), 1) < HEAD_DIM
    kcol = lax.broadcasted_iota(jnp.int32, (1, BAND), 1)

    def offsets(blk):
        r0 = pl.multiple_of(blk * Q_BLOCK, Q_BLOCK)
        return r0, pl.multiple_of(qi * tq + r0, Q_BLOCK)

    def scores(slot, blk):
        r0, p0 = offsets(blk)
        q2 = q_ref[0, pl.ds(r0, Q_BLOCK), :]
        zero = jnp.zeros_like(q2)
        qs = jnp.concatenate([jnp.where(head_a, q2, zero), jnp.where(head_a, zero, q2)], axis=0)
        s_ref[slot] = _dot_nt(qs, kpad_ref[pl.ds(p0, BAND), :])

    def softmax(slot, blk, masked):
        _, p0 = offsets(blk)
        for r in range(0, 2 * Q_BLOCK, CHUNK_ROWS):
            rows = slice(r, r + CHUNK_ROWS)
            s = s_ref[slot, rows, :] + bias_ref[0, rows, :]
            if masked:
                s = jnp.where(kcol + p0 >= PAD, s, NEG)
            p = jnp.exp(s - jnp.max(s, axis=-1, keepdims=True))
            l_ref[slot, rows, :] = jnp.broadcast_to(jnp.sum(p, axis=-1, keepdims=True), (CHUNK_ROWS, LANES))
            p_ref[slot, rows, :] = p.astype(p_ref.dtype)

    def values(slot, blk):
        r0, p0 = offsets(blk)
        o = _dot(p_ref[slot], vpad_ref[pl.ds(p0, BAND), :]) / l_ref[slot]
        o_ref[0, pl.ds(r0, Q_BLOCK), :] = jnp.where(head_a, o[:Q_BLOCK], o[Q_BLOCK:]).astype(o_ref.dtype)

    def group(i, masked):
        first = i * CHUNK_GROUP
        scores(0, first)
        for n in range(CHUNK_GROUP):
            if n + 1 < CHUNK_GROUP:
                scores((n + 1) % 2, first + n + 1)
            softmax(n % 2, first + n, masked)
            values(n % 2, first + n)

    def masked_group(i, carry):
        group(i, True)
        return carry

    def plain_group(i, carry):
        group(i, False)
        return carry

    group_rows = Q_BLOCK * CHUNK_GROUP
    n_groups = tq // group_rows
    n_masked = jnp.clip((PAD - qi * tq + group_rows - 1) // group_rows, 0, n_groups)
    lax.fori_loop(0, n_masked, masked_group, 0)
    lax.fori_loop(n_masked, n_groups, plain_group, 0)


def _chunk_attention(qc, kc, vc, bias, tq):
    b, s, _ = qc.shape
    assert tq % (CHUNK_GROUP * Q_BLOCK) == 0
    return pl.pallas_call(
        functools.partial(_chunk_body, tq=tq),
        grid=(b, N_PAIR, s // tq),
        in_specs=[pl.BlockSpec((1, tq, LANES), lambda i, p, j: (i, j, p)),
                  pl.BlockSpec((1, s, LANES), lambda i, p, j: (i, 0, p)),
                  pl.BlockSpec((1, s, LANES), lambda i, p, j: (i, 0, p)),
                  pl.BlockSpec((1, 2 * Q_BLOCK, BAND), lambda i, p, j: (p, 0, 0))],
        out_specs=pl.BlockSpec((1, tq, LANES), lambda i, p, j: (i, j, p)),
        out_shape=jax.ShapeDtypeStruct((b, s, W_ATT), jnp.bfloat16),
        scratch_shapes=[pltpu.VMEM((PAD + s, LANES), jnp.bfloat16),
                        pltpu.VMEM((PAD + s, LANES), jnp.bfloat16),
                        pltpu.VMEM((2, 2 * Q_BLOCK, BAND), jnp.float32),
                        pltpu.VMEM((2, 2 * Q_BLOCK, BAND), jnp.bfloat16),
                        pltpu.VMEM((2, 2 * Q_BLOCK, LANES), jnp.float32)],
        compiler_params=pltpu.CompilerParams(
            dimension_semantics=("arbitrary", "arbitrary", "arbitrary"), vmem_limit_bytes=VMEM_LIMIT),
        name="chunk_attention",
    )(qc, kc, vc, bias)


def _out_ffn_body(x_ref, oa_ref, ob_ref, gate_ref, wa_ref, wb_ref, wo_ref, g2_ref, wu_ref, wd_ref, gf_ref,
                  o_ref, *, ff_chunk, final):
    d = x_ref.shape[-1]
    ya = _dot(oa_ref[0], wa_ref[...])
    yb = _dot(ob_ref[0], wb_ref[...])
    ga = gate_ref[0, :, :d].astype(jnp.float32)
    gb = gate_ref[0, :, d:].astype(jnp.float32)
    merged = (ga * ya + gb * yb).astype(jnp.bfloat16)
    x1 = x_ref[0] + _dot(merged, wo_ref[...])
    h2 = _rms(x1, g2_ref[...]).astype(jnp.bfloat16)
    acc = x1
    for c0 in range(0, wu_ref.shape[1], ff_chunk):
        u = jnp.maximum(_dot(h2, wu_ref[:, c0:c0 + ff_chunk]), 0.0)
        acc = acc + _dot((u * u).astype(jnp.bfloat16), wd_ref[c0:c0 + ff_chunk, :])
    if final:
        acc = _rms(acc, gf_ref[...])
    o_ref[0] = acc


def _out_ffn(x, oa, ob, gates, wa, wb, wo, g2, wu, wd, gf, tm, final):
    b, s, d = x.shape
    tok = lambda w: pl.BlockSpec((1, tm, w), lambda i, j: (i, j, 0))
    return pl.pallas_call(
        functools.partial(_out_ffn_body, ff_chunk=min(1024, wu.shape[1]), final=final),
        grid=(b, s // tm),
        in_specs=[tok(d), tok(W_ATT), tok(W_ATT), tok(2 * d)]
                 + [_const_spec(a.shape) for a in (wa, wb, wo, g2, wu, wd, gf)],
        out_specs=tok(d),
        out_shape=jax.ShapeDtypeStruct((b, s, d), jnp.float32),
        compiler_params=pltpu.CompilerParams(
            dimension_semantics=("arbitrary", "arbitrary"), vmem_limit_bytes=VMEM_LIMIT),
        name="out_ffn",
    )(x, oa, ob, gates, wa, wb, wo, g2, wu, wd, gf)


def _aug_placement():
    eq = [[0.0] * W_ATT for _ in range(LANES)]
    ek = [[0.0] * W_ATT for _ in range(LANES)]
    one = 3 * H_FOX
    for h in range(H_FOX):
        for piece in range(3):
            eq[piece * H_FOX + h][HEAD_DIM * h + piece] = 1.0
            eq[one][HEAD_DIM * h + 3 + piece] = 1.0
            ek[one][HEAD_DIM * h + piece] = 1.0
            ek[piece * H_FOX + h][HEAD_DIM * h + 3 + piece] = -1.0
    return jnp.array(eq, jnp.bfloat16), jnp.array(ek, jnp.bfloat16)


def kernel(x, norm1, w_in, forget_bias, rel_bias, w_branch_a, w_branch_b, w_out, norm2, w_up, w_down, final_norm):
    b, s, d = x.shape
    depth = w_in.shape[0]
    bf16 = jnp.bfloat16
    tm = min(512, s)
    assert tm // PROJ_SPLIT == FOX_TILE
    tqc = min(1024, s)
    eq, ek = _aug_placement()
    head_of_col = lax.broadcasted_iota(jnp.int32, (W_ATT, LANES), 0) // HEAD_DIM
    hsum = (head_of_col == lax.broadcasted_iota(jnp.int32, (W_ATT, LANES), 1)).astype(bf16)
    gf = final_norm.reshape(1, d)
    o = 3 * W_ATT + H_FOX
    for l in range(depth):
        w = w_in[l]
        wq = (w[:, :W_ATT] * SCALE).astype(bf16)
        wk = w[:, W_ATT:2 * W_ATT].astype(bf16)
        wv = w[:, 2 * W_ATT:3 * W_ATT].astype(bf16)
        wf = jnp.pad(w[:, 3 * W_ATT:o], ((0, 0), (0, LANES - H_FOX))).astype(bf16)
        wc = jnp.concatenate([w[:, o:o + W_ATT] * SCALE, w[:, o + W_ATT:o + 3 * W_ATT]], axis=1).astype(bf16)
        wg = w[:, o + 3 * W_ATT:].astype(bf16)
        bf = jnp.pad(forget_bias[l], (0, LANES - H_FOX)).reshape(1, LANES)
        qf, kf, vf, qc, kc, vc, gates, stat = _in_proj(
            x, norm1[l].reshape(1, d), wq, wk, wv, wc, wg, wf, bf, eq, ek, hsum, tm)

        stats = stat[:, :, :4, :H_FOX].transpose(2, 0, 3, 1).reshape(4, -1)
        o_a = _fox(tuple(stats), qf, kf, vf, FOX_TILE)

        far = rel_bias[l][:, 2 * MAX_REL:]
        g_ext = jnp.concatenate([jnp.broadcast_to(far, (H_CHK, PAD - MAX_REL + 1)),
                                 rel_bias[l][:, 2 * MAX_REL - 1:0:-1],
                                 jnp.broadcast_to(far, (H_CHK, Q_BLOCK))], axis=1)
        bias = _band_bias(g_ext.reshape(H_CHK, 1, -1)).reshape(N_PAIR, 2 * Q_BLOCK, BAND)
        o_b = _chunk_attention(qc, kc, vc, bias, tqc)

        x = _out_ffn(x, o_a, o_b, gates, w_branch_a[l].astype(bf16), w_branch_b[l].astype(bf16),
                     w_out[l].astype(bf16), norm2[l].reshape(1, d), w_up[l].astype(bf16),
                     w_down[l].astype(bf16), gf, tm, final=(l == depth - 1))
    return x
```

```python
import functools
import math

import jax
import jax.numpy as jnp
from jax import lax
from jax.experimental import pallas as pl
from jax.experimental.pallas import tpu as pltpu

HEAD_DIM = 64
H_FOX = 8
H_CHK = 8
N_PAIR = 4
W_ATT = H_FOX * HEAD_DIM
CHUNK = 64
Q_BLOCK = 128
LEFT_CHUNKS = 8
PAD = LEFT_CHUNKS * CHUNK
BAND = PAD + Q_BLOCK
MAX_REL = 128
EPS = 1e-6
NEG = -1e30
LANES = 128
SCALE = 1.0 / math.sqrt(HEAD_DIM)
F32_BIG = 3.0e38
PROJ_SPLIT = 2
FOX_ROWS = 32
FOX_TRIP = 4
CHUNK_ROWS = 32
CHUNK_GROUP = 8
SKIP_MARGIN = 30.0
NORM_SLACK = 1.02
VMEM_LIMIT = 60 * 1024 * 1024
COL_Q, COL_K, COL_V, COL_C = 0, W_ATT, 2 * W_ATT, 3 * W_ATT
COL_G = COL_C + 3 * W_ATT

_NT = (((1,), (1,)), ((), ()))


def _dot(a, b):
    return jnp.dot(a, b, preferred_element_type=jnp.float32)


def _dot_nt(a, b):
    return lax.dot_general(a, b, _NT, preferred_element_type=jnp.float32)


def _rms(x, g):
    ms = jnp.mean(x * x, axis=-1, keepdims=True)
    return x * lax.rsqrt(ms + EPS) * g


def _const_spec(shape):
    nd = len(shape)
    return pl.BlockSpec(shape, lambda *_: (0,) * nd, pipeline_mode=pl.Buffered(1))


def _in_proj_body(x_ref, g_ref, w_ref, bf_ref, eq_ref, ek_ref, hsum_ref,
                  qf_ref, kf_ref, vf_ref, qc_ref, kc_ref, vc_ref, gate_ref, stat_ref, carry_ref):
    tm = x_ref.shape[1]
    th = tm // PROJ_SPLIT
    col_f = w_ref.shape[1] - LANES

    @pl.when(pl.program_id(1) == 0)
    def _():
        carry_ref[...] = jnp.zeros_like(carry_ref)

    lane = lax.broadcasted_iota(jnp.int32, (th, LANES), 1)
    row = lax.broadcasted_iota(jnp.int32, (th, LANES), 0)

    def max_norm(t):
        t = t.astype(jnp.float32)
        sq = _dot((t * t).astype(jnp.bfloat16), hsum_ref[...])
        return jnp.max(sq, axis=0, keepdims=True)

    c_first, c_last, q_sq, k_sq = None, None, None, None
    for part in range(PROJ_SPLIT):
        rows = slice(part * th, (part + 1) * th)
        h = _rms(x_ref[0, rows, :], g_ref[...]).astype(jnp.bfloat16)

        logf = jax.nn.log_sigmoid(_dot(h, w_ref[:, col_f:col_f + LANES]) + bf_ref[...])
        c = jnp.where(lane < H_FOX, logf, 0.0)
        k = 1
        while k < th:
            c = c + jnp.where(row >= k, pltpu.roll(c, k, 0), 0.0)
            k *= 2
        c = c + carry_ref[...]
        carry_ref[...] = c[th - 1:th, :]
        c_first = c[0:1, :] if part == 0 else c_first
        c_last = c[th - 1:th, :]

        hi = c.astype(jnp.bfloat16).astype(jnp.float32)
        r1 = c - hi
        mid = r1.astype(jnp.bfloat16).astype(jnp.float32)
        lo = (r1 - mid).astype(jnp.bfloat16).astype(jnp.float32)
        pieces = hi + pltpu.roll(mid, H_FOX, 1) + pltpu.roll(lo, 2 * H_FOX, 1)
        pieces = jnp.where(lane == 3 * H_FOX, 1.0, pieces).astype(jnp.bfloat16)
        aug_q = _dot(pieces, eq_ref[...]).astype(jnp.bfloat16)
        aug_k = _dot(pieces, ek_ref[...]).astype(jnp.bfloat16)

        q = _dot(h, w_ref[:, COL_Q:COL_K]).astype(jnp.bfloat16)
        kk = _dot(h, w_ref[:, COL_K:COL_V]).astype(jnp.bfloat16)
        for p in range(N_PAIR):
            src = slice(p * LANES, (p + 1) * LANES)
            qf_ref[0, rows, 2 * p * LANES:(2 * p + 1) * LANES] = q[:, src]
            qf_ref[0, rows, (2 * p + 1) * LANES:(2 * p + 2) * LANES] = aug_q[:, src]
            kf_ref[0, rows, 2 * p * LANES:(2 * p + 1) * LANES] = kk[:, src]
            kf_ref[0, rows, (2 * p + 1) * LANES:(2 * p + 2) * LANES] = aug_k[:, src]
        vf_ref[0, rows, :] = _dot(h, w_ref[:, COL_V:COL_C]).astype(jnp.bfloat16)
        q_sq = max_norm(q) if part == 0 else jnp.maximum(q_sq, max_norm(q))
        k_sq = max_norm(kk) if part == 0 else jnp.maximum(k_sq, max_norm(kk))

        pc = _dot(h, w_ref[:, COL_C:COL_G])
        qc_ref[0, rows, :] = pc[:, :W_ATT].astype(jnp.bfloat16)
        kc_ref[0, rows, :] = pc[:, W_ATT:2 * W_ATT].astype(jnp.bfloat16)
        vc_ref[0, rows, :] = pc[:, 2 * W_ATT:].astype(jnp.bfloat16)
        gate_ref[0, rows, :] = jax.nn.sigmoid(_dot(h, w_ref[:, COL_G:col_f])).astype(jnp.bfloat16)

    sub = lax.broadcasted_iota(jnp.int32, (8, LANES), 0)
    stat = jnp.where(sub == 0, c_first, 0.0)
    stat = jnp.where(sub == 1, c_last, stat)
    stat = jnp.where(sub == 2, jnp.sqrt(q_sq), stat)
    stat_ref[0, 0] = jnp.where(sub == 3, jnp.sqrt(k_sq), stat)


def _in_proj(x, g, w_all, bf, eq, ek, hsum, tm):
    b, s, d = x.shape
    tok = lambda width: pl.BlockSpec((1, tm, width), lambda i, j: (i, j, 0))
    bf16 = jnp.bfloat16
    out_shape = [jax.ShapeDtypeStruct((b, s, width), bf16)
                 for width in (2 * W_ATT, 2 * W_ATT, W_ATT, W_ATT, W_ATT, W_ATT, 2 * d)]
    out_shape.append(jax.ShapeDtypeStruct((b, s // tm, 8, LANES), jnp.float32))
    consts = (g, w_all, bf, eq, ek, hsum)
    return pl.pallas_call(
        _in_proj_body,
        grid=(b, s // tm),
        in_specs=[tok(d)] + [_const_spec(a.shape) for a in consts],
        out_specs=[tok(sh.shape[-1]) for sh in out_shape[:-1]]
                  + [pl.BlockSpec((1, 1, 8, LANES), lambda i, j: (i, j, 0, 0))],
        out_shape=out_shape,
        scratch_shapes=[pltpu.VMEM((1, LANES), jnp.float32)],
        compiler_params=pltpu.CompilerParams(
            dimension_semantics=("arbitrary", "arbitrary"), vmem_limit_bytes=VMEM_LIMIT),
        name="in_proj",
    )(x, *consts)


def _fox_body(cfirst_ref, clast_ref, qnorm_ref, knorm_ref, q_ref, k_ref, v_ref, o_ref,
              s_ref, p_ref, m_ref, l_ref, acc_ref, bad_ref, *, tq):
    n_tiles = q_ref.shape[1] // tq
    lane2 = lax.broadcasted_iota(jnp.int32, (1, 2 * LANES), 1) % LANES
    head_a = lax.broadcasted_iota(jnp.int32, (1, LANES), 1) < HEAD_DIM
    base = [(pl.program_id(0) * H_FOX + 2 * pl.program_id(1) + head) * n_tiles for head in range(2)]

    def key_rows(j):
        return pl.ds(pl.multiple_of(j * tq, tq), tq)

    def scores(slot, q_head, j):
        s_ref[slot] = _dot_nt(q_head, k_ref[0, key_rows(j), :])

    def softmax(slot, head, mode):
        for r in range(0, tq, FOX_ROWS):
            rows = slice(r, r + FOX_ROWS)
            s = s_ref[slot, rows, :]
            if mode in ("diagonal", "diagonal_self"):
                row = lax.broadcasted_iota(jnp.int32, (FOX_ROWS, tq), 0) + r
                col = lax.broadcasted_iota(jnp.int32, (FOX_ROWS, tq), 1)
                if mode == "diagonal":
                    s = jnp.where(col <= row, s, NEG)
                    m = jnp.max(s, axis=-1, keepdims=True)
                else:
                    own = slice(r // LANES * LANES, (r // LANES + 1) * LANES)
                    lane = lax.broadcasted_iota(jnp.int32, (FOX_ROWS, LANES), 1)
                    sub = lax.broadcasted_iota(jnp.int32, (FOX_ROWS, LANES), 0)
                    m = jnp.sum(jnp.where(lane == sub + r % LANES, s[:, own], 0.0), axis=-1, keepdims=True)
                    s = jnp.where(col <= row, s, NEG)
                m = jnp.broadcast_to(m, (FOX_ROWS, LANES))
                m_ref[head, rows, :] = m
            elif mode == "exact":
                m_old = m_ref[head, rows, :]
                m = jnp.maximum(m_old, jnp.max(s, axis=-1, keepdims=True))
                alpha = jnp.exp(m_old - m)
                m_ref[head, rows, :] = m
            else:
                m = m_ref[head, rows, :]
            lsum = None
            for g in range(tq // LANES):
                cols = slice(g * LANES, (g + 1) * LANES)
                pg = jnp.exp(s[:, cols] - m)
                p_ref[slot, rows, cols] = pg.astype(p_ref.dtype)
                lsum = pg if lsum is None else lsum + pg
            if mode in ("diagonal", "diagonal_self"):
                l_ref[head, rows, :] = lsum
            elif mode == "exact":
                l_ref[head, rows, :] = l_ref[head, rows, :] * alpha + lsum
                acc_ref[head, rows, :] = acc_ref[head, rows, :] * alpha
            else:
                l_ref[head, rows, :] = l_ref[head, rows, :] + lsum

    def values(slot, head, j, mode):
        pv = _dot(p_ref[slot], v_ref[0, key_rows(j), :])
        acc_ref[head] = pv if mode in ("diagonal", "diagonal_self") else acc_ref[head] + pv

    def unit_run(q_of, units):
        n_slots = s_ref.shape[0]
        for n in range(min(2, len(units))):
            scores(n, q_of(units[n][1]), units[n][0])
        for n, (tile, head, mode) in enumerate(units):
            softmax(n % n_slots, head, mode)
            if n + 2 < len(units):
                scores((n + 2) % n_slots, q_of(units[n + 2][1]), units[n + 2][0])
            values(n % n_slots, head, tile, mode)

    def run_tiles(q_of, make_units, first, total):
        def full(t, carry):
            unit_run(q_of, make_units(first - FOX_TRIP * t, FOX_TRIP))
            return carry

        n_full = total // FOX_TRIP
        lax.fori_loop(0, n_full, full, 0)
        done = n_full * FOX_TRIP
        count = FOX_TRIP // 2
        while count >= 1:
            fits = (total - done) >= count

            @pl.when(fits)
            def _(start=first - done, count=count):
                unit_run(q_of, make_units(start, count))

            done = jnp.where(fits, done + count, done)
            count //= 2

    def query_tile(qi, kmax, exact):
        q_rows = pl.ds(pl.multiple_of(qi * tq, tq), tq)
        qcat = q_ref[0, q_rows, :]
        zero = jnp.zeros_like(qcat)
        qh = (jnp.where(lane2 < HEAD_DIM, qcat, zero), jnp.where(lane2 >= HEAD_DIM, qcat, zero))

        kmax = [jnp.maximum(kmax[head], knorm_ref[base[head] + qi]) for head in range(2)]
        reach = []
        for head in range(2):
            bound = (cfirst_ref[base[head] + qi] + 2.0 * NORM_SLACK * qnorm_ref[base[head] + qi] * kmax[head]
                     + SKIP_MARGIN)

            def reaches(n, head=head, bound=bound):
                t = jnp.maximum(qi - 1 - n, 0)
                return (n < qi) & (bound - clast_ref[base[head] + t] > 0.0)

            reach.append(lax.while_loop(reaches, lambda n: n + 1, 0))
        n_both, n_long = jnp.minimum(reach[0], reach[1]), jnp.maximum(reach[0], reach[1])

        def both_heads(j, count, first_mode="lazy"):
            return [(j - i, head, first_mode if i == 0 else "lazy") for i in range(count) for head in range(2)]

        if exact:
            unit_run(qh.__getitem__, both_heads(qi, 1, "diagonal"))

            def step(t, carry):
                unit_run(qh.__getitem__, both_heads(qi - 1 - t, 1, "exact"))
                return carry

            lax.fori_loop(0, n_long, step, 0)
        else:
            total = n_both + 1
            count, first_len = FOX_TRIP, 1
            while count >= 1:
                fits = total >= count
                if count < FOX_TRIP:
                    fits = fits & (total < 2 * count)

                @pl.when(fits)
                def _(count=count):
                    unit_run(qh.__getitem__, both_heads(qi, count, "diagonal_self"))

                first_len = jnp.where(fits, count, first_len)
                count //= 2
            run_tiles(qh.__getitem__, both_heads, qi - first_len, total - first_len)

            long_head = jnp.where(reach[1] > reach[0], 1, 0)
            q_long = jnp.where((lane2 < HEAD_DIM) == (long_head == 0), qcat, zero)
            run_tiles(lambda head: q_long, lambda j, count: [(j - i, long_head, "lazy") for i in range(count)],
                      qi - total, n_long - n_both)

        outs = []
        for head in range(2):
            acc, lp = acc_ref[head], l_ref[head]
            outs.append(acc / jnp.sum(lp, axis=-1, keepdims=True))
            if not exact:
                finite = (jnp.abs(acc) <= F32_BIG) & (lp <= F32_BIG)
                bad_ref[...] = jnp.maximum(bad_ref[...], jnp.where(finite, 0.0, 1.0))
        o_ref[0, q_rows, :] = jnp.where(head_a, outs[0], outs[1]).astype(o_ref.dtype)
        return kmax

    def sweep(exact):
        def body(qi, kmax):
            return tuple(query_tile(qi, kmax, exact))

        lax.fori_loop(0, n_tiles, body, (jnp.float32(0.0), jnp.float32(0.0)))

    bad_ref[...] = jnp.zeros_like(bad_ref)
    sweep(exact=False)

    @pl.when(jnp.max(bad_ref[...]) > 0.0)
    def _():
        sweep(exact=True)


def _fox(stats, qf, kf, vf, tq):
    b, s, _ = qf.shape
    smem = pl.BlockSpec(memory_space=pltpu.SMEM)
    return pl.pallas_call(
        functools.partial(_fox_body, tq=tq),
        grid=(b, N_PAIR),
        in_specs=[smem, smem, smem, smem,
                  pl.BlockSpec((1, s, 2 * LANES), lambda i, p: (i, 0, p)),
                  pl.BlockSpec((1, s, 2 * LANES), lambda i, p: (i, 0, p)),
                  pl.BlockSpec((1, s, LANES), lambda i, p: (i, 0, p))],
        out_specs=pl.BlockSpec((1, s, LANES), lambda i, p: (i, 0, p)),
        out_shape=jax.ShapeDtypeStruct((b, s, W_ATT), jnp.bfloat16),
        scratch_shapes=[pltpu.VMEM((4, tq, tq), jnp.float32),
                        pltpu.VMEM((4, tq, tq), jnp.bfloat16),
                        pltpu.VMEM((2, tq, LANES), jnp.float32),
                        pltpu.VMEM((2, tq, LANES), jnp.float32),
                        pltpu.VMEM((2, tq, LANES), jnp.float32),
                        pltpu.VMEM((tq, LANES), jnp.float32)],
        compiler_params=pltpu.CompilerParams(
            dimension_semantics=("arbitrary", "arbitrary"), vmem_limit_bytes=VMEM_LIMIT),
        name="fox_attention",
    )(*stats, qf, kf, vf)


def _band_bias_body(g_ref, o_ref):
    width = g_ref.shape[-1]
    base = jnp.broadcast_to(g_ref[0], (Q_BLOCK, width))
    toeplitz = pltpu.roll(base, 0, 1, stride=1, stride_axis=0)[:, :BAND]
    qrow = lax.broadcasted_iota(jnp.int32, (Q_BLOCK, BAND), 0)
    kcol = lax.broadcasted_iota(jnp.int32, (Q_BLOCK, BAND), 1)
    cq = qrow // CHUNK
    ck = kcol // CHUNK - LEFT_CHUNKS
    valid = (ck <= cq) & (ck >= cq - LEFT_CHUNKS)
    o_ref[0] = jnp.where(valid, toeplitz, NEG)


def _band_bias(g_ext):
    h, _, width = g_ext.shape
    return pl.pallas_call(
        _band_bias_body,
        grid=(h,),
        in_specs=[pl.BlockSpec((1, 1, width), lambda i: (i, 0, 0))],
        out_specs=pl.BlockSpec((1, Q_BLOCK, BAND), lambda i: (i, 0, 0)),
        out_shape=jax.ShapeDtypeStruct((h, Q_BLOCK, BAND), jnp.float32),
        name="band_bias",
    )(g_ext)


def _chunk_body(q_ref, k_ref, v_ref, bias_ref, o_ref, kpad_ref, vpad_ref, s_ref, p_ref, l_ref, *, tq):
    qi = pl.program_id(2)
    s_len = k_ref.shape[1]

    @pl.when(qi == 0)
    def _():
        zeros = jnp.zeros((PAD, LANES), kpad_ref.dtype)
        kpad_ref[:PAD, :] = zeros
        vpad_ref[:PAD, :] = zeros
        kpad_ref[PAD:PAD + s_len, :] = k_ref[0]
        vpad_ref[PAD:PAD + s_len, :] = v_ref[0]

    head_a = lax.broadcasted_iota(jnp.int32, (1, LANES), 1) < HEAD_DIM
    kcol = lax.broadcasted_iota(jnp.int32, (1, BAND), 1)

    def offsets(blk):
        r0 = pl.multiple_of(blk * Q_BLOCK, Q_BLOCK)
        return r0, pl.multiple_of(qi * tq + r0, Q_BLOCK)

    def scores(slot, blk):
        r0, p0 = offsets(blk)
        q2 = q_ref[0, pl.ds(r0, Q_BLOCK), :]
        zero = jnp.zeros_like(q2)
        qs = jnp.concatenate([jnp.where(head_a, q2, zero), jnp.where(head_a, zero, q2)], axis=0)
        s_ref[slot] = _dot_nt(qs, kpad_ref[pl.ds(p0, BAND), :])

    def softmax(slot, blk, masked):
        _, p0 = offsets(blk)
        for r in range(0, 2 * Q_BLOCK, CHUNK_ROWS):
            rows = slice(r, r + CHUNK_ROWS)
            s = s_ref[slot, rows, :] + bias_ref[0, rows, :]
            if masked:
                s = jnp.where(kcol + p0 >= PAD, s, NEG)
            p = jnp.exp(s - jnp.max(s, axis=-1, keepdims=True))
            l_ref[slot, rows, :] = jnp.broadcast_to(jnp.sum(p, axis=-1, keepdims=True), (CHUNK_ROWS, LANES))
            p_ref[slot, rows, :] = p.astype(p_ref.dtype)

    def values(slot, blk):
        r0, p0 = offsets(blk)
        o = _dot(p_ref[slot], vpad_ref[pl.ds(p0, BAND), :]) / l_ref[slot]
        o_ref[0, pl.ds(r0, Q_BLOCK), :] = jnp.where(head_a, o[:Q_BLOCK], o[Q_BLOCK:]).astype(o_ref.dtype)

    def group(i, masked):
        first = i * CHUNK_GROUP
        scores(0, first)
        for n in range(CHUNK_GROUP):
            if n + 1 < CHUNK_GROUP:
                scores((n + 1) % 2, first + n + 1)
            softmax(n % 2, first + n, masked)
            values(n % 2, first + n)

    def masked_group(i, carry):
        group(i, True)
        return carry

    def plain_group(i, carry):
        group(i, False)
        return carry

    group_rows = Q_BLOCK * CHUNK_GROUP
    n_groups = tq // group_rows
    n_masked = jnp.clip((PAD - qi * tq + group_rows - 1) // group_rows, 0, n_groups)
    lax.fori_loop(0, n_masked, masked_group, 0)
    lax.fori_loop(n_masked, n_groups, plain_group, 0)


def _chunk_attention(qc, kc, vc, bias, tq):
    b, s, _ = qc.shape
    assert tq % (CHUNK_GROUP * Q_BLOCK) == 0
    return pl.pallas_call(
        functools.partial(_chunk_body, tq=tq),
        grid=(b, N_PAIR, s // tq),
        in_specs=[pl.BlockSpec((1, tq, LANES), lambda i, p, j: (i, j, p)),
                  pl.BlockSpec((1, s, LANES), lambda i, p, j: (i, 0, p)),
                  pl.BlockSpec((1, s, LANES), lambda i, p, j: (i, 0, p)),
                  pl.BlockSpec((1, 2 * Q_BLOCK, BAND), lambda i, p, j: (p, 0, 0))],
        out_specs=pl.BlockSpec((1, tq, LANES), lambda i, p, j: (i, j, p)),
        out_shape=jax.ShapeDtypeStruct((b, s, W_ATT), jnp.bfloat16),
        scratch_shapes=[pltpu.VMEM((PAD + s, LANES), jnp.bfloat16),
                        pltpu.VMEM((PAD + s, LANES), jnp.bfloat16),
                        pltpu.VMEM((2, 2 * Q_BLOCK, BAND), jnp.float32),
                        pltpu.VMEM((2, 2 * Q_BLOCK, BAND), jnp.bfloat16),
                        pltpu.VMEM((2, 2 * Q_BLOCK, LANES), jnp.float32)],
        compiler_params=pltpu.CompilerParams(
            dimension_semantics=("arbitrary", "arbitrary", "arbitrary"), vmem_limit_bytes=VMEM_LIMIT),
        name="chunk_attention",
    )(qc, kc, vc, bias)


def _out_ffn_body(x_ref, oa_ref, ob_ref, gate_ref, wa_ref, wb_ref, wo_ref, g2_ref, wu_ref, wd_ref, gf_ref,
                  o_ref, *, ff_chunk, final):
    d = x_ref.shape[-1]
    th = x_ref.shape[1] // PROJ_SPLIT
    for part in range(PROJ_SPLIT):
        rows = slice(part * th, (part + 1) * th)
        ya = _dot(oa_ref[0, rows, :], wa_ref[...])
        yb = _dot(ob_ref[0, rows, :], wb_ref[...])
        ga = gate_ref[0, rows, :d].astype(jnp.float32)
        gb = gate_ref[0, rows, d:].astype(jnp.float32)
        merged = (ga * ya + gb * yb).astype(jnp.bfloat16)
        x1 = x_ref[0, rows, :] + _dot(merged, wo_ref[...])
        h2 = _rms(x1, g2_ref[...]).astype(jnp.bfloat16)
        acc = x1
        for c0 in range(0, wu_ref.shape[1], ff_chunk):
            u = jnp.maximum(_dot(h2, wu_ref[:, c0:c0 + ff_chunk]), 0.0)
            acc = acc + _dot((u * u).astype(jnp.bfloat16), wd_ref[c0:c0 + ff_chunk, :])
        if final:
            acc = _rms(acc, gf_ref[...])
        o_ref[0, rows, :] = acc


def _out_ffn(x, oa, ob, gates, wa, wb, wo, g2, wu, wd, gf, tm, final):
    b, s, d = x.shape
    tok = lambda w: pl.BlockSpec((1, tm, w), lambda i, j: (i, j, 0))
    return pl.pallas_call(
        functools.partial(_out_ffn_body, ff_chunk=min(1024, wu.shape[1]), final=final),
        grid=(b, s // tm),
        in_specs=[tok(d), tok(W_ATT), tok(W_ATT), tok(2 * d)]
                 + [_const_spec(a.shape) for a in (wa, wb, wo, g2, wu, wd, gf)],
        out_specs=tok(d),
        out_shape=jax.ShapeDtypeStruct((b, s, d), jnp.float32),
        compiler_params=pltpu.CompilerParams(
            dimension_semantics=("arbitrary", "arbitrary"), vmem_limit_bytes=VMEM_LIMIT),
        name="out_ffn",
    )(x, oa, ob, gates, wa, wb, wo, g2, wu, wd, gf)


def _aug_placement():
    eq = [[0.0] * W_ATT for _ in range(LANES)]
    ek = [[0.0] * W_ATT for _ in range(LANES)]
    one = 3 * H_FOX
    for h in range(H_FOX):
        for piece in range(3):
            eq[piece * H_FOX + h][HEAD_DIM * h + piece] = 1.0
            eq[one][HEAD_DIM * h + 3 + piece] = 1.0
            ek[one][HEAD_DIM * h + piece] = 1.0
            ek[piece * H_FOX + h][HEAD_DIM * h + 3 + piece] = -1.0
    return jnp.array(eq, jnp.bfloat16), jnp.array(ek, jnp.bfloat16)


def kernel(x, norm1, w_in, forget_bias, rel_bias, w_branch_a, w_branch_b, w_out, norm2, w_up, w_down, final_norm):
    b, s, d = x.shape
    depth = w_in.shape[0]
    bf16 = jnp.bfloat16
    tm = min(512, s)
    tqc = min(1024, s)
    eq, ek = _aug_placement()
    head_of_col = lax.broadcasted_iota(jnp.int32, (W_ATT, LANES), 0) // HEAD_DIM
    hsum = (head_of_col == lax.broadcasted_iota(jnp.int32, (W_ATT, LANES), 1)).astype(bf16)
    gf = final_norm.reshape(1, d)
    o = 3 * W_ATT + H_FOX
    col = lax.broadcasted_iota(jnp.int32, (1, w_in.shape[-1]), 1)
    col_scale = jnp.where((col < W_ATT) | ((col >= o) & (col < o + W_ATT)), SCALE, 1.0)
    for l in range(depth):
        w = w_in[l] * col_scale
        w_all = jnp.concatenate([w[:, :3 * W_ATT], w[:, o:], w[:, 3 * W_ATT:o],
                                 jnp.zeros((d, LANES - H_FOX), w.dtype)], axis=1).astype(bf16)
        assert w_all.shape[1] == COL_G + 2 * d + LANES
        bf = jnp.pad(forget_bias[l], (0, LANES - H_FOX)).reshape(1, LANES)
        qf, kf, vf, qc, kc, vc, gates, stat = _in_proj(
            x, norm1[l].reshape(1, d), w_all, bf, eq, ek, hsum, tm)

        stats = stat[:, :, :4, :H_FOX].transpose(2, 0, 3, 1).reshape(4, -1)
        o_a = _fox(tuple(stats), qf, kf, vf, tm)

        far = rel_bias[l][:, 2 * MAX_REL:]
        g_ext = jnp.concatenate([jnp.broadcast_to(far, (H_CHK, PAD - MAX_REL + 1)),
                                 rel_bias[l][:, 2 * MAX_REL - 1:0:-1],
                                 jnp.broadcast_to(far, (H_CHK, Q_BLOCK))], axis=1)
        bias = _band_bias(g_ext.reshape(H_CHK, 1, -1)).reshape(N_PAIR, 2 * Q_BLOCK, BAND)
        o_b = _chunk_attention(qc, kc, vc, bias, tqc)

        x = _out_ffn(x, o_a, o_b, gates, w_branch_a[l].astype(bf16), w_branch_b[l].astype(bf16),
                     w_out[l].astype(bf16), norm2[l].reshape(1, d), w_up[l].astype(bf16),
                     w_down[l].astype(bf16), gf, tm, final=(l == depth - 1))
    return x
```

```python
import functools
import math

import jax
import jax.numpy as jnp
from jax import lax
from jax.experimental import pallas as pl
from jax.experimental.pallas import tpu as pltpu

HEAD_DIM = 64
H_FOX = 8
H_CHK = 8
N_PAIR = 4
W_ATT = H_FOX * HEAD_DIM
CHUNK = 64
Q_BLOCK = 128
LEFT_CHUNKS = 8
PAD = LEFT_CHUNKS * CHUNK
BAND = PAD + Q_BLOCK
MAX_REL = 128
EPS = 1e-6
NEG = -1e30
LANES = 128
SCALE = 1.0 / math.sqrt(HEAD_DIM)
F32_BIG = 3.0e38
PROJ_SPLIT = 2
FOX_ROWS = 32
FOX_TRIP = 4
CHUNK_ROWS = 32
CHUNK_GROUP = 8
SKIP_MARGIN = 30.0
NORM_SLACK = 1.02
VMEM_LIMIT = 60 * 1024 * 1024

_NT = (((1,), (1,)), ((), ()))


def _dot(a, b):
    return jnp.dot(a, b, preferred_element_type=jnp.float32)


def _dot_nt(a, b):
    return lax.dot_general(a, b, _NT, preferred_element_type=jnp.float32)


def _rms(x, g):
    ms = jnp.mean(x * x, axis=-1, keepdims=True)
    return x * lax.rsqrt(ms + EPS) * g


def _const_spec(shape):
    nd = len(shape)
    return pl.BlockSpec(shape, lambda *_: (0,) * nd, pipeline_mode=pl.Buffered(1))


def _in_proj_body(x_ref, g_ref, wq_ref, wk_ref, wv_ref, wc_ref, wg_ref, wf_ref, bf_ref, eq_ref, ek_ref, hsum_ref,
                  qf_ref, kf_ref, vf_ref, qc_ref, kc_ref, vc_ref, gate_ref, stat_ref, carry_ref):
    tm = x_ref.shape[1]
    th = tm // PROJ_SPLIT

    @pl.when(pl.program_id(1) == 0)
    def _():
        carry_ref[...] = jnp.zeros_like(carry_ref)

    lane = lax.broadcasted_iota(jnp.int32, (th, LANES), 1)
    row = lax.broadcasted_iota(jnp.int32, (th, LANES), 0)

    def max_norm(t):
        t = t.astype(jnp.float32)
        sq = _dot((t * t).astype(jnp.bfloat16), hsum_ref[...])
        return jnp.max(sq, axis=0, keepdims=True)

    c_first, c_last, q_sq, k_sq = None, None, None, None
    for part in range(PROJ_SPLIT):
        rows = slice(part * th, (part + 1) * th)
        h = _rms(x_ref[0, rows, :], g_ref[...]).astype(jnp.bfloat16)

        logf = jax.nn.log_sigmoid(_dot(h, wf_ref[...]) + bf_ref[...])
        c = jnp.where(lane < H_FOX, logf, 0.0)
        k = 1
        while k < th:
            c = c + jnp.where(row >= k, pltpu.roll(c, k, 0), 0.0)
            k *= 2
        c = c + carry_ref[...]
        carry_ref[...] = c[th - 1:th, :]
        c_first = c[0:1, :] if part == 0 else c_first
        c_last = c[th - 1:th, :]

        hi = c.astype(jnp.bfloat16).astype(jnp.float32)
        r1 = c - hi
        mid = r1.astype(jnp.bfloat16).astype(jnp.float32)
        lo = (r1 - mid).astype(jnp.bfloat16).astype(jnp.float32)
        pieces = hi + pltpu.roll(mid, H_FOX, 1) + pltpu.roll(lo, 2 * H_FOX, 1)
        pieces = jnp.where(lane == 3 * H_FOX, 1.0, pieces).astype(jnp.bfloat16)
        aug_q = _dot(pieces, eq_ref[...]).astype(jnp.bfloat16)
        aug_k = _dot(pieces, ek_ref[...]).astype(jnp.bfloat16)

        q = _dot(h, wq_ref[...]).astype(jnp.bfloat16)
        kk = _dot(h, wk_ref[...]).astype(jnp.bfloat16)
        for p in range(N_PAIR):
            src = slice(p * LANES, (p + 1) * LANES)
            qf_ref[0, rows, 2 * p * LANES:(2 * p + 1) * LANES] = q[:, src]
            qf_ref[0, rows, (2 * p + 1) * LANES:(2 * p + 2) * LANES] = aug_q[:, src]
            kf_ref[0, rows, 2 * p * LANES:(2 * p + 1) * LANES] = kk[:, src]
            kf_ref[0, rows, (2 * p + 1) * LANES:(2 * p + 2) * LANES] = aug_k[:, src]
        vf_ref[0, rows, :] = _dot(h, wv_ref[...]).astype(jnp.bfloat16)
        q_sq = max_norm(q) if part == 0 else jnp.maximum(q_sq, max_norm(q))
        k_sq = max_norm(kk) if part == 0 else jnp.maximum(k_sq, max_norm(kk))

        pc = _dot(h, wc_ref[...])
        qc_ref[0, rows, :] = pc[:, :W_ATT].astype(jnp.bfloat16)
        kc_ref[0, rows, :] = pc[:, W_ATT:2 * W_ATT].astype(jnp.bfloat16)
        vc_ref[0, rows, :] = pc[:, 2 * W_ATT:].astype(jnp.bfloat16)
        gate_ref[0, rows, :] = jax.nn.sigmoid(_dot(h, wg_ref[...])).astype(jnp.bfloat16)

    sub = lax.broadcasted_iota(jnp.int32, (8, LANES), 0)
    stat = jnp.where(sub == 0, c_first, 0.0)
    stat = jnp.where(sub == 1, c_last, stat)
    stat = jnp.where(sub == 2, jnp.sqrt(q_sq), stat)
    stat_ref[0, 0] = jnp.where(sub == 3, jnp.sqrt(k_sq), stat)


def _in_proj(x, g, wq, wk, wv, wc, wg, wf, bf, eq, ek, hsum, tm):
    b, s, d = x.shape
    tok = lambda width: pl.BlockSpec((1, tm, width), lambda i, j: (i, j, 0))
    bf16 = jnp.bfloat16
    out_shape = [jax.ShapeDtypeStruct((b, s, width), bf16)
                 for width in (2 * W_ATT, 2 * W_ATT, W_ATT, W_ATT, W_ATT, W_ATT, 2 * d)]
    out_shape.append(jax.ShapeDtypeStruct((b, s // tm, 8, LANES), jnp.float32))
    consts = (g, wq, wk, wv, wc, wg, wf, bf, eq, ek, hsum)
    return pl.pallas_call(
        _in_proj_body,
        grid=(b, s // tm),
        in_specs=[tok(d)] + [_const_spec(a.shape) for a in consts],
        out_specs=[tok(sh.shape[-1]) for sh in out_shape[:-1]]
                  + [pl.BlockSpec((1, 1, 8, LANES), lambda i, j: (i, j, 0, 0))],
        out_shape=out_shape,
        scratch_shapes=[pltpu.VMEM((1, LANES), jnp.float32)],
        compiler_params=pltpu.CompilerParams(
            dimension_semantics=("arbitrary", "arbitrary"), vmem_limit_bytes=VMEM_LIMIT),
        name="in_proj",
    )(x, *consts)


def _fox_body(cfirst_ref, clast_ref, qnorm_ref, knorm_ref, q_ref, k_ref, v_ref, o_ref,
              s_ref, p_ref, m_ref, l_ref, acc_ref, bad_ref, *, tq):
    n_tiles = q_ref.shape[1] // tq
    lane2 = lax.broadcasted_iota(jnp.int32, (1, 2 * LANES), 1) % LANES
    head_a = lax.broadcasted_iota(jnp.int32, (1, LANES), 1) < HEAD_DIM
    base = [(pl.program_id(0) * H_FOX + 2 * pl.program_id(1) + head) * n_tiles for head in range(2)]

    def key_rows(j):
        return pl.ds(pl.multiple_of(j * tq, tq), tq)

    def scores(slot, q_head, j):
        s_ref[slot] = _dot_nt(q_head, k_ref[0, key_rows(j), :])

    def softmax(slot, head, mode):
        for r in range(0, tq, FOX_ROWS):
            rows = slice(r, r + FOX_ROWS)
            s = s_ref[slot, rows, :]
            if mode in ("diagonal", "diagonal_self"):
                row = lax.broadcasted_iota(jnp.int32, (FOX_ROWS, tq), 0) + r
                col = lax.broadcasted_iota(jnp.int32, (FOX_ROWS, tq), 1)
                if mode == "diagonal":
                    s = jnp.where(col <= row, s, NEG)
                    m = jnp.max(s, axis=-1, keepdims=True)
                else:
                    own = slice(r // LANES * LANES, (r // LANES + 1) * LANES)
                    lane = lax.broadcasted_iota(jnp.int32, (FOX_ROWS, LANES), 1)
                    sub = lax.broadcasted_iota(jnp.int32, (FOX_ROWS, LANES), 0)
                    m = jnp.sum(jnp.where(lane == sub + r % LANES, s[:, own], 0.0), axis=-1, keepdims=True)
                    s = jnp.where(col <= row, s, NEG)
                m = jnp.broadcast_to(m, (FOX_ROWS, LANES))
                m_ref[head, rows, :] = m
            elif mode == "exact":
                m_old = m_ref[head, rows, :]
                m = jnp.maximum(m_old, jnp.max(s, axis=-1, keepdims=True))
                alpha = jnp.exp(m_old - m)
                m_ref[head, rows, :] = m
            else:
                m = m_ref[head, rows, :]
            lsum = None
            for g in range(tq // LANES):
                cols = slice(g * LANES, (g + 1) * LANES)
                pg = jnp.exp(s[:, cols] - m)
                p_ref[slot, rows, cols] = pg.astype(p_ref.dtype)
                lsum = pg if lsum is None else lsum + pg
            if mode in ("diagonal", "diagonal_self"):
                l_ref[head, rows, :] = lsum
            elif mode == "exact":
                l_ref[head, rows, :] = l_ref[head, rows, :] * alpha + lsum
                acc_ref[head, rows, :] = acc_ref[head, rows, :] * alpha
            else:
                l_ref[head, rows, :] = l_ref[head, rows, :] + lsum

    def values(slot, head, j, mode):
        pv = _dot(p_ref[slot], v_ref[0, key_rows(j), :])
        acc_ref[head] = pv if mode in ("diagonal", "diagonal_self") else acc_ref[head] + pv

    def unit_run(q_of, units):
        n_slots = s_ref.shape[0]
        for n in range(min(2, len(units))):
            scores(n, q_of(units[n][1]), units[n][0])
        for n, (tile, head, mode) in enumerate(units):
            softmax(n % n_slots, head, mode)
            if n + 2 < len(units):
                scores((n + 2) % n_slots, q_of(units[n + 2][1]), units[n + 2][0])
            values(n % n_slots, head, tile, mode)

    def run_tiles(q_of, make_units, first, total):
        def full(t, carry):
            unit_run(q_of, make_units(first - FOX_TRIP * t, FOX_TRIP))
            return carry

        n_full = total // FOX_TRIP
        lax.fori_loop(0, n_full, full, 0)
        done = n_full * FOX_TRIP
        count = FOX_TRIP // 2
        while count >= 1:
            fits = (total - done) >= count

            @pl.when(fits)
            def _(start=first - done, count=count):
                unit_run(q_of, make_units(start, count))

            done = jnp.where(fits, done + count, done)
            count //= 2

    def query_tile(qi, kmax, exact):
        q_rows = pl.ds(pl.multiple_of(qi * tq, tq), tq)
        qcat = q_ref[0, q_rows, :]
        zero = jnp.zeros_like(qcat)
        qh = (jnp.where(lane2 < HEAD_DIM, qcat, zero), jnp.where(lane2 >= HEAD_DIM, qcat, zero))

        kmax = [jnp.maximum(kmax[head], knorm_ref[base[head] + qi]) for head in range(2)]
        reach = []
        for head in range(2):
            bound = (cfirst_ref[base[head] + qi] + 2.0 * NORM_SLACK * qnorm_ref[base[head] + qi] * kmax[head]
                     + SKIP_MARGIN)

            def reaches(n, head=head, bound=bound):
                t = jnp.maximum(qi - 1 - n, 0)
                return (n < qi) & (bound - clast_ref[base[head] + t] > 0.0)

            reach.append(lax.while_loop(reaches, lambda n: n + 1, 0))
        n_both, n_long = jnp.minimum(reach[0], reach[1]), jnp.maximum(reach[0], reach[1])

        def both_heads(j, count, first_mode="lazy"):
            return [(j - i, head, first_mode if i == 0 else "lazy") for i in range(count) for head in range(2)]

        if exact:
            unit_run(qh.__getitem__, both_heads(qi, 1, "diagonal"))

            def step(t, carry):
                unit_run(qh.__getitem__, both_heads(qi - 1 - t, 1, "exact"))
                return carry

            lax.fori_loop(0, n_long, step, 0)
        else:
            total = n_both + 1
            count, first_len = FOX_TRIP, 1
            while count >= 1:
                fits = total >= count
                if count < FOX_TRIP:
                    fits = fits & (total < 2 * count)

                @pl.when(fits)
                def _(count=count):
                    unit_run(qh.__getitem__, both_heads(qi, count, "diagonal_self"))

                first_len = jnp.where(fits, count, first_len)
                count //= 2
            run_tiles(qh.__getitem__, both_heads, qi - first_len, total - first_len)

            long_head = jnp.where(reach[1] > reach[0], 1, 0)
            q_long = jnp.where((lane2 < HEAD_DIM) == (long_head == 0), qcat, zero)
            run_tiles(lambda head: q_long, lambda j, count: [(j - i, long_head, "lazy") for i in range(count)],
                      qi - total, n_long - n_both)

        outs = []
        for head in range(2):
            acc, lp = acc_ref[head], l_ref[head]
            outs.append(acc / jnp.sum(lp, axis=-1, keepdims=True))
            if not exact:
                finite = (jnp.abs(acc) <= F32_BIG) & (lp <= F32_BIG)
                bad_ref[...] = jnp.maximum(bad_ref[...], jnp.where(finite, 0.0, 1.0))
        o_ref[0, q_rows, :] = jnp.where(head_a, outs[0], outs[1]).astype(o_ref.dtype)
        return kmax

    def sweep(exact):
        def body(qi, kmax):
            return tuple(query_tile(qi, kmax, exact))

        lax.fori_loop(0, n_tiles, body, (jnp.float32(0.0), jnp.float32(0.0)))

    bad_ref[...] = jnp.zeros_like(bad_ref)
    sweep(exact=False)

    @pl.when(jnp.max(bad_ref[...]) > 0.0)
    def _():
        sweep(exact=True)


def _fox(stats, qf, kf, vf, tq):
    b, s, _ = qf.shape
    smem = pl.BlockSpec(memory_space=pltpu.SMEM)
    return pl.pallas_call(
        functools.partial(_fox_body, tq=tq),
        grid=(b, N_PAIR),
        in_specs=[smem, smem, smem, smem,
                  pl.BlockSpec((1, s, 2 * LANES), lambda i, p: (i, 0, p)),
                  pl.BlockSpec((1, s, 2 * LANES), lambda i, p: (i, 0, p)),
                  pl.BlockSpec((1, s, LANES), lambda i, p: (i, 0, p))],
        out_specs=pl.BlockSpec((1, s, LANES), lambda i, p: (i, 0, p)),
        out_shape=jax.ShapeDtypeStruct((b, s, W_ATT), jnp.bfloat16),
        scratch_shapes=[pltpu.VMEM((4, tq, tq), jnp.float32),
                        pltpu.VMEM((4, tq, tq), jnp.bfloat16),
                        pltpu.VMEM((2, tq, LANES), jnp.float32),
                        pltpu.VMEM((2, tq, LANES), jnp.float32),
                        pltpu.VMEM((2, tq, LANES), jnp.float32),
                        pltpu.VMEM((tq, LANES), jnp.float32)],
        compiler_params=pltpu.CompilerParams(
            dimension_semantics=("arbitrary", "arbitrary"), vmem_limit_bytes=VMEM_LIMIT),
        name="fox_attention",
    )(*stats, qf, kf, vf)


def _band_bias_body(g_ref, o_ref):
    width = g_ref.shape[-1]
    base = jnp.broadcast_to(g_ref[0], (Q_BLOCK, width))
    toeplitz = pltpu.roll(base, 0, 1, stride=1, stride_axis=0)[:, :BAND]
    qrow = lax.broadcasted_iota(jnp.int32, (Q_BLOCK, BAND), 0)
    kcol = lax.broadcasted_iota(jnp.int32, (Q_BLOCK, BAND), 1)
    cq = qrow // CHUNK
    ck = kcol // CHUNK - LEFT_CHUNKS
    valid = (ck <= cq) & (ck >= cq - LEFT_CHUNKS)
    o_ref[0] = jnp.where(valid, toeplitz, NEG)


def _band_bias(g_ext):
    h, _, width = g_ext.shape
    return pl.pallas_call(
        _band_bias_body,
        grid=(h,),
        in_specs=[pl.BlockSpec((1, 1, width), lambda i: (i, 0, 0))],
        out_specs=pl.BlockSpec((1, Q_BLOCK, BAND), lambda i: (i, 0, 0)),
        out_shape=jax.ShapeDtypeStruct((h, Q_BLOCK, BAND), jnp.float32),
        name="band_bias",
    )(g_ext)


def _chunk_body(q_ref, k_ref, v_ref, bias_ref, o_ref, kpad_ref, vpad_ref, s_ref, p_ref, l_ref, *, tq):
    qi = pl.program_id(2)
    s_len = k_ref.shape[1]

    @pl.when(qi == 0)
    def _():
        zeros = jnp.zeros((PAD, LANES), kpad_ref.dtype)
        kpad_ref[:PAD, :] = zeros
        vpad_ref[:PAD, :] = zeros
        kpad_ref[PAD:PAD + s_len, :] = k_ref[0]
        vpad_ref[PAD:PAD + s_len, :] = v_ref[0]

    head_a = lax.broadcasted_iota(jnp.int32, (1, LANES), 1) < HEAD_DIM
    kcol = lax.broadcasted_iota(jnp.int32, (1, BAND), 1)

    def offsets(blk):
        r0 = pl.multiple_of(blk * Q_BLOCK, Q_BLOCK)
        return r0, pl.multiple_of(qi * tq + r0, Q_BLOCK)

    def scores(slot, blk):
        r0, p0 = offsets(blk)
        q2 = q_ref[0, pl.ds(r0, Q_BLOCK), :]
        zero = jnp.zeros_like(q2)
        qs = jnp.concatenate([jnp.where(head_a, q2, zero), jnp.where(head_a, zero, q2)], axis=0)
        s_ref[slot] = _dot_nt(qs, kpad_ref[pl.ds(p0, BAND), :])

    def softmax(slot, blk, masked):
        _, p0 = offsets(blk)
        for r in range(0, 2 * Q_BLOCK, CHUNK_ROWS):
            rows = slice(r, r + CHUNK_ROWS)
            s = s_ref[slot, rows, :] + bias_ref[0, rows, :]
            if masked:
                s = jnp.where(kcol + p0 >= PAD, s, NEG)
            p = jnp.exp(s - jnp.max(s, axis=-1, keepdims=True))
            l_ref[slot, rows, :] = jnp.broadcast_to(jnp.sum(p, axis=-1, keepdims=True), (CHUNK_ROWS, LANES))
            p_ref[slot, rows, :] = p.astype(p_ref.dtype)

    def values(slot, blk):
        r0, p0 = offsets(blk)
        o = _dot(p_ref[slot], vpad_ref[pl.ds(p0, BAND), :]) / l_ref[slot]
        o_ref[0, pl.ds(r0, Q_BLOCK), :] = jnp.where(head_a, o[:Q_BLOCK], o[Q_BLOCK:]).astype(o_ref.dtype)

    def group(i, masked):
        first = i * CHUNK_GROUP
        scores(0, first)
        for n in range(CHUNK_GROUP):
            if n + 1 < CHUNK_GROUP:
                scores((n + 1) % 2, first + n + 1)
            softmax(n % 2, first + n, masked)
            values(n % 2, first + n)

    def masked_group(i, carry):
        group(i, True)
        return carry

    def plain_group(i, carry):
        group(i, False)
        return carry

    group_rows = Q_BLOCK * CHUNK_GROUP
    n_groups = tq // group_rows
    n_masked = jnp.clip((PAD - qi * tq + group_rows - 1) // group_rows, 0, n_groups)
    lax.fori_loop(0, n_masked, masked_group, 0)
    lax.fori_loop(n_masked, n_groups, plain_group, 0)


def _chunk_attention(qc, kc, vc, bias, tq):
    b, s, _ = qc.shape
    assert tq % (CHUNK_GROUP * Q_BLOCK) == 0
    return pl.pallas_call(
        functools.partial(_chunk_body, tq=tq),
        grid=(b, N_PAIR, s // tq),
        in_specs=[pl.BlockSpec((1, tq, LANES), lambda i, p, j: (i, j, p)),
                  pl.BlockSpec((1, s, LANES), lambda i, p, j: (i, 0, p)),
                  pl.BlockSpec((1, s, LANES), lambda i, p, j: (i, 0, p)),
                  pl.BlockSpec((1, 2 * Q_BLOCK, BAND), lambda i, p, j: (p, 0, 0))],
        out_specs=pl.BlockSpec((1, tq, LANES), lambda i, p, j: (i, j, p)),
        out_shape=jax.ShapeDtypeStruct((b, s, W_ATT), jnp.bfloat16),
        scratch_shapes=[pltpu.VMEM((PAD + s, LANES), jnp.bfloat16),
                        pltpu.VMEM((PAD + s, LANES), jnp.bfloat16),
                        pltpu.VMEM((2, 2 * Q_BLOCK, BAND), jnp.float32),
                        pltpu.VMEM((2, 2 * Q_BLOCK, BAND), jnp.bfloat16),
                        pltpu.VMEM((2, 2 * Q_BLOCK, LANES), jnp.float32)],
        compiler_params=pltpu.CompilerParams(
            dimension_semantics=("arbitrary", "arbitrary", "arbitrary"), vmem_limit_bytes=VMEM_LIMIT),
        name="chunk_attention",
    )(qc, kc, vc, bias)


def _out_ffn_body(x_ref, oa_ref, ob_ref, gate_ref, wa_ref, wb_ref, wo_ref, g2_ref, wu_ref, wd_ref, gf_ref,
                  o_ref, *, ff_chunk, final):
    d = x_ref.shape[-1]
    ya = _dot(oa_ref[0], wa_ref[...])
    yb = _dot(ob_ref[0], wb_ref[...])
    ga = gate_ref[0, :, :d].astype(jnp.float32)
    gb = gate_ref[0, :, d:].astype(jnp.float32)
    merged = (ga * ya + gb * yb).astype(jnp.bfloat16)
    x1 = x_ref[0] + _dot(merged, wo_ref[...])
    h2 = _rms(x1, g2_ref[...]).astype(jnp.bfloat16)
    acc = x1
    for c0 in range(0, wu_ref.shape[1], ff_chunk):
        u = jnp.maximum(_dot(h2, wu_ref[:, c0:c0 + ff_chunk]), 0.0)
        acc = acc + _dot((u * u).astype(jnp.bfloat16), wd_ref[c0:c0 + ff_chunk, :])
    if final:
        acc = _rms(acc, gf_ref[...])
    o_ref[0] = acc


def _out_ffn(x, oa, ob, gates, wa, wb, wo, g2, wu, wd, gf, tm, final):
    b, s, d = x.shape
    tok = lambda w: pl.BlockSpec((1, tm, w), lambda i, j: (i, j, 0))
    return pl.pallas_call(
        functools.partial(_out_ffn_body, ff_chunk=min(1024, wu.shape[1]), final=final),
        grid=(b, s // tm),
        in_specs=[tok(d), tok(W_ATT), tok(W_ATT), tok(2 * d)]
                 + [_const_spec(a.shape) for a in (wa, wb, wo, g2, wu, wd, gf)],
        out_specs=tok(d),
        out_shape=jax.ShapeDtypeStruct((b, s, d), jnp.float32),
        compiler_params=pltpu.CompilerParams(
            dimension_semantics=("arbitrary", "arbitrary"), vmem_limit_bytes=VMEM_LIMIT),
        name="out_ffn",
    )(x, oa, ob, gates, wa, wb, wo, g2, wu, wd, gf)


def _aug_placement():
    eq = [[0.0] * W_ATT for _ in range(LANES)]
    ek = [[0.0] * W_ATT for _ in range(LANES)]
    one = 3 * H_FOX
    for h in range(H_FOX):
        for piece in range(3):
            eq[piece * H_FOX + h][HEAD_DIM * h + piece] = 1.0
            eq[one][HEAD_DIM * h + 3 + piece] = 1.0
            ek[one][HEAD_DIM * h + piece] = 1.0
            ek[piece * H_FOX + h][HEAD_DIM * h + 3 + piece] = -1.0
    return jnp.array(eq, jnp.bfloat16), jnp.array(ek, jnp.bfloat16)


def kernel(x, norm1, w_in, forget_bias, rel_bias, w_branch_a, w_branch_b, w_out, norm2, w_up, w_down, final_norm):
    b, s, d = x.shape
    depth = w_in.shape[0]
    bf16 = jnp.bfloat16
    tm = min(512, s)
    tqc = min(1024, s)
    eq, ek = _aug_placement()
    head_of_col = lax.broadcasted_iota(jnp.int32, (W_ATT, LANES), 0) // HEAD_DIM
    hsum = (head_of_col == lax.broadcasted_iota(jnp.int32, (W_ATT, LANES), 1)).astype(bf16)
    gf = final_norm.reshape(1, d)
    o = 3 * W_ATT + H_FOX
    for l in range(depth):
        w = w_in[l]
        wq = (w[:, :W_ATT] * SCALE).astype(bf16)
        wk = w[:, W_ATT:2 * W_ATT].astype(bf16)
        wv = w[:, 2 * W_ATT:3 * W_ATT].astype(bf16)
        wf = jnp.pad(w[:, 3 * W_ATT:o], ((0, 0), (0, LANES - H_FOX))).astype(bf16)
        wc = jnp.concatenate([w[:, o:o + W_ATT] * SCALE, w[:, o + W_ATT:o + 3 * W_ATT]], axis=1).astype(bf16)
        wg = w[:, o + 3 * W_ATT:].astype(bf16)
        bf = jnp.pad(forget_bias[l], (0, LANES - H_FOX)).reshape(1, LANES)
        qf, kf, vf, qc, kc, vc, gates, stat = _in_proj(
            x, norm1[l].reshape(1, d), wq, wk, wv, wc, wg, wf, bf, eq, ek, hsum, tm)

        stats = stat[:, :, :4, :H_FOX].transpose(2, 0, 3, 1).reshape(4, -1)
        o_a = _fox(tuple(stats), qf, kf, vf, tm)

        far = rel_bias[l][:, 2 * MAX_REL:]
        g_ext = jnp.concatenate([jnp.broadcast_to(far, (H_CHK, PAD - MAX_REL + 1)),
                                 rel_bias[l][:, 2 * MAX_REL - 1:0:-1],
                                 jnp.broadcast_to(far, (H_CHK, Q_BLOCK))], axis=1)
        bias = _band_bias(g_ext.reshape(H_CHK, 1, -1)).reshape(N_PAIR, 2 * Q_BLOCK, BAND)
        o_b = _chunk_attention(qc, kc, vc, bias, tqc)

        x = _out_ffn(x, o_a, o_b, gates, w_branch_a[l].astype(bf16), w_branch_b[l].astype(bf16),
                     w_out[l].astype(bf16), norm2[l].reshape(1, d), w_up[l].astype(bf16),
                     w_down[l].astype(bf16), gf, tm, final=(l == depth - 1))
    return x
```

```python
import functools
import math

import jax
import jax.numpy as jnp
from jax import lax
from jax.experimental import pallas as pl
from jax.experimental.pallas import tpu as pltpu

HEAD_DIM = 64
H_FOX = 8
H_CHK = 8
N_PAIR = 4
W_ATT = H_FOX * HEAD_DIM
CHUNK = 64
Q_BLOCK = 128
LEFT_CHUNKS = 8
PAD = LEFT_CHUNKS * CHUNK
BAND = PAD + Q_BLOCK
MAX_REL = 128
EPS = 1e-6
NEG = -1e30
LANES = 128
SCALE = 1.0 / math.sqrt(HEAD_DIM)
LOG2E = math.log2(math.e)
F32_BIG = 3.0e38
PROJ_SPLIT = 2
FOX_ROWS = 32
FOX_TRIP = 4
CHUNK_ROWS = 32
CHUNK_GROUP = 8
SKIP_MARGIN = 30.0
NORM_SLACK = 1.02
VMEM_LIMIT = 60 * 1024 * 1024

_NT = (((1,), (1,)), ((), ()))


def _dot(a, b):
    return jnp.dot(a, b, preferred_element_type=jnp.float32)


def _dot_nt(a, b):
    return lax.dot_general(a, b, _NT, preferred_element_type=jnp.float32)


def _rms(x, g):
    ms = jnp.mean(x * x, axis=-1, keepdims=True)
    return x * lax.rsqrt(ms + EPS) * g


def _const_spec(shape):
    nd = len(shape)
    return pl.BlockSpec(shape, lambda *_: (0,) * nd, pipeline_mode=pl.Buffered(1))


def _in_proj_body(x_ref, g_ref, wq_ref, wk_ref, wv_ref, wc_ref, wg_ref, wf_ref, bf_ref, eq_ref, ek_ref, hsum_ref,
                  qf_ref, kf_ref, vf_ref, qc_ref, kc_ref, vc_ref, gate_ref, stat_ref, carry_ref):
    tm = x_ref.shape[1]
    th = tm // PROJ_SPLIT

    @pl.when(pl.program_id(1) == 0)
    def _():
        carry_ref[...] = jnp.zeros_like(carry_ref)

    lane = lax.broadcasted_iota(jnp.int32, (th, LANES), 1)
    row = lax.broadcasted_iota(jnp.int32, (th, LANES), 0)

    def max_norm(t):
        t = t.astype(jnp.float32)
        sq = _dot((t * t).astype(jnp.bfloat16), hsum_ref[...])
        return jnp.max(sq, axis=0, keepdims=True)

    c_first, c_last, q_sq, k_sq = None, None, None, None
    for part in range(PROJ_SPLIT):
        rows = slice(part * th, (part + 1) * th)
        h = _rms(x_ref[0, rows, :], g_ref[...]).astype(jnp.bfloat16)

        logf = jax.nn.log_sigmoid(_dot(h, wf_ref[...]) + bf_ref[...])
        c = jnp.where(lane < H_FOX, logf, 0.0)
        k = 1
        while k < th:
            c = c + jnp.where(row >= k, pltpu.roll(c, k, 0), 0.0)
            k *= 2
        c = c + carry_ref[...]
        carry_ref[...] = c[th - 1:th, :]
        c_first = c[0:1, :] if part == 0 else c_first
        c_last = c[th - 1:th, :]

        hi = c.astype(jnp.bfloat16).astype(jnp.float32)
        r1 = c - hi
        mid = r1.astype(jnp.bfloat16).astype(jnp.float32)
        lo = (r1 - mid).astype(jnp.bfloat16).astype(jnp.float32)
        pieces = hi + pltpu.roll(mid, H_FOX, 1) + pltpu.roll(lo, 2 * H_FOX, 1)
        pieces = jnp.where(lane == 3 * H_FOX, 1.0, pieces).astype(jnp.bfloat16)
        aug_q = _dot(pieces, eq_ref[...]).astype(jnp.bfloat16)
        aug_k = _dot(pieces, ek_ref[...]).astype(jnp.bfloat16)

        q = _dot(h, wq_ref[...]).astype(jnp.bfloat16)
        kk = _dot(h, wk_ref[...]).astype(jnp.bfloat16)
        for p in range(N_PAIR):
            src = slice(p * LANES, (p + 1) * LANES)
            qf_ref[0, rows, 2 * p * LANES:(2 * p + 1) * LANES] = q[:, src]
            qf_ref[0, rows, (2 * p + 1) * LANES:(2 * p + 2) * LANES] = aug_q[:, src]
            kf_ref[0, rows, 2 * p * LANES:(2 * p + 1) * LANES] = kk[:, src]
            kf_ref[0, rows, (2 * p + 1) * LANES:(2 * p + 2) * LANES] = aug_k[:, src]
        vf_ref[0, rows, :] = _dot(h, wv_ref[...]).astype(jnp.bfloat16)
        q_sq = max_norm(q) if part == 0 else jnp.maximum(q_sq, max_norm(q))
        k_sq = max_norm(kk) if part == 0 else jnp.maximum(k_sq, max_norm(kk))

        pc = _dot(h, wc_ref[...])
        qc_ref[0, rows, :] = pc[:, :W_ATT].astype(jnp.bfloat16)
        kc_ref[0, rows, :] = pc[:, W_ATT:2 * W_ATT].astype(jnp.bfloat16)
        vc_ref[0, rows, :] = pc[:, 2 * W_ATT:].astype(jnp.bfloat16)
        gate_ref[0, rows, :] = jax.nn.sigmoid(_dot(h, wg_ref[...])).astype(jnp.bfloat16)

    sub = lax.broadcasted_iota(jnp.int32, (8, LANES), 0)
    stat = jnp.where(sub == 0, c_first, 0.0)
    stat = jnp.where(sub == 1, c_last, stat)
    stat = jnp.where(sub == 2, jnp.sqrt(q_sq), stat)
    stat_ref[0, 0] = jnp.where(sub == 3, jnp.sqrt(k_sq), stat)


def _in_proj(x, g, wq, wk, wv, wc, wg, wf, bf, eq, ek, hsum, tm):
    b, s, d = x.shape
    tok = lambda width: pl.BlockSpec((1, tm, width), lambda i, j: (i, j, 0))
    bf16 = jnp.bfloat16
    out_shape = [jax.ShapeDtypeStruct((b, s, width), bf16)
                 for width in (2 * W_ATT, 2 * W_ATT, W_ATT, W_ATT, W_ATT, W_ATT, 2 * d)]
    out_shape.append(jax.ShapeDtypeStruct((b, s // tm, 8, LANES), jnp.float32))
    consts = (g, wq, wk, wv, wc, wg, wf, bf, eq, ek, hsum)
    return pl.pallas_call(
        _in_proj_body,
        grid=(b, s // tm),
        in_specs=[tok(d)] + [_const_spec(a.shape) for a in consts],
        out_specs=[tok(sh.shape[-1]) for sh in out_shape[:-1]]
                  + [pl.BlockSpec((1, 1, 8, LANES), lambda i, j: (i, j, 0, 0))],
        out_shape=out_shape,
        scratch_shapes=[pltpu.VMEM((1, LANES), jnp.float32)],
        compiler_params=pltpu.CompilerParams(
            dimension_semantics=("arbitrary", "arbitrary"), vmem_limit_bytes=VMEM_LIMIT),
        name="in_proj",
    )(x, *consts)


def _fox_body(cfirst_ref, clast_ref, qnorm_ref, knorm_ref, q_ref, k_ref, v_ref, o_ref,
              s_ref, p_ref, m_ref, l_ref, acc_ref, bad_ref, *, tq):
    n_tiles = q_ref.shape[1] // tq
    lane2 = lax.broadcasted_iota(jnp.int32, (1, 2 * LANES), 1) % LANES
    head_a = lax.broadcasted_iota(jnp.int32, (1, LANES), 1) < HEAD_DIM
    base = [(pl.program_id(0) * H_FOX + 2 * pl.program_id(1) + head) * n_tiles for head in range(2)]

    def key_rows(j):
        return pl.ds(pl.multiple_of(j * tq, tq), tq)

    def scores(slot, q_head, j):
        s_ref[slot] = _dot_nt(q_head, k_ref[0, key_rows(j), :])

    def softmax(slot, head, mode):
        for r in range(0, tq, FOX_ROWS):
            rows = slice(r, r + FOX_ROWS)
            s = s_ref[slot, rows, :]
            if mode in ("diagonal", "diagonal_self"):
                row = lax.broadcasted_iota(jnp.int32, (FOX_ROWS, tq), 0) + r
                col = lax.broadcasted_iota(jnp.int32, (FOX_ROWS, tq), 1)
                if mode == "diagonal":
                    s = jnp.where(col <= row, s, NEG)
                    m = jnp.max(s, axis=-1, keepdims=True)
                else:
                    own = slice(r // LANES * LANES, (r // LANES + 1) * LANES)
                    lane = lax.broadcasted_iota(jnp.int32, (FOX_ROWS, LANES), 1)
                    sub = lax.broadcasted_iota(jnp.int32, (FOX_ROWS, LANES), 0)
                    m = jnp.sum(jnp.where(lane == sub + r % LANES, s[:, own], 0.0), axis=-1, keepdims=True)
                    s = jnp.where(col <= row, s, NEG)
                m = jnp.broadcast_to(m, (FOX_ROWS, LANES))
                m_ref[head, rows, :] = m
            elif mode == "exact":
                m_old = m_ref[head, rows, :]
                m = jnp.maximum(m_old, jnp.max(s, axis=-1, keepdims=True))
                alpha = jnp.exp(m_old - m)
                m_ref[head, rows, :] = m
            else:
                m = m_ref[head, rows, :]
            lsum = None
            for g in range(tq // LANES):
                cols = slice(g * LANES, (g + 1) * LANES)
                pg = jnp.exp(s[:, cols] - m)
                p_ref[slot, rows, cols] = pg.astype(p_ref.dtype)
                lsum = pg if lsum is None else lsum + pg
            if mode in ("diagonal", "diagonal_self"):
                l_ref[head, rows, :] = lsum
            elif mode == "exact":
                l_ref[head, rows, :] = l_ref[head, rows, :] * alpha + lsum
                acc_ref[head, rows, :] = acc_ref[head, rows, :] * alpha
            else:
                l_ref[head, rows, :] = l_ref[head, rows, :] + lsum

    def values(slot, head, j, mode):
        pv = _dot(p_ref[slot], v_ref[0, key_rows(j), :])
        acc_ref[head] = pv if mode in ("diagonal", "diagonal_self") else acc_ref[head] + pv

    def unit_run(q_of, units):
        n_slots = s_ref.shape[0]
        for n in range(min(2, len(units))):
            scores(n, q_of(units[n][1]), units[n][0])
        for n, (tile, head, mode) in enumerate(units):
            softmax(n % n_slots, head, mode)
            if n + 2 < len(units):
                scores((n + 2) % n_slots, q_of(units[n + 2][1]), units[n + 2][0])
            values(n % n_slots, head, tile, mode)

    def run_tiles(q_of, make_units, first, total):
        def full(t, carry):
            unit_run(q_of, make_units(first - FOX_TRIP * t, FOX_TRIP))
            return carry

        n_full = total // FOX_TRIP
        lax.fori_loop(0, n_full, full, 0)
        done = n_full * FOX_TRIP
        count = FOX_TRIP // 2
        while count >= 1:
            fits = (total - done) >= count

            @pl.when(fits)
            def _(start=first - done, count=count):
                unit_run(q_of, make_units(start, count))

            done = jnp.where(fits, done + count, done)
            count //= 2

    def query_tile(qi, kmax, exact):
        q_rows = pl.ds(pl.multiple_of(qi * tq, tq), tq)
        qcat = q_ref[0, q_rows, :]
        zero = jnp.zeros_like(qcat)
        qh = (jnp.where(lane2 < HEAD_DIM, qcat, zero), jnp.where(lane2 >= HEAD_DIM, qcat, zero))

        kmax = [jnp.maximum(kmax[head], knorm_ref[base[head] + qi]) for head in range(2)]
        reach = []
        for head in range(2):
            bound = (cfirst_ref[base[head] + qi] + 2.0 * NORM_SLACK * qnorm_ref[base[head] + qi] * kmax[head]
                     + SKIP_MARGIN)

            def reaches(n, head=head, bound=bound):
                t = jnp.maximum(qi - 1 - n, 0)
                return (n < qi) & (bound - clast_ref[base[head] + t] > 0.0)

            reach.append(lax.while_loop(reaches, lambda n: n + 1, 0))
        n_both, n_long = jnp.minimum(reach[0], reach[1]), jnp.maximum(reach[0], reach[1])

        def both_heads(j, count, first_mode="lazy"):
            return [(j - i, head, first_mode if i == 0 else "lazy") for i in range(count) for head in range(2)]

        if exact:
            unit_run(qh.__getitem__, both_heads(qi, 1, "diagonal"))

            def step(t, carry):
                unit_run(qh.__getitem__, both_heads(qi - 1 - t, 1, "exact"))
                return carry

            lax.fori_loop(0, n_long, step, 0)
        else:
            total = n_both + 1
            count, first_len = FOX_TRIP, 1
            while count >= 1:
                fits = total >= count
                if count < FOX_TRIP:
                    fits = fits & (total < 2 * count)

                @pl.when(fits)
                def _(count=count):
                    unit_run(qh.__getitem__, both_heads(qi, count, "diagonal_self"))

                first_len = jnp.where(fits, count, first_len)
                count //= 2
            run_tiles(qh.__getitem__, both_heads, qi - first_len, total - first_len)

            long_head = jnp.where(reach[1] > reach[0], 1, 0)
            q_long = jnp.where((lane2 < HEAD_DIM) == (long_head == 0), qcat, zero)
            run_tiles(lambda head: q_long, lambda j, count: [(j - i, long_head, "lazy") for i in range(count)],
                      qi - total, n_long - n_both)

        outs = []
        for head in range(2):
            acc, lp = acc_ref[head], l_ref[head]
            outs.append(acc / jnp.sum(lp, axis=-1, keepdims=True))
            if not exact:
                finite = (jnp.abs(acc) <= F32_BIG) & (lp <= F32_BIG)
                bad_ref[...] = jnp.maximum(bad_ref[...], jnp.where(finite, 0.0, 1.0))
        o_ref[0, q_rows, :] = jnp.where(head_a, outs[0], outs[1]).astype(o_ref.dtype)
        return kmax

    def sweep(exact):
        def body(qi, kmax):
            return tuple(query_tile(qi, kmax, exact))

        lax.fori_loop(0, n_tiles, body, (jnp.float32(0.0), jnp.float32(0.0)))

    bad_ref[...] = jnp.zeros_like(bad_ref)
    sweep(exact=False)

    @pl.when(jnp.max(bad_ref[...]) > 0.0)
    def _():
        sweep(exact=True)


def _fox(stats, qf, kf, vf, tq):
    b, s, _ = qf.shape
    smem = pl.BlockSpec(memory_space=pltpu.SMEM)
    return pl.pallas_call(
        functools.partial(_fox_body, tq=tq),
        grid=(b, N_PAIR),
        in_specs=[smem, smem, smem, smem,
                  pl.BlockSpec((1, s, 2 * LANES), lambda i, p: (i, 0, p)),
                  pl.BlockSpec((1, s, 2 * LANES), lambda i, p: (i, 0, p)),
                  pl.BlockSpec((1, s, LANES), lambda i, p: (i, 0, p))],
        out_specs=pl.BlockSpec((1, s, LANES), lambda i, p: (i, 0, p)),
        out_shape=jax.ShapeDtypeStruct((b, s, W_ATT), jnp.bfloat16),
        scratch_shapes=[pltpu.VMEM((4, tq, tq), jnp.float32),
                        pltpu.VMEM((4, tq, tq), jnp.bfloat16),
                        pltpu.VMEM((2, tq, LANES), jnp.float32),
                        pltpu.VMEM((2, tq, LANES), jnp.float32),
                        pltpu.VMEM((2, tq, LANES), jnp.float32),
                        pltpu.VMEM((tq, LANES), jnp.float32)],
        compiler_params=pltpu.CompilerParams(
            dimension_semantics=("arbitrary", "arbitrary"), vmem_limit_bytes=VMEM_LIMIT),
        name="fox_attention",
    )(*stats, qf, kf, vf)


def _band_bias_body(g_ref, o_ref):
    width = g_ref.shape[-1]
    base = jnp.broadcast_to(g_ref[0], (Q_BLOCK, width))
    toeplitz = pltpu.roll(base, 0, 1, stride=1, stride_axis=0)[:, :BAND]
    qrow = lax.broadcasted_iota(jnp.int32, (Q_BLOCK, BAND), 0)
    kcol = lax.broadcasted_iota(jnp.int32, (Q_BLOCK, BAND), 1)
    cq = qrow // CHUNK
    ck = kcol // CHUNK - LEFT_CHUNKS
    valid = (ck <= cq) & (ck >= cq - LEFT_CHUNKS)
    o_ref[0] = jnp.where(valid, toeplitz * LOG2E, NEG)


def _band_bias(g_ext):
    h, _, width = g_ext.shape
    return pl.pallas_call(
        _band_bias_body,
        grid=(h,),
        in_specs=[pl.BlockSpec((1, 1, width), lambda i: (i, 0, 0))],
        out_specs=pl.BlockSpec((1, Q_BLOCK, BAND), lambda i: (i, 0, 0)),
        out_shape=jax.ShapeDtypeStruct((h, Q_BLOCK, BAND), jnp.float32),
        name="band_bias",
    )(g_ext)


def _chunk_body(q_ref, k_ref, v_ref, bias_ref, o_ref, kpad_ref, vpad_ref, s_ref, p_ref, l_ref, *, tq):
    qi = pl.program_id(2)
    s_len = k_ref.shape[1]

    @pl.when(qi == 0)
    def _():
        zeros = jnp.zeros((PAD, LANES), kpad_ref.dtype)
        kpad_ref[:PAD, :] = zeros
        vpad_ref[:PAD, :] = zeros
        kpad_ref[PAD:PAD + s_len, :] = k_ref[0]
        vpad_ref[PAD:PAD + s_len, :] = v_ref[0]

    head_a = lax.broadcasted_iota(jnp.int32, (1, LANES), 1) < HEAD_DIM
    kcol = lax.broadcasted_iota(jnp.int32, (1, BAND), 1)

    def offsets(blk):
        r0 = pl.multiple_of(blk * Q_BLOCK, Q_BLOCK)
        return r0, pl.multiple_of(qi * tq + r0, Q_BLOCK)

    def scores(slot, blk):
        r0, p0 = offsets(blk)
        q2 = q_ref[0, pl.ds(r0, Q_BLOCK), :]
        zero = jnp.zeros_like(q2)
        qs = jnp.concatenate([jnp.where(head_a, q2, zero), jnp.where(head_a, zero, q2)], axis=0)
        s_ref[slot] = _dot_nt(qs, kpad_ref[pl.ds(p0, BAND), :])

    def softmax(slot, blk, masked):
        _, p0 = offsets(blk)
        for r in range(0, 2 * Q_BLOCK, CHUNK_ROWS):
            rows = slice(r, r + CHUNK_ROWS)
            s = s_ref[slot, rows, :] + bias_ref[0, rows, :]
            if masked:
                s = jnp.where(kcol + p0 >= PAD, s, NEG)
            p = jnp.exp2(s - jnp.max(s, axis=-1, keepdims=True))
            l_ref[slot, rows, :] = jnp.broadcast_to(jnp.sum(p, axis=-1, keepdims=True), (CHUNK_ROWS, LANES))
            p_ref[slot, rows, :] = p.astype(p_ref.dtype)

    def values(slot, blk):
        r0, p0 = offsets(blk)
        o = _dot(p_ref[slot], vpad_ref[pl.ds(p0, BAND), :]) / l_ref[slot]
        o_ref[0, pl.ds(r0, Q_BLOCK), :] = jnp.where(head_a, o[:Q_BLOCK], o[Q_BLOCK:]).astype(o_ref.dtype)

    def group(i, masked):
        first = i * CHUNK_GROUP
        scores(0, first)
        for n in range(CHUNK_GROUP):
            if n + 1 < CHUNK_GROUP:
                scores((n + 1) % 2, first + n + 1)
            softmax(n % 2, first + n, masked)
            values(n % 2, first + n)

    def masked_group(i, carry):
        group(i, True)
        return carry

    def plain_group(i, carry):
        group(i, False)
        return carry

    group_rows = Q_BLOCK * CHUNK_GROUP
    n_groups = tq // group_rows
    n_masked = jnp.clip((PAD - qi * tq + group_rows - 1) // group_rows, 0, n_groups)
    lax.fori_loop(0, n_masked, masked_group, 0)
    lax.fori_loop(n_masked, n_groups, plain_group, 0)


def _chunk_attention(qc, kc, vc, bias, tq):
    b, s, _ = qc.shape
    assert tq % (CHUNK_GROUP * Q_BLOCK) == 0
    return pl.pallas_call(
        functools.partial(_chunk_body, tq=tq),
        grid=(b, N_PAIR, s // tq),
        in_specs=[pl.BlockSpec((1, tq, LANES), lambda i, p, j: (i, j, p)),
                  pl.BlockSpec((1, s, LANES), lambda i, p, j: (i, 0, p)),
                  pl.BlockSpec((1, s, LANES), lambda i, p, j: (i, 0, p)),
                  pl.BlockSpec((1, 2 * Q_BLOCK, BAND), lambda i, p, j: (p, 0, 0))],
        out_specs=pl.BlockSpec((1, tq, LANES), lambda i, p, j: (i, j, p)),
        out_shape=jax.ShapeDtypeStruct((b, s, W_ATT), jnp.bfloat16),
        scratch_shapes=[pltpu.VMEM((PAD + s, LANES), jnp.bfloat16),
                        pltpu.VMEM((PAD + s, LANES), jnp.bfloat16),
                        pltpu.VMEM((2, 2 * Q_BLOCK, BAND), jnp.float32),
                        pltpu.VMEM((2, 2 * Q_BLOCK, BAND), jnp.bfloat16),
                        pltpu.VMEM((2, 2 * Q_BLOCK, LANES), jnp.float32)],
        compiler_params=pltpu.CompilerParams(
            dimension_semantics=("arbitrary", "arbitrary", "arbitrary"), vmem_limit_bytes=VMEM_LIMIT),
        name="chunk_attention",
    )(qc, kc, vc, bias)


def _out_ffn_body(x_ref, oa_ref, ob_ref, gate_ref, wa_ref, wb_ref, wo_ref, g2_ref, wu_ref, wd_ref, gf_ref,
                  o_ref, *, ff_chunk, final):
    d = x_ref.shape[-1]
    ya = _dot(oa_ref[0], wa_ref[...])
    yb = _dot(ob_ref[0], wb_ref[...])
    ga = gate_ref[0, :, :d].astype(jnp.float32)
    gb = gate_ref[0, :, d:].astype(jnp.float32)
    merged = (ga * ya + gb * yb).astype(jnp.bfloat16)
    x1 = x_ref[0] + _dot(merged, wo_ref[...])
    h2 = _rms(x1, g2_ref[...]).astype(jnp.bfloat16)
    acc = x1
    for c0 in range(0, wu_ref.shape[1], ff_chunk):
        u = jnp.maximum(_dot(h2, wu_ref[:, c0:c0 + ff_chunk]), 0.0)
        acc = acc + _dot((u * u).astype(jnp.bfloat16), wd_ref[c0:c0 + ff_chunk, :])
    if final:
        acc = _rms(acc, gf_ref[...])
    o_ref[0] = acc


def _out_ffn(x, oa, ob, gates, wa, wb, wo, g2, wu, wd, gf, tm, final):
    b, s, d = x.shape
    tok = lambda w: pl.BlockSpec((1, tm, w), lambda i, j: (i, j, 0))
    return pl.pallas_call(
        functools.partial(_out_ffn_body, ff_chunk=min(1024, wu.shape[1]), final=final),
        grid=(b, s // tm),
        in_specs=[tok(d), tok(W_ATT), tok(W_ATT), tok(2 * d)]
                 + [_const_spec(a.shape) for a in (wa, wb, wo, g2, wu, wd, gf)],
        out_specs=tok(d),
        out_shape=jax.ShapeDtypeStruct((b, s, d), jnp.float32),
        compiler_params=pltpu.CompilerParams(
            dimension_semantics=("arbitrary", "arbitrary"), vmem_limit_bytes=VMEM_LIMIT),
        name="out_ffn",
    )(x, oa, ob, gates, wa, wb, wo, g2, wu, wd, gf)


def _aug_placement():
    eq = [[0.0] * W_ATT for _ in range(LANES)]
    ek = [[0.0] * W_ATT for _ in range(LANES)]
    one = 3 * H_FOX
    for h in range(H_FOX):
        for piece in range(3):
            eq[piece * H_FOX + h][HEAD_DIM * h + piece] = 1.0
            eq[one][HEAD_DIM * h + 3 + piece] = 1.0
            ek[one][HEAD_DIM * h + piece] = 1.0
            ek[piece * H_FOX + h][HEAD_DIM * h + 3 + piece] = -1.0
    return jnp.array(eq, jnp.bfloat16), jnp.array(ek, jnp.bfloat16)


def kernel(x, norm1, w_in, forget_bias, rel_bias, w_branch_a, w_branch_b, w_out, norm2, w_up, w_down, final_norm):
    b, s, d = x.shape
    depth = w_in.shape[0]
    bf16 = jnp.bfloat16
    tm = min(512, s)
    tqc = min(1024, s)
    eq, ek = _aug_placement()
    head_of_col = lax.broadcasted_iota(jnp.int32, (W_ATT, LANES), 0) // HEAD_DIM
    hsum = (head_of_col == lax.broadcasted_iota(jnp.int32, (W_ATT, LANES), 1)).astype(bf16)
    gf = final_norm.reshape(1, d)
    o = 3 * W_ATT + H_FOX
    for l in range(depth):
        w = w_in[l]
        wq = (w[:, :W_ATT] * SCALE).astype(bf16)
        wk = w[:, W_ATT:2 * W_ATT].astype(bf16)
        wv = w[:, 2 * W_ATT:3 * W_ATT].astype(bf16)
        wf = jnp.pad(w[:, 3 * W_ATT:o], ((0, 0), (0, LANES - H_FOX))).astype(bf16)
        wc = jnp.concatenate([w[:, o:o + W_ATT] * (SCALE * LOG2E), w[:, o + W_ATT:o + 3 * W_ATT]],
                             axis=1).astype(bf16)
        wg = w[:, o + 3 * W_ATT:].astype(bf16)
        bf = jnp.pad(forget_bias[l], (0, LANES - H_FOX)).reshape(1, LANES)
        qf, kf, vf, qc, kc, vc, gates, stat = _in_proj(
            x, norm1[l].reshape(1, d), wq, wk, wv, wc, wg, wf, bf, eq, ek, hsum, tm)

        stats = stat[:, :, :4, :H_FOX].transpose(2, 0, 3, 1).reshape(4, -1)
        o_a = _fox(tuple(stats), qf, kf, vf, tm)

        far = rel_bias[l][:, 2 * MAX_REL:]
        g_ext = jnp.concatenate([jnp.broadcast_to(far, (H_CHK, PAD - MAX_REL + 1)),
                                 rel_bias[l][:, 2 * MAX_REL - 1:0:-1],
                                 jnp.broadcast_to(far, (H_CHK, Q_BLOCK))], axis=1)
        bias = _band_bias(g_ext.reshape(H_CHK, 1, -1)).reshape(N_PAIR, 2 * Q_BLOCK, BAND)
        o_b = _chunk_attention(qc, kc, vc, bias, tqc)

        x = _out_ffn(x, o_a, o_b, gates, w_branch_a[l].astype(bf16), w_branch_b[l].astype(bf16),
                     w_out[l].astype(bf16), norm2[l].reshape(1, d), w_up[l].astype(bf16),
                     w_down[l].astype(bf16), gf, tm, final=(l == depth - 1))
    return x
```

```python
import functools
import math

import jax
import jax.numpy as jnp
from jax import lax
from jax.experimental import pallas as pl
from jax.experimental.pallas import tpu as pltpu

HEAD_DIM = 64
H_FOX = 8
H_CHK = 8
N_PAIR = 4
W_ATT = H_FOX * HEAD_DIM
CHUNK = 64
Q_BLOCK = 128
LEFT_CHUNKS = 8
PAD = LEFT_CHUNKS * CHUNK
BAND = PAD + Q_BLOCK
MAX_REL = 128
EPS = 1e-6
NEG = -1e30
LANES = 128
SCALE = 1.0 / math.sqrt(HEAD_DIM)
LOG2E = math.log2(math.e)
F32_BIG = 3.0e38
PROJ_SPLIT = 2
FOX_ROWS = 32
FOX_TRIP = 4
CHUNK_ROWS = 32
CHUNK_GROUP = 8
SKIP_MARGIN = 30.0
NORM_SLACK = 1.02
VMEM_LIMIT = 60 * 1024 * 1024

_NT = (((1,), (1,)), ((), ()))


def _dot(a, b):
    return jnp.dot(a, b, preferred_element_type=jnp.float32)


def _dot_nt(a, b):
    return lax.dot_general(a, b, _NT, preferred_element_type=jnp.float32)


def _rms(x, g):
    ms = jnp.mean(x * x, axis=-1, keepdims=True)
    return x * lax.rsqrt(ms + EPS) * g


def _const_spec(shape):
    nd = len(shape)
    return pl.BlockSpec(shape, lambda *_: (0,) * nd, pipeline_mode=pl.Buffered(1))


def _in_proj_body(x_ref, g_ref, wq_ref, wk_ref, wv_ref, wc_ref, wg_ref, wf_ref, bf_ref, eq_ref, ek_ref, hsum_ref,
                  qf_ref, kf_ref, vf_ref, qc_ref, kc_ref, vc_ref, gate_ref, stat_ref, carry_ref):
    tm = x_ref.shape[1]
    th = tm // PROJ_SPLIT

    @pl.when(pl.program_id(1) == 0)
    def _():
        carry_ref[...] = jnp.zeros_like(carry_ref)

    lane = lax.broadcasted_iota(jnp.int32, (th, LANES), 1)
    row = lax.broadcasted_iota(jnp.int32, (th, LANES), 0)

    def max_norm(t):
        t = t.astype(jnp.float32)
        sq = _dot((t * t).astype(jnp.bfloat16), hsum_ref[...])
        return jnp.max(sq, axis=0, keepdims=True)

    c_first, c_last, q_sq, k_sq = None, None, None, None
    for part in range(PROJ_SPLIT):
        rows = slice(part * th, (part + 1) * th)
        h = _rms(x_ref[0, rows, :], g_ref[...]).astype(jnp.bfloat16)

        logf = jax.nn.log_sigmoid(_dot(h, wf_ref[...]) + bf_ref[...]) * LOG2E
        c = jnp.where(lane < H_FOX, logf, 0.0)
        k = 1
        while k < th:
            c = c + jnp.where(row >= k, pltpu.roll(c, k, 0), 0.0)
            k *= 2
        c = c + carry_ref[...]
        carry_ref[...] = c[th - 1:th, :]
        c_first = c[0:1, :] if part == 0 else c_first
        c_last = c[th - 1:th, :]

        hi = c.astype(jnp.bfloat16).astype(jnp.float32)
        r1 = c - hi
        mid = r1.astype(jnp.bfloat16).astype(jnp.float32)
        lo = (r1 - mid).astype(jnp.bfloat16).astype(jnp.float32)
        pieces = hi + pltpu.roll(mid, H_FOX, 1) + pltpu.roll(lo, 2 * H_FOX, 1)
        pieces = jnp.where(lane == 3 * H_FOX, 1.0, pieces).astype(jnp.bfloat16)
        aug_q = _dot(pieces, eq_ref[...]).astype(jnp.bfloat16)
        aug_k = _dot(pieces, ek_ref[...]).astype(jnp.bfloat16)

        q = _dot(h, wq_ref[...]).astype(jnp.bfloat16)
        kk = _dot(h, wk_ref[...]).astype(jnp.bfloat16)
        for p in range(N_PAIR):
            src = slice(p * LANES, (p + 1) * LANES)
            qf_ref[0, rows, 2 * p * LANES:(2 * p + 1) * LANES] = q[:, src]
            qf_ref[0, rows, (2 * p + 1) * LANES:(2 * p + 2) * LANES] = aug_q[:, src]
            kf_ref[0, rows, 2 * p * LANES:(2 * p + 1) * LANES] = kk[:, src]
            kf_ref[0, rows, (2 * p + 1) * LANES:(2 * p + 2) * LANES] = aug_k[:, src]
        vf_ref[0, rows, :] = _dot(h, wv_ref[...]).astype(jnp.bfloat16)
        q_sq = max_norm(q) if part == 0 else jnp.maximum(q_sq, max_norm(q))
        k_sq = max_norm(kk) if part == 0 else jnp.maximum(k_sq, max_norm(kk))

        pc = _dot(h, wc_ref[...])
        qc_ref[0, rows, :] = pc[:, :W_ATT].astype(jnp.bfloat16)
        kc_ref[0, rows, :] = pc[:, W_ATT:2 * W_ATT].astype(jnp.bfloat16)
        vc_ref[0, rows, :] = pc[:, 2 * W_ATT:].astype(jnp.bfloat16)
        gate_ref[0, rows, :] = jax.nn.sigmoid(_dot(h, wg_ref[...])).astype(jnp.bfloat16)

    sub = lax.broadcasted_iota(jnp.int32, (8, LANES), 0)
    stat = jnp.where(sub == 0, c_first, 0.0)
    stat = jnp.where(sub == 1, c_last, stat)
    stat = jnp.where(sub == 2, jnp.sqrt(q_sq), stat)
    stat_ref[0, 0] = jnp.where(sub == 3, jnp.sqrt(k_sq), stat)


def _in_proj(x, g, wq, wk, wv, wc, wg, wf, bf, eq, ek, hsum, tm):
    b, s, d = x.shape
    tok = lambda width: pl.BlockSpec((1, tm, width), lambda i, j: (i, j, 0))
    bf16 = jnp.bfloat16
    out_shape = [jax.ShapeDtypeStruct((b, s, width), bf16)
                 for width in (2 * W_ATT, 2 * W_ATT, W_ATT, W_ATT, W_ATT, W_ATT, 2 * d)]
    out_shape.append(jax.ShapeDtypeStruct((b, s // tm, 8, LANES), jnp.float32))
    consts = (g, wq, wk, wv, wc, wg, wf, bf, eq, ek, hsum)
    return pl.pallas_call(
        _in_proj_body,
        grid=(b, s // tm),
        in_specs=[tok(d)] + [_const_spec(a.shape) for a in consts],
        out_specs=[tok(sh.shape[-1]) for sh in out_shape[:-1]]
                  + [pl.BlockSpec((1, 1, 8, LANES), lambda i, j: (i, j, 0, 0))],
        out_shape=out_shape,
        scratch_shapes=[pltpu.VMEM((1, LANES), jnp.float32)],
        compiler_params=pltpu.CompilerParams(
            dimension_semantics=("arbitrary", "arbitrary"), vmem_limit_bytes=VMEM_LIMIT),
        name="in_proj",
    )(x, *consts)


def _fox_body(cfirst_ref, clast_ref, qnorm_ref, knorm_ref, q_ref, k_ref, v_ref, o_ref,
              s_ref, p_ref, m_ref, l_ref, acc_ref, bad_ref, *, tq):
    n_tiles = q_ref.shape[1] // tq
    lane2 = lax.broadcasted_iota(jnp.int32, (1, 2 * LANES), 1) % LANES
    head_a = lax.broadcasted_iota(jnp.int32, (1, LANES), 1) < HEAD_DIM
    base = [(pl.program_id(0) * H_FOX + 2 * pl.program_id(1) + head) * n_tiles for head in range(2)]

    def key_rows(j):
        return pl.ds(pl.multiple_of(j * tq, tq), tq)

    def scores(slot, q_head, j):
        s_ref[slot] = _dot_nt(q_head, k_ref[0, key_rows(j), :])

    def softmax(slot, head, mode):
        for r in range(0, tq, FOX_ROWS):
            rows = slice(r, r + FOX_ROWS)
            s = s_ref[slot, rows, :]
            if mode in ("diagonal", "diagonal_self"):
                row = lax.broadcasted_iota(jnp.int32, (FOX_ROWS, tq), 0) + r
                col = lax.broadcasted_iota(jnp.int32, (FOX_ROWS, tq), 1)
                if mode == "diagonal":
                    s = jnp.where(col <= row, s, NEG)
                    m = jnp.max(s, axis=-1, keepdims=True)
                else:
                    own = slice(r // LANES * LANES, (r // LANES + 1) * LANES)
                    lane = lax.broadcasted_iota(jnp.int32, (FOX_ROWS, LANES), 1)
                    sub = lax.broadcasted_iota(jnp.int32, (FOX_ROWS, LANES), 0)
                    m = jnp.sum(jnp.where(lane == sub + r % LANES, s[:, own], 0.0), axis=-1, keepdims=True)
                    s = jnp.where(col <= row, s, NEG)
                m = jnp.broadcast_to(m, (FOX_ROWS, LANES))
                m_ref[head, rows, :] = m
            elif mode == "exact":
                m_old = m_ref[head, rows, :]
                m = jnp.maximum(m_old, jnp.max(s, axis=-1, keepdims=True))
                alpha = jnp.exp2(m_old - m)
                m_ref[head, rows, :] = m
            else:
                m = m_ref[head, rows, :]
            lsum = None
            for g in range(tq // LANES):
                cols = slice(g * LANES, (g + 1) * LANES)
                pg = jnp.exp2(s[:, cols] - m)
                p_ref[slot, rows, cols] = pg.astype(p_ref.dtype)
                lsum = pg if lsum is None else lsum + pg
            if mode in ("diagonal", "diagonal_self"):
                l_ref[head, rows, :] = lsum
            elif mode == "exact":
                l_ref[head, rows, :] = l_ref[head, rows, :] * alpha + lsum
                acc_ref[head, rows, :] = acc_ref[head, rows, :] * alpha
            else:
                l_ref[head, rows, :] = l_ref[head, rows, :] + lsum

    def values(slot, head, j, mode):
        pv = _dot(p_ref[slot], v_ref[0, key_rows(j), :])
        acc_ref[head] = pv if mode in ("diagonal", "diagonal_self") else acc_ref[head] + pv

    def unit_run(q_of, units):
        n_slots = s_ref.shape[0]
        for n in range(min(2, len(units))):
            scores(n, q_of(units[n][1]), units[n][0])
        for n, (tile, head, mode) in enumerate(units):
            softmax(n % n_slots, head, mode)
            if n + 2 < len(units):
                scores((n + 2) % n_slots, q_of(units[n + 2][1]), units[n + 2][0])
            values(n % n_slots, head, tile, mode)

    def run_tiles(q_of, make_units, first, total):
        def full(t, carry):
            unit_run(q_of, make_units(first - FOX_TRIP * t, FOX_TRIP))
            return carry

        n_full = total // FOX_TRIP
        lax.fori_loop(0, n_full, full, 0)
        done = n_full * FOX_TRIP
        count = FOX_TRIP // 2
        while count >= 1:
            fits = (total - done) >= count

            @pl.when(fits)
            def _(start=first - done, count=count):
                unit_run(q_of, make_units(start, count))

            done = jnp.where(fits, done + count, done)
            count //= 2

    def query_tile(qi, kmax, exact):
        q_rows = pl.ds(pl.multiple_of(qi * tq, tq), tq)
        qcat = q_ref[0, q_rows, :]
        zero = jnp.zeros_like(qcat)
        qh = (jnp.where(lane2 < HEAD_DIM, qcat, zero), jnp.where(lane2 >= HEAD_DIM, qcat, zero))

        kmax = [jnp.maximum(kmax[head], knorm_ref[base[head] + qi]) for head in range(2)]
        reach = []
        for head in range(2):
            bound = (cfirst_ref[base[head] + qi] + 2.0 * NORM_SLACK * qnorm_ref[base[head] + qi] * kmax[head]
                     + SKIP_MARGIN * LOG2E)

            def reaches(n, head=head, bound=bound):
                t = jnp.maximum(qi - 1 - n, 0)
                return (n < qi) & (bound - clast_ref[base[head] + t] > 0.0)

            reach.append(lax.while_loop(reaches, lambda n: n + 1, 0))
        n_both, n_long = jnp.minimum(reach[0], reach[1]), jnp.maximum(reach[0], reach[1])

        def both_heads(j, count, first_mode="lazy"):
            return [(j - i, head, first_mode if i == 0 else "lazy") for i in range(count) for head in range(2)]

        if exact:
            unit_run(qh.__getitem__, both_heads(qi, 1, "diagonal"))

            def step(t, carry):
                unit_run(qh.__getitem__, both_heads(qi - 1 - t, 1, "exact"))
                return carry

            lax.fori_loop(0, n_long, step, 0)
        else:
            total = n_both + 1
            count, first_len = FOX_TRIP, 1
            while count >= 1:
                fits = total >= count
                if count < FOX_TRIP:
                    fits = fits & (total < 2 * count)

                @pl.when(fits)
                def _(count=count):
                    unit_run(qh.__getitem__, both_heads(qi, count, "diagonal_self"))

                first_len = jnp.where(fits, count, first_len)
                count //= 2
            run_tiles(qh.__getitem__, both_heads, qi - first_len, total - first_len)

            long_head = jnp.where(reach[1] > reach[0], 1, 0)
            q_long = jnp.where((lane2 < HEAD_DIM) == (long_head == 0), qcat, zero)
            run_tiles(lambda head: q_long, lambda j, count: [(j - i, long_head, "lazy") for i in range(count)],
                      qi - total, n_long - n_both)

        outs = []
        for head in range(2):
            acc, lp = acc_ref[head], l_ref[head]
            outs.append(acc / jnp.sum(lp, axis=-1, keepdims=True))
            if not exact:
                finite = (jnp.abs(acc) <= F32_BIG) & (lp <= F32_BIG)
                bad_ref[...] = jnp.maximum(bad_ref[...], jnp.where(finite, 0.0, 1.0))
        o_ref[0, q_rows, :] = jnp.where(head_a, outs[0], outs[1]).astype(o_ref.dtype)
        return kmax

    def sweep(exact):
        def body(qi, kmax):
            return tuple(query_tile(qi, kmax, exact))

        lax.fori_loop(0, n_tiles, body, (jnp.float32(0.0), jnp.float32(0.0)))

    bad_ref[...] = jnp.zeros_like(bad_ref)
    sweep(exact=False)

    @pl.when(jnp.max(bad_ref[...]) > 0.0)
    def _():
        sweep(exact=True)


def _fox(stats, qf, kf, vf, tq):
    b, s, _ = qf.shape
    smem = pl.BlockSpec(memory_space=pltpu.SMEM)
    return pl.pallas_call(
        functools.partial(_fox_body, tq=tq),
        grid=(b, N_PAIR),
        in_specs=[smem, smem, smem, smem,
                  pl.BlockSpec((1, s, 2 * LANES), lambda i, p: (i, 0, p)),
                  pl.BlockSpec((1, s, 2 * LANES), lambda i, p: (i, 0, p)),
                  pl.BlockSpec((1, s, LANES), lambda i, p: (i, 0, p))],
        out_specs=pl.BlockSpec((1, s, LANES), lambda i, p: (i, 0, p)),
        out_shape=jax.ShapeDtypeStruct((b, s, W_ATT), jnp.bfloat16),
        scratch_shapes=[pltpu.VMEM((4, tq, tq), jnp.float32),
                        pltpu.VMEM((4, tq, tq), jnp.bfloat16),
                        pltpu.VMEM((2, tq, LANES), jnp.float32),
                        pltpu.VMEM((2, tq, LANES), jnp.float32),
                        pltpu.VMEM((2, tq, LANES), jnp.float32),
                        pltpu.VMEM((tq, LANES), jnp.float32)],
        compiler_params=pltpu.CompilerParams(
            dimension_semantics=("arbitrary", "arbitrary"), vmem_limit_bytes=VMEM_LIMIT),
        name="fox_attention",
    )(*stats, qf, kf, vf)


def _band_bias_body(g_ref, o_ref):
    width = g_ref.shape[-1]
    base = jnp.broadcast_to(g_ref[0], (Q_BLOCK, width))
    toeplitz = pltpu.roll(base, 0, 1, stride=1, stride_axis=0)[:, :BAND]
    qrow = lax.broadcasted_iota(jnp.int32, (Q_BLOCK, BAND), 0)
    kcol = lax.broadcasted_iota(jnp.int32, (Q_BLOCK, BAND), 1)
    cq = qrow // CHUNK
    ck = kcol // CHUNK - LEFT_CHUNKS
    valid = (ck <= cq) & (ck >= cq - LEFT_CHUNKS)
    o_ref[0] = jnp.where(valid, toeplitz * LOG2E, NEG)


def _band_bias(g_ext):
    h, _, width = g_ext.shape
    return pl.pallas_call(
        _band_bias_body,
        grid=(h,),
        in_specs=[pl.BlockSpec((1, 1, width), lambda i: (i, 0, 0))],
        out_specs=pl.BlockSpec((1, Q_BLOCK, BAND), lambda i: (i, 0, 0)),
        out_shape=jax.ShapeDtypeStruct((h, Q_BLOCK, BAND), jnp.float32),
        name="band_bias",
    )(g_ext)


def _chunk_body(q_ref, k_ref, v_ref, bias_ref, o_ref, kpad_ref, vpad_ref, s_ref, p_ref, l_ref, *, tq):
    qi = pl.program_id(2)
    s_len = k_ref.shape[1]

    @pl.when(qi == 0)
    def _():
        zeros = jnp.zeros((PAD, LANES), kpad_ref.dtype)
        kpad_ref[:PAD, :] = zeros
        vpad_ref[:PAD, :] = zeros
        kpad_ref[PAD:PAD + s_len, :] = k_ref[0]
        vpad_ref[PAD:PAD + s_len, :] = v_ref[0]

    head_a = lax.broadcasted_iota(jnp.int32, (1, LANES), 1) < HEAD_DIM
    kcol = lax.broadcasted_iota(jnp.int32, (1, BAND), 1)

    def offsets(blk):
        r0 = pl.multiple_of(blk * Q_BLOCK, Q_BLOCK)
        return r0, pl.multiple_of(qi * tq + r0, Q_BLOCK)

    def scores(slot, blk):
        r0, p0 = offsets(blk)
        q2 = q_ref[0, pl.ds(r0, Q_BLOCK), :]
        zero = jnp.zeros_like(q2)
        qs = jnp.concatenate([jnp.where(head_a, q2, zero), jnp.where(head_a, zero, q2)], axis=0)
        s_ref[slot] = _dot_nt(qs, kpad_ref[pl.ds(p0, BAND), :])

    def softmax(slot, blk, masked):
        _, p0 = offsets(blk)
        for r in range(0, 2 * Q_BLOCK, CHUNK_ROWS):
            rows = slice(r, r + CHUNK_ROWS)
            s = s_ref[slot, rows, :] + bias_ref[0, rows, :]
            if masked:
                s = jnp.where(kcol + p0 >= PAD, s, NEG)
            p = jnp.exp2(s - jnp.max(s, axis=-1, keepdims=True))
            l_ref[slot, rows, :] = jnp.broadcast_to(jnp.sum(p, axis=-1, keepdims=True), (CHUNK_ROWS, LANES))
            p_ref[slot, rows, :] = p.astype(p_ref.dtype)

    def values(slot, blk):
        r0, p0 = offsets(blk)
        o = _dot(p_ref[slot], vpad_ref[pl.ds(p0, BAND), :]) / l_ref[slot]
        o_ref[0, pl.ds(r0, Q_BLOCK), :] = jnp.where(head_a, o[:Q_BLOCK], o[Q_BLOCK:]).astype(o_ref.dtype)

    def group(i, masked):
        first = i * CHUNK_GROUP
        scores(0, first)
        for n in range(CHUNK_GROUP):
            if n + 1 < CHUNK_GROUP:
                scores((n + 1) % 2, first + n + 1)
            softmax(n % 2, first + n, masked)
            values(n % 2, first + n)

    def masked_group(i, carry):
        group(i, True)
        return carry

    def plain_group(i, carry):
        group(i, False)
        return carry

    group_rows = Q_BLOCK * CHUNK_GROUP
    n_groups = tq // group_rows
    n_masked = jnp.clip((PAD - qi * tq + group_rows - 1) // group_rows, 0, n_groups)
    lax.fori_loop(0, n_masked, masked_group, 0)
    lax.fori_loop(n_masked, n_groups, plain_group, 0)


def _chunk_attention(qc, kc, vc, bias, tq):
    b, s, _ = qc.shape
    assert tq % (CHUNK_GROUP * Q_BLOCK) == 0
    return pl.pallas_call(
        functools.partial(_chunk_body, tq=tq),
        grid=(b, N_PAIR, s // tq),
        in_specs=[pl.BlockSpec((1, tq, LANES), lambda i, p, j: (i, j, p)),
                  pl.BlockSpec((1, s, LANES), lambda i, p, j: (i, 0, p)),
                  pl.BlockSpec((1, s, LANES), lambda i, p, j: (i, 0, p)),
                  pl.BlockSpec((1, 2 * Q_BLOCK, BAND), lambda i, p, j: (p, 0, 0))],
        out_specs=pl.BlockSpec((1, tq, LANES), lambda i, p, j: (i, j, p)),
        out_shape=jax.ShapeDtypeStruct((b, s, W_ATT), jnp.bfloat16),
        scratch_shapes=[pltpu.VMEM((PAD + s, LANES), jnp.bfloat16),
                        pltpu.VMEM((PAD + s, LANES), jnp.bfloat16),
                        pltpu.VMEM((2, 2 * Q_BLOCK, BAND), jnp.float32),
                        pltpu.VMEM((2, 2 * Q_BLOCK, BAND), jnp.bfloat16),
                        pltpu.VMEM((2, 2 * Q_BLOCK, LANES), jnp.float32)],
        compiler_params=pltpu.CompilerParams(
            dimension_semantics=("arbitrary", "arbitrary", "arbitrary"), vmem_limit_bytes=VMEM_LIMIT),
        name="chunk_attention",
    )(qc, kc, vc, bias)


def _out_ffn_body(x_ref, oa_ref, ob_ref, gate_ref, wa_ref, wb_ref, wo_ref, g2_ref, wu_ref, wd_ref, gf_ref,
                  o_ref, *, ff_chunk, final):
    d = x_ref.shape[-1]
    ya = _dot(oa_ref[0], wa_ref[...])
    yb = _dot(ob_ref[0], wb_ref[...])
    ga = gate_ref[0, :, :d].astype(jnp.float32)
    gb = gate_ref[0, :, d:].astype(jnp.float32)
    merged = (ga * ya + gb * yb).astype(jnp.bfloat16)
    x1 = x_ref[0] + _dot(merged, wo_ref[...])
    h2 = _rms(x1, g2_ref[...]).astype(jnp.bfloat16)
    acc = x1
    for c0 in range(0, wu_ref.shape[1], ff_chunk):
        u = jnp.maximum(_dot(h2, wu_ref[:, c0:c0 + ff_chunk]), 0.0)
        acc = acc + _dot((u * u).astype(jnp.bfloat16), wd_ref[c0:c0 + ff_chunk, :])
    if final:
        acc = _rms(acc, gf_ref[...])
    o_ref[0] = acc


def _out_ffn(x, oa, ob, gates, wa, wb, wo, g2, wu, wd, gf, tm, final):
    b, s, d = x.shape
    tok = lambda w: pl.BlockSpec((1, tm, w), lambda i, j: (i, j, 0))
    return pl.pallas_call(
        functools.partial(_out_ffn_body, ff_chunk=min(1024, wu.shape[1]), final=final),
        grid=(b, s // tm),
        in_specs=[tok(d), tok(W_ATT), tok(W_ATT), tok(2 * d)]
                 + [_const_spec(a.shape) for a in (wa, wb, wo, g2, wu, wd, gf)],
        out_specs=tok(d),
        out_shape=jax.ShapeDtypeStruct((b, s, d), jnp.float32),
        compiler_params=pltpu.CompilerParams(
            dimension_semantics=("arbitrary", "arbitrary"), vmem_limit_bytes=VMEM_LIMIT),
        name="out_ffn",
    )(x, oa, ob, gates, wa, wb, wo, g2, wu, wd, gf)


def _aug_placement():
    eq = [[0.0] * W_ATT for _ in range(LANES)]
    ek = [[0.0] * W_ATT for _ in range(LANES)]
    one = 3 * H_FOX
    for h in range(H_FOX):
        for piece in range(3):
            eq[piece * H_FOX + h][HEAD_DIM * h + piece] = 1.0
            eq[one][HEAD_DIM * h + 3 + piece] = 1.0
            ek[one][HEAD_DIM * h + piece] = 1.0
            ek[piece * H_FOX + h][HEAD_DIM * h + 3 + piece] = -1.0
    return jnp.array(eq, jnp.bfloat16), jnp.array(ek, jnp.bfloat16)


def kernel(x, norm1, w_in, forget_bias, rel_bias, w_branch_a, w_branch_b, w_out, norm2, w_up, w_down, final_norm):
    b, s, d = x.shape
    depth = w_in.shape[0]
    bf16 = jnp.bfloat16
    tm = min(512, s)
    tqc = min(1024, s)
    eq, ek = _aug_placement()
    head_of_col = lax.broadcasted_iota(jnp.int32, (W_ATT, LANES), 0) // HEAD_DIM
    hsum = (head_of_col == lax.broadcasted_iota(jnp.int32, (W_ATT, LANES), 1)).astype(bf16)
    gf = final_norm.reshape(1, d)
    o = 3 * W_ATT + H_FOX
    for l in range(depth):
        w = w_in[l]
        wq = (w[:, :W_ATT] * (SCALE * LOG2E)).astype(bf16)
        wk = w[:, W_ATT:2 * W_ATT].astype(bf16)
        wv = w[:, 2 * W_ATT:3 * W_ATT].astype(bf16)
        wf = jnp.pad(w[:, 3 * W_ATT:o], ((0, 0), (0, LANES - H_FOX))).astype(bf16)
        wc = jnp.concatenate([w[:, o:o + W_ATT] * (SCALE * LOG2E), w[:, o + W_ATT:o + 3 * W_ATT]],
                             axis=1).astype(bf16)
        wg = w[:, o + 3 * W_ATT:].astype(bf16)
        bf = jnp.pad(forget_bias[l], (0, LANES - H_FOX)).reshape(1, LANES)
        qf, kf, vf, qc, kc, vc, gates, stat = _in_proj(
            x, norm1[l].reshape(1, d), wq, wk, wv, wc, wg, wf, bf, eq, ek, hsum, tm)

        stats = stat[:, :, :4, :H_FOX].transpose(2, 0, 3, 1).reshape(4, -1)
        o_a = _fox(tuple(stats), qf, kf, vf, tm)

        far = rel_bias[l][:, 2 * MAX_REL:]
        g_ext = jnp.concatenate([jnp.broadcast_to(far, (H_CHK, PAD - MAX_REL + 1)),
                                 rel_bias[l][:, 2 * MAX_REL - 1:0:-1],
                                 jnp.broadcast_to(far, (H_CHK, Q_BLOCK))], axis=1)
        bias = _band_bias(g_ext.reshape(H_CHK, 1, -1)).reshape(N_PAIR, 2 * Q_BLOCK, BAND)
        o_b = _chunk_attention(qc, kc, vc, bias, tqc)

        x = _out_ffn(x, o_a, o_b, gates, w_branch_a[l].astype(bf16), w_branch_b[l].astype(bf16),
                     w_out[l].astype(bf16), norm2[l].reshape(1, d), w_up[l].astype(bf16),
                     w_down[l].astype(bf16), gf, tm, final=(l == depth - 1))
    return x
```

```python
import functools
import math

import jax
import jax.numpy as jnp
from jax import lax
from jax.experimental import pallas as pl
from jax.experimental.pallas import tpu as pltpu

HEAD_DIM = 64
H_FOX = 8
H_CHK = 8
N_PAIR = 4
W_ATT = H_FOX * HEAD_DIM
CHUNK = 64
Q_BLOCK = 128
LEFT_CHUNKS = 8
PAD = LEFT_CHUNKS * CHUNK
BAND = PAD + Q_BLOCK
MAX_REL = 128
EPS = 1e-6
NEG = -1e30
LANES = 128
SCALE = 1.0 / math.sqrt(HEAD_DIM)
LOG2E = math.log2(math.e)
F32_BIG = 3.0e38
PROJ_SPLIT = 2
FOX_ROWS = 32
FOX_TRIP = 4
FOX_AHEAD = 2
CHUNK_ROWS = 32
CHUNK_GROUP = 8
SKIP_MARGIN = 30.0
NORM_SLACK = 1.02
VMEM_LIMIT = 60 * 1024 * 1024

_NT = (((1,), (1,)), ((), ()))


def _dot(a, b):
    return jnp.dot(a, b, preferred_element_type=jnp.float32)


def _dot_nt(a, b):
    return lax.dot_general(a, b, _NT, preferred_element_type=jnp.float32)


def _rms(x, g):
    ms = jnp.mean(x * x, axis=-1, keepdims=True)
    return x * lax.rsqrt(ms + EPS) * g


def _const_spec(shape):
    nd = len(shape)
    return pl.BlockSpec(shape, lambda *_: (0,) * nd, pipeline_mode=pl.Buffered(1))


def _in_proj_body(x_ref, g_ref, wq_ref, wk_ref, wv_ref, wc_ref, wg_ref, wf_ref, bf_ref, eq_ref, ek_ref, hsum_ref,
                  wo32_ref, wu32_ref, wd32_ref,
                  qf_ref, kf_ref, vf_ref, qc_ref, kc_ref, vc_ref, gate_ref, stat_ref, wo_ref, wu_ref, wd_ref,
                  carry_ref):
    tm = x_ref.shape[1]
    th = tm // PROJ_SPLIT

    wo_ref[...] = wo32_ref[...].astype(wo_ref.dtype)
    wu_ref[...] = wu32_ref[...].astype(wu_ref.dtype)
    wd_ref[...] = wd32_ref[...].astype(wd_ref.dtype)

    @pl.when(pl.program_id(1) == 0)
    def _():
        carry_ref[...] = jnp.zeros_like(carry_ref)

    lane = lax.broadcasted_iota(jnp.int32, (th, LANES), 1)
    row = lax.broadcasted_iota(jnp.int32, (th, LANES), 0)

    def max_norm(t):
        t = t.astype(jnp.float32)
        sq = _dot((t * t).astype(jnp.bfloat16), hsum_ref[...])
        return jnp.max(sq, axis=0, keepdims=True)

    c_first, c_last, q_sq, k_sq = None, None, None, None
    for part in range(PROJ_SPLIT):
        rows = slice(part * th, (part + 1) * th)
        h = _rms(x_ref[0, rows, :], g_ref[...]).astype(jnp.bfloat16)

        logf = jax.nn.log_sigmoid(_dot(h, wf_ref[...]) + bf_ref[...]) * LOG2E
        c = jnp.where(lane < H_FOX, logf, 0.0)
        k = 1
        while k < th:
            c = c + jnp.where(row >= k, pltpu.roll(c, k, 0), 0.0)
            k *= 2
        c = c + carry_ref[...]
        carry_ref[...] = c[th - 1:th, :]
        c_first = c[0:1, :] if part == 0 else c_first
        c_last = c[th - 1:th, :]

        hi = c.astype(jnp.bfloat16).astype(jnp.float32)
        r1 = c - hi
        mid = r1.astype(jnp.bfloat16).astype(jnp.float32)
        lo = (r1 - mid).astype(jnp.bfloat16).astype(jnp.float32)
        pieces = hi + pltpu.roll(mid, H_FOX, 1) + pltpu.roll(lo, 2 * H_FOX, 1)
        pieces = jnp.where(lane == 3 * H_FOX, 1.0, pieces).astype(jnp.bfloat16)
        aug_q = _dot(pieces, eq_ref[...]).astype(jnp.bfloat16)
        aug_k = _dot(pieces, ek_ref[...]).astype(jnp.bfloat16)

        q = _dot(h, wq_ref[...]).astype(jnp.bfloat16)
        kk = _dot(h, wk_ref[...]).astype(jnp.bfloat16)
        for p in range(N_PAIR):
            src = slice(p * LANES, (p + 1) * LANES)
            qf_ref[0, rows, 2 * p * LANES:(2 * p + 1) * LANES] = q[:, src]
            qf_ref[0, rows, (2 * p + 1) * LANES:(2 * p + 2) * LANES] = aug_q[:, src]
            kf_ref[0, rows, 2 * p * LANES:(2 * p + 1) * LANES] = kk[:, src]
            kf_ref[0, rows, (2 * p + 1) * LANES:(2 * p + 2) * LANES] = aug_k[:, src]
        vf_ref[0, rows, :] = _dot(h, wv_ref[...]).astype(jnp.bfloat16)
        q_sq = max_norm(q) if part == 0 else jnp.maximum(q_sq, max_norm(q))
        k_sq = max_norm(kk) if part == 0 else jnp.maximum(k_sq, max_norm(kk))

        pc = _dot(h, wc_ref[...])
        qc_ref[0, rows, :] = pc[:, :W_ATT].astype(jnp.bfloat16)
        kc_ref[0, rows, :] = pc[:, W_ATT:2 * W_ATT].astype(jnp.bfloat16)
        vc_ref[0, rows, :] = pc[:, 2 * W_ATT:].astype(jnp.bfloat16)
        gate_ref[0, rows, :] = jax.nn.sigmoid(_dot(h, wg_ref[...])).astype(jnp.bfloat16)

    sub = lax.broadcasted_iota(jnp.int32, (8, LANES), 0)
    stat = jnp.where(sub == 0, c_first, 0.0)
    stat = jnp.where(sub == 1, c_last, stat)
    stat = jnp.where(sub == 2, jnp.sqrt(q_sq), stat)
    stat_ref[0, 0] = jnp.where(sub == 3, jnp.sqrt(k_sq), stat)


def _in_proj(x, g, wq, wk, wv, wc, wg, wf, bf, eq, ek, hsum, to_cast, tm):
    b, s, d = x.shape
    n_steps = b * (s // tm)
    tok = lambda width: pl.BlockSpec((1, tm, width), lambda i, j: (i, j, 0))
    strip = lambda w: pl.BlockSpec((w.shape[0] // n_steps, w.shape[1]), lambda i, j: (i * (s // tm) + j, 0))
    bf16 = jnp.bfloat16
    assert all(w.shape[0] % (16 * n_steps) == 0 for w in to_cast)
    out_shape = [jax.ShapeDtypeStruct((b, s, width), bf16)
                 for width in (2 * W_ATT, 2 * W_ATT, W_ATT, W_ATT, W_ATT, W_ATT, 2 * d)]
    out_shape.append(jax.ShapeDtypeStruct((b, s // tm, 8, LANES), jnp.float32))
    out_shape += [jax.ShapeDtypeStruct(w.shape, bf16) for w in to_cast]
    consts = (g, wq, wk, wv, wc, wg, wf, bf, eq, ek, hsum)
    return pl.pallas_call(
        _in_proj_body,
        grid=(b, s // tm),
        in_specs=[tok(d)] + [_const_spec(a.shape) for a in consts] + [strip(w) for w in to_cast],
        out_specs=[tok(sh.shape[-1]) for sh in out_shape[:7]]
                  + [pl.BlockSpec((1, 1, 8, LANES), lambda i, j: (i, j, 0, 0))]
                  + [strip(w) for w in to_cast],
        out_shape=out_shape,
        scratch_shapes=[pltpu.VMEM((1, LANES), jnp.float32)],
        compiler_params=pltpu.CompilerParams(
            dimension_semantics=("arbitrary", "arbitrary"), vmem_limit_bytes=VMEM_LIMIT),
        name="in_proj",
    )(x, *consts, *to_cast)


def _fox_body(cfirst_ref, clast_ref, qnorm_ref, knorm_ref, q_ref, k_ref, v_ref, o_ref,
              s_ref, p_ref, m_ref, l_ref, acc_ref, bad_ref, *, tq):
    n_tiles = q_ref.shape[1] // tq
    lane2 = lax.broadcasted_iota(jnp.int32, (1, 2 * LANES), 1) % LANES
    head_a = lax.broadcasted_iota(jnp.int32, (1, LANES), 1) < HEAD_DIM
    base = [(pl.program_id(0) * H_FOX + 2 * pl.program_id(1) + head) * n_tiles for head in range(2)]

    def key_rows(j):
        return pl.ds(pl.multiple_of(j * tq, tq), tq)

    def scores(slot, q_head, j):
        s_ref[slot] = _dot_nt(q_head, k_ref[0, key_rows(j), :])

    def softmax(slot, head, mode):
        for r in range(0, tq, FOX_ROWS):
            rows = slice(r, r + FOX_ROWS)
            s = s_ref[slot, rows, :]
            if mode in ("diagonal", "diagonal_self"):
                row = lax.broadcasted_iota(jnp.int32, (FOX_ROWS, tq), 0) + r
                col = lax.broadcasted_iota(jnp.int32, (FOX_ROWS, tq), 1)
                if mode == "diagonal":
                    s = jnp.where(col <= row, s, NEG)
                    m = jnp.max(s, axis=-1, keepdims=True)
                else:
                    own = slice(r // LANES * LANES, (r // LANES + 1) * LANES)
                    lane = lax.broadcasted_iota(jnp.int32, (FOX_ROWS, LANES), 1)
                    sub = lax.broadcasted_iota(jnp.int32, (FOX_ROWS, LANES), 0)
                    m = jnp.sum(jnp.where(lane == sub + r % LANES, s[:, own], 0.0), axis=-1, keepdims=True)
                    s = jnp.where(col <= row, s, NEG)
                m = jnp.broadcast_to(m, (FOX_ROWS, LANES))
                m_ref[head, rows, :] = m
            elif mode == "exact":
                m_old = m_ref[head, rows, :]
                m = jnp.maximum(m_old, jnp.max(s, axis=-1, keepdims=True))
                alpha = jnp.exp2(m_old - m)
                m_ref[head, rows, :] = m
            else:
                m = m_ref[head, rows, :]
            lsum = None
            for g in range(tq // LANES):
                cols = slice(g * LANES, (g + 1) * LANES)
                pg = jnp.exp2(s[:, cols] - m)
                p_ref[slot, rows, cols] = pg.astype(p_ref.dtype)
                lsum = pg if lsum is None else lsum + pg
            if mode in ("diagonal", "diagonal_self"):
                l_ref[head, rows, :] = lsum
            elif mode == "exact":
                l_ref[head, rows, :] = l_ref[head, rows, :] * alpha + lsum
                acc_ref[head, rows, :] = acc_ref[head, rows, :] * alpha
            else:
                l_ref[head, rows, :] = l_ref[head, rows, :] + lsum

    def values(slot, head, j, mode):
        pv = _dot(p_ref[slot], v_ref[0, key_rows(j), :])
        acc_ref[head] = pv if mode in ("diagonal", "diagonal_self") else acc_ref[head] + pv

    def unit_run(q_of, units):
        n_slots = s_ref.shape[0]
        for n in range(min(FOX_AHEAD, len(units))):
            scores(n, q_of(units[n][1]), units[n][0])
        for n, (tile, head, mode) in enumerate(units):
            softmax(n % n_slots, head, mode)
            if n + FOX_AHEAD < len(units):
                scores((n + FOX_AHEAD) % n_slots, q_of(units[n + FOX_AHEAD][1]), units[n + FOX_AHEAD][0])
            values(n % n_slots, head, tile, mode)

    def run_tiles(q_of, make_units, first, total):
        def full(t, carry):
            unit_run(q_of, make_units(first - FOX_TRIP * t, FOX_TRIP))
            return carry

        n_full = total // FOX_TRIP
        lax.fori_loop(0, n_full, full, 0)
        done = n_full * FOX_TRIP
        count = FOX_TRIP // 2
        while count >= 1:
            fits = (total - done) >= count

            @pl.when(fits)
            def _(start=first - done, count=count):
                unit_run(q_of, make_units(start, count))

            done = jnp.where(fits, done + count, done)
            count //= 2

    def query_tile(qi, kmax, exact):
        q_rows = pl.ds(pl.multiple_of(qi * tq, tq), tq)
        qcat = q_ref[0, q_rows, :]
        zero = jnp.zeros_like(qcat)
        qh = (jnp.where(lane2 < HEAD_DIM, qcat, zero), jnp.where(lane2 >= HEAD_DIM, qcat, zero))

        kmax = [jnp.maximum(kmax[head], knorm_ref[base[head] + qi]) for head in range(2)]
        reach = []
        for head in range(2):
            bound = (cfirst_ref[base[head] + qi] + 2.0 * NORM_SLACK * qnorm_ref[base[head] + qi] * kmax[head]
                     + SKIP_MARGIN * LOG2E)

            def reaches(n, head=head, bound=bound):
                t = jnp.maximum(qi - 1 - n, 0)
                return (n < qi) & (bound - clast_ref[base[head] + t] > 0.0)

            reach.append(lax.while_loop(reaches, lambda n: n + 1, 0))
        n_both, n_long = jnp.minimum(reach[0], reach[1]), jnp.maximum(reach[0], reach[1])

        def both_heads(j, count, first_mode="lazy"):
            return [(j - i, head, first_mode if i == 0 else "lazy") for i in range(count) for head in range(2)]

        if exact:
            unit_run(qh.__getitem__, both_heads(qi, 1, "diagonal"))

            def step(t, carry):
                unit_run(qh.__getitem__, both_heads(qi - 1 - t, 1, "exact"))
                return carry

            lax.fori_loop(0, n_long, step, 0)
        else:
            total = n_both + 1
            count, first_len = FOX_TRIP, 1
            while count >= 1:
                fits = total >= count
                if count < FOX_TRIP:
                    fits = fits & (total < 2 * count)

                @pl.when(fits)
                def _(count=count):
                    unit_run(qh.__getitem__, both_heads(qi, count, "diagonal_self"))

                first_len = jnp.where(fits, count, first_len)
                count //= 2
            run_tiles(qh.__getitem__, both_heads, qi - first_len, total - first_len)

            long_head = jnp.where(reach[1] > reach[0], 1, 0)
            q_long = jnp.where((lane2 < HEAD_DIM) == (long_head == 0), qcat, zero)
            run_tiles(lambda head: q_long, lambda j, count: [(j - i, long_head, "lazy") for i in range(count)],
                      qi - total, n_long - n_both)

        outs = []
        for head in range(2):
            acc, lp = acc_ref[head], l_ref[head]
            outs.append(acc / jnp.sum(lp, axis=-1, keepdims=True))
            if not exact:
                finite = (jnp.abs(acc) <= F32_BIG) & (lp <= F32_BIG)
                bad_ref[...] = jnp.maximum(bad_ref[...], jnp.where(finite, 0.0, 1.0))
        o_ref[0, q_rows, :] = jnp.where(head_a, outs[0], outs[1]).astype(o_ref.dtype)
        return kmax

    def sweep(exact):
        def body(qi, kmax):
            return tuple(query_tile(qi, kmax, exact))

        lax.fori_loop(0, n_tiles, body, (jnp.float32(0.0), jnp.float32(0.0)))

    bad_ref[...] = jnp.zeros_like(bad_ref)
    sweep(exact=False)

    @pl.when(jnp.max(bad_ref[...]) > 0.0)
    def _():
        sweep(exact=True)


def _fox(stats, qf, kf, vf, tq):
    b, s, _ = qf.shape
    smem = pl.BlockSpec(memory_space=pltpu.SMEM)
    return pl.pallas_call(
        functools.partial(_fox_body, tq=tq),
        grid=(b, N_PAIR),
        in_specs=[smem, smem, smem, smem,
                  pl.BlockSpec((1, s, 2 * LANES), lambda i, p: (i, 0, p)),
                  pl.BlockSpec((1, s, 2 * LANES), lambda i, p: (i, 0, p)),
                  pl.BlockSpec((1, s, LANES), lambda i, p: (i, 0, p))],
        out_specs=pl.BlockSpec((1, s, LANES), lambda i, p: (i, 0, p)),
        out_shape=jax.ShapeDtypeStruct((b, s, W_ATT), jnp.bfloat16),
        scratch_shapes=[pltpu.VMEM((2 * FOX_AHEAD, tq, tq), jnp.float32),
                        pltpu.VMEM((2 * FOX_AHEAD, tq, tq), jnp.bfloat16),
                        pltpu.VMEM((2, tq, LANES), jnp.float32),
                        pltpu.VMEM((2, tq, LANES), jnp.float32),
                        pltpu.VMEM((2, tq, LANES), jnp.float32),
                        pltpu.VMEM((tq, LANES), jnp.float32)],
        compiler_params=pltpu.CompilerParams(
            dimension_semantics=("arbitrary", "arbitrary"), vmem_limit_bytes=VMEM_LIMIT),
        name="fox_attention",
    )(*stats, qf, kf, vf)


def _band_bias_body(g_ref, o_ref):
    width = g_ref.shape[-1]
    base = jnp.broadcast_to(g_ref[0], (Q_BLOCK, width))
    toeplitz = pltpu.roll(base, 0, 1, stride=1, stride_axis=0)[:, :BAND]
    qrow = lax.broadcasted_iota(jnp.int32, (Q_BLOCK, BAND), 0)
    kcol = lax.broadcasted_iota(jnp.int32, (Q_BLOCK, BAND), 1)
    cq = qrow // CHUNK
    ck = kcol // CHUNK - LEFT_CHUNKS
    valid = (ck <= cq) & (ck >= cq - LEFT_CHUNKS)
    o_ref[0] = jnp.where(valid, toeplitz * LOG2E, NEG)


def _band_bias(g_ext):
    h, _, width = g_ext.shape
    return pl.pallas_call(
        _band_bias_body,
        grid=(h,),
        in_specs=[pl.BlockSpec((1, 1, width), lambda i: (i, 0, 0))],
        out_specs=pl.BlockSpec((1, Q_BLOCK, BAND), lambda i: (i, 0, 0)),
        out_shape=jax.ShapeDtypeStruct((h, Q_BLOCK, BAND), jnp.float32),
        name="band_bias",
    )(g_ext)


def _chunk_body(q_ref, k_ref, v_ref, bias_ref, o_ref, kpad_ref, vpad_ref, s_ref, p_ref, l_ref, *, tq):
    qi = pl.program_id(2)
    s_len = k_ref.shape[1]

    @pl.when(qi == 0)
    def _():
        zeros = jnp.zeros((PAD, LANES), kpad_ref.dtype)
        kpad_ref[:PAD, :] = zeros
        vpad_ref[:PAD, :] = zeros
        kpad_ref[PAD:PAD + s_len, :] = k_ref[0]
        vpad_ref[PAD:PAD + s_len, :] = v_ref[0]

    head_a = lax.broadcasted_iota(jnp.int32, (1, LANES), 1) < HEAD_DIM
    kcol = lax.broadcasted_iota(jnp.int32, (1, BAND), 1)

    def offsets(blk):
        r0 = pl.multiple_of(blk * Q_BLOCK, Q_BLOCK)
        return r0, pl.multiple_of(qi * tq + r0, Q_BLOCK)

    def scores(slot, blk):
        r0, p0 = offsets(blk)
        q2 = q_ref[0, pl.ds(r0, Q_BLOCK), :]
        zero = jnp.zeros_like(q2)
        qs = jnp.concatenate([jnp.where(head_a, q2, zero), jnp.where(head_a, zero, q2)], axis=0)
        s_ref[slot] = _dot_nt(qs, kpad_ref[pl.ds(p0, BAND), :])

    def softmax(slot, blk, masked):
        _, p0 = offsets(blk)
        for r in range(0, 2 * Q_BLOCK, CHUNK_ROWS):
            rows = slice(r, r + CHUNK_ROWS)
            s = s_ref[slot, rows, :] + bias_ref[0, rows, :]
            if masked:
                s = jnp.where(kcol + p0 >= PAD, s, NEG)
            p = jnp.exp2(s - jnp.max(s, axis=-1, keepdims=True))
            l_ref[slot, rows, :] = jnp.broadcast_to(jnp.sum(p, axis=-1, keepdims=True), (CHUNK_ROWS, LANES))
            p_ref[slot, rows, :] = p.astype(p_ref.dtype)

    def values(slot, blk):
        r0, p0 = offsets(blk)
        o = _dot(p_ref[slot], vpad_ref[pl.ds(p0, BAND), :]) / l_ref[slot]
        o_ref[0, pl.ds(r0, Q_BLOCK), :] = jnp.where(head_a, o[:Q_BLOCK], o[Q_BLOCK:]).astype(o_ref.dtype)

    def group(i, masked):
        first = i * CHUNK_GROUP
        scores(0, first)
        for n in range(CHUNK_GROUP):
            if n + 1 < CHUNK_GROUP:
                scores((n + 1) % 2, first + n + 1)
            softmax(n % 2, first + n, masked)
            values(n % 2, first + n)

    def masked_group(i, carry):
        group(i, True)
        return carry

    def plain_group(i, carry):
        group(i, False)
        return carry

    group_rows = Q_BLOCK * CHUNK_GROUP
    n_groups = tq // group_rows
    n_masked = jnp.clip((PAD - qi * tq + group_rows - 1) // group_rows, 0, n_groups)
    lax.fori_loop(0, n_masked, masked_group, 0)
    lax.fori_loop(n_masked, n_groups, plain_group, 0)


def _chunk_attention(qc, kc, vc, bias, tq):
    b, s, _ = qc.shape
    assert tq % (CHUNK_GROUP * Q_BLOCK) == 0
    return pl.pallas_call(
        functools.partial(_chunk_body, tq=tq),
        grid=(b, N_PAIR, s // tq),
        in_specs=[pl.BlockSpec((1, tq, LANES), lambda i, p, j: (i, j, p)),
                  pl.BlockSpec((1, s, LANES), lambda i, p, j: (i, 0, p)),
                  pl.BlockSpec((1, s, LANES), lambda i, p, j: (i, 0, p)),
                  pl.BlockSpec((1, 2 * Q_BLOCK, BAND), lambda i, p, j: (p, 0, 0))],
        out_specs=pl.BlockSpec((1, tq, LANES), lambda i, p, j: (i, j, p)),
        out_shape=jax.ShapeDtypeStruct((b, s, W_ATT), jnp.bfloat16),
        scratch_shapes=[pltpu.VMEM((PAD + s, LANES), jnp.bfloat16),
                        pltpu.VMEM((PAD + s, LANES), jnp.bfloat16),
                        pltpu.VMEM((2, 2 * Q_BLOCK, BAND), jnp.float32),
                        pltpu.VMEM((2, 2 * Q_BLOCK, BAND), jnp.bfloat16),
                        pltpu.VMEM((2, 2 * Q_BLOCK, LANES), jnp.float32)],
        compiler_params=pltpu.CompilerParams(
            dimension_semantics=("arbitrary", "arbitrary", "arbitrary"), vmem_limit_bytes=VMEM_LIMIT),
        name="chunk_attention",
    )(qc, kc, vc, bias)


def _out_ffn_body(x_ref, oa_ref, ob_ref, gate_ref, wa_ref, wb_ref, wo_ref, g2_ref, wu_ref, wd_ref, gf_ref,
                  o_ref, *, ff_chunk, final):
    d = x_ref.shape[-1]
    ya = _dot(oa_ref[0], wa_ref[...])
    yb = _dot(ob_ref[0], wb_ref[...])
    ga = gate_ref[0, :, :d].astype(jnp.float32)
    gb = gate_ref[0, :, d:].astype(jnp.float32)
    merged = (ga * ya + gb * yb).astype(jnp.bfloat16)
    x1 = x_ref[0] + _dot(merged, wo_ref[...])
    h2 = _rms(x1, g2_ref[...]).astype(jnp.bfloat16)
    acc = x1
    for c0 in range(0, wu_ref.shape[1], ff_chunk):
        u = jnp.maximum(_dot(h2, wu_ref[:, c0:c0 + ff_chunk]), 0.0)
        acc = acc + _dot((u * u).astype(jnp.bfloat16), wd_ref[c0:c0 + ff_chunk, :])
    if final:
        acc = _rms(acc, gf_ref[...])
    o_ref[0] = acc


def _out_ffn(x, oa, ob, gates, wa, wb, wo, g2, wu, wd, gf, tm, final):
    b, s, d = x.shape
    tok = lambda w: pl.BlockSpec((1, tm, w), lambda i, j: (i, j, 0))
    return pl.pallas_call(
        functools.partial(_out_ffn_body, ff_chunk=min(1024, wu.shape[1]), final=final),
        grid=(b, s // tm),
        in_specs=[tok(d), tok(W_ATT), tok(W_ATT), tok(2 * d)]
                 + [_const_spec(a.shape) for a in (wa, wb, wo, g2, wu, wd, gf)],
        out_specs=tok(d),
        out_shape=jax.ShapeDtypeStruct((b, s, d), jnp.float32),
        compiler_params=pltpu.CompilerParams(
            dimension_semantics=("arbitrary", "arbitrary"), vmem_limit_bytes=VMEM_LIMIT),
        name="out_ffn",
    )(x, oa, ob, gates, wa, wb, wo, g2, wu, wd, gf)


def _aug_placement():
    eq = [[0.0] * W_ATT for _ in range(LANES)]
    ek = [[0.0] * W_ATT for _ in range(LANES)]
    one = 3 * H_FOX
    for h in range(H_FOX):
        for piece in range(3):
            eq[piece * H_FOX + h][HEAD_DIM * h + piece] = 1.0
            eq[one][HEAD_DIM * h + 3 + piece] = 1.0
            ek[one][HEAD_DIM * h + piece] = 1.0
            ek[piece * H_FOX + h][HEAD_DIM * h + 3 + piece] = -1.0
    return jnp.array(eq, jnp.bfloat16), jnp.array(ek, jnp.bfloat16)


def kernel(x, norm1, w_in, forget_bias, rel_bias, w_branch_a, w_branch_b, w_out, norm2, w_up, w_down, final_norm):
    b, s, d = x.shape
    depth = w_in.shape[0]
    bf16 = jnp.bfloat16
    tm = min(512, s)
    tqc = min(1024, s)
    eq, ek = _aug_placement()
    head_of_col = lax.broadcasted_iota(jnp.int32, (W_ATT, LANES), 0) // HEAD_DIM
    hsum = (head_of_col == lax.broadcasted_iota(jnp.int32, (W_ATT, LANES), 1)).astype(bf16)
    gf = final_norm.reshape(1, d)
    o = 3 * W_ATT + H_FOX
    for l in range(depth):
        w = w_in[l]
        wq = (w[:, :W_ATT] * (SCALE * LOG2E)).astype(bf16)
        wk = w[:, W_ATT:2 * W_ATT].astype(bf16)
        wv = w[:, 2 * W_ATT:3 * W_ATT].astype(bf16)
        wf = jnp.pad(w[:, 3 * W_ATT:o], ((0, 0), (0, LANES - H_FOX))).astype(bf16)
        wc = jnp.concatenate([w[:, o:o + W_ATT] * (SCALE * LOG2E), w[:, o + W_ATT:o + 3 * W_ATT]],
                             axis=1).astype(bf16)
        wg = w[:, o + 3 * W_ATT:].astype(bf16)
        bf = jnp.pad(forget_bias[l], (0, LANES - H_FOX)).reshape(1, LANES)
        qf, kf, vf, qc, kc, vc, gates, stat, wo, wu, wd = _in_proj(
            x, norm1[l].reshape(1, d), wq, wk, wv, wc, wg, wf, bf, eq, ek, hsum,
            (w_out[l], w_up[l], w_down[l]), tm)

        stats = stat[:, :, :4, :H_FOX].transpose(2, 0, 3, 1).reshape(4, -1)
        o_a = _fox(tuple(stats), qf, kf, vf, tm)

        far = rel_bias[l][:, 2 * MAX_REL:]
        g_ext = jnp.concatenate([jnp.broadcast_to(far, (H_CHK, PAD - MAX_REL + 1)),
                                 rel_bias[l][:, 2 * MAX_REL - 1:0:-1],
                                 jnp.broadcast_to(far, (H_CHK, Q_BLOCK))], axis=1)
        bias = _band_bias(g_ext.reshape(H_CHK, 1, -1)).reshape(N_PAIR, 2 * Q_BLOCK, BAND)
        o_b = _chunk_attention(qc, kc, vc, bias, tqc)

        x = _out_ffn(x, o_a, o_b, gates, w_branch_a[l].astype(bf16), w_branch_b[l].astype(bf16),
                     wo, norm2[l].reshape(1, d), wu, wd, gf, tm, final=(l == depth - 1))
    return x
```

```python
import functools
import math

import jax
import jax.numpy as jnp
from jax import lax
from jax.experimental import pallas as pl
from jax.experimental.pallas import tpu as pltpu

HEAD_DIM = 64
H_FOX = 8
H_CHK = 8
N_PAIR = 4
W_ATT = H_FOX * HEAD_DIM
CHUNK = 64
Q_BLOCK = 128
LEFT_CHUNKS = 8
PAD = LEFT_CHUNKS * CHUNK
BAND = PAD + Q_BLOCK
MAX_REL = 128
EPS = 1e-6
NEG = -1e30
LANES = 128
SCALE = 1.0 / math.sqrt(HEAD_DIM)
LOG2E = math.log2(math.e)
F32_BIG = 3.0e38
PROJ_SPLIT = 2
FOX_ROWS = 32
FOX_TRIP = 4
FOX_AHEAD = 2
CHUNK_ROWS = 32
CHUNK_GROUP = 8
SKIP_MARGIN = 30.0
NORM_SLACK = 1.02
VMEM_LIMIT = 60 * 1024 * 1024

_NT = (((1,), (1,)), ((), ()))


def _dot(a, b):
    return jnp.dot(a, b, preferred_element_type=jnp.float32)


def _dot_nt(a, b):
    return lax.dot_general(a, b, _NT, preferred_element_type=jnp.float32)


def _rms(x, g):
    ms = jnp.mean(x * x, axis=-1, keepdims=True)
    return x * lax.rsqrt(ms + EPS) * g


def _const_spec(shape):
    nd = len(shape)
    return pl.BlockSpec(shape, lambda *_: (0,) * nd, pipeline_mode=pl.Buffered(1))


def _in_proj_body(x_ref, g_ref, wq_ref, wk_ref, wv_ref, wc_ref, wg_ref, wf_ref, bf_ref, eq_ref, ek_ref, hsum_ref,
                  wo32_ref, wu32_ref, wd32_ref,
                  qf_ref, kf_ref, vf_ref, qc_ref, kc_ref, vc_ref, gate_ref, stat_ref, wo_ref, wu_ref, wd_ref,
                  carry_ref):
    tm = x_ref.shape[1]
    th = tm // PROJ_SPLIT

    wo_ref[...] = wo32_ref[...].astype(wo_ref.dtype)
    wu_ref[...] = wu32_ref[...].astype(wu_ref.dtype)
    wd_ref[...] = wd32_ref[...].astype(wd_ref.dtype)

    @pl.when(pl.program_id(1) == 0)
    def _():
        carry_ref[...] = jnp.zeros_like(carry_ref)

    lane = lax.broadcasted_iota(jnp.int32, (th, LANES), 1)
    row = lax.broadcasted_iota(jnp.int32, (th, LANES), 0)

    def max_norm(t):
        t = t.astype(jnp.float32)
        sq = _dot((t * t).astype(jnp.bfloat16), hsum_ref[...])
        return jnp.max(sq, axis=0, keepdims=True)

    c_first, c_last, q_sq, k_sq = None, None, None, None
    for part in range(PROJ_SPLIT):
        rows = slice(part * th, (part + 1) * th)
        h = _rms(x_ref[0, rows, :], g_ref[...]).astype(jnp.bfloat16)

        logf = jax.nn.log_sigmoid(_dot(h, wf_ref[...]) + bf_ref[...]) * LOG2E
        c = jnp.where(lane < H_FOX, logf, 0.0)
        k = 1
        while k < th:
            c = c + jnp.where(row >= k, pltpu.roll(c, k, 0), 0.0)
            k *= 2
        c = c + carry_ref[...]
        carry_ref[...] = c[th - 1:th, :]
        c_first = c[0:1, :] if part == 0 else c_first
        c_last = c[th - 1:th, :]

        hi = c.astype(jnp.bfloat16).astype(jnp.float32)
        r1 = c - hi
        mid = r1.astype(jnp.bfloat16).astype(jnp.float32)
        lo = (r1 - mid).astype(jnp.bfloat16).astype(jnp.float32)
        pieces = hi + pltpu.roll(mid, H_FOX, 1) + pltpu.roll(lo, 2 * H_FOX, 1)
        pieces = jnp.where(lane == 3 * H_FOX, 1.0, pieces).astype(jnp.bfloat16)
        aug_q = _dot(pieces, eq_ref[...]).astype(jnp.bfloat16)
        aug_k = _dot(pieces, ek_ref[...]).astype(jnp.bfloat16)

        q = _dot(h, wq_ref[...]).astype(jnp.bfloat16)
        kk = _dot(h, wk_ref[...]).astype(jnp.bfloat16)
        for p in range(N_PAIR):
            src = slice(p * LANES, (p + 1) * LANES)
            qf_ref[0, rows, 2 * p * LANES:(2 * p + 1) * LANES] = q[:, src]
            qf_ref[0, rows, (2 * p + 1) * LANES:(2 * p + 2) * LANES] = aug_q[:, src]
            kf_ref[0, rows, 2 * p * LANES:(2 * p + 1) * LANES] = kk[:, src]
            kf_ref[0, rows, (2 * p + 1) * LANES:(2 * p + 2) * LANES] = aug_k[:, src]
        vf_ref[0, rows, :] = _dot(h, wv_ref[...]).astype(jnp.bfloat16)
        q_sq = max_norm(q) if part == 0 else jnp.maximum(q_sq, max_norm(q))
        k_sq = max_norm(kk) if part == 0 else jnp.maximum(k_sq, max_norm(kk))

        pc = _dot(h, wc_ref[...])
        qc_ref[0, rows, :] = pc[:, :W_ATT].astype(jnp.bfloat16)
        kc_ref[0, rows, :] = pc[:, W_ATT:2 * W_ATT].astype(jnp.bfloat16)
        vc_ref[0, rows, :] = pc[:, 2 * W_ATT:].astype(jnp.bfloat16)
        gate_ref[0, rows, :] = jax.nn.sigmoid(_dot(h, wg_ref[...])).astype(jnp.bfloat16)

    sub = lax.broadcasted_iota(jnp.int32, (8, LANES), 0)
    stat = jnp.where(sub == 0, c_first, 0.0)
    stat = jnp.where(sub == 1, c_last, stat)
    stat = jnp.where(sub == 2, jnp.sqrt(q_sq), stat)
    stat_ref[0, 0] = jnp.where(sub == 3, jnp.sqrt(k_sq), stat)


def _in_proj(x, g, wq, wk, wv, wc, wg, wf, bf, eq, ek, hsum, to_cast, layer, tm):
    b, s, d = x.shape
    n_steps = b * (s // tm)
    tok = lambda width: pl.BlockSpec((1, tm, width), lambda i, j: (i, j, 0))
    strip_in = lambda w: pl.BlockSpec((None, w.shape[1] // n_steps, w.shape[2]),
                                      lambda i, j: (layer, i * (s // tm) + j, 0))
    strip_out = lambda w: pl.BlockSpec((w.shape[1] // n_steps, w.shape[2]), lambda i, j: (i * (s // tm) + j, 0))
    bf16 = jnp.bfloat16
    assert all(w.shape[1] % (16 * n_steps) == 0 for w in to_cast)
    out_shape = [jax.ShapeDtypeStruct((b, s, width), bf16)
                 for width in (2 * W_ATT, 2 * W_ATT, W_ATT, W_ATT, W_ATT, W_ATT, 2 * d)]
    out_shape.append(jax.ShapeDtypeStruct((b, s // tm, 8, LANES), jnp.float32))
    out_shape += [jax.ShapeDtypeStruct(w.shape[1:], bf16) for w in to_cast]
    consts = (g, wq, wk, wv, wc, wg, wf, bf, eq, ek, hsum)
    return pl.pallas_call(
        _in_proj_body,
        grid=(b, s // tm),
        in_specs=[tok(d)] + [_const_spec(a.shape) for a in consts] + [strip_in(w) for w in to_cast],
        out_specs=[tok(sh.shape[-1]) for sh in out_shape[:7]]
                  + [pl.BlockSpec((1, 1, 8, LANES), lambda i, j: (i, j, 0, 0))]
                  + [strip_out(w) for w in to_cast],
        out_shape=out_shape,
        scratch_shapes=[pltpu.VMEM((1, LANES), jnp.float32)],
        compiler_params=pltpu.CompilerParams(
            dimension_semantics=("arbitrary", "arbitrary"), vmem_limit_bytes=VMEM_LIMIT),
        name="in_proj",
    )(x, *consts, *to_cast)


def _fox_body(cfirst_ref, clast_ref, qnorm_ref, knorm_ref, q_ref, k_ref, v_ref, o_ref,
              s_ref, p_ref, m_ref, l_ref, acc_ref, bad_ref, *, tq):
    n_tiles = q_ref.shape[1] // tq
    lane2 = lax.broadcasted_iota(jnp.int32, (1, 2 * LANES), 1) % LANES
    head_a = lax.broadcasted_iota(jnp.int32, (1, LANES), 1) < HEAD_DIM
    base = [(pl.program_id(0) * H_FOX + 2 * pl.program_id(1) + head) * n_tiles for head in range(2)]

    def key_rows(j):
        return pl.ds(pl.multiple_of(j * tq, tq), tq)

    def scores(slot, q_head, j):
        s_ref[slot] = _dot_nt(q_head, k_ref[0, key_rows(j), :])

    def softmax(slot, head, mode):
        for r in range(0, tq, FOX_ROWS):
            rows = slice(r, r + FOX_ROWS)
            s = s_ref[slot, rows, :]
            if mode in ("diagonal", "diagonal_self"):
                row = lax.broadcasted_iota(jnp.int32, (FOX_ROWS, tq), 0) + r
                col = lax.broadcasted_iota(jnp.int32, (FOX_ROWS, tq), 1)
                if mode == "diagonal":
                    s = jnp.where(col <= row, s, NEG)
                    m = jnp.max(s, axis=-1, keepdims=True)
                else:
                    own = slice(r // LANES * LANES, (r // LANES + 1) * LANES)
                    lane = lax.broadcasted_iota(jnp.int32, (FOX_ROWS, LANES), 1)
                    sub = lax.broadcasted_iota(jnp.int32, (FOX_ROWS, LANES), 0)
                    m = jnp.sum(jnp.where(lane == sub + r % LANES, s[:, own], 0.0), axis=-1, keepdims=True)
                    s = jnp.where(col <= row, s, NEG)
                m = jnp.broadcast_to(m, (FOX_ROWS, LANES))
                m_ref[head, rows, :] = m
            elif mode == "exact":
                m_old = m_ref[head, rows, :]
                m = jnp.maximum(m_old, jnp.max(s, axis=-1, keepdims=True))
                alpha = jnp.exp2(m_old - m)
                m_ref[head, rows, :] = m
            else:
                m = m_ref[head, rows, :]
            lsum = None
            for g in range(tq // LANES):
                cols = slice(g * LANES, (g + 1) * LANES)
                pg = jnp.exp2(s[:, cols] - m)
                p_ref[slot, rows, cols] = pg.astype(p_ref.dtype)
                lsum = pg if lsum is None else lsum + pg
            if mode in ("diagonal", "diagonal_self"):
                l_ref[head, rows, :] = lsum
            elif mode == "exact":
                l_ref[head, rows, :] = l_ref[head, rows, :] * alpha + lsum
                acc_ref[head, rows, :] = acc_ref[head, rows, :] * alpha
            else:
                l_ref[head, rows, :] = l_ref[head, rows, :] + lsum

    def values(slot, head, j, mode):
        pv = _dot(p_ref[slot], v_ref[0, key_rows(j), :])
        acc_ref[head] = pv if mode in ("diagonal", "diagonal_self") else acc_ref[head] + pv

    def unit_run(q_of, units):
        n_slots = s_ref.shape[0]
        for n in range(min(FOX_AHEAD, len(units))):
            scores(n, q_of(units[n][1]), units[n][0])
        for n, (tile, head, mode) in enumerate(units):
            softmax(n % n_slots, head, mode)
            if n + FOX_AHEAD < len(units):
                scores((n + FOX_AHEAD) % n_slots, q_of(units[n + FOX_AHEAD][1]), units[n + FOX_AHEAD][0])
            values(n % n_slots, head, tile, mode)

    def run_tiles(q_of, make_units, first, total):
        def full(t, carry):
            unit_run(q_of, make_units(first - FOX_TRIP * t, FOX_TRIP))
            return carry

        n_full = total // FOX_TRIP
        lax.fori_loop(0, n_full, full, 0)
        done = n_full * FOX_TRIP
        count = FOX_TRIP // 2
        while count >= 1:
            fits = (total - done) >= count

            @pl.when(fits)
            def _(start=first - done, count=count):
                unit_run(q_of, make_units(start, count))

            done = jnp.where(fits, done + count, done)
            count //= 2

    def query_tile(qi, kmax, exact):
        q_rows = pl.ds(pl.multiple_of(qi * tq, tq), tq)
        qcat = q_ref[0, q_rows, :]
        zero = jnp.zeros_like(qcat)
        qh = (jnp.where(lane2 < HEAD_DIM, qcat, zero), jnp.where(lane2 >= HEAD_DIM, qcat, zero))

        kmax = [jnp.maximum(kmax[head], knorm_ref[base[head] + qi]) for head in range(2)]
        reach = []
        for head in range(2):
            bound = (cfirst_ref[base[head] + qi] + 2.0 * NORM_SLACK * qnorm_ref[base[head] + qi] * kmax[head]
                     + SKIP_MARGIN * LOG2E)

            def reaches(n, head=head, bound=bound):
                t = jnp.maximum(qi - 1 - n, 0)
                return (n < qi) & (bound - clast_ref[base[head] + t] > 0.0)

            reach.append(lax.while_loop(reaches, lambda n: n + 1, 0))
        n_both, n_long = jnp.minimum(reach[0], reach[1]), jnp.maximum(reach[0], reach[1])

        def both_heads(j, count, first_mode="lazy"):
            return [(j - i, head, first_mode if i == 0 else "lazy") for i in range(count) for head in range(2)]

        if exact:
            unit_run(qh.__getitem__, both_heads(qi, 1, "diagonal"))

            def step(t, carry):
                unit_run(qh.__getitem__, both_heads(qi - 1 - t, 1, "exact"))
                return carry

            lax.fori_loop(0, n_long, step, 0)
        else:
            total = n_both + 1
            count, first_len = FOX_TRIP, 1
            while count >= 1:
                fits = total >= count
                if count < FOX_TRIP:
                    fits = fits & (total < 2 * count)

                @pl.when(fits)
                def _(count=count):
                    unit_run(qh.__getitem__, both_heads(qi, count, "diagonal_self"))

                first_len = jnp.where(fits, count, first_len)
                count //= 2
            run_tiles(qh.__getitem__, both_heads, qi - first_len, total - first_len)

            long_head = jnp.where(reach[1] > reach[0], 1, 0)
            q_long = jnp.where((lane2 < HEAD_DIM) == (long_head == 0), qcat, zero)
            run_tiles(lambda head: q_long, lambda j, count: [(j - i, long_head, "lazy") for i in range(count)],
                      qi - total, n_long - n_both)

        outs = []
        for head in range(2):
            acc, lp = acc_ref[head], l_ref[head]
            outs.append(acc / jnp.sum(lp, axis=-1, keepdims=True))
            if not exact:
                finite = (jnp.abs(acc) <= F32_BIG) & (lp <= F32_BIG)
                bad_ref[...] = jnp.maximum(bad_ref[...], jnp.where(finite, 0.0, 1.0))
        o_ref[0, q_rows, :] = jnp.where(head_a, outs[0], outs[1]).astype(o_ref.dtype)
        return kmax

    def sweep(exact):
        def body(qi, kmax):
            return tuple(query_tile(qi, kmax, exact))

        lax.fori_loop(0, n_tiles, body, (jnp.float32(0.0), jnp.float32(0.0)))

    bad_ref[...] = jnp.zeros_like(bad_ref)
    sweep(exact=False)

    @pl.when(jnp.max(bad_ref[...]) > 0.0)
    def _():
        sweep(exact=True)


def _fox(stats, qf, kf, vf, tq):
    b, s, _ = qf.shape
    smem = pl.BlockSpec(memory_space=pltpu.SMEM)
    return pl.pallas_call(
        functools.partial(_fox_body, tq=tq),
        grid=(b, N_PAIR),
        in_specs=[smem, smem, smem, smem,
                  pl.BlockSpec((1, s, 2 * LANES), lambda i, p: (i, 0, p)),
                  pl.BlockSpec((1, s, 2 * LANES), lambda i, p: (i, 0, p)),
                  pl.BlockSpec((1, s, LANES), lambda i, p: (i, 0, p))],
        out_specs=pl.BlockSpec((1, s, LANES), lambda i, p: (i, 0, p)),
        out_shape=jax.ShapeDtypeStruct((b, s, W_ATT), jnp.bfloat16),
        scratch_shapes=[pltpu.VMEM((2 * FOX_AHEAD, tq, tq), jnp.float32),
                        pltpu.VMEM((2 * FOX_AHEAD, tq, tq), jnp.bfloat16),
                        pltpu.VMEM((2, tq, LANES), jnp.float32),
                        pltpu.VMEM((2, tq, LANES), jnp.float32),
                        pltpu.VMEM((2, tq, LANES), jnp.float32),
                        pltpu.VMEM((tq, LANES), jnp.float32)],
        compiler_params=pltpu.CompilerParams(
            dimension_semantics=("arbitrary", "arbitrary"), vmem_limit_bytes=VMEM_LIMIT),
        name="fox_attention",
    )(*stats, qf, kf, vf)


def _band_bias_body(g_ref, o_ref):
    width = g_ref.shape[-1]
    base = jnp.broadcast_to(g_ref[0], (Q_BLOCK, width))
    toeplitz = pltpu.roll(base, 0, 1, stride=1, stride_axis=0)[:, :BAND]
    qrow = lax.broadcasted_iota(jnp.int32, (Q_BLOCK, BAND), 0)
    kcol = lax.broadcasted_iota(jnp.int32, (Q_BLOCK, BAND), 1)
    cq = qrow // CHUNK
    ck = kcol // CHUNK - LEFT_CHUNKS
    valid = (ck <= cq) & (ck >= cq - LEFT_CHUNKS)
    o_ref[0] = jnp.where(valid, toeplitz * LOG2E, NEG)


def _band_bias(g_ext):
    h, _, width = g_ext.shape
    return pl.pallas_call(
        _band_bias_body,
        grid=(h,),
        in_specs=[pl.BlockSpec((1, 1, width), lambda i: (i, 0, 0))],
        out_specs=pl.BlockSpec((1, Q_BLOCK, BAND), lambda i: (i, 0, 0)),
        out_shape=jax.ShapeDtypeStruct((h, Q_BLOCK, BAND), jnp.float32),
        name="band_bias",
    )(g_ext)


def _chunk_body(q_ref, k_ref, v_ref, bias_ref, o_ref, kpad_ref, vpad_ref, s_ref, p_ref, l_ref, *, tq):
    qi = pl.program_id(2)
    s_len = k_ref.shape[1]

    @pl.when(qi == 0)
    def _():
        zeros = jnp.zeros((PAD, LANES), kpad_ref.dtype)
        kpad_ref[:PAD, :] = zeros
        vpad_ref[:PAD, :] = zeros
        kpad_ref[PAD:PAD + s_len, :] = k_ref[0]
        vpad_ref[PAD:PAD + s_len, :] = v_ref[0]

    head_a = lax.broadcasted_iota(jnp.int32, (1, LANES), 1) < HEAD_DIM
    kcol = lax.broadcasted_iota(jnp.int32, (1, BAND), 1)

    def offsets(blk):
        r0 = pl.multiple_of(blk * Q_BLOCK, Q_BLOCK)
        return r0, pl.multiple_of(qi * tq + r0, Q_BLOCK)

    def scores(slot, blk):
        r0, p0 = offsets(blk)
        q2 = q_ref[0, pl.ds(r0, Q_BLOCK), :]
        zero = jnp.zeros_like(q2)
        qs = jnp.concatenate([jnp.where(head_a, q2, zero), jnp.where(head_a, zero, q2)], axis=0)
        s_ref[slot] = _dot_nt(qs, kpad_ref[pl.ds(p0, BAND), :])

    def softmax(slot, blk, masked):
        _, p0 = offsets(blk)
        for r in range(0, 2 * Q_BLOCK, CHUNK_ROWS):
            rows = slice(r, r + CHUNK_ROWS)
            s = s_ref[slot, rows, :] + bias_ref[0, rows, :]
            if masked:
                s = jnp.where(kcol + p0 >= PAD, s, NEG)
            p = jnp.exp2(s - jnp.max(s, axis=-1, keepdims=True))
            l_ref[slot, rows, :] = jnp.broadcast_to(jnp.sum(p, axis=-1, keepdims=True), (CHUNK_ROWS, LANES))
            p_ref[slot, rows, :] = p.astype(p_ref.dtype)

    def values(slot, blk):
        r0, p0 = offsets(blk)
        o = _dot(p_ref[slot], vpad_ref[pl.ds(p0, BAND), :]) / l_ref[slot]
        o_ref[0, pl.ds(r0, Q_BLOCK), :] = jnp.where(head_a, o[:Q_BLOCK], o[Q_BLOCK:]).astype(o_ref.dtype)

    def group(i, masked):
        first = i * CHUNK_GROUP
        scores(0, first)
        for n in range(CHUNK_GROUP):
            if n + 1 < CHUNK_GROUP:
                scores((n + 1) % 2, first + n + 1)
            softmax(n % 2, first + n, masked)
            values(n % 2, first + n)

    def masked_group(i, carry):
        group(i, True)
        return carry

    def plain_group(i, carry):
        group(i, False)
        return carry

    group_rows = Q_BLOCK * CHUNK_GROUP
    n_groups = tq // group_rows
    n_masked = jnp.clip((PAD - qi * tq + group_rows - 1) // group_rows, 0, n_groups)
    lax.fori_loop(0, n_masked, masked_group, 0)
    lax.fori_loop(n_masked, n_groups, plain_group, 0)


def _chunk_attention(qc, kc, vc, bias, tq):
    b, s, _ = qc.shape
    assert tq % (CHUNK_GROUP * Q_BLOCK) == 0
    return pl.pallas_call(
        functools.partial(_chunk_body, tq=tq),
        grid=(b, N_PAIR, s // tq),
        in_specs=[pl.BlockSpec((1, tq, LANES), lambda i, p, j: (i, j, p)),
                  pl.BlockSpec((1, s, LANES), lambda i, p, j: (i, 0, p)),
                  pl.BlockSpec((1, s, LANES), lambda i, p, j: (i, 0, p)),
                  pl.BlockSpec((1, 2 * Q_BLOCK, BAND), lambda i, p, j: (p, 0, 0))],
        out_specs=pl.BlockSpec((1, tq, LANES), lambda i, p, j: (i, j, p)),
        out_shape=jax.ShapeDtypeStruct((b, s, W_ATT), jnp.bfloat16),
        scratch_shapes=[pltpu.VMEM((PAD + s, LANES), jnp.bfloat16),
                        pltpu.VMEM((PAD + s, LANES), jnp.bfloat16),
                        pltpu.VMEM((2, 2 * Q_BLOCK, BAND), jnp.float32),
                        pltpu.VMEM((2, 2 * Q_BLOCK, BAND), jnp.bfloat16),
                        pltpu.VMEM((2, 2 * Q_BLOCK, LANES), jnp.float32)],
        compiler_params=pltpu.CompilerParams(
            dimension_semantics=("arbitrary", "arbitrary", "arbitrary"), vmem_limit_bytes=VMEM_LIMIT),
        name="chunk_attention",
    )(qc, kc, vc, bias)


def _out_ffn_body(x_ref, oa_ref, ob_ref, gate_ref, wa_ref, wb_ref, wo_ref, g2_ref, wu_ref, wd_ref, gf_ref,
                  o_ref, *, ff_chunk, final):
    d = x_ref.shape[-1]
    ya = _dot(oa_ref[0], wa_ref[...])
    yb = _dot(ob_ref[0], wb_ref[...])
    ga = gate_ref[0, :, :d].astype(jnp.float32)
    gb = gate_ref[0, :, d:].astype(jnp.float32)
    merged = (ga * ya + gb * yb).astype(jnp.bfloat16)
    x1 = x_ref[0] + _dot(merged, wo_ref[...])
    h2 = _rms(x1, g2_ref[...]).astype(jnp.bfloat16)
    acc = x1
    for c0 in range(0, wu_ref.shape[1], ff_chunk):
        u = jnp.maximum(_dot(h2, wu_ref[:, c0:c0 + ff_chunk]), 0.0)
        acc = acc + _dot((u * u).astype(jnp.bfloat16), wd_ref[c0:c0 + ff_chunk, :])
    if final:
        acc = _rms(acc, gf_ref[...])
    o_ref[0] = acc


def _out_ffn(x, oa, ob, gates, wa, wb, wo, g2, wu, wd, gf, tm, final):
    b, s, d = x.shape
    tok = lambda w: pl.BlockSpec((1, tm, w), lambda i, j: (i, j, 0))
    return pl.pallas_call(
        functools.partial(_out_ffn_body, ff_chunk=min(1024, wu.shape[1]), final=final),
        grid=(b, s // tm),
        in_specs=[tok(d), tok(W_ATT), tok(W_ATT), tok(2 * d)]
                 + [_const_spec(a.shape) for a in (wa, wb, wo, g2, wu, wd, gf)],
        out_specs=tok(d),
        out_shape=jax.ShapeDtypeStruct((b, s, d), jnp.float32),
        compiler_params=pltpu.CompilerParams(
            dimension_semantics=("arbitrary", "arbitrary"), vmem_limit_bytes=VMEM_LIMIT),
        name="out_ffn",
    )(x, oa, ob, gates, wa, wb, wo, g2, wu, wd, gf)


def _aug_placement():
    eq = [[0.0] * W_ATT for _ in range(LANES)]
    ek = [[0.0] * W_ATT for _ in range(LANES)]
    one = 3 * H_FOX
    for h in range(H_FOX):
        for piece in range(3):
            eq[piece * H_FOX + h][HEAD_DIM * h + piece] = 1.0
            eq[one][HEAD_DIM * h + 3 + piece] = 1.0
            ek[one][HEAD_DIM * h + piece] = 1.0
            ek[piece * H_FOX + h][HEAD_DIM * h + 3 + piece] = -1.0
    return jnp.array(eq, jnp.bfloat16), jnp.array(ek, jnp.bfloat16)


def kernel(x, norm1, w_in, forget_bias, rel_bias, w_branch_a, w_branch_b, w_out, norm2, w_up, w_down, final_norm):
    b, s, d = x.shape
    depth = w_in.shape[0]
    bf16 = jnp.bfloat16
    tm = min(512, s)
    tqc = min(1024, s)
    eq, ek = _aug_placement()
    head_of_col = lax.broadcasted_iota(jnp.int32, (W_ATT, LANES), 0) // HEAD_DIM
    hsum = (head_of_col == lax.broadcasted_iota(jnp.int32, (W_ATT, LANES), 1)).astype(bf16)
    gf = final_norm.reshape(1, d)
    o = 3 * W_ATT + H_FOX
    for l in range(depth):
        w = w_in[l]
        wq = (w[:, :W_ATT] * (SCALE * LOG2E)).astype(bf16)
        wk = w[:, W_ATT:2 * W_ATT].astype(bf16)
        wv = w[:, 2 * W_ATT:3 * W_ATT].astype(bf16)
        wf = jnp.pad(w[:, 3 * W_ATT:o], ((0, 0), (0, LANES - H_FOX))).astype(bf16)
        wc = jnp.concatenate([w[:, o:o + W_ATT] * (SCALE * LOG2E), w[:, o + W_ATT:o + 3 * W_ATT]],
                             axis=1).astype(bf16)
        wg = w[:, o + 3 * W_ATT:].astype(bf16)
        bf = jnp.pad(forget_bias[l], (0, LANES - H_FOX)).reshape(1, LANES)
        qf, kf, vf, qc, kc, vc, gates, stat, wo, wu, wd = _in_proj(
            x, norm1[l].reshape(1, d), wq, wk, wv, wc, wg, wf, bf, eq, ek, hsum,
            (w_out, w_up, w_down), l, tm)

        stats = stat[:, :, :4, :H_FOX].transpose(2, 0, 3, 1).reshape(4, -1)
        o_a = _fox(tuple(stats), qf, kf, vf, tm)

        far = rel_bias[l][:, 2 * MAX_REL:]
        g_ext = jnp.concatenate([jnp.broadcast_to(far, (H_CHK, PAD - MAX_REL + 1)),
                                 rel_bias[l][:, 2 * MAX_REL - 1:0:-1],
                                 jnp.broadcast_to(far, (H_CHK, Q_BLOCK))], axis=1)
        bias = _band_bias(g_ext.reshape(H_CHK, 1, -1)).reshape(N_PAIR, 2 * Q_BLOCK, BAND)
        o_b = _chunk_attention(qc, kc, vc, bias, tqc)

        x = _out_ffn(x, o_a, o_b, gates, w_branch_a[l].astype(bf16), w_branch_b[l].astype(bf16),
                     wo, norm2[l].reshape(1, d), wu, wd, gf, tm, final=(l == depth - 1))
    return x
```

```python
import functools
import math

import jax
import jax.numpy as jnp
from jax import lax
from jax.experimental import pallas as pl
from jax.experimental.pallas import tpu as pltpu

HEAD_DIM = 64
H_FOX = 8
H_CHK = 8
N_PAIR = 4
W_ATT = H_FOX * HEAD_DIM
CHUNK = 64
Q_BLOCK = 128
LEFT_CHUNKS = 8
PAD = LEFT_CHUNKS * CHUNK
BAND = PAD + Q_BLOCK
MAX_REL = 128
EPS = 1e-6
NEG = -1e30
LANES = 128
SCALE = 1.0 / math.sqrt(HEAD_DIM)
LOG2E = math.log2(math.e)
F32_BIG = 3.0e38
PROJ_SPLIT = 2
FOX_ROWS = 32
FOX_TRIP = 4
FOX_AHEAD = 2
CHUNK_ROWS = 32
CHUNK_GROUP = 8
SKIP_MARGIN = 30.0
NORM_SLACK = 1.02
VMEM_LIMIT = 60 * 1024 * 1024

_NT = (((1,), (1,)), ((), ()))


def _dot(a, b):
    return jnp.dot(a, b, preferred_element_type=jnp.float32)


def _dot_nt(a, b):
    return lax.dot_general(a, b, _NT, preferred_element_type=jnp.float32)


def _rms(x, g):
    ms = jnp.mean(x * x, axis=-1, keepdims=True)
    return x * lax.rsqrt(ms + EPS) * g


def _const_spec(shape):
    nd = len(shape)
    return pl.BlockSpec(shape, lambda *_: (0,) * nd, pipeline_mode=pl.Buffered(1))


def _in_proj_body(x_ref, g_ref, wq_ref, wk_ref, wv_ref, wc_ref, wg_ref, wf_ref, bf_ref, eq_ref, ek_ref, hsum_ref,
                  wo32_ref, wu32_ref, wd32_ref,
                  qf_ref, kf_ref, vf_ref, qc_ref, kc_ref, vc_ref, gate_ref, stat_ref, wo_ref, wu_ref, wd_ref,
                  carry_ref):
    tm = x_ref.shape[1]
    th = tm // PROJ_SPLIT

    wo_ref[...] = wo32_ref[...].astype(wo_ref.dtype)
    wu_ref[...] = wu32_ref[...].astype(wu_ref.dtype)
    wd_ref[...] = wd32_ref[...].astype(wd_ref.dtype)

    @pl.when(pl.program_id(1) == 0)
    def _():
        carry_ref[...] = jnp.zeros_like(carry_ref)

    lane = lax.broadcasted_iota(jnp.int32, (th, LANES), 1)
    row = lax.broadcasted_iota(jnp.int32, (th, LANES), 0)

    def max_norm(t):
        t = t.astype(jnp.float32)
        sq = _dot((t * t).astype(jnp.bfloat16), hsum_ref[...])
        return jnp.max(sq, axis=0, keepdims=True)

    c_first, c_last, q_sq, k_sq = None, None, None, None
    for part in range(PROJ_SPLIT):
        rows = slice(part * th, (part + 1) * th)
        h = _rms(x_ref[0, rows, :], g_ref[...]).astype(jnp.bfloat16)

        logf = jax.nn.log_sigmoid(_dot(h, wf_ref[...]) + bf_ref[...]) * LOG2E
        c = jnp.where(lane < H_FOX, logf, 0.0)
        k = 1
        while k < th:
            c = c + jnp.where(row >= k, pltpu.roll(c, k, 0), 0.0)
            k *= 2
        c = c + carry_ref[...]
        carry_ref[...] = c[th - 1:th, :]
        c_first = c[0:1, :] if part == 0 else c_first
        c_last = c[th - 1:th, :]

        hi = c.astype(jnp.bfloat16).astype(jnp.float32)
        r1 = c - hi
        mid = r1.astype(jnp.bfloat16).astype(jnp.float32)
        lo = (r1 - mid).astype(jnp.bfloat16).astype(jnp.float32)
        pieces = hi + pltpu.roll(mid, H_FOX, 1) + pltpu.roll(lo, 2 * H_FOX, 1)
        pieces = jnp.where(lane == 3 * H_FOX, 1.0, pieces).astype(jnp.bfloat16)
        aug_q = _dot(pieces, eq_ref[...]).astype(jnp.bfloat16)
        aug_k = _dot(pieces, ek_ref[...]).astype(jnp.bfloat16)

        q = _dot(h, wq_ref[...]).astype(jnp.bfloat16)
        kk = _dot(h, wk_ref[...]).astype(jnp.bfloat16)
        for p in range(N_PAIR):
            src = slice(p * LANES, (p + 1) * LANES)
            qf_ref[0, rows, 2 * p * LANES:(2 * p + 1) * LANES] = q[:, src]
            qf_ref[0, rows, (2 * p + 1) * LANES:(2 * p + 2) * LANES] = aug_q[:, src]
            kf_ref[0, rows, 2 * p * LANES:(2 * p + 1) * LANES] = kk[:, src]
            kf_ref[0, rows, (2 * p + 1) * LANES:(2 * p + 2) * LANES] = aug_k[:, src]
        vf_ref[0, rows, :] = _dot(h, wv_ref[...]).astype(jnp.bfloat16)
        q_sq = max_norm(q) if part == 0 else jnp.maximum(q_sq, max_norm(q))
        k_sq = max_norm(kk) if part == 0 else jnp.maximum(k_sq, max_norm(kk))

        pc = _dot(h, wc_ref[...])
        qc_ref[0, rows, :] = pc[:, :W_ATT].astype(jnp.bfloat16)
        kc_ref[0, rows, :] = pc[:, W_ATT:2 * W_ATT].astype(jnp.bfloat16)
        vc_ref[0, rows, :] = pc[:, 2 * W_ATT:].astype(jnp.bfloat16)
        gate_ref[0, rows, :] = jax.nn.sigmoid(_dot(h, wg_ref[...])).astype(jnp.bfloat16)

    sub = lax.broadcasted_iota(jnp.int32, (8, LANES), 0)
    stat = jnp.where(sub == 0, c_first, 0.0)
    stat = jnp.where(sub == 1, c_last, stat)
    stat = jnp.where(sub == 2, jnp.sqrt(q_sq), stat)
    stat_ref[0, 0] = jnp.where(sub == 3, jnp.sqrt(k_sq), stat)


def _in_proj(x, g, wq, wk, wv, wc, wg, wf, bf, eq, ek, hsum, to_cast, layer, tm):
    b, s, d = x.shape
    n_steps = b * (s // tm)
    tok = lambda width: pl.BlockSpec((1, tm, width), lambda i, j: (i, j, 0))
    strip_in = lambda w: pl.BlockSpec((None, w.shape[1] // n_steps, w.shape[2]),
                                      lambda i, j: (layer, i * (s // tm) + j, 0))
    strip_out = lambda w: pl.BlockSpec((w.shape[1] // n_steps, w.shape[2]), lambda i, j: (i * (s // tm) + j, 0))
    bf16 = jnp.bfloat16
    assert all(w.shape[1] % (16 * n_steps) == 0 for w in to_cast)
    out_shape = [jax.ShapeDtypeStruct((b, s, width), bf16)
                 for width in (2 * W_ATT, 2 * W_ATT, W_ATT, W_ATT, W_ATT, W_ATT, 2 * d)]
    out_shape.append(jax.ShapeDtypeStruct((b, s // tm, 8, LANES), jnp.float32))
    out_shape += [jax.ShapeDtypeStruct(w.shape[1:], bf16) for w in to_cast]
    consts = (g, wq, wk, wv, wc, wg, wf, bf, eq, ek, hsum)
    return pl.pallas_call(
        _in_proj_body,
        grid=(b, s // tm),
        in_specs=[tok(d)] + [_const_spec(a.shape) for a in consts] + [strip_in(w) for w in to_cast],
        out_specs=[tok(sh.shape[-1]) for sh in out_shape[:7]]
                  + [pl.BlockSpec((1, 1, 8, LANES), lambda i, j: (i, j, 0, 0))]
                  + [strip_out(w) for w in to_cast],
        out_shape=out_shape,
        scratch_shapes=[pltpu.VMEM((1, LANES), jnp.float32)],
        compiler_params=pltpu.CompilerParams(
            dimension_semantics=("arbitrary", "arbitrary"), vmem_limit_bytes=VMEM_LIMIT),
        name="in_proj",
    )(x, *consts, *to_cast)


def _fox_body(cfirst_ref, clast_ref, qnorm_ref, knorm_ref, q_ref, k_ref, v_ref, o_ref,
              s_ref, p_ref, m_ref, l_ref, acc_ref, bad_ref, *, tq):
    n_tiles = q_ref.shape[1] // tq
    lane2 = lax.broadcasted_iota(jnp.int32, (1, 2 * LANES), 1) % LANES
    head_a = lax.broadcasted_iota(jnp.int32, (1, LANES), 1) < HEAD_DIM
    base = [(pl.program_id(0) * H_FOX + 2 * pl.program_id(1) + head) * n_tiles for head in range(2)]

    def key_rows(j):
        return pl.ds(pl.multiple_of(j * tq, tq), tq)

    def scores(slot, q_head, j):
        s_ref[slot] = _dot_nt(q_head, k_ref[0, key_rows(j), :])

    def softmax(slot, head, mode):
        for r in range(0, tq, FOX_ROWS):
            rows = slice(r, r + FOX_ROWS)
            s = s_ref[slot, rows, :]
            if mode in ("diagonal", "diagonal_self"):
                row = lax.broadcasted_iota(jnp.int32, (FOX_ROWS, tq), 0) + r
                col = lax.broadcasted_iota(jnp.int32, (FOX_ROWS, tq), 1)
                if mode == "diagonal":
                    s = jnp.where(col <= row, s, NEG)
                    m = jnp.max(s, axis=-1, keepdims=True)
                else:
                    own = slice(r // LANES * LANES, (r // LANES + 1) * LANES)
                    lane = lax.broadcasted_iota(jnp.int32, (FOX_ROWS, LANES), 1)
                    sub = lax.broadcasted_iota(jnp.int32, (FOX_ROWS, LANES), 0)
                    m = jnp.sum(jnp.where(lane == sub + r % LANES, s[:, own], 0.0), axis=-1, keepdims=True)
                    s = jnp.where(col <= row, s, NEG)
                m = jnp.broadcast_to(m, (FOX_ROWS, LANES))
                m_ref[head, rows, :] = m
            elif mode == "exact":
                m_old = m_ref[head, rows, :]
                m = jnp.maximum(m_old, jnp.max(s, axis=-1, keepdims=True))
                alpha = jnp.exp2(m_old - m)
                m_ref[head, rows, :] = m
            else:
                m = m_ref[head, rows, :]
            lsum = None
            for g in range(tq // LANES):
                cols = slice(g * LANES, (g + 1) * LANES)
                pg = jnp.exp2(s[:, cols] - m)
                p_ref[slot, rows, cols] = pg.astype(p_ref.dtype)
                lsum = pg if lsum is None else lsum + pg
            if mode in ("diagonal", "diagonal_self"):
                l_ref[head, rows, :] = lsum
            elif mode == "exact":
                l_ref[head, rows, :] = l_ref[head, rows, :] * alpha + lsum
                acc_ref[head, rows, :] = acc_ref[head, rows, :] * alpha
            else:
                l_ref[head, rows, :] = l_ref[head, rows, :] + lsum

    def values(slot, head, j, mode):
        pv = _dot(p_ref[slot], v_ref[0, key_rows(j), :])
        acc_ref[head] = pv if mode in ("diagonal", "diagonal_self") else acc_ref[head] + pv

    def unit_run(q_of, units):
        n_slots = s_ref.shape[0]
        for n in range(min(FOX_AHEAD, len(units))):
            scores(n, q_of(units[n][1]), units[n][0])
        for n, (tile, head, mode) in enumerate(units):
            softmax(n % n_slots, head, mode)
            if n + FOX_AHEAD < len(units):
                scores((n + FOX_AHEAD) % n_slots, q_of(units[n + FOX_AHEAD][1]), units[n + FOX_AHEAD][0])
            values(n % n_slots, head, tile, mode)

    def run_tiles(q_of, make_units, first, total):
        def full(t, carry):
            unit_run(q_of, make_units(first - FOX_TRIP * t, FOX_TRIP))
            return carry

        n_full = total // FOX_TRIP
        lax.fori_loop(0, n_full, full, 0)
        done = n_full * FOX_TRIP
        count = FOX_TRIP // 2
        while count >= 1:
            fits = (total - done) >= count

            @pl.when(fits)
            def _(start=first - done, count=count):
                unit_run(q_of, make_units(start, count))

            done = jnp.where(fits, done + count, done)
            count //= 2

    def query_tile(qi, kmax, exact):
        q_rows = pl.ds(pl.multiple_of(qi * tq, tq), tq)
        qcat = q_ref[0, q_rows, :]
        zero = jnp.zeros_like(qcat)
        qh = (jnp.where(lane2 < HEAD_DIM, qcat, zero), jnp.where(lane2 >= HEAD_DIM, qcat, zero))

        kmax = [jnp.maximum(kmax[head], knorm_ref[base[head] + qi]) for head in range(2)]
        reach = []
        for head in range(2):
            bound = (cfirst_ref[base[head] + qi] + 2.0 * NORM_SLACK * qnorm_ref[base[head] + qi] * kmax[head]
                     + SKIP_MARGIN * LOG2E)

            def reaches(n, head=head, bound=bound):
                t = jnp.maximum(qi - 1 - n, 0)
                return (n < qi) & (bound - clast_ref[base[head] + t] > 0.0)

            reach.append(lax.while_loop(reaches, lambda n: n + 1, 0))
        n_both, n_long = jnp.minimum(reach[0], reach[1]), jnp.maximum(reach[0], reach[1])

        def both_heads(j, count, first_mode="lazy"):
            return [(j - i, head, first_mode if i == 0 else "lazy") for i in range(count) for head in range(2)]

        if exact:
            unit_run(qh.__getitem__, both_heads(qi, 1, "diagonal"))

            def step(t, carry):
                unit_run(qh.__getitem__, both_heads(qi - 1 - t, 1, "exact"))
                return carry

            lax.fori_loop(0, n_long, step, 0)
        else:
            total = n_both + 1
            count, first_len = FOX_TRIP, 1
            while count >= 1:
                fits = total >= count
                if count < FOX_TRIP:
                    fits = fits & (total < 2 * count)

                @pl.when(fits)
                def _(count=count):
                    unit_run(qh.__getitem__, both_heads(qi, count, "diagonal_self"))

                first_len = jnp.where(fits, count, first_len)
                count //= 2
            run_tiles(qh.__getitem__, both_heads, qi - first_len, total - first_len)

            long_head = jnp.where(reach[1] > reach[0], 1, 0)
            q_long = jnp.where((lane2 < HEAD_DIM) == (long_head == 0), qcat, zero)
            run_tiles(lambda head: q_long, lambda j, count: [(j - i, long_head, "lazy") for i in range(count)],
                      qi - total, n_long - n_both)

        outs = []
        for head in range(2):
            acc, lp = acc_ref[head], l_ref[head]
            outs.append(acc / jnp.sum(lp, axis=-1, keepdims=True))
            if not exact:
                finite = (jnp.abs(acc) <= F32_BIG) & (lp <= F32_BIG)
                bad_ref[...] = jnp.maximum(bad_ref[...], jnp.where(finite, 0.0, 1.0))
        o_ref[0, q_rows, :] = jnp.where(head_a, outs[0], outs[1]).astype(o_ref.dtype)
        return kmax

    def sweep(exact):
        def body(qi, kmax):
            return tuple(query_tile(qi, kmax, exact))

        lax.fori_loop(0, n_tiles, body, (jnp.float32(0.0), jnp.float32(0.0)))

    bad_ref[...] = jnp.zeros_like(bad_ref)
    sweep(exact=False)

    @pl.when(jnp.max(bad_ref[...]) > 0.0)
    def _():
        sweep(exact=True)


def _fox(stats, qf, kf, vf, tq):
    b, s, _ = qf.shape
    smem = pl.BlockSpec(memory_space=pltpu.SMEM)
    return pl.pallas_call(
        functools.partial(_fox_body, tq=tq),
        grid=(b, N_PAIR),
        in_specs=[smem, smem, smem, smem,
                  pl.BlockSpec((1, s, 2 * LANES), lambda i, p: (i, 0, p)),
                  pl.BlockSpec((1, s, 2 * LANES), lambda i, p: (i, 0, p)),
                  pl.BlockSpec((1, s, LANES), lambda i, p: (i, 0, p))],
        out_specs=pl.BlockSpec((1, s, LANES), lambda i, p: (i, 0, p)),
        out_shape=jax.ShapeDtypeStruct((b, s, W_ATT), jnp.bfloat16),
        scratch_shapes=[pltpu.VMEM((2 * FOX_AHEAD, tq, tq), jnp.float32),
                        pltpu.VMEM((2 * FOX_AHEAD, tq, tq), jnp.bfloat16),
                        pltpu.VMEM((2, tq, LANES), jnp.float32),
                        pltpu.VMEM((2, tq, LANES), jnp.float32),
                        pltpu.VMEM((2, tq, LANES), jnp.float32),
                        pltpu.VMEM((tq, LANES), jnp.float32)],
        compiler_params=pltpu.CompilerParams(
            dimension_semantics=("arbitrary", "arbitrary"), vmem_limit_bytes=VMEM_LIMIT),
        name="fox_attention",
    )(*stats, qf, kf, vf)


def _band_bias_body(g_ref, o_ref):
    width = g_ref.shape[-1]
    base = jnp.broadcast_to(g_ref[0], (Q_BLOCK, width))
    toeplitz = pltpu.roll(base, 0, 1, stride=1, stride_axis=0)[:, :BAND]
    qrow = lax.broadcasted_iota(jnp.int32, (Q_BLOCK, BAND), 0)
    kcol = lax.broadcasted_iota(jnp.int32, (Q_BLOCK, BAND), 1)
    cq = qrow // CHUNK
    ck = kcol // CHUNK - LEFT_CHUNKS
    valid = (ck <= cq) & (ck >= cq - LEFT_CHUNKS)
    o_ref[0] = jnp.where(valid, toeplitz * LOG2E, NEG)


def _band_bias(g_ext):
    h, _, width = g_ext.shape
    return pl.pallas_call(
        _band_bias_body,
        grid=(h,),
        in_specs=[pl.BlockSpec((1, 1, width), lambda i: (i, 0, 0))],
        out_specs=pl.BlockSpec((1, Q_BLOCK, BAND), lambda i: (i, 0, 0)),
        out_shape=jax.ShapeDtypeStruct((h, Q_BLOCK, BAND), jnp.float32),
        name="band_bias",
    )(g_ext)


def _chunk_body(q_ref, k_ref, v_ref, bias_ref, o_ref, kpad_ref, vpad_ref, s_ref, p_ref, l_ref, *, tq):
    qi = pl.program_id(2)
    s_len = k_ref.shape[1]

    @pl.when(qi == 0)
    def _():
        zeros = jnp.zeros((PAD, LANES), kpad_ref.dtype)
        kpad_ref[:PAD, :] = zeros
        vpad_ref[:PAD, :] = zeros
        kpad_ref[PAD:PAD + s_len, :] = k_ref[0]
        vpad_ref[PAD:PAD + s_len, :] = v_ref[0]

    head_a = lax.broadcasted_iota(jnp.int32, (1, LANES), 1) < HEAD_DIM
    kcol = lax.broadcasted_iota(jnp.int32, (1, BAND), 1)

    def offsets(blk):
        r0 = pl.multiple_of(blk * Q_BLOCK, Q_BLOCK)
        return r0, pl.multiple_of(qi * tq + r0, Q_BLOCK)

    def scores(slot, blk):
        r0, p0 = offsets(blk)
        q2 = q_ref[0, pl.ds(r0, Q_BLOCK), :]
        zero = jnp.zeros_like(q2)
        qs = jnp.concatenate([jnp.where(head_a, q2, zero), jnp.where(head_a, zero, q2)], axis=0)
        s_ref[slot] = _dot_nt(qs, kpad_ref[pl.ds(p0, BAND), :])

    def softmax(slot, blk, masked):
        _, p0 = offsets(blk)
        for r in range(0, 2 * Q_BLOCK, CHUNK_ROWS):
            rows = slice(r, r + CHUNK_ROWS)
            s = s_ref[slot, rows, :] + bias_ref[0, rows, :]
            if masked:
                s = jnp.where(kcol + p0 >= PAD, s, NEG)
            p = jnp.exp2(s - jnp.max(s, axis=-1, keepdims=True))
            l_ref[slot, rows, :] = jnp.broadcast_to(jnp.sum(p, axis=-1, keepdims=True), (CHUNK_ROWS, LANES))
            p_ref[slot, rows, :] = p.astype(p_ref.dtype)

    def values(slot, blk):
        r0, p0 = offsets(blk)
        o = _dot(p_ref[slot], vpad_ref[pl.ds(p0, BAND), :]) / l_ref[slot]
        o_ref[0, pl.ds(r0, Q_BLOCK), :] = jnp.where(head_a, o[:Q_BLOCK], o[Q_BLOCK:]).astype(o_ref.dtype)

    def group(i, masked):
        first = i * CHUNK_GROUP
        scores(0, first)
        for n in range(CHUNK_GROUP):
            if n + 1 < CHUNK_GROUP:
                scores((n + 1) % 2, first + n + 1)
            softmax(n % 2, first + n, masked)
            values(n % 2, first + n)

    def masked_group(i, carry):
        group(i, True)
        return carry

    def plain_group(i, carry):
        group(i, False)
        return carry

    group_rows = Q_BLOCK * CHUNK_GROUP
    n_groups = tq // group_rows
    n_masked = jnp.clip((PAD - qi * tq + group_rows - 1) // group_rows, 0, n_groups)
    lax.fori_loop(0, n_masked, masked_group, 0)
    lax.fori_loop(n_masked, n_groups, plain_group, 0)


def _chunk_attention(qc, kc, vc, bias, tq):
    b, s, _ = qc.shape
    assert tq % (CHUNK_GROUP * Q_BLOCK) == 0
    return pl.pallas_call(
        functools.partial(_chunk_body, tq=tq),
        grid=(b, N_PAIR, s // tq),
        in_specs=[pl.BlockSpec((1, tq, LANES), lambda i, p, j: (i, j, p)),
                  pl.BlockSpec((1, s, LANES), lambda i, p, j: (i, 0, p)),
                  pl.BlockSpec((1, s, LANES), lambda i, p, j: (i, 0, p)),
                  pl.BlockSpec((1, 2 * Q_BLOCK, BAND), lambda i, p, j: (p, 0, 0))],
        out_specs=pl.BlockSpec((1, tq, LANES), lambda i, p, j: (i, j, p)),
        out_shape=jax.ShapeDtypeStruct((b, s, W_ATT), jnp.bfloat16),
        scratch_shapes=[pltpu.VMEM((PAD + s, LANES), jnp.bfloat16),
                        pltpu.VMEM((PAD + s, LANES), jnp.bfloat16),
                        pltpu.VMEM((2, 2 * Q_BLOCK, BAND), jnp.float32),
                        pltpu.VMEM((2, 2 * Q_BLOCK, BAND), jnp.bfloat16),
                        pltpu.VMEM((2, 2 * Q_BLOCK, LANES), jnp.float32)],
        compiler_params=pltpu.CompilerParams(
            dimension_semantics=("arbitrary", "arbitrary", "arbitrary"), vmem_limit_bytes=VMEM_LIMIT),
        name="chunk_attention",
    )(qc, kc, vc, bias)


def _out_ffn_body(x_ref, oa_ref, ob_ref, gate_ref, wa_ref, wb_ref, wo_ref, g2_ref, wu_ref, wd_ref, gf_ref,
                  o_ref, *, ff_chunk, final):
    d = x_ref.shape[-1]
    ya = _dot(oa_ref[0], wa_ref[...])
    yb = _dot(ob_ref[0], wb_ref[...])
    ga = gate_ref[0, :, :d].astype(jnp.float32)
    gb = gate_ref[0, :, d:].astype(jnp.float32)
    merged = (ga * ya + gb * yb).astype(jnp.bfloat16)
    x1 = x_ref[0] + _dot(merged, wo_ref[...])
    h2 = _rms(x1, g2_ref[...]).astype(jnp.bfloat16)
    acc = x1
    for c0 in range(0, wu_ref.shape[1], ff_chunk):
        u = jnp.maximum(_dot(h2, wu_ref[:, c0:c0 + ff_chunk]), 0.0)
        acc = acc + _dot((u * u).astype(jnp.bfloat16), wd_ref[c0:c0 + ff_chunk, :])
    if final:
        acc = _rms(acc, gf_ref[...])
    o_ref[0] = acc


def _out_ffn(x, oa, ob, gates, wa, wb, wo, g2, wu, wd, gf, tm, final):
    b, s, d = x.shape
    tok = lambda w: pl.BlockSpec((1, tm, w), lambda i, j: (i, j, 0))
    return pl.pallas_call(
        functools.partial(_out_ffn_body, ff_chunk=min(1024, wu.shape[1]), final=final),
        grid=(b, s // tm),
        in_specs=[tok(d), tok(W_ATT), tok(W_ATT), tok(2 * d)]
                 + [_const_spec(a.shape) for a in (wa, wb, wo, g2, wu, wd, gf)],
        out_specs=tok(d),
        out_shape=jax.ShapeDtypeStruct((b, s, d), jnp.float32),
        compiler_params=pltpu.CompilerParams(
            dimension_semantics=("arbitrary", "arbitrary"), vmem_limit_bytes=VMEM_LIMIT),
        name="out_ffn",
    )(x, oa, ob, gates, wa, wb, wo, g2, wu, wd, gf)


def _aug_placement():
    eq = [[0.0] * W_ATT for _ in range(LANES)]
    ek = [[0.0] * W_ATT for _ in range(LANES)]
    one = 3 * H_FOX
    for h in range(H_FOX):
        for piece in range(3):
            eq[piece * H_FOX + h][HEAD_DIM * h + piece] = 1.0
            eq[one][HEAD_DIM * h + 3 + piece] = 1.0
            ek[one][HEAD_DIM * h + piece] = 1.0
            ek[piece * H_FOX + h][HEAD_DIM * h + 3 + piece] = -1.0
    return jnp.array(eq, jnp.bfloat16), jnp.array(ek, jnp.bfloat16)


def kernel(x, norm1, w_in, forget_bias, rel_bias, w_branch_a, w_branch_b, w_out, norm2, w_up, w_down, final_norm):
    b, s, d = x.shape
    depth = w_in.shape[0]
    bf16 = jnp.bfloat16
    tm = min(512, s)
    tqc = min(2048, s)
    eq, ek = _aug_placement()
    head_of_col = lax.broadcasted_iota(jnp.int32, (W_ATT, LANES), 0) // HEAD_DIM
    hsum = (head_of_col == lax.broadcasted_iota(jnp.int32, (W_ATT, LANES), 1)).astype(bf16)
    gf = final_norm.reshape(1, d)
    o = 3 * W_ATT + H_FOX
    for l in range(depth):
        w = w_in[l]
        wq = (w[:, :W_ATT] * (SCALE * LOG2E)).astype(bf16)
        wk = w[:, W_ATT:2 * W_ATT].astype(bf16)
        wv = w[:, 2 * W_ATT:3 * W_ATT].astype(bf16)
        wf = jnp.pad(w[:, 3 * W_ATT:o], ((0, 0), (0, LANES - H_FOX))).astype(bf16)
        wc = jnp.concatenate([w[:, o:o + W_ATT] * (SCALE * LOG2E), w[:, o + W_ATT:o + 3 * W_ATT]],
                             axis=1).astype(bf16)
        wg = w[:, o + 3 * W_ATT:].astype(bf16)
        bf = jnp.pad(forget_bias[l], (0, LANES - H_FOX)).reshape(1, LANES)
        qf, kf, vf, qc, kc, vc, gates, stat, wo, wu, wd = _in_proj(
            x, norm1[l].reshape(1, d), wq, wk, wv, wc, wg, wf, bf, eq, ek, hsum,
            (w_out, w_up, w_down), l, tm)

        stats = stat[:, :, :4, :H_FOX].transpose(2, 0, 3, 1).reshape(4, -1)
        o_a = _fox(tuple(stats), qf, kf, vf, tm)

        far = rel_bias[l][:, 2 * MAX_REL:]
        g_ext = jnp.concatenate([jnp.broadcast_to(far, (H_CHK, PAD - MAX_REL + 1)),
                                 rel_bias[l][:, 2 * MAX_REL - 1:0:-1],
                                 jnp.broadcast_to(far, (H_CHK, Q_BLOCK))], axis=1)
        bias = _band_bias(g_ext.reshape(H_CHK, 1, -1)).reshape(N_PAIR, 2 * Q_BLOCK, BAND)
        o_b = _chunk_attention(qc, kc, vc, bias, tqc)

        x = _out_ffn(x, o_a, o_b, gates, w_branch_a[l].astype(bf16), w_branch_b[l].astype(bf16),
                     wo, norm2[l].reshape(1, d), wu, wd, gf, tm, final=(l == depth - 1))
    return x
```

```python
import functools
import math

import jax
import jax.numpy as jnp
from jax import lax
from jax.experimental import pallas as pl
from jax.experimental.pallas import tpu as pltpu

HEAD_DIM = 64
H_FOX = 8
H_CHK = 8
N_PAIR = 4
W_ATT = H_FOX * HEAD_DIM
CHUNK = 64
Q_BLOCK = 128
LEFT_CHUNKS = 8
PAD = LEFT_CHUNKS * CHUNK
BAND = PAD + Q_BLOCK
MAX_REL = 128
EPS = 1e-6
NEG = -1e30
LANES = 128
SCALE = 1.0 / math.sqrt(HEAD_DIM)
LOG2E = math.log2(math.e)
F32_BIG = 3.0e38
PROJ_SPLIT = 2
FOX_ROWS = 32
FOX_TRIP = 4
FOX_AHEAD = 2
CHUNK_ROWS = 32
CHUNK_GROUP = 8
SKIP_MARGIN = 30.0
NORM_SLACK = 1.02
VMEM_LIMIT = 60 * 1024 * 1024

_NT = (((1,), (1,)), ((), ()))


def _dot(a, b):
    return jnp.dot(a, b, preferred_element_type=jnp.float32)


def _dot_nt(a, b):
    return lax.dot_general(a, b, _NT, preferred_element_type=jnp.float32)


def _rms(x, g):
    ms = jnp.mean(x * x, axis=-1, keepdims=True)
    return x * lax.rsqrt(ms + EPS) * g


def _const_spec(shape):
    nd = len(shape)
    return pl.BlockSpec(shape, lambda *_: (0,) * nd, pipeline_mode=pl.Buffered(1))


def _in_proj_body(x_ref, g_ref, wq_ref, wk_ref, wv_ref, wc_ref, wg_ref, wf_ref, bf_ref, eq_ref, ek_ref, hsum_ref,
                  wo32_ref, wu32_ref, wd32_ref,
                  qf_ref, kf_ref, vf_ref, qc_ref, kc_ref, vc_ref, gate_ref, stat_ref, wo_ref, wu_ref, wd_ref,
                  carry_ref):
    tm = x_ref.shape[1]
    th = tm // PROJ_SPLIT

    wo_ref[...] = wo32_ref[...].astype(wo_ref.dtype)
    wu_ref[...] = wu32_ref[...].astype(wu_ref.dtype)
    wd_ref[...] = wd32_ref[...].astype(wd_ref.dtype)

    @pl.when(pl.program_id(1) == 0)
    def _():
        carry_ref[...] = jnp.zeros_like(carry_ref)

    lane = lax.broadcasted_iota(jnp.int32, (th, LANES), 1)
    row = lax.broadcasted_iota(jnp.int32, (th, LANES), 0)

    def max_norm(t):
        t = t.astype(jnp.float32)
        sq = _dot((t * t).astype(jnp.bfloat16), hsum_ref[...])
        return jnp.max(sq, axis=0, keepdims=True)

    c_first, c_last, q_sq, k_sq = None, None, None, None
    for part in range(PROJ_SPLIT):
        rows = slice(part * th, (part + 1) * th)
        h = _rms(x_ref[0, rows, :], g_ref[...]).astype(jnp.bfloat16)

        logf = jax.nn.log_sigmoid(_dot(h, wf_ref[...]) + bf_ref[...]) * LOG2E
        c = jnp.where(lane < H_FOX, logf, 0.0)
        k = 1
        while k < th:
            c = c + jnp.where(row >= k, pltpu.roll(c, k, 0), 0.0)
            k *= 2
        c = c + carry_ref[...]
        carry_ref[...] = c[th - 1:th, :]
        c_first = c[0:1, :] if part == 0 else c_first
        c_last = c[th - 1:th, :]

        hi = c.astype(jnp.bfloat16).astype(jnp.float32)
        r1 = c - hi
        mid = r1.astype(jnp.bfloat16).astype(jnp.float32)
        lo = (r1 - mid).astype(jnp.bfloat16).astype(jnp.float32)
        pieces = hi + pltpu.roll(mid, H_FOX, 1) + pltpu.roll(lo, 2 * H_FOX, 1)
        pieces = jnp.where(lane == 3 * H_FOX, 1.0, pieces).astype(jnp.bfloat16)
        aug_q = _dot(pieces, eq_ref[...]).astype(jnp.bfloat16)
        aug_k = _dot(pieces, ek_ref[...]).astype(jnp.bfloat16)

        q = _dot(h, wq_ref[...]).astype(jnp.bfloat16)
        kk = _dot(h, wk_ref[...]).astype(jnp.bfloat16)
        for p in range(N_PAIR):
            src = slice(p * LANES, (p + 1) * LANES)
            qf_ref[0, rows, 2 * p * LANES:(2 * p + 1) * LANES] = q[:, src]
            qf_ref[0, rows, (2 * p + 1) * LANES:(2 * p + 2) * LANES] = aug_q[:, src]
            kf_ref[0, rows, 2 * p * LANES:(2 * p + 1) * LANES] = kk[:, src]
            kf_ref[0, rows, (2 * p + 1) * LANES:(2 * p + 2) * LANES] = aug_k[:, src]
        vf_ref[0, rows, :] = _dot(h, wv_ref[...]).astype(jnp.bfloat16)
        q_sq = max_norm(q) if part == 0 else jnp.maximum(q_sq, max_norm(q))
        k_sq = max_norm(kk) if part == 0 else jnp.maximum(k_sq, max_norm(kk))

        pc = _dot(h, wc_ref[...])
        qc_ref[0, rows, :] = pc[:, :W_ATT].astype(jnp.bfloat16)
        kc_ref[0, rows, :] = pc[:, W_ATT:2 * W_ATT].astype(jnp.bfloat16)
        vc_ref[0, rows, :] = pc[:, 2 * W_ATT:].astype(jnp.bfloat16)
        gate_ref[0, rows, :] = jax.nn.sigmoid(_dot(h, wg_ref[...])).astype(jnp.bfloat16)

    sub = lax.broadcasted_iota(jnp.int32, (8, LANES), 0)
    stat = jnp.where(sub == 0, c_first, 0.0)
    stat = jnp.where(sub == 1, c_last, stat)
    stat = jnp.where(sub == 2, jnp.sqrt(q_sq), stat)
    stat_ref[0, 0] = jnp.where(sub == 3, jnp.sqrt(k_sq), stat)


def _in_proj(x, g, wq, wk, wv, wc, wg, wf, bf, eq, ek, hsum, to_cast, layer, tm):
    b, s, d = x.shape
    n_steps = b * (s // tm)
    tok = lambda width: pl.BlockSpec((1, tm, width), lambda i, j: (i, j, 0))
    strip_in = lambda w: pl.BlockSpec((None, w.shape[1] // n_steps, w.shape[2]),
                                      lambda i, j: (layer, i * (s // tm) + j, 0))
    strip_out = lambda w: pl.BlockSpec((w.shape[1] // n_steps, w.shape[2]), lambda i, j: (i * (s // tm) + j, 0))
    bf16 = jnp.bfloat16
    assert all(w.shape[1] % (16 * n_steps) == 0 for w in to_cast)
    out_shape = [jax.ShapeDtypeStruct((b, s, width), bf16)
                 for width in (2 * W_ATT, 2 * W_ATT, W_ATT, W_ATT, W_ATT, W_ATT, 2 * d)]
    out_shape.append(jax.ShapeDtypeStruct((b, s // tm, 8, LANES), jnp.float32))
    out_shape += [jax.ShapeDtypeStruct(w.shape[1:], bf16) for w in to_cast]
    consts = (g, wq, wk, wv, wc, wg, wf, bf, eq, ek, hsum)
    return pl.pallas_call(
        _in_proj_body,
        grid=(b, s // tm),
        in_specs=[tok(d)] + [_const_spec(a.shape) for a in consts] + [strip_in(w) for w in to_cast],
        out_specs=[tok(sh.shape[-1]) for sh in out_shape[:7]]
                  + [pl.BlockSpec((1, 1, 8, LANES), lambda i, j: (i, j, 0, 0))]
                  + [strip_out(w) for w in to_cast],
        out_shape=out_shape,
        scratch_shapes=[pltpu.VMEM((1, LANES), jnp.float32)],
        compiler_params=pltpu.CompilerParams(
            dimension_semantics=("arbitrary", "arbitrary"), vmem_limit_bytes=VMEM_LIMIT),
        name="in_proj",
    )(x, *consts, *to_cast)


def _fox_body(cfirst_ref, clast_ref, qnorm_ref, knorm_ref, q_ref, k_ref, v_ref, o_ref,
              s_ref, p_ref, m_ref, l_ref, acc_ref, bad_ref, *, tq):
    n_tiles = q_ref.shape[1] // tq
    lane2 = lax.broadcasted_iota(jnp.int32, (1, 2 * LANES), 1) % LANES
    head_a = lax.broadcasted_iota(jnp.int32, (1, LANES), 1) < HEAD_DIM
    base = [(pl.program_id(0) * H_FOX + 2 * pl.program_id(1) + head) * n_tiles for head in range(2)]

    def key_rows(j):
        return pl.ds(pl.multiple_of(j * tq, tq), tq)

    def scores(slot, q_head, j):
        s_ref[slot] = _dot_nt(q_head, k_ref[0, key_rows(j), :])

    def softmax(slot, head, mode):
        for r in range(0, tq, FOX_ROWS):
            rows = slice(r, r + FOX_ROWS)
            s = s_ref[slot, rows, :]
            if mode in ("diagonal", "diagonal_self"):
                row = lax.broadcasted_iota(jnp.int32, (FOX_ROWS, tq), 0) + r
                col = lax.broadcasted_iota(jnp.int32, (FOX_ROWS, tq), 1)
                if mode == "diagonal":
                    s = jnp.where(col <= row, s, NEG)
                    m = jnp.max(s, axis=-1, keepdims=True)
                else:
                    own = slice(r // LANES * LANES, (r // LANES + 1) * LANES)
                    lane = lax.broadcasted_iota(jnp.int32, (FOX_ROWS, LANES), 1)
                    sub = lax.broadcasted_iota(jnp.int32, (FOX_ROWS, LANES), 0)
                    m = jnp.sum(jnp.where(lane == sub + r % LANES, s[:, own], 0.0), axis=-1, keepdims=True)
                    s = jnp.where(col <= row, s, NEG)
                m = jnp.broadcast_to(m, (FOX_ROWS, LANES))
                m_ref[head, rows, :] = m
            elif mode == "exact":
                m_old = m_ref[head, rows, :]
                m = jnp.maximum(m_old, jnp.max(s, axis=-1, keepdims=True))
                alpha = jnp.exp2(m_old - m)
                m_ref[head, rows, :] = m
            else:
                m = m_ref[head, rows, :]
            lsum = None
            for g in range(tq // LANES):
                cols = slice(g * LANES, (g + 1) * LANES)
                pg = jnp.exp2(s[:, cols] - m)
                p_ref[slot, rows, cols] = pg.astype(p_ref.dtype)
                lsum = pg if lsum is None else lsum + pg
            if mode in ("diagonal", "diagonal_self"):
                l_ref[head, rows, :] = lsum
            elif mode == "exact":
                l_ref[head, rows, :] = l_ref[head, rows, :] * alpha + lsum
                acc_ref[head, rows, :] = acc_ref[head, rows, :] * alpha
            else:
                l_ref[head, rows, :] = l_ref[head, rows, :] + lsum

    def values(slot, head, j, mode):
        pv = _dot(p_ref[slot], v_ref[0, key_rows(j), :])
        acc_ref[head] = pv if mode in ("diagonal", "diagonal_self") else acc_ref[head] + pv

    def unit_run(q_of, units):
        n_slots = s_ref.shape[0]
        for n in range(min(FOX_AHEAD, len(units))):
            scores(n, q_of(units[n][1]), units[n][0])
        for n, (tile, head, mode) in enumerate(units):
            softmax(n % n_slots, head, mode)
            if n + FOX_AHEAD < len(units):
                scores((n + FOX_AHEAD) % n_slots, q_of(units[n + FOX_AHEAD][1]), units[n + FOX_AHEAD][0])
            values(n % n_slots, head, tile, mode)

    def run_tiles(q_of, make_units, first, total):
        def full(t, carry):
            unit_run(q_of, make_units(first - FOX_TRIP * t, FOX_TRIP))
            return carry

        n_full = total // FOX_TRIP
        lax.fori_loop(0, n_full, full, 0)
        done = n_full * FOX_TRIP
        count = FOX_TRIP // 2
        while count >= 1:
            fits = (total - done) >= count

            @pl.when(fits)
            def _(start=first - done, count=count):
                unit_run(q_of, make_units(start, count))

            done = jnp.where(fits, done + count, done)
            count //= 2

    def query_tile(qi, kmax, exact):
        q_rows = pl.ds(pl.multiple_of(qi * tq, tq), tq)
        qcat = q_ref[0, q_rows, :]
        zero = jnp.zeros_like(qcat)
        qh = (jnp.where(lane2 < HEAD_DIM, qcat, zero), jnp.where(lane2 >= HEAD_DIM, qcat, zero))

        kmax = [jnp.maximum(kmax[head], knorm_ref[base[head] + qi]) for head in range(2)]
        reach = []
        for head in range(2):
            bound = (cfirst_ref[base[head] + qi] + 2.0 * NORM_SLACK * qnorm_ref[base[head] + qi] * kmax[head]
                     + SKIP_MARGIN * LOG2E)

            def reaches(n, head=head, bound=bound):
                t = jnp.maximum(qi - 1 - n, 0)
                return (n < qi) & (bound - clast_ref[base[head] + t] > 0.0)

            reach.append(lax.while_loop(reaches, lambda n: n + 1, 0))
        n_both, n_long = jnp.minimum(reach[0], reach[1]), jnp.maximum(reach[0], reach[1])

        def both_heads(j, count, first_mode="lazy"):
            return [(j - i, head, first_mode if i == 0 else "lazy") for i in range(count) for head in range(2)]

        if exact:
            unit_run(qh.__getitem__, both_heads(qi, 1, "diagonal"))

            def step(t, carry):
                unit_run(qh.__getitem__, both_heads(qi - 1 - t, 1, "exact"))
                return carry

            lax.fori_loop(0, n_long, step, 0)
        else:
            total = n_both + 1
            count, first_len = FOX_TRIP, 1
            while count >= 1:
                fits = total >= count
                if count < FOX_TRIP:
                    fits = fits & (total < 2 * count)

                @pl.when(fits)
                def _(count=count):
                    unit_run(qh.__getitem__, both_heads(qi, count, "diagonal_self"))

                first_len = jnp.where(fits, count, first_len)
                count //= 2
            run_tiles(qh.__getitem__, both_heads, qi - first_len, total - first_len)

            long_head = jnp.where(reach[1] > reach[0], 1, 0)
            q_long = jnp.where((lane2 < HEAD_DIM) == (long_head == 0), qcat, zero)
            run_tiles(lambda head: q_long, lambda j, count: [(j - i, long_head, "lazy") for i in range(count)],
                      qi - total, n_long - n_both)

        outs = []
        for head in range(2):
            acc, lp = acc_ref[head], l_ref[head]
            outs.append(acc / jnp.sum(lp, axis=-1, keepdims=True))
            if not exact:
                finite = (jnp.abs(acc) <= F32_BIG) & (lp <= F32_BIG)
                bad_ref[...] = jnp.maximum(bad_ref[...], jnp.where(finite, 0.0, 1.0))
        o_ref[0, q_rows, :] = jnp.where(head_a, outs[0], outs[1]).astype(o_ref.dtype)
        return kmax

    def sweep(exact):
        def body(qi, kmax):
            return tuple(query_tile(qi, kmax, exact))

        lax.fori_loop(0, n_tiles, body, (jnp.float32(0.0), jnp.float32(0.0)))

    bad_ref[...] = jnp.zeros_like(bad_ref)
    sweep(exact=False)

    @pl.when(jnp.max(bad_ref[...]) > 0.0)
    def _():
        sweep(exact=True)


def _fox(stats, qf, kf, vf, tq):
    b, s, _ = qf.shape
    smem = pl.BlockSpec(memory_space=pltpu.SMEM)
    return pl.pallas_call(
        functools.partial(_fox_body, tq=tq),
        grid=(b, N_PAIR),
        in_specs=[smem, smem, smem, smem,
                  pl.BlockSpec((1, s, 2 * LANES), lambda i, p: (i, 0, p)),
                  pl.BlockSpec((1, s, 2 * LANES), lambda i, p: (i, 0, p)),
                  pl.BlockSpec((1, s, LANES), lambda i, p: (i, 0, p))],
        out_specs=pl.BlockSpec((1, s, LANES), lambda i, p: (i, 0, p)),
        out_shape=jax.ShapeDtypeStruct((b, s, W_ATT), jnp.bfloat16),
        scratch_shapes=[pltpu.VMEM((2 * FOX_AHEAD, tq, tq), jnp.float32),
                        pltpu.VMEM((2 * FOX_AHEAD, tq, tq), jnp.bfloat16),
                        pltpu.VMEM((2, tq, LANES), jnp.float32),
                        pltpu.VMEM((2, tq, LANES), jnp.float32),
                        pltpu.VMEM((2, tq, LANES), jnp.float32),
                        pltpu.VMEM((tq, LANES), jnp.float32)],
        compiler_params=pltpu.CompilerParams(
            dimension_semantics=("arbitrary", "arbitrary"), vmem_limit_bytes=VMEM_LIMIT),
        name="fox_attention",
    )(*stats, qf, kf, vf)


def _band_bias_body(g_ref, o_ref):
    width = g_ref.shape[-1]
    base = jnp.broadcast_to(g_ref[0], (Q_BLOCK, width))
    toeplitz = pltpu.roll(base, 0, 1, stride=1, stride_axis=0)[:, :BAND]
    qrow = lax.broadcasted_iota(jnp.int32, (Q_BLOCK, BAND), 0)
    kcol = lax.broadcasted_iota(jnp.int32, (Q_BLOCK, BAND), 1)
    cq = qrow // CHUNK
    ck = kcol // CHUNK - LEFT_CHUNKS
    valid = (ck <= cq) & (ck >= cq - LEFT_CHUNKS)
    o_ref[0] = jnp.where(valid, toeplitz * LOG2E, NEG)


def _band_bias(g_ext):
    h, _, width = g_ext.shape
    return pl.pallas_call(
        _band_bias_body,
        grid=(h,),
        in_specs=[pl.BlockSpec((1, 1, width), lambda i: (i, 0, 0))],
        out_specs=pl.BlockSpec((1, Q_BLOCK, BAND), lambda i: (i, 0, 0)),
        out_shape=jax.ShapeDtypeStruct((h, Q_BLOCK, BAND), jnp.float32),
        name="band_bias",
    )(g_ext)


def _chunk_body(q_ref, k_ref, v_ref, bias_ref, o_ref, kpad_ref, vpad_ref, s_ref, p_ref, l_ref, *, tq):
    qi = pl.program_id(2)
    s_len = k_ref.shape[1]

    @pl.when(qi == 0)
    def _():
        zeros = jnp.zeros((PAD, LANES), kpad_ref.dtype)
        kpad_ref[:PAD, :] = zeros
        vpad_ref[:PAD, :] = zeros
        kpad_ref[PAD:PAD + s_len, :] = k_ref[0]
        vpad_ref[PAD:PAD + s_len, :] = v_ref[0]

    head_a = lax.broadcasted_iota(jnp.int32, (1, LANES), 1) < HEAD_DIM
    kcol = lax.broadcasted_iota(jnp.int32, (1, BAND), 1)

    def offsets(blk):
        r0 = pl.multiple_of(blk * Q_BLOCK, Q_BLOCK)
        return r0, pl.multiple_of(qi * tq + r0, Q_BLOCK)

    def scores(slot, blk):
        r0, p0 = offsets(blk)
        q2 = q_ref[0, pl.ds(r0, Q_BLOCK), :]
        zero = jnp.zeros_like(q2)
        qs = jnp.concatenate([jnp.where(head_a, q2, zero), jnp.where(head_a, zero, q2)], axis=0)
        s_ref[slot] = _dot_nt(qs, kpad_ref[pl.ds(p0, BAND), :])

    def softmax(slot, blk, masked):
        _, p0 = offsets(blk)
        for r in range(0, 2 * Q_BLOCK, CHUNK_ROWS):
            rows = slice(r, r + CHUNK_ROWS)
            s = s_ref[slot, rows, :] + bias_ref[0, rows, :]
            if masked:
                s = jnp.where(kcol + p0 >= PAD, s, NEG)
            p = jnp.exp2(s - jnp.max(s, axis=-1, keepdims=True))
            l_ref[slot, rows, :] = jnp.broadcast_to(jnp.sum(p, axis=-1, keepdims=True), (CHUNK_ROWS, LANES))
            p_ref[slot, rows, :] = p.astype(p_ref.dtype)

    def values(slot, blk):
        r0, p0 = offsets(blk)
        o = _dot(p_ref[slot], vpad_ref[pl.ds(p0, BAND), :]) / l_ref[slot]
        o_ref[0, pl.ds(r0, Q_BLOCK), :] = jnp.where(head_a, o[:Q_BLOCK], o[Q_BLOCK:]).astype(o_ref.dtype)

    def group(i, masked):
        first = i * CHUNK_GROUP
        scores(0, first)
        for n in range(CHUNK_GROUP):
            if n + 1 < CHUNK_GROUP:
                scores((n + 1) % 2, first + n + 1)
            softmax(n % 2, first + n, masked)
            values(n % 2, first + n)

    def masked_group(i, carry):
        group(i, True)
        return carry

    def plain_group(i, carry):
        group(i, False)
        return carry

    group_rows = Q_BLOCK * CHUNK_GROUP
    n_groups = tq // group_rows
    n_masked = jnp.clip((PAD - qi * tq + group_rows - 1) // group_rows, 0, n_groups)
    lax.fori_loop(0, n_masked, masked_group, 0)
    lax.fori_loop(n_masked, n_groups, plain_group, 0)


def _chunk_attention(qc, kc, vc, bias, tq):
    b, s, _ = qc.shape
    assert tq % (CHUNK_GROUP * Q_BLOCK) == 0
    return pl.pallas_call(
        functools.partial(_chunk_body, tq=tq),
        grid=(b, N_PAIR, s // tq),
        in_specs=[pl.BlockSpec((1, tq, LANES), lambda i, p, j: (i, j, p)),
                  pl.BlockSpec((1, s, LANES), lambda i, p, j: (i, 0, p)),
                  pl.BlockSpec((1, s, LANES), lambda i, p, j: (i, 0, p)),
                  pl.BlockSpec((1, 2 * Q_BLOCK, BAND), lambda i, p, j: (p, 0, 0))],
        out_specs=pl.BlockSpec((1, tq, LANES), lambda i, p, j: (i, j, p)),
        out_shape=jax.ShapeDtypeStruct((b, s, W_ATT), jnp.bfloat16),
        scratch_shapes=[pltpu.VMEM((PAD + s, LANES), jnp.bfloat16),
                        pltpu.VMEM((PAD + s, LANES), jnp.bfloat16),
                        pltpu.VMEM((2, 2 * Q_BLOCK, BAND), jnp.float32),
                        pltpu.VMEM((2, 2 * Q_BLOCK, BAND), jnp.bfloat16),
                        pltpu.VMEM((2, 2 * Q_BLOCK, LANES), jnp.float32)],
        compiler_params=pltpu.CompilerParams(
            dimension_semantics=("arbitrary", "arbitrary", "arbitrary"), vmem_limit_bytes=VMEM_LIMIT),
        name="chunk_attention",
    )(qc, kc, vc, bias)


def _out_ffn_body(x_ref, oa_ref, ob_ref, gate_ref, wa_ref, wb_ref, wo_ref, g2_ref, wu_ref, wd_ref, gf_ref,
                  o_ref, *, ff_chunk, final):
    d = x_ref.shape[-1]
    ya = _dot(oa_ref[0], wa_ref[...])
    yb = _dot(ob_ref[0], wb_ref[...])
    ga = gate_ref[0, :, :d].astype(jnp.float32)
    gb = gate_ref[0, :, d:].astype(jnp.float32)
    merged = (ga * ya + gb * yb).astype(jnp.bfloat16)
    x1 = x_ref[0] + _dot(merged, wo_ref[...])
    h2 = _rms(x1, g2_ref[...]).astype(jnp.bfloat16)
    acc = x1
    for c0 in range(0, wu_ref.shape[1], ff_chunk):
        u = jnp.maximum(_dot(h2, wu_ref[:, c0:c0 + ff_chunk]), 0.0)
        acc = acc + _dot((u * u).astype(jnp.bfloat16), wd_ref[c0:c0 + ff_chunk, :])
    if final:
        acc = _rms(acc, gf_ref[...])
    o_ref[0] = acc


def _out_ffn(x, oa, ob, gates, wa, wb, wo, g2, wu, wd, gf, tm, final):
    b, s, d = x.shape
    tok = lambda w: pl.BlockSpec((1, tm, w), lambda i, j: (i, j, 0))
    return pl.pallas_call(
        functools.partial(_out_ffn_body, ff_chunk=min(1024, wu.shape[1]), final=final),
        grid=(b, s // tm),
        in_specs=[tok(d), tok(W_ATT), tok(W_ATT), tok(2 * d)]
                 + [_const_spec(a.shape) for a in (wa, wb, wo, g2, wu, wd, gf)],
        out_specs=tok(d),
        out_shape=jax.ShapeDtypeStruct((b, s, d), jnp.float32),
        compiler_params=pltpu.CompilerParams(
            dimension_semantics=("arbitrary", "arbitrary"), vmem_limit_bytes=VMEM_LIMIT),
        name="out_ffn",
    )(x, oa, ob, gates, wa, wb, wo, g2, wu, wd, gf)


def _aug_placement():
    eq = [[0.0] * W_ATT for _ in range(LANES)]
    ek = [[0.0] * W_ATT for _ in range(LANES)]
    one = 3 * H_FOX
    for h in range(H_FOX):
        for piece in range(3):
            eq[piece * H_FOX + h][HEAD_DIM * h + piece] = 1.0
            eq[one][HEAD_DIM * h + 3 + piece] = 1.0
            ek[one][HEAD_DIM * h + piece] = 1.0
            ek[piece * H_FOX + h][HEAD_DIM * h + 3 + piece] = -1.0
    return jnp.array(eq, jnp.bfloat16), jnp.array(ek, jnp.bfloat16)


def kernel(x, norm1, w_in, forget_bias, rel_bias, w_branch_a, w_branch_b, w_out, norm2, w_up, w_down, final_norm):
    b, s, d = x.shape
    depth = w_in.shape[0]
    bf16 = jnp.bfloat16
    tm = min(512, s)
    tqc = s
    eq, ek = _aug_placement()
    head_of_col = lax.broadcasted_iota(jnp.int32, (W_ATT, LANES), 0) // HEAD_DIM
    hsum = (head_of_col == lax.broadcasted_iota(jnp.int32, (W_ATT, LANES), 1)).astype(bf16)
    gf = final_norm.reshape(1, d)
    o = 3 * W_ATT + H_FOX
    for l in range(depth):
        w = w_in[l]
        wq = (w[:, :W_ATT] * (SCALE * LOG2E)).astype(bf16)
        wk = w[:, W_ATT:2 * W_ATT].astype(bf16)
        wv = w[:, 2 * W_ATT:3 * W_ATT].astype(bf16)
        wf = jnp.pad(w[:, 3 * W_ATT:o], ((0, 0), (0, LANES - H_FOX))).astype(bf16)
        wc = jnp.concatenate([w[:, o:o + W_ATT] * (SCALE * LOG2E), w[:, o + W_ATT:o + 3 * W_ATT]],
                             axis=1).astype(bf16)
        wg = w[:, o + 3 * W_ATT:].astype(bf16)
        bf = jnp.pad(forget_bias[l], (0, LANES - H_FOX)).reshape(1, LANES)
        qf, kf, vf, qc, kc, vc, gates, stat, wo, wu, wd = _in_proj(
            x, norm1[l].reshape(1, d), wq, wk, wv, wc, wg, wf, bf, eq, ek, hsum,
            (w_out, w_up, w_down), l, tm)

        stats = stat[:, :, :4, :H_FOX].transpose(2, 0, 3, 1).reshape(4, -1)
        o_a = _fox(tuple(stats), qf, kf, vf, tm)

        far = rel_bias[l][:, 2 * MAX_REL:]
        g_ext = jnp.concatenate([jnp.broadcast_to(far, (H_CHK, PAD - MAX_REL + 1)),
                                 rel_bias[l][:, 2 * MAX_REL - 1:0:-1],
                                 jnp.broadcast_to(far, (H_CHK, Q_BLOCK))], axis=1)
        bias = _band_bias(g_ext.reshape(H_CHK, 1, -1)).reshape(N_PAIR, 2 * Q_BLOCK, BAND)
        o_b = _chunk_attention(qc, kc, vc, bias, tqc)

        x = _out_ffn(x, o_a, o_b, gates, w_branch_a[l].astype(bf16), w_branch_b[l].astype(bf16),
                     wo, norm2[l].reshape(1, d), wu, wd, gf, tm, final=(l == depth - 1))
    return x
```

```python
import functools
import math

import jax
import jax.numpy as jnp
from jax import lax
from jax.experimental import pallas as pl
from jax.experimental.pallas import tpu as pltpu

HEAD_DIM = 64
H_FOX = 8
H_CHK = 8
N_PAIR = 4
W_ATT = H_FOX * HEAD_DIM
CHUNK = 64
Q_BLOCK = 128
LEFT_CHUNKS = 8
PAD = LEFT_CHUNKS * CHUNK
BAND = PAD + Q_BLOCK
MAX_REL = 128
EPS = 1e-6
NEG = -1e30
LANES = 128
SCALE = 1.0 / math.sqrt(HEAD_DIM)
LOG2E = math.log2(math.e)
F32_BIG = 3.0e38
PROJ_SPLIT = 2
FOX_ROWS = 32
FOX_TRIP = 8
FOX_AHEAD = 2
CHUNK_ROWS = 32
CHUNK_GROUP = 8
SKIP_MARGIN = 30.0
NORM_SLACK = 1.02
VMEM_LIMIT = 60 * 1024 * 1024

_NT = (((1,), (1,)), ((), ()))


def _dot(a, b):
    return jnp.dot(a, b, preferred_element_type=jnp.float32)


def _dot_nt(a, b):
    return lax.dot_general(a, b, _NT, preferred_element_type=jnp.float32)


def _rms(x, g):
    ms = jnp.mean(x * x, axis=-1, keepdims=True)
    return x * lax.rsqrt(ms + EPS) * g


def _const_spec(shape):
    nd = len(shape)
    return pl.BlockSpec(shape, lambda *_: (0,) * nd, pipeline_mode=pl.Buffered(1))


def _in_proj_body(x_ref, g_ref, wq_ref, wk_ref, wv_ref, wc_ref, wg_ref, wf_ref, bf_ref, eq_ref, ek_ref, hsum_ref,
                  wo32_ref, wu32_ref, wd32_ref,
                  qf_ref, kf_ref, vf_ref, qc_ref, kc_ref, vc_ref, gate_ref, stat_ref, wo_ref, wu_ref, wd_ref,
                  carry_ref):
    tm = x_ref.shape[1]
    th = tm // PROJ_SPLIT

    wo_ref[...] = wo32_ref[...].astype(wo_ref.dtype)
    wu_ref[...] = wu32_ref[...].astype(wu_ref.dtype)
    wd_ref[...] = wd32_ref[...].astype(wd_ref.dtype)

    @pl.when(pl.program_id(1) == 0)
    def _():
        carry_ref[...] = jnp.zeros_like(carry_ref)

    lane = lax.broadcasted_iota(jnp.int32, (th, LANES), 1)
    row = lax.broadcasted_iota(jnp.int32, (th, LANES), 0)

    def max_norm(t):
        t = t.astype(jnp.float32)
        sq = _dot((t * t).astype(jnp.bfloat16), hsum_ref[...])
        return jnp.max(sq, axis=0, keepdims=True)

    c_first, c_last, q_sq, k_sq = None, None, None, None
    for part in range(PROJ_SPLIT):
        rows = slice(part * th, (part + 1) * th)
        h = _rms(x_ref[0, rows, :], g_ref[...]).astype(jnp.bfloat16)

        logf = jax.nn.log_sigmoid(_dot(h, wf_ref[...]) + bf_ref[...]) * LOG2E
        c = jnp.where(lane < H_FOX, logf, 0.0)
        k = 1
        while k < th:
            c = c + jnp.where(row >= k, pltpu.roll(c, k, 0), 0.0)
            k *= 2
        c = c + carry_ref[...]
        carry_ref[...] = c[th - 1:th, :]
        c_first = c[0:1, :] if part == 0 else c_first
        c_last = c[th - 1:th, :]

        hi = c.astype(jnp.bfloat16).astype(jnp.float32)
        r1 = c - hi
        mid = r1.astype(jnp.bfloat16).astype(jnp.float32)
        lo = (r1 - mid).astype(jnp.bfloat16).astype(jnp.float32)
        pieces = hi + pltpu.roll(mid, H_FOX, 1) + pltpu.roll(lo, 2 * H_FOX, 1)
        pieces = jnp.where(lane == 3 * H_FOX, 1.0, pieces).astype(jnp.bfloat16)
        aug_q = _dot(pieces, eq_ref[...]).astype(jnp.bfloat16)
        aug_k = _dot(pieces, ek_ref[...]).astype(jnp.bfloat16)

        q = _dot(h, wq_ref[...]).astype(jnp.bfloat16)
        kk = _dot(h, wk_ref[...]).astype(jnp.bfloat16)
        for p in range(N_PAIR):
            src = slice(p * LANES, (p + 1) * LANES)
            qf_ref[0, rows, 2 * p * LANES:(2 * p + 1) * LANES] = q[:, src]
            qf_ref[0, rows, (2 * p + 1) * LANES:(2 * p + 2) * LANES] = aug_q[:, src]
            kf_ref[0, rows, 2 * p * LANES:(2 * p + 1) * LANES] = kk[:, src]
            kf_ref[0, rows, (2 * p + 1) * LANES:(2 * p + 2) * LANES] = aug_k[:, src]
        vf_ref[0, rows, :] = _dot(h, wv_ref[...]).astype(jnp.bfloat16)
        q_sq = max_norm(q) if part == 0 else jnp.maximum(q_sq, max_norm(q))
        k_sq = max_norm(kk) if part == 0 else jnp.maximum(k_sq, max_norm(kk))

        pc = _dot(h, wc_ref[...])
        qc_ref[0, rows, :] = pc[:, :W_ATT].astype(jnp.bfloat16)
        kc_ref[0, rows, :] = pc[:, W_ATT:2 * W_ATT].astype(jnp.bfloat16)
        vc_ref[0, rows, :] = pc[:, 2 * W_ATT:].astype(jnp.bfloat16)
        gate_ref[0, rows, :] = jax.nn.sigmoid(_dot(h, wg_ref[...])).astype(jnp.bfloat16)

    sub = lax.broadcasted_iota(jnp.int32, (8, LANES), 0)
    stat = jnp.where(sub == 0, c_first, 0.0)
    stat = jnp.where(sub == 1, c_last, stat)
    stat = jnp.where(sub == 2, jnp.sqrt(q_sq), stat)
    stat_ref[0, 0] = jnp.where(sub == 3, jnp.sqrt(k_sq), stat)


def _in_proj(x, g, wq, wk, wv, wc, wg, wf, bf, eq, ek, hsum, to_cast, layer, tm):
    b, s, d = x.shape
    n_steps = b * (s // tm)
    tok = lambda width: pl.BlockSpec((1, tm, width), lambda i, j: (i, j, 0))
    strip_in = lambda w: pl.BlockSpec((None, w.shape[1] // n_steps, w.shape[2]),
                                      lambda i, j: (layer, i * (s // tm) + j, 0))
    strip_out = lambda w: pl.BlockSpec((w.shape[1] // n_steps, w.shape[2]), lambda i, j: (i * (s // tm) + j, 0))
    bf16 = jnp.bfloat16
    assert all(w.shape[1] % (16 * n_steps) == 0 for w in to_cast)
    out_shape = [jax.ShapeDtypeStruct((b, s, width), bf16)
                 for width in (2 * W_ATT, 2 * W_ATT, W_ATT, W_ATT, W_ATT, W_ATT, 2 * d)]
    out_shape.append(jax.ShapeDtypeStruct((b, s // tm, 8, LANES), jnp.float32))
    out_shape += [jax.ShapeDtypeStruct(w.shape[1:], bf16) for w in to_cast]
    consts = (g, wq, wk, wv, wc, wg, wf, bf, eq, ek, hsum)
    return pl.pallas_call(
        _in_proj_body,
        grid=(b, s // tm),
        in_specs=[tok(d)] + [_const_spec(a.shape) for a in consts] + [strip_in(w) for w in to_cast],
        out_specs=[tok(sh.shape[-1]) for sh in out_shape[:7]]
                  + [pl.BlockSpec((1, 1, 8, LANES), lambda i, j: (i, j, 0, 0))]
                  + [strip_out(w) for w in to_cast],
        out_shape=out_shape,
        scratch_shapes=[pltpu.VMEM((1, LANES), jnp.float32)],
        compiler_params=pltpu.CompilerParams(
            dimension_semantics=("arbitrary", "arbitrary"), vmem_limit_bytes=VMEM_LIMIT),
        name="in_proj",
    )(x, *consts, *to_cast)


def _fox_body(cfirst_ref, clast_ref, qnorm_ref, knorm_ref, q_ref, k_ref, v_ref, o_ref,
              s_ref, p_ref, m_ref, l_ref, acc_ref, bad_ref, *, tq):
    n_tiles = q_ref.shape[1] // tq
    lane2 = lax.broadcasted_iota(jnp.int32, (1, 2 * LANES), 1) % LANES
    head_a = lax.broadcasted_iota(jnp.int32, (1, LANES), 1) < HEAD_DIM
    base = [(pl.program_id(0) * H_FOX + 2 * pl.program_id(1) + head) * n_tiles for head in range(2)]

    def key_rows(j):
        return pl.ds(pl.multiple_of(j * tq, tq), tq)

    def scores(slot, q_head, j):
        s_ref[slot] = _dot_nt(q_head, k_ref[0, key_rows(j), :])

    def softmax(slot, head, mode):
        for r in range(0, tq, FOX_ROWS):
            rows = slice(r, r + FOX_ROWS)
            s = s_ref[slot, rows, :]
            if mode in ("diagonal", "diagonal_self"):
                row = lax.broadcasted_iota(jnp.int32, (FOX_ROWS, tq), 0) + r
                col = lax.broadcasted_iota(jnp.int32, (FOX_ROWS, tq), 1)
                if mode == "diagonal":
                    s = jnp.where(col <= row, s, NEG)
                    m = jnp.max(s, axis=-1, keepdims=True)
                else:
                    own = slice(r // LANES * LANES, (r // LANES + 1) * LANES)
                    lane = lax.broadcasted_iota(jnp.int32, (FOX_ROWS, LANES), 1)
                    sub = lax.broadcasted_iota(jnp.int32, (FOX_ROWS, LANES), 0)
                    m = jnp.sum(jnp.where(lane == sub + r % LANES, s[:, own], 0.0), axis=-1, keepdims=True)
                    s = jnp.where(col <= row, s, NEG)
                m = jnp.broadcast_to(m, (FOX_ROWS, LANES))
                m_ref[head, rows, :] = m
            elif mode == "exact":
                m_old = m_ref[head, rows, :]
                m = jnp.maximum(m_old, jnp.max(s, axis=-1, keepdims=True))
                alpha = jnp.exp2(m_old - m)
                m_ref[head, rows, :] = m
            else:
                m = m_ref[head, rows, :]
            lsum = None
            for g in range(tq // LANES):
                cols = slice(g * LANES, (g + 1) * LANES)
                pg = jnp.exp2(s[:, cols] - m)
                p_ref[slot, rows, cols] = pg.astype(p_ref.dtype)
                lsum = pg if lsum is None else lsum + pg
            if mode in ("diagonal", "diagonal_self"):
                l_ref[head, rows, :] = lsum
            elif mode == "exact":
                l_ref[head, rows, :] = l_ref[head, rows, :] * alpha + lsum
                acc_ref[head, rows, :] = acc_ref[head, rows, :] * alpha
            else:
                l_ref[head, rows, :] = l_ref[head, rows, :] + lsum

    def values(slot, head, j, mode):
        pv = _dot(p_ref[slot], v_ref[0, key_rows(j), :])
        acc_ref[head] = pv if mode in ("diagonal", "diagonal_self") else acc_ref[head] + pv

    def unit_run(q_of, units):
        n_slots = s_ref.shape[0]
        for n in range(min(FOX_AHEAD, len(units))):
            scores(n, q_of(units[n][1]), units[n][0])
        for n, (tile, head, mode) in enumerate(units):
            softmax(n % n_slots, head, mode)
            if n + FOX_AHEAD < len(units):
                scores((n + FOX_AHEAD) % n_slots, q_of(units[n + FOX_AHEAD][1]), units[n + FOX_AHEAD][0])
            values(n % n_slots, head, tile, mode)

    def run_tiles(q_of, make_units, first, total):
        def full(t, carry):
            unit_run(q_of, make_units(first - FOX_TRIP * t, FOX_TRIP))
            return carry

        n_full = total // FOX_TRIP
        lax.fori_loop(0, n_full, full, 0)
        done = n_full * FOX_TRIP
        count = FOX_TRIP // 2
        while count >= 1:
            fits = (total - done) >= count

            @pl.when(fits)
            def _(start=first - done, count=count):
                unit_run(q_of, make_units(start, count))

            done = jnp.where(fits, done + count, done)
            count //= 2

    def query_tile(qi, kmax, exact):
        q_rows = pl.ds(pl.multiple_of(qi * tq, tq), tq)
        qcat = q_ref[0, q_rows, :]
        zero = jnp.zeros_like(qcat)
        qh = (jnp.where(lane2 < HEAD_DIM, qcat, zero), jnp.where(lane2 >= HEAD_DIM, qcat, zero))

        kmax = [jnp.maximum(kmax[head], knorm_ref[base[head] + qi]) for head in range(2)]
        reach = []
        for head in range(2):
            bound = (cfirst_ref[base[head] + qi] + 2.0 * NORM_SLACK * qnorm_ref[base[head] + qi] * kmax[head]
                     + SKIP_MARGIN * LOG2E)

            def reaches(n, head=head, bound=bound):
                t = jnp.maximum(qi - 1 - n, 0)
                return (n < qi) & (bound - clast_ref[base[head] + t] > 0.0)

            reach.append(lax.while_loop(reaches, lambda n: n + 1, 0))
        n_both, n_long = jnp.minimum(reach[0], reach[1]), jnp.maximum(reach[0], reach[1])

        def both_heads(j, count, first_mode="lazy"):
            return [(j - i, head, first_mode if i == 0 else "lazy") for i in range(count) for head in range(2)]

        if exact:
            unit_run(qh.__getitem__, both_heads(qi, 1, "diagonal"))

            def step(t, carry):
                unit_run(qh.__getitem__, both_heads(qi - 1 - t, 1, "exact"))
                return carry

            lax.fori_loop(0, n_long, step, 0)
        else:
            total = n_both + 1
            count, first_len = FOX_TRIP, 1
            while count >= 1:
                fits = total >= count
                if count < FOX_TRIP:
                    fits = fits & (total < 2 * count)

                @pl.when(fits)
                def _(count=count):
                    unit_run(qh.__getitem__, both_heads(qi, count, "diagonal_self"))

                first_len = jnp.where(fits, count, first_len)
                count //= 2
            run_tiles(qh.__getitem__, both_heads, qi - first_len, total - first_len)

            long_head = jnp.where(reach[1] > reach[0], 1, 0)
            q_long = jnp.where((lane2 < HEAD_DIM) == (long_head == 0), qcat, zero)
            run_tiles(lambda head: q_long, lambda j, count: [(j - i, long_head, "lazy") for i in range(count)],
                      qi - total, n_long - n_both)

        outs = []
        for head in range(2):
            acc, lp = acc_ref[head], l_ref[head]
            outs.append(acc / jnp.sum(lp, axis=-1, keepdims=True))
            if not exact:
                finite = (jnp.abs(acc) <= F32_BIG) & (lp <= F32_BIG)
                bad_ref[...] = jnp.maximum(bad_ref[...], jnp.where(finite, 0.0, 1.0))
        o_ref[0, q_rows, :] = jnp.where(head_a, outs[0], outs[1]).astype(o_ref.dtype)
        return kmax

    def sweep(exact):
        def body(qi, kmax):
            return tuple(query_tile(qi, kmax, exact))

        lax.fori_loop(0, n_tiles, body, (jnp.float32(0.0), jnp.float32(0.0)))

    bad_ref[...] = jnp.zeros_like(bad_ref)
    sweep(exact=False)

    @pl.when(jnp.max(bad_ref[...]) > 0.0)
    def _():
        sweep(exact=True)


def _fox(stats, qf, kf, vf, tq):
    b, s, _ = qf.shape
    smem = pl.BlockSpec(memory_space=pltpu.SMEM)
    return pl.pallas_call(
        functools.partial(_fox_body, tq=tq),
        grid=(b, N_PAIR),
        in_specs=[smem, smem, smem, smem,
                  pl.BlockSpec((1, s, 2 * LANES), lambda i, p: (i, 0, p)),
                  pl.BlockSpec((1, s, 2 * LANES), lambda i, p: (i, 0, p)),
                  pl.BlockSpec((1, s, LANES), lambda i, p: (i, 0, p))],
        out_specs=pl.BlockSpec((1, s, LANES), lambda i, p: (i, 0, p)),
        out_shape=jax.ShapeDtypeStruct((b, s, W_ATT), jnp.bfloat16),
        scratch_shapes=[pltpu.VMEM((2 * FOX_AHEAD, tq, tq), jnp.float32),
                        pltpu.VMEM((2 * FOX_AHEAD, tq, tq), jnp.bfloat16),
                        pltpu.VMEM((2, tq, LANES), jnp.float32),
                        pltpu.VMEM((2, tq, LANES), jnp.float32),
                        pltpu.VMEM((2, tq, LANES), jnp.float32),
                        pltpu.VMEM((tq, LANES), jnp.float32)],
        compiler_params=pltpu.CompilerParams(
            dimension_semantics=("arbitrary", "arbitrary"), vmem_limit_bytes=VMEM_LIMIT),
        name="fox_attention",
    )(*stats, qf, kf, vf)


def _band_bias_body(g_ref, o_ref):
    width = g_ref.shape[-1]
    base = jnp.broadcast_to(g_ref[0], (Q_BLOCK, width))
    toeplitz = pltpu.roll(base, 0, 1, stride=1, stride_axis=0)[:, :BAND]
    qrow = lax.broadcasted_iota(jnp.int32, (Q_BLOCK, BAND), 0)
    kcol = lax.broadcasted_iota(jnp.int32, (Q_BLOCK, BAND), 1)
    cq = qrow // CHUNK
    ck = kcol // CHUNK - LEFT_CHUNKS
    valid = (ck <= cq) & (ck >= cq - LEFT_CHUNKS)
    o_ref[0] = jnp.where(valid, toeplitz * LOG2E, NEG)


def _band_bias(g_ext):
    h, _, width = g_ext.shape
    return pl.pallas_call(
        _band_bias_body,
        grid=(h,),
        in_specs=[pl.BlockSpec((1, 1, width), lambda i: (i, 0, 0))],
        out_specs=pl.BlockSpec((1, Q_BLOCK, BAND), lambda i: (i, 0, 0)),
        out_shape=jax.ShapeDtypeStruct((h, Q_BLOCK, BAND), jnp.float32),
        name="band_bias",
    )(g_ext)


def _chunk_body(q_ref, k_ref, v_ref, bias_ref, o_ref, kpad_ref, vpad_ref, s_ref, p_ref, l_ref, *, tq):
    qi = pl.program_id(2)
    s_len = k_ref.shape[1]

    @pl.when(qi == 0)
    def _():
        zeros = jnp.zeros((PAD, LANES), kpad_ref.dtype)
        kpad_ref[:PAD, :] = zeros
        vpad_ref[:PAD, :] = zeros
        kpad_ref[PAD:PAD + s_len, :] = k_ref[0]
        vpad_ref[PAD:PAD + s_len, :] = v_ref[0]

    head_a = lax.broadcasted_iota(jnp.int32, (1, LANES), 1) < HEAD_DIM
    kcol = lax.broadcasted_iota(jnp.int32, (1, BAND), 1)

    def offsets(blk):
        r0 = pl.multiple_of(blk * Q_BLOCK, Q_BLOCK)
        return r0, pl.multiple_of(qi * tq + r0, Q_BLOCK)

    def scores(slot, blk):
        r0, p0 = offsets(blk)
        q2 = q_ref[0, pl.ds(r0, Q_BLOCK), :]
        zero = jnp.zeros_like(q2)
        qs = jnp.concatenate([jnp.where(head_a, q2, zero), jnp.where(head_a, zero, q2)], axis=0)
        s_ref[slot] = _dot_nt(qs, kpad_ref[pl.ds(p0, BAND), :])

    def softmax(slot, blk, masked):
        _, p0 = offsets(blk)
        for r in range(0, 2 * Q_BLOCK, CHUNK_ROWS):
            rows = slice(r, r + CHUNK_ROWS)
            s = s_ref[slot, rows, :] + bias_ref[0, rows, :]
            if masked:
                s = jnp.where(kcol + p0 >= PAD, s, NEG)
            p = jnp.exp2(s - jnp.max(s, axis=-1, keepdims=True))
            l_ref[slot, rows, :] = jnp.broadcast_to(jnp.sum(p, axis=-1, keepdims=True), (CHUNK_ROWS, LANES))
            p_ref[slot, rows, :] = p.astype(p_ref.dtype)

    def values(slot, blk):
        r0, p0 = offsets(blk)
        o = _dot(p_ref[slot], vpad_ref[pl.ds(p0, BAND), :]) / l_ref[slot]
        o_ref[0, pl.ds(r0, Q_BLOCK), :] = jnp.where(head_a, o[:Q_BLOCK], o[Q_BLOCK:]).astype(o_ref.dtype)

    def group(i, masked):
        first = i * CHUNK_GROUP
        scores(0, first)
        for n in range(CHUNK_GROUP):
            if n + 1 < CHUNK_GROUP:
                scores((n + 1) % 2, first + n + 1)
            softmax(n % 2, first + n, masked)
            values(n % 2, first + n)

    def masked_group(i, carry):
        group(i, True)
        return carry

    def plain_group(i, carry):
        group(i, False)
        return carry

    group_rows = Q_BLOCK * CHUNK_GROUP
    n_groups = tq // group_rows
    n_masked = jnp.clip((PAD - qi * tq + group_rows - 1) // group_rows, 0, n_groups)
    lax.fori_loop(0, n_masked, masked_group, 0)
    lax.fori_loop(n_masked, n_groups, plain_group, 0)


def _chunk_attention(qc, kc, vc, bias, tq):
    b, s, _ = qc.shape
    assert tq % (CHUNK_GROUP * Q_BLOCK) == 0
    return pl.pallas_call(
        functools.partial(_chunk_body, tq=tq),
        grid=(b, N_PAIR, s // tq),
        in_specs=[pl.BlockSpec((1, tq, LANES), lambda i, p, j: (i, j, p)),
                  pl.BlockSpec((1, s, LANES), lambda i, p, j: (i, 0, p)),
                  pl.BlockSpec((1, s, LANES), lambda i, p, j: (i, 0, p)),
                  pl.BlockSpec((1, 2 * Q_BLOCK, BAND), lambda i, p, j: (p, 0, 0))],
        out_specs=pl.BlockSpec((1, tq, LANES), lambda i, p, j: (i, j, p)),
        out_shape=jax.ShapeDtypeStruct((b, s, W_ATT), jnp.bfloat16),
        scratch_shapes=[pltpu.VMEM((PAD + s, LANES), jnp.bfloat16),
                        pltpu.VMEM((PAD + s, LANES), jnp.bfloat16),
                        pltpu.VMEM((2, 2 * Q_BLOCK, BAND), jnp.float32),
                        pltpu.VMEM((2, 2 * Q_BLOCK, BAND), jnp.bfloat16),
                        pltpu.VMEM((2, 2 * Q_BLOCK, LANES), jnp.float32)],
        compiler_params=pltpu.CompilerParams(
            dimension_semantics=("arbitrary", "arbitrary", "arbitrary"), vmem_limit_bytes=VMEM_LIMIT),
        name="chunk_attention",
    )(qc, kc, vc, bias)


def _out_ffn_body(x_ref, oa_ref, ob_ref, gate_ref, wa_ref, wb_ref, wo_ref, g2_ref, wu_ref, wd_ref, gf_ref,
                  o_ref, *, ff_chunk, final):
    d = x_ref.shape[-1]
    ya = _dot(oa_ref[0], wa_ref[...])
    yb = _dot(ob_ref[0], wb_ref[...])
    ga = gate_ref[0, :, :d].astype(jnp.float32)
    gb = gate_ref[0, :, d:].astype(jnp.float32)
    merged = (ga * ya + gb * yb).astype(jnp.bfloat16)
    x1 = x_ref[0] + _dot(merged, wo_ref[...])
    h2 = _rms(x1, g2_ref[...]).astype(jnp.bfloat16)
    acc = x1
    for c0 in range(0, wu_ref.shape[1], ff_chunk):
        u = jnp.maximum(_dot(h2, wu_ref[:, c0:c0 + ff_chunk]), 0.0)
        acc = acc + _dot((u * u).astype(jnp.bfloat16), wd_ref[c0:c0 + ff_chunk, :])
    if final:
        acc = _rms(acc, gf_ref[...])
    o_ref[0] = acc


def _out_ffn(x, oa, ob, gates, wa, wb, wo, g2, wu, wd, gf, tm, final):
    b, s, d = x.shape
    tok = lambda w: pl.BlockSpec((1, tm, w), lambda i, j: (i, j, 0))
    return pl.pallas_call(
        functools.partial(_out_ffn_body, ff_chunk=min(1024, wu.shape[1]), final=final),
        grid=(b, s // tm),
        in_specs=[tok(d), tok(W_ATT), tok(W_ATT), tok(2 * d)]
                 + [_const_spec(a.shape) for a in (wa, wb, wo, g2, wu, wd, gf)],
        out_specs=tok(d),
        out_shape=jax.ShapeDtypeStruct((b, s, d), jnp.float32),
        compiler_params=pltpu.CompilerParams(
            dimension_semantics=("arbitrary", "arbitrary"), vmem_limit_bytes=VMEM_LIMIT),
        name="out_ffn",
    )(x, oa, ob, gates, wa, wb, wo, g2, wu, wd, gf)


def _aug_placement():
    eq = [[0.0] * W_ATT for _ in range(LANES)]
    ek = [[0.0] * W_ATT for _ in range(LANES)]
    one = 3 * H_FOX
    for h in range(H_FOX):
        for piece in range(3):
            eq[piece * H_FOX + h][HEAD_DIM * h + piece] = 1.0
            eq[one][HEAD_DIM * h + 3 + piece] = 1.0
            ek[one][HEAD_DIM * h + piece] = 1.0
            ek[piece * H_FOX + h][HEAD_DIM * h + 3 + piece] = -1.0
    return jnp.array(eq, jnp.bfloat16), jnp.array(ek, jnp.bfloat16)


def kernel(x, norm1, w_in, forget_bias, rel_bias, w_branch_a, w_branch_b, w_out, norm2, w_up, w_down, final_norm):
    b, s, d = x.shape
    depth = w_in.shape[0]
    bf16 = jnp.bfloat16
    tm = min(512, s)
    tqc = s
    eq, ek = _aug_placement()
    head_of_col = lax.broadcasted_iota(jnp.int32, (W_ATT, LANES), 0) // HEAD_DIM
    hsum = (head_of_col == lax.broadcasted_iota(jnp.int32, (W_ATT, LANES), 1)).astype(bf16)
    gf = final_norm.reshape(1, d)
    o = 3 * W_ATT + H_FOX
    for l in range(depth):
        w = w_in[l]
        wq = (w[:, :W_ATT] * (SCALE * LOG2E)).astype(bf16)
        wk = w[:, W_ATT:2 * W_ATT].astype(bf16)
        wv = w[:, 2 * W_ATT:3 * W_ATT].astype(bf16)
        wf = jnp.pad(w[:, 3 * W_ATT:o], ((0, 0), (0, LANES - H_FOX))).astype(bf16)
        wc = jnp.concatenate([w[:, o:o + W_ATT] * (SCALE * LOG2E), w[:, o + W_ATT:o + 3 * W_ATT]],
                             axis=1).astype(bf16)
        wg = w[:, o + 3 * W_ATT:].astype(bf16)
        bf = jnp.pad(forget_bias[l], (0, LANES - H_FOX)).reshape(1, LANES)
        qf, kf, vf, qc, kc, vc, gates, stat, wo, wu, wd = _in_proj(
            x, norm1[l].reshape(1, d), wq, wk, wv, wc, wg, wf, bf, eq, ek, hsum,
            (w_out, w_up, w_down), l, tm)

        stats = stat[:, :, :4, :H_FOX].transpose(2, 0, 3, 1).reshape(4, -1)
        o_a = _fox(tuple(stats), qf, kf, vf, tm)

        far = rel_bias[l][:, 2 * MAX_REL:]
        g_ext = jnp.concatenate([jnp.broadcast_to(far, (H_CHK, PAD - MAX_REL + 1)),
                                 rel_bias[l][:, 2 * MAX_REL - 1:0:-1],
                                 jnp.broadcast_to(far, (H_CHK, Q_BLOCK))], axis=1)
        bias = _band_bias(g_ext.reshape(H_CHK, 1, -1)).reshape(N_PAIR, 2 * Q_BLOCK, BAND)
        o_b = _chunk_attention(qc, kc, vc, bias, tqc)

        x = _out_ffn(x, o_a, o_b, gates, w_branch_a[l].astype(bf16), w_branch_b[l].astype(bf16),
                     wo, norm2[l].reshape(1, d), wu, wd, gf, tm, final=(l == depth - 1))
    return x
```

```python
import functools
import math

import jax
import jax.numpy as jnp
from jax import lax
from jax.experimental import pallas as pl
from jax.experimental.pallas import tpu as pltpu

HEAD_DIM = 64
H_FOX = 8
H_CHK = 8
N_PAIR = 4
W_ATT = H_FOX * HEAD_DIM
CHUNK = 64
Q_BLOCK = 128
LEFT_CHUNKS = 8
PAD = LEFT_CHUNKS * CHUNK
BAND = PAD + Q_BLOCK
MAX_REL = 128
EPS = 1e-6
NEG = -1e30
LANES = 128
SCALE = 1.0 / math.sqrt(HEAD_DIM)
LOG2E = math.log2(math.e)
F32_BIG = 3.0e38
PROJ_SPLIT = 2
FOX_ROWS = 32
FOX_TRIP = 8
FOX_AHEAD = 2
CHUNK_ROWS = 32
CHUNK_GROUP = 16
SKIP_MARGIN = 30.0
NORM_SLACK = 1.02
VMEM_LIMIT = 60 * 1024 * 1024

_NT = (((1,), (1,)), ((), ()))


def _dot(a, b):
    return jnp.dot(a, b, preferred_element_type=jnp.float32)


def _dot_nt(a, b):
    return lax.dot_general(a, b, _NT, preferred_element_type=jnp.float32)


def _rms(x, g):
    ms = jnp.mean(x * x, axis=-1, keepdims=True)
    return x * lax.rsqrt(ms + EPS) * g


def _const_spec(shape):
    nd = len(shape)
    return pl.BlockSpec(shape, lambda *_: (0,) * nd, pipeline_mode=pl.Buffered(1))


def _in_proj_body(x_ref, g_ref, wq_ref, wk_ref, wv_ref, wc_ref, wg_ref, wf_ref, bf_ref, eq_ref, ek_ref, hsum_ref,
                  wo32_ref, wu32_ref, wd32_ref,
                  qf_ref, kf_ref, vf_ref, qc_ref, kc_ref, vc_ref, gate_ref, stat_ref, wo_ref, wu_ref, wd_ref,
                  carry_ref):
    tm = x_ref.shape[1]
    th = tm // PROJ_SPLIT

    wo_ref[...] = wo32_ref[...].astype(wo_ref.dtype)
    wu_ref[...] = wu32_ref[...].astype(wu_ref.dtype)
    wd_ref[...] = wd32_ref[...].astype(wd_ref.dtype)

    @pl.when(pl.program_id(1) == 0)
    def _():
        carry_ref[...] = jnp.zeros_like(carry_ref)

    lane = lax.broadcasted_iota(jnp.int32, (th, LANES), 1)
    row = lax.broadcasted_iota(jnp.int32, (th, LANES), 0)

    def max_norm(t):
        t = t.astype(jnp.float32)
        sq = _dot((t * t).astype(jnp.bfloat16), hsum_ref[...])
        return jnp.max(sq, axis=0, keepdims=True)

    c_first, c_last, q_sq, k_sq = None, None, None, None
    for part in range(PROJ_SPLIT):
        rows = slice(part * th, (part + 1) * th)
        h = _rms(x_ref[0, rows, :], g_ref[...]).astype(jnp.bfloat16)

        logf = jax.nn.log_sigmoid(_dot(h, wf_ref[...]) + bf_ref[...]) * LOG2E
        c = jnp.where(lane < H_FOX, logf, 0.0)
        k = 1
        while k < th:
            c = c + jnp.where(row >= k, pltpu.roll(c, k, 0), 0.0)
            k *= 2
        c = c + carry_ref[...]
        carry_ref[...] = c[th - 1:th, :]
        c_first = c[0:1, :] if part == 0 else c_first
        c_last = c[th - 1:th, :]

        hi = c.astype(jnp.bfloat16).astype(jnp.float32)
        r1 = c - hi
        mid = r1.astype(jnp.bfloat16).astype(jnp.float32)
        lo = (r1 - mid).astype(jnp.bfloat16).astype(jnp.float32)
        pieces = hi + pltpu.roll(mid, H_FOX, 1) + pltpu.roll(lo, 2 * H_FOX, 1)
        pieces = jnp.where(lane == 3 * H_FOX, 1.0, pieces).astype(jnp.bfloat16)
        aug_q = _dot(pieces, eq_ref[...]).astype(jnp.bfloat16)
        aug_k = _dot(pieces, ek_ref[...]).astype(jnp.bfloat16)

        q = _dot(h, wq_ref[...]).astype(jnp.bfloat16)
        kk = _dot(h, wk_ref[...]).astype(jnp.bfloat16)
        for p in range(N_PAIR):
            src = slice(p * LANES, (p + 1) * LANES)
            qf_ref[0, rows, 2 * p * LANES:(2 * p + 1) * LANES] = q[:, src]
            qf_ref[0, rows, (2 * p + 1) * LANES:(2 * p + 2) * LANES] = aug_q[:, src]
            kf_ref[0, rows, 2 * p * LANES:(2 * p + 1) * LANES] = kk[:, src]
            kf_ref[0, rows, (2 * p + 1) * LANES:(2 * p + 2) * LANES] = aug_k[:, src]
        vf_ref[0, rows, :] = _dot(h, wv_ref[...]).astype(jnp.bfloat16)
        q_sq = max_norm(q) if part == 0 else jnp.maximum(q_sq, max_norm(q))
        k_sq = max_norm(kk) if part == 0 else jnp.maximum(k_sq, max_norm(kk))

        pc = _dot(h, wc_ref[...])
        qc_ref[0, rows, :] = pc[:, :W_ATT].astype(jnp.bfloat16)
        kc_ref[0, rows, :] = pc[:, W_ATT:2 * W_ATT].astype(jnp.bfloat16)
        vc_ref[0, rows, :] = pc[:, 2 * W_ATT:].astype(jnp.bfloat16)
        gate_ref[0, rows, :] = jax.nn.sigmoid(_dot(h, wg_ref[...])).astype(jnp.bfloat16)

    sub = lax.broadcasted_iota(jnp.int32, (8, LANES), 0)
    stat = jnp.where(sub == 0, c_first, 0.0)
    stat = jnp.where(sub == 1, c_last, stat)
    stat = jnp.where(sub == 2, jnp.sqrt(q_sq), stat)
    stat_ref[0, 0] = jnp.where(sub == 3, jnp.sqrt(k_sq), stat)


def _in_proj(x, g, wq, wk, wv, wc, wg, wf, bf, eq, ek, hsum, to_cast, layer, tm):
    b, s, d = x.shape
    n_steps = b * (s // tm)
    tok = lambda width: pl.BlockSpec((1, tm, width), lambda i, j: (i, j, 0))
    strip_in = lambda w: pl.BlockSpec((None, w.shape[1] // n_steps, w.shape[2]),
                                      lambda i, j: (layer, i * (s // tm) + j, 0))
    strip_out = lambda w: pl.BlockSpec((w.shape[1] // n_steps, w.shape[2]), lambda i, j: (i * (s // tm) + j, 0))
    bf16 = jnp.bfloat16
    assert all(w.shape[1] % (16 * n_steps) == 0 for w in to_cast)
    out_shape = [jax.ShapeDtypeStruct((b, s, width), bf16)
                 for width in (2 * W_ATT, 2 * W_ATT, W_ATT, W_ATT, W_ATT, W_ATT, 2 * d)]
    out_shape.append(jax.ShapeDtypeStruct((b, s // tm, 8, LANES), jnp.float32))
    out_shape += [jax.ShapeDtypeStruct(w.shape[1:], bf16) for w in to_cast]
    consts = (g, wq, wk, wv, wc, wg, wf, bf, eq, ek, hsum)
    return pl.pallas_call(
        _in_proj_body,
        grid=(b, s // tm),
        in_specs=[tok(d)] + [_const_spec(a.shape) for a in consts] + [strip_in(w) for w in to_cast],
        out_specs=[tok(sh.shape[-1]) for sh in out_shape[:7]]
                  + [pl.BlockSpec((1, 1, 8, LANES), lambda i, j: (i, j, 0, 0))]
                  + [strip_out(w) for w in to_cast],
        out_shape=out_shape,
        scratch_shapes=[pltpu.VMEM((1, LANES), jnp.float32)],
        compiler_params=pltpu.CompilerParams(
            dimension_semantics=("arbitrary", "arbitrary"), vmem_limit_bytes=VMEM_LIMIT),
        name="in_proj",
    )(x, *consts, *to_cast)


def _fox_body(cfirst_ref, clast_ref, qnorm_ref, knorm_ref, q_ref, k_ref, v_ref, o_ref,
              s_ref, p_ref, m_ref, l_ref, acc_ref, bad_ref, *, tq):
    n_tiles = q_ref.shape[1] // tq
    lane2 = lax.broadcasted_iota(jnp.int32, (1, 2 * LANES), 1) % LANES
    head_a = lax.broadcasted_iota(jnp.int32, (1, LANES), 1) < HEAD_DIM
    base = [(pl.program_id(0) * H_FOX + 2 * pl.program_id(1) + head) * n_tiles for head in range(2)]

    def key_rows(j):
        return pl.ds(pl.multiple_of(j * tq, tq), tq)

    def scores(slot, q_head, j):
        s_ref[slot] = _dot_nt(q_head, k_ref[0, key_rows(j), :])

    def softmax(slot, head, mode):
        for r in range(0, tq, FOX_ROWS):
            rows = slice(r, r + FOX_ROWS)
            s = s_ref[slot, rows, :]
            if mode in ("diagonal", "diagonal_self"):
                row = lax.broadcasted_iota(jnp.int32, (FOX_ROWS, tq), 0) + r
                col = lax.broadcasted_iota(jnp.int32, (FOX_ROWS, tq), 1)
                if mode == "diagonal":
                    s = jnp.where(col <= row, s, NEG)
                    m = jnp.max(s, axis=-1, keepdims=True)
                else:
                    own = slice(r // LANES * LANES, (r // LANES + 1) * LANES)
                    lane = lax.broadcasted_iota(jnp.int32, (FOX_ROWS, LANES), 1)
                    sub = lax.broadcasted_iota(jnp.int32, (FOX_ROWS, LANES), 0)
                    m = jnp.sum(jnp.where(lane == sub + r % LANES, s[:, own], 0.0), axis=-1, keepdims=True)
                    s = jnp.where(col <= row, s, NEG)
                m = jnp.broadcast_to(m, (FOX_ROWS, LANES))
                m_ref[head, rows, :] = m
            elif mode == "exact":
                m_old = m_ref[head, rows, :]
                m = jnp.maximum(m_old, jnp.max(s, axis=-1, keepdims=True))
                alpha = jnp.exp2(m_old - m)
                m_ref[head, rows, :] = m
            else:
                m = m_ref[head, rows, :]
            lsum = None
            for g in range(tq // LANES):
                cols = slice(g * LANES, (g + 1) * LANES)
                pg = jnp.exp2(s[:, cols] - m)
                p_ref[slot, rows, cols] = pg.astype(p_ref.dtype)
                lsum = pg if lsum is None else lsum + pg
            if mode in ("diagonal", "diagonal_self"):
                l_ref[head, rows, :] = lsum
            elif mode == "exact":
                l_ref[head, rows, :] = l_ref[head, rows, :] * alpha + lsum
                acc_ref[head, rows, :] = acc_ref[head, rows, :] * alpha
            else:
                l_ref[head, rows, :] = l_ref[head, rows, :] + lsum

    def values(slot, head, j, mode):
        pv = _dot(p_ref[slot], v_ref[0, key_rows(j), :])
        acc_ref[head] = pv if mode in ("diagonal", "diagonal_self") else acc_ref[head] + pv

    def unit_run(q_of, units):
        n_slots = s_ref.shape[0]
        for n in range(min(FOX_AHEAD, len(units))):
            scores(n, q_of(units[n][1]), units[n][0])
        for n, (tile, head, mode) in enumerate(units):
            softmax(n % n_slots, head, mode)
            if n + FOX_AHEAD < len(units):
                scores((n + FOX_AHEAD) % n_slots, q_of(units[n + FOX_AHEAD][1]), units[n + FOX_AHEAD][0])
            values(n % n_slots, head, tile, mode)

    def run_tiles(q_of, make_units, first, total):
        def full(t, carry):
            unit_run(q_of, make_units(first - FOX_TRIP * t, FOX_TRIP))
            return carry

        n_full = total // FOX_TRIP
        lax.fori_loop(0, n_full, full, 0)
        done = n_full * FOX_TRIP
        count = FOX_TRIP // 2
        while count >= 1:
            fits = (total - done) >= count

            @pl.when(fits)
            def _(start=first - done, count=count):
                unit_run(q_of, make_units(start, count))

            done = jnp.where(fits, done + count, done)
            count //= 2

    def query_tile(qi, kmax, exact):
        q_rows = pl.ds(pl.multiple_of(qi * tq, tq), tq)
        qcat = q_ref[0, q_rows, :]
        zero = jnp.zeros_like(qcat)
        qh = (jnp.where(lane2 < HEAD_DIM, qcat, zero), jnp.where(lane2 >= HEAD_DIM, qcat, zero))

        kmax = [jnp.maximum(kmax[head], knorm_ref[base[head] + qi]) for head in range(2)]
        reach = []
        for head in range(2):
            bound = (cfirst_ref[base[head] + qi] + 2.0 * NORM_SLACK * qnorm_ref[base[head] + qi] * kmax[head]
                     + SKIP_MARGIN * LOG2E)

            def reaches(n, head=head, bound=bound):
                t = jnp.maximum(qi - 1 - n, 0)
                return (n < qi) & (bound - clast_ref[base[head] + t] > 0.0)

            reach.append(lax.while_loop(reaches, lambda n: n + 1, 0))
        n_both, n_long = jnp.minimum(reach[0], reach[1]), jnp.maximum(reach[0], reach[1])

        def both_heads(j, count, first_mode="lazy"):
            return [(j - i, head, first_mode if i == 0 else "lazy") for i in range(count) for head in range(2)]

        if exact:
            unit_run(qh.__getitem__, both_heads(qi, 1, "diagonal"))

            def step(t, carry):
                unit_run(qh.__getitem__, both_heads(qi - 1 - t, 1, "exact"))
                return carry

            lax.fori_loop(0, n_long, step, 0)
        else:
            total = n_both + 1
            count, first_len = FOX_TRIP, 1
            while count >= 1:
                fits = total >= count
                if count < FOX_TRIP:
                    fits = fits & (total < 2 * count)

                @pl.when(fits)
                def _(count=count):
                    unit_run(qh.__getitem__, both_heads(qi, count, "diagonal_self"))

                first_len = jnp.where(fits, count, first_len)
                count //= 2
            run_tiles(qh.__getitem__, both_heads, qi - first_len, total - first_len)

            long_head = jnp.where(reach[1] > reach[0], 1, 0)
            q_long = jnp.where((lane2 < HEAD_DIM) == (long_head == 0), qcat, zero)
            run_tiles(lambda head: q_long, lambda j, count: [(j - i, long_head, "lazy") for i in range(count)],
                      qi - total, n_long - n_both)

        outs = []
        for head in range(2):
            acc, lp = acc_ref[head], l_ref[head]
            outs.append(acc / jnp.sum(lp, axis=-1, keepdims=True))
            if not exact:
                finite = (jnp.abs(acc) <= F32_BIG) & (lp <= F32_BIG)
                bad_ref[...] = jnp.maximum(bad_ref[...], jnp.where(finite, 0.0, 1.0))
        o_ref[0, q_rows, :] = jnp.where(head_a, outs[0], outs[1]).astype(o_ref.dtype)
        return kmax

    def sweep(exact):
        def body(qi, kmax):
            return tuple(query_tile(qi, kmax, exact))

        lax.fori_loop(0, n_tiles, body, (jnp.float32(0.0), jnp.float32(0.0)))

    bad_ref[...] = jnp.zeros_like(bad_ref)
    sweep(exact=False)

    @pl.when(jnp.max(bad_ref[...]) > 0.0)
    def _():
        sweep(exact=True)


def _fox(stats, qf, kf, vf, tq):
    b, s, _ = qf.shape
    smem = pl.BlockSpec(memory_space=pltpu.SMEM)
    return pl.pallas_call(
        functools.partial(_fox_body, tq=tq),
        grid=(b, N_PAIR),
        in_specs=[smem, smem, smem, smem,
                  pl.BlockSpec((1, s, 2 * LANES), lambda i, p: (i, 0, p)),
                  pl.BlockSpec((1, s, 2 * LANES), lambda i, p: (i, 0, p)),
                  pl.BlockSpec((1, s, LANES), lambda i, p: (i, 0, p))],
        out_specs=pl.BlockSpec((1, s, LANES), lambda i, p: (i, 0, p)),
        out_shape=jax.ShapeDtypeStruct((b, s, W_ATT), jnp.bfloat16),
        scratch_shapes=[pltpu.VMEM((2 * FOX_AHEAD, tq, tq), jnp.float32),
                        pltpu.VMEM((2 * FOX_AHEAD, tq, tq), jnp.bfloat16),
                        pltpu.VMEM((2, tq, LANES), jnp.float32),
                        pltpu.VMEM((2, tq, LANES), jnp.float32),
                        pltpu.VMEM((2, tq, LANES), jnp.float32),
                        pltpu.VMEM((tq, LANES), jnp.float32)],
        compiler_params=pltpu.CompilerParams(
            dimension_semantics=("arbitrary", "arbitrary"), vmem_limit_bytes=VMEM_LIMIT),
        name="fox_attention",
    )(*stats, qf, kf, vf)


def _band_bias_body(g_ref, o_ref):
    width = g_ref.shape[-1]
    base = jnp.broadcast_to(g_ref[0], (Q_BLOCK, width))
    toeplitz = pltpu.roll(base, 0, 1, stride=1, stride_axis=0)[:, :BAND]
    qrow = lax.broadcasted_iota(jnp.int32, (Q_BLOCK, BAND), 0)
    kcol = lax.broadcasted_iota(jnp.int32, (Q_BLOCK, BAND), 1)
    cq = qrow // CHUNK
    ck = kcol // CHUNK - LEFT_CHUNKS
    valid = (ck <= cq) & (ck >= cq - LEFT_CHUNKS)
    o_ref[0] = jnp.where(valid, toeplitz * LOG2E, NEG)


def _band_bias(g_ext):
    h, _, width = g_ext.shape
    return pl.pallas_call(
        _band_bias_body,
        grid=(h,),
        in_specs=[pl.BlockSpec((1, 1, width), lambda i: (i, 0, 0))],
        out_specs=pl.BlockSpec((1, Q_BLOCK, BAND), lambda i: (i, 0, 0)),
        out_shape=jax.ShapeDtypeStruct((h, Q_BLOCK, BAND), jnp.float32),
        name="band_bias",
    )(g_ext)


def _chunk_body(q_ref, k_ref, v_ref, bias_ref, o_ref, kpad_ref, vpad_ref, s_ref, p_ref, l_ref, *, tq):
    qi = pl.program_id(2)
    s_len = k_ref.shape[1]

    @pl.when(qi == 0)
    def _():
        zeros = jnp.zeros((PAD, LANES), kpad_ref.dtype)
        kpad_ref[:PAD, :] = zeros
        vpad_ref[:PAD, :] = zeros
        kpad_ref[PAD:PAD + s_len, :] = k_ref[0]
        vpad_ref[PAD:PAD + s_len, :] = v_ref[0]

    head_a = lax.broadcasted_iota(jnp.int32, (1, LANES), 1) < HEAD_DIM
    kcol = lax.broadcasted_iota(jnp.int32, (1, BAND), 1)

    def offsets(blk):
        r0 = pl.multiple_of(blk * Q_BLOCK, Q_BLOCK)
        return r0, pl.multiple_of(qi * tq + r0, Q_BLOCK)

    def scores(slot, blk):
        r0, p0 = offsets(blk)
        q2 = q_ref[0, pl.ds(r0, Q_BLOCK), :]
        zero = jnp.zeros_like(q2)
        qs = jnp.concatenate([jnp.where(head_a, q2, zero), jnp.where(head_a, zero, q2)], axis=0)
        s_ref[slot] = _dot_nt(qs, kpad_ref[pl.ds(p0, BAND), :])

    def softmax(slot, blk, masked):
        _, p0 = offsets(blk)
        for r in range(0, 2 * Q_BLOCK, CHUNK_ROWS):
            rows = slice(r, r + CHUNK_ROWS)
            s = s_ref[slot, rows, :] + bias_ref[0, rows, :]
            if masked:
                s = jnp.where(kcol + p0 >= PAD, s, NEG)
            p = jnp.exp2(s - jnp.max(s, axis=-1, keepdims=True))
            l_ref[slot, rows, :] = jnp.broadcast_to(jnp.sum(p, axis=-1, keepdims=True), (CHUNK_ROWS, LANES))
            p_ref[slot, rows, :] = p.astype(p_ref.dtype)

    def values(slot, blk):
        r0, p0 = offsets(blk)
        o = _dot(p_ref[slot], vpad_ref[pl.ds(p0, BAND), :]) / l_ref[slot]
        o_ref[0, pl.ds(r0, Q_BLOCK), :] = jnp.where(head_a, o[:Q_BLOCK], o[Q_BLOCK:]).astype(o_ref.dtype)

    def group(i, masked):
        first = i * CHUNK_GROUP
        scores(0, first)
        for n in range(CHUNK_GROUP):
            if n + 1 < CHUNK_GROUP:
                scores((n + 1) % 2, first + n + 1)
            softmax(n % 2, first + n, masked)
            values(n % 2, first + n)

    def masked_group(i, carry):
        group(i, True)
        return carry

    def plain_group(i, carry):
        group(i, False)
        return carry

    group_rows = Q_BLOCK * CHUNK_GROUP
    n_groups = tq // group_rows
    n_masked = jnp.clip((PAD - qi * tq + group_rows - 1) // group_rows, 0, n_groups)
    lax.fori_loop(0, n_masked, masked_group, 0)
    lax.fori_loop(n_masked, n_groups, plain_group, 0)


def _chunk_attention(qc, kc, vc, bias, tq):
    b, s, _ = qc.shape
    assert tq % (CHUNK_GROUP * Q_BLOCK) == 0
    return pl.pallas_call(
        functools.partial(_chunk_body, tq=tq),
        grid=(b, N_PAIR, s // tq),
        in_specs=[pl.BlockSpec((1, tq, LANES), lambda i, p, j: (i, j, p)),
                  pl.BlockSpec((1, s, LANES), lambda i, p, j: (i, 0, p)),
                  pl.BlockSpec((1, s, LANES), lambda i, p, j: (i, 0, p)),
                  pl.BlockSpec((1, 2 * Q_BLOCK, BAND), lambda i, p, j: (p, 0, 0))],
        out_specs=pl.BlockSpec((1, tq, LANES), lambda i, p, j: (i, j, p)),
        out_shape=jax.ShapeDtypeStruct((b, s, W_ATT), jnp.bfloat16),
        scratch_shapes=[pltpu.VMEM((PAD + s, LANES), jnp.bfloat16),
                        pltpu.VMEM((PAD + s, LANES), jnp.bfloat16),
                        pltpu.VMEM((2, 2 * Q_BLOCK, BAND), jnp.float32),
                        pltpu.VMEM((2, 2 * Q_BLOCK, BAND), jnp.bfloat16),
                        pltpu.VMEM((2, 2 * Q_BLOCK, LANES), jnp.float32)],
        compiler_params=pltpu.CompilerParams(
            dimension_semantics=("arbitrary", "arbitrary", "arbitrary"), vmem_limit_bytes=VMEM_LIMIT),
        name="chunk_attention",
    )(qc, kc, vc, bias)


def _out_ffn_body(x_ref, oa_ref, ob_ref, gate_ref, wa_ref, wb_ref, wo_ref, g2_ref, wu_ref, wd_ref, gf_ref,
                  o_ref, *, ff_chunk, final):
    d = x_ref.shape[-1]
    ya = _dot(oa_ref[0], wa_ref[...])
    yb = _dot(ob_ref[0], wb_ref[...])
    ga = gate_ref[0, :, :d].astype(jnp.float32)
    gb = gate_ref[0, :, d:].astype(jnp.float32)
    merged = (ga * ya + gb * yb).astype(jnp.bfloat16)
    x1 = x_ref[0] + _dot(merged, wo_ref[...])
    h2 = _rms(x1, g2_ref[...]).astype(jnp.bfloat16)
    acc = x1
    for c0 in range(0, wu_ref.shape[1], ff_chunk):
        u = jnp.maximum(_dot(h2, wu_ref[:, c0:c0 + ff_chunk]), 0.0)
        acc = acc + _dot((u * u).astype(jnp.bfloat16), wd_ref[c0:c0 + ff_chunk, :])
    if final:
        acc = _rms(acc, gf_ref[...])
    o_ref[0] = acc


def _out_ffn(x, oa, ob, gates, wa, wb, wo, g2, wu, wd, gf, tm, final):
    b, s, d = x.shape
    tok = lambda w: pl.BlockSpec((1, tm, w), lambda i, j: (i, j, 0))
    return pl.pallas_call(
        functools.partial(_out_ffn_body, ff_chunk=min(1024, wu.shape[1]), final=final),
        grid=(b, s // tm),
        in_specs=[tok(d), tok(W_ATT), tok(W_ATT), tok(2 * d)]
                 + [_const_spec(a.shape) for a in (wa, wb, wo, g2, wu, wd, gf)],
        out_specs=tok(d),
        out_shape=jax.ShapeDtypeStruct((b, s, d), jnp.float32),
        compiler_params=pltpu.CompilerParams(
            dimension_semantics=("arbitrary", "arbitrary"), vmem_limit_bytes=VMEM_LIMIT),
        name="out_ffn",
    )(x, oa, ob, gates, wa, wb, wo, g2, wu, wd, gf)


def _aug_placement():
    eq = [[0.0] * W_ATT for _ in range(LANES)]
    ek = [[0.0] * W_ATT for _ in range(LANES)]
    one = 3 * H_FOX
    for h in range(H_FOX):
        for piece in range(3):
            eq[piece * H_FOX + h][HEAD_DIM * h + piece] = 1.0
            eq[one][HEAD_DIM * h + 3 + piece] = 1.0
            ek[one][HEAD_DIM * h + piece] = 1.0
            ek[piece * H_FOX + h][HEAD_DIM * h + 3 + piece] = -1.0
    return jnp.array(eq, jnp.bfloat16), jnp.array(ek, jnp.bfloat16)


def kernel(x, norm1, w_in, forget_bias, rel_bias, w_branch_a, w_branch_b, w_out, norm2, w_up, w_down, final_norm):
    b, s, d = x.shape
    depth = w_in.shape[0]
    bf16 = jnp.bfloat16
    tm = min(512, s)
    tqc = s
    eq, ek = _aug_placement()
    head_of_col = lax.broadcasted_iota(jnp.int32, (W_ATT, LANES), 0) // HEAD_DIM
    hsum = (head_of_col == lax.broadcasted_iota(jnp.int32, (W_ATT, LANES), 1)).astype(bf16)
    gf = final_norm.reshape(1, d)
    o = 3 * W_ATT + H_FOX
    for l in range(depth):
        w = w_in[l]
        wq = (w[:, :W_ATT] * (SCALE * LOG2E)).astype(bf16)
        wk = w[:, W_ATT:2 * W_ATT].astype(bf16)
        wv = w[:, 2 * W_ATT:3 * W_ATT].astype(bf16)
        wf = jnp.pad(w[:, 3 * W_ATT:o], ((0, 0), (0, LANES - H_FOX))).astype(bf16)
        wc = jnp.concatenate([w[:, o:o + W_ATT] * (SCALE * LOG2E), w[:, o + W_ATT:o + 3 * W_ATT]],
                             axis=1).astype(bf16)
        wg = w[:, o + 3 * W_ATT:].astype(bf16)
        bf = jnp.pad(forget_bias[l], (0, LANES - H_FOX)).reshape(1, LANES)
        qf, kf, vf, qc, kc, vc, gates, stat, wo, wu, wd = _in_proj(
            x, norm1[l].reshape(1, d), wq, wk, wv, wc, wg, wf, bf, eq, ek, hsum,
            (w_out, w_up, w_down), l, tm)

        stats = stat[:, :, :4, :H_FOX].transpose(2, 0, 3, 1).reshape(4, -1)
        o_a = _fox(tuple(stats), qf, kf, vf, tm)

        far = rel_bias[l][:, 2 * MAX_REL:]
        g_ext = jnp.concatenate([jnp.broadcast_to(far, (H_CHK, PAD - MAX_REL + 1)),
                                 rel_bias[l][:, 2 * MAX_REL - 1:0:-1],
                                 jnp.broadcast_to(far, (H_CHK, Q_BLOCK))], axis=1)
        bias = _band_bias(g_ext.reshape(H_CHK, 1, -1)).reshape(N_PAIR, 2 * Q_BLOCK, BAND)
        o_b = _chunk_attention(qc, kc, vc, bias, tqc)

        x = _out_ffn(x, o_a, o_b, gates, w_branch_a[l].astype(bf16), w_branch_b[l].astype(bf16),
                     wo, norm2[l].reshape(1, d), wu, wd, gf, tm, final=(l == depth - 1))
    return x
```

```python
import functools
import math

import jax
import jax.numpy as jnp
from jax import lax
from jax.experimental import pallas as pl
from jax.experimental.pallas import tpu as pltpu

HEAD_DIM = 64
H_FOX = 8
H_CHK = 8
N_PAIR = 4
W_ATT = H_FOX * HEAD_DIM
CHUNK = 64
Q_BLOCK = 128
LEFT_CHUNKS = 8
PAD = LEFT_CHUNKS * CHUNK
BAND = PAD + Q_BLOCK
MAX_REL = 128
EPS = 1e-6
NEG = -1e30
LANES = 128
SCALE = 1.0 / math.sqrt(HEAD_DIM)
LOG2E = math.log2(math.e)
F32_BIG = 3.0e38
PROJ_SPLIT = 2
FOX_ROWS = 32
FOX_TRIP = 8
FOX_AHEAD = 2
CHUNK_ROWS = 32
CHUNK_GROUP = 32
SKIP_MARGIN = 30.0
NORM_SLACK = 1.02
VMEM_LIMIT = 60 * 1024 * 1024

_NT = (((1,), (1,)), ((), ()))


def _dot(a, b):
    return jnp.dot(a, b, preferred_element_type=jnp.float32)


def _dot_nt(a, b):
    return lax.dot_general(a, b, _NT, preferred_element_type=jnp.float32)


def _rms(x, g):
    ms = jnp.mean(x * x, axis=-1, keepdims=True)
    return x * lax.rsqrt(ms + EPS) * g


def _const_spec(shape):
    nd = len(shape)
    return pl.BlockSpec(shape, lambda *_: (0,) * nd, pipeline_mode=pl.Buffered(1))


def _in_proj_body(x_ref, g_ref, wq_ref, wk_ref, wv_ref, wc_ref, wg_ref, wf_ref, bf_ref, eq_ref, ek_ref, hsum_ref,
                  wo32_ref, wu32_ref, wd32_ref,
                  qf_ref, kf_ref, vf_ref, qc_ref, kc_ref, vc_ref, gate_ref, stat_ref, wo_ref, wu_ref, wd_ref,
                  carry_ref):
    tm = x_ref.shape[1]
    th = tm // PROJ_SPLIT

    wo_ref[...] = wo32_ref[...].astype(wo_ref.dtype)
    wu_ref[...] = wu32_ref[...].astype(wu_ref.dtype)
    wd_ref[...] = wd32_ref[...].astype(wd_ref.dtype)

    @pl.when(pl.program_id(1) == 0)
    def _():
        carry_ref[...] = jnp.zeros_like(carry_ref)

    lane = lax.broadcasted_iota(jnp.int32, (th, LANES), 1)
    row = lax.broadcasted_iota(jnp.int32, (th, LANES), 0)

    def max_norm(t):
        t = t.astype(jnp.float32)
        sq = _dot((t * t).astype(jnp.bfloat16), hsum_ref[...])
        return jnp.max(sq, axis=0, keepdims=True)

    c_first, c_last, q_sq, k_sq = None, None, None, None
    for part in range(PROJ_SPLIT):
        rows = slice(part * th, (part + 1) * th)
        h = _rms(x_ref[0, rows, :], g_ref[...]).astype(jnp.bfloat16)

        logf = jax.nn.log_sigmoid(_dot(h, wf_ref[...]) + bf_ref[...]) * LOG2E
        c = jnp.where(lane < H_FOX, logf, 0.0)
        k = 1
        while k < th:
            c = c + jnp.where(row >= k, pltpu.roll(c, k, 0), 0.0)
            k *= 2
        c = c + carry_ref[...]
        carry_ref[...] = c[th - 1:th, :]
        c_first = c[0:1, :] if part == 0 else c_first
        c_last = c[th - 1:th, :]

        hi = c.astype(jnp.bfloat16).astype(jnp.float32)
        r1 = c - hi
        mid = r1.astype(jnp.bfloat16).astype(jnp.float32)
        lo = (r1 - mid).astype(jnp.bfloat16).astype(jnp.float32)
        pieces = hi + pltpu.roll(mid, H_FOX, 1) + pltpu.roll(lo, 2 * H_FOX, 1)
        pieces = jnp.where(lane == 3 * H_FOX, 1.0, pieces).astype(jnp.bfloat16)
        aug_q = _dot(pieces, eq_ref[...]).astype(jnp.bfloat16)
        aug_k = _dot(pieces, ek_ref[...]).astype(jnp.bfloat16)

        q = _dot(h, wq_ref[...]).astype(jnp.bfloat16)
        kk = _dot(h, wk_ref[...]).astype(jnp.bfloat16)
        for p in range(N_PAIR):
            src = slice(p * LANES, (p + 1) * LANES)
            qf_ref[0, rows, 2 * p * LANES:(2 * p + 1) * LANES] = q[:, src]
            qf_ref[0, rows, (2 * p + 1) * LANES:(2 * p + 2) * LANES] = aug_q[:, src]
            kf_ref[0, rows, 2 * p * LANES:(2 * p + 1) * LANES] = kk[:, src]
            kf_ref[0, rows, (2 * p + 1) * LANES:(2 * p + 2) * LANES] = aug_k[:, src]
        vf_ref[0, rows, :] = _dot(h, wv_ref[...]).astype(jnp.bfloat16)
        q_sq = max_norm(q) if part == 0 else jnp.maximum(q_sq, max_norm(q))
        k_sq = max_norm(kk) if part == 0 else jnp.maximum(k_sq, max_norm(kk))

        pc = _dot(h, wc_ref[...])
        qc_ref[0, rows, :] = pc[:, :W_ATT].astype(jnp.bfloat16)
        kc_ref[0, rows, :] = pc[:, W_ATT:2 * W_ATT].astype(jnp.bfloat16)
        vc_ref[0, rows, :] = pc[:, 2 * W_ATT:].astype(jnp.bfloat16)
        gate_ref[0, rows, :] = jax.nn.sigmoid(_dot(h, wg_ref[...])).astype(jnp.bfloat16)

    sub = lax.broadcasted_iota(jnp.int32, (8, LANES), 0)
    stat = jnp.where(sub == 0, c_first, 0.0)
    stat = jnp.where(sub == 1, c_last, stat)
    stat = jnp.where(sub == 2, jnp.sqrt(q_sq), stat)
    stat_ref[0, 0] = jnp.where(sub == 3, jnp.sqrt(k_sq), stat)


def _in_proj(x, g, wq, wk, wv, wc, wg, wf, bf, eq, ek, hsum, to_cast, layer, tm):
    b, s, d = x.shape
    n_steps = b * (s // tm)
    tok = lambda width: pl.BlockSpec((1, tm, width), lambda i, j: (i, j, 0))
    strip_in = lambda w: pl.BlockSpec((None, w.shape[1] // n_steps, w.shape[2]),
                                      lambda i, j: (layer, i * (s // tm) + j, 0))
    strip_out = lambda w: pl.BlockSpec((w.shape[1] // n_steps, w.shape[2]), lambda i, j: (i * (s // tm) + j, 0))
    bf16 = jnp.bfloat16
    assert all(w.shape[1] % (16 * n_steps) == 0 for w in to_cast)
    out_shape = [jax.ShapeDtypeStruct((b, s, width), bf16)
                 for width in (2 * W_ATT, 2 * W_ATT, W_ATT, W_ATT, W_ATT, W_ATT, 2 * d)]
    out_shape.append(jax.ShapeDtypeStruct((b, s // tm, 8, LANES), jnp.float32))
    out_shape += [jax.ShapeDtypeStruct(w.shape[1:], bf16) for w in to_cast]
    consts = (g, wq, wk, wv, wc, wg, wf, bf, eq, ek, hsum)
    return pl.pallas_call(
        _in_proj_body,
        grid=(b, s // tm),
        in_specs=[tok(d)] + [_const_spec(a.shape) for a in consts] + [strip_in(w) for w in to_cast],
        out_specs=[tok(sh.shape[-1]) for sh in out_shape[:7]]
                  + [pl.BlockSpec((1, 1, 8, LANES), lambda i, j: (i, j, 0, 0))]
                  + [strip_out(w) for w in to_cast],
        out_shape=out_shape,
        scratch_shapes=[pltpu.VMEM((1, LANES), jnp.float32)],
        compiler_params=pltpu.CompilerParams(
            dimension_semantics=("arbitrary", "arbitrary"), vmem_limit_bytes=VMEM_LIMIT),
        name="in_proj",
    )(x, *consts, *to_cast)


def _fox_body(cfirst_ref, clast_ref, qnorm_ref, knorm_ref, q_ref, k_ref, v_ref, o_ref,
              s_ref, p_ref, m_ref, l_ref, acc_ref, bad_ref, *, tq):
    n_tiles = q_ref.shape[1] // tq
    lane2 = lax.broadcasted_iota(jnp.int32, (1, 2 * LANES), 1) % LANES
    head_a = lax.broadcasted_iota(jnp.int32, (1, LANES), 1) < HEAD_DIM
    base = [(pl.program_id(0) * H_FOX + 2 * pl.program_id(1) + head) * n_tiles for head in range(2)]

    def key_rows(j):
        return pl.ds(pl.multiple_of(j * tq, tq), tq)

    def scores(slot, q_head, j):
        s_ref[slot] = _dot_nt(q_head, k_ref[0, key_rows(j), :])

    def softmax(slot, head, mode):
        for r in range(0, tq, FOX_ROWS):
            rows = slice(r, r + FOX_ROWS)
            s = s_ref[slot, rows, :]
            if mode in ("diagonal", "diagonal_self"):
                row = lax.broadcasted_iota(jnp.int32, (FOX_ROWS, tq), 0) + r
                col = lax.broadcasted_iota(jnp.int32, (FOX_ROWS, tq), 1)
                if mode == "diagonal":
                    s = jnp.where(col <= row, s, NEG)
                    m = jnp.max(s, axis=-1, keepdims=True)
                else:
                    own = slice(r // LANES * LANES, (r // LANES + 1) * LANES)
                    lane = lax.broadcasted_iota(jnp.int32, (FOX_ROWS, LANES), 1)
                    sub = lax.broadcasted_iota(jnp.int32, (FOX_ROWS, LANES), 0)
                    m = jnp.sum(jnp.where(lane == sub + r % LANES, s[:, own], 0.0), axis=-1, keepdims=True)
                    s = jnp.where(col <= row, s, NEG)
                m = jnp.broadcast_to(m, (FOX_ROWS, LANES))
                m_ref[head, rows, :] = m
            elif mode == "exact":
                m_old = m_ref[head, rows, :]
                m = jnp.maximum(m_old, jnp.max(s, axis=-1, keepdims=True))
                alpha = jnp.exp2(m_old - m)
                m_ref[head, rows, :] = m
            else:
                m = m_ref[head, rows, :]
            lsum = None
            for g in range(tq // LANES):
                cols = slice(g * LANES, (g + 1) * LANES)
                pg = jnp.exp2(s[:, cols] - m)
                p_ref[slot, rows, cols] = pg.astype(p_ref.dtype)
                lsum = pg if lsum is None else lsum + pg
            if mode in ("diagonal", "diagonal_self"):
                l_ref[head, rows, :] = lsum
            elif mode == "exact":
                l_ref[head, rows, :] = l_ref[head, rows, :] * alpha + lsum
                acc_ref[head, rows, :] = acc_ref[head, rows, :] * alpha
            else:
                l_ref[head, rows, :] = l_ref[head, rows, :] + lsum

    def values(slot, head, j, mode):
        pv = _dot(p_ref[slot], v_ref[0, key_rows(j), :])
        acc_ref[head] = pv if mode in ("diagonal", "diagonal_self") else acc_ref[head] + pv

    def unit_run(q_of, units):
        n_slots = s_ref.shape[0]
        for n in range(min(FOX_AHEAD, len(units))):
            scores(n, q_of(units[n][1]), units[n][0])
        for n, (tile, head, mode) in enumerate(units):
            softmax(n % n_slots, head, mode)
            if n + FOX_AHEAD < len(units):
                scores((n + FOX_AHEAD) % n_slots, q_of(units[n + FOX_AHEAD][1]), units[n + FOX_AHEAD][0])
            values(n % n_slots, head, tile, mode)

    def run_tiles(q_of, make_units, first, total):
        def full(t, carry):
            unit_run(q_of, make_units(first - FOX_TRIP * t, FOX_TRIP))
            return carry

        n_full = total // FOX_TRIP
        lax.fori_loop(0, n_full, full, 0)
        done = n_full * FOX_TRIP
        count = FOX_TRIP // 2
        while count >= 1:
            fits = (total - done) >= count

            @pl.when(fits)
            def _(start=first - done, count=count):
                unit_run(q_of, make_units(start, count))

            done = jnp.where(fits, done + count, done)
            count //= 2

    def query_tile(qi, kmax, exact):
        q_rows = pl.ds(pl.multiple_of(qi * tq, tq), tq)
        qcat = q_ref[0, q_rows, :]
        zero = jnp.zeros_like(qcat)
        qh = (jnp.where(lane2 < HEAD_DIM, qcat, zero), jnp.where(lane2 >= HEAD_DIM, qcat, zero))

        kmax = [jnp.maximum(kmax[head], knorm_ref[base[head] + qi]) for head in range(2)]
        reach = []
        for head in range(2):
            bound = (cfirst_ref[base[head] + qi] + 2.0 * NORM_SLACK * qnorm_ref[base[head] + qi] * kmax[head]
                     + SKIP_MARGIN * LOG2E)

            def reaches(n, head=head, bound=bound):
                t = jnp.maximum(qi - 1 - n, 0)
                return (n < qi) & (bound - clast_ref[base[head] + t] > 0.0)

            reach.append(lax.while_loop(reaches, lambda n: n + 1, 0))
        n_both, n_long = jnp.minimum(reach[0], reach[1]), jnp.maximum(reach[0], reach[1])

        def both_heads(j, count, first_mode="lazy"):
            return [(j - i, head, first_mode if i == 0 else "lazy") for i in range(count) for head in range(2)]

        if exact:
            unit_run(qh.__getitem__, both_heads(qi, 1, "diagonal"))

            def step(t, carry):
                unit_run(qh.__getitem__, both_heads(qi - 1 - t, 1, "exact"))
                return carry

            lax.fori_loop(0, n_long, step, 0)
        else:
            total = n_both + 1
            count, first_len = FOX_TRIP, 1
            while count >= 1:
                fits = total >= count
                if count < FOX_TRIP:
                    fits = fits & (total < 2 * count)

                @pl.when(fits)
                def _(count=count):
                    unit_run(qh.__getitem__, both_heads(qi, count, "diagonal_self"))

                first_len = jnp.where(fits, count, first_len)
                count //= 2
            run_tiles(qh.__getitem__, both_heads, qi - first_len, total - first_len)

            long_head = jnp.where(reach[1] > reach[0], 1, 0)
            q_long = jnp.where((lane2 < HEAD_DIM) == (long_head == 0), qcat, zero)
            run_tiles(lambda head: q_long, lambda j, count: [(j - i, long_head, "lazy") for i in range(count)],
                      qi - total, n_long - n_both)

        outs = []
        for head in range(2):
            acc, lp = acc_ref[head], l_ref[head]
            outs.append(acc / jnp.sum(lp, axis=-1, keepdims=True))
            if not exact:
                finite = (jnp.abs(acc) <= F32_BIG) & (lp <= F32_BIG)
                bad_ref[...] = jnp.maximum(bad_ref[...], jnp.where(finite, 0.0, 1.0))
        o_ref[0, q_rows, :] = jnp.where(head_a, outs[0], outs[1]).astype(o_ref.dtype)
        return kmax

    def sweep(exact):
        def body(qi, kmax):
            return tuple(query_tile(qi, kmax, exact))

        lax.fori_loop(0, n_tiles, body, (jnp.float32(0.0), jnp.float32(0.0)))

    bad_ref[...] = jnp.zeros_like(bad_ref)
    sweep(exact=False)

    @pl.when(jnp.max(bad_ref[...]) > 0.0)
    def _():
        sweep(exact=True)


def _fox(stats, qf, kf, vf, tq):
    b, s, _ = qf.shape
    smem = pl.BlockSpec(memory_space=pltpu.SMEM)
    return pl.pallas_call(
        functools.partial(_fox_body, tq=tq),
        grid=(b, N_PAIR),
        in_specs=[smem, smem, smem, smem,
                  pl.BlockSpec((1, s, 2 * LANES), lambda i, p: (i, 0, p)),
                  pl.BlockSpec((1, s, 2 * LANES), lambda i, p: (i, 0, p)),
                  pl.BlockSpec((1, s, LANES), lambda i, p: (i, 0, p))],
        out_specs=pl.BlockSpec((1, s, LANES), lambda i, p: (i, 0, p)),
        out_shape=jax.ShapeDtypeStruct((b, s, W_ATT), jnp.bfloat16),
        scratch_shapes=[pltpu.VMEM((2 * FOX_AHEAD, tq, tq), jnp.float32),
                        pltpu.VMEM((2 * FOX_AHEAD, tq, tq), jnp.bfloat16),
                        pltpu.VMEM((2, tq, LANES), jnp.float32),
                        pltpu.VMEM((2, tq, LANES), jnp.float32),
                        pltpu.VMEM((2, tq, LANES), jnp.float32),
                        pltpu.VMEM((tq, LANES), jnp.float32)],
        compiler_params=pltpu.CompilerParams(
            dimension_semantics=("arbitrary", "arbitrary"), vmem_limit_bytes=VMEM_LIMIT),
        name="fox_attention",
    )(*stats, qf, kf, vf)


def _band_bias_body(g_ref, o_ref):
    width = g_ref.shape[-1]
    base = jnp.broadcast_to(g_ref[0], (Q_BLOCK, width))
    toeplitz = pltpu.roll(base, 0, 1, stride=1, stride_axis=0)[:, :BAND]
    qrow = lax.broadcasted_iota(jnp.int32, (Q_BLOCK, BAND), 0)
    kcol = lax.broadcasted_iota(jnp.int32, (Q_BLOCK, BAND), 1)
    cq = qrow // CHUNK
    ck = kcol // CHUNK - LEFT_CHUNKS
    valid = (ck <= cq) & (ck >= cq - LEFT_CHUNKS)
    o_ref[0] = jnp.where(valid, toeplitz * LOG2E, NEG)


def _band_bias(g_ext):
    h, _, width = g_ext.shape
    return pl.pallas_call(
        _band_bias_body,
        grid=(h,),
        in_specs=[pl.BlockSpec((1, 1, width), lambda i: (i, 0, 0))],
        out_specs=pl.BlockSpec((1, Q_BLOCK, BAND), lambda i: (i, 0, 0)),
        out_shape=jax.ShapeDtypeStruct((h, Q_BLOCK, BAND), jnp.float32),
        name="band_bias",
    )(g_ext)


def _chunk_body(q_ref, k_ref, v_ref, bias_ref, o_ref, kpad_ref, vpad_ref, s_ref, p_ref, l_ref, *, tq):
    qi = pl.program_id(2)
    s_len = k_ref.shape[1]

    @pl.when(qi == 0)
    def _():
        zeros = jnp.zeros((PAD, LANES), kpad_ref.dtype)
        kpad_ref[:PAD, :] = zeros
        vpad_ref[:PAD, :] = zeros
        kpad_ref[PAD:PAD + s_len, :] = k_ref[0]
        vpad_ref[PAD:PAD + s_len, :] = v_ref[0]

    head_a = lax.broadcasted_iota(jnp.int32, (1, LANES), 1) < HEAD_DIM
    kcol = lax.broadcasted_iota(jnp.int32, (1, BAND), 1)

    def offsets(blk):
        r0 = pl.multiple_of(blk * Q_BLOCK, Q_BLOCK)
        return r0, pl.multiple_of(qi * tq + r0, Q_BLOCK)

    def scores(slot, blk):
        r0, p0 = offsets(blk)
        q2 = q_ref[0, pl.ds(r0, Q_BLOCK), :]
        zero = jnp.zeros_like(q2)
        qs = jnp.concatenate([jnp.where(head_a, q2, zero), jnp.where(head_a, zero, q2)], axis=0)
        s_ref[slot] = _dot_nt(qs, kpad_ref[pl.ds(p0, BAND), :])

    def softmax(slot, blk, masked):
        _, p0 = offsets(blk)
        for r in range(0, 2 * Q_BLOCK, CHUNK_ROWS):
            rows = slice(r, r + CHUNK_ROWS)
            s = s_ref[slot, rows, :] + bias_ref[0, rows, :]
            if masked:
                s = jnp.where(kcol + p0 >= PAD, s, NEG)
            p = jnp.exp2(s - jnp.max(s, axis=-1, keepdims=True))
            l_ref[slot, rows, :] = jnp.broadcast_to(jnp.sum(p, axis=-1, keepdims=True), (CHUNK_ROWS, LANES))
            p_ref[slot, rows, :] = p.astype(p_ref.dtype)

    def values(slot, blk):
        r0, p0 = offsets(blk)
        o = _dot(p_ref[slot], vpad_ref[pl.ds(p0, BAND), :]) / l_ref[slot]
        o_ref[0, pl.ds(r0, Q_BLOCK), :] = jnp.where(head_a, o[:Q_BLOCK], o[Q_BLOCK:]).astype(o_ref.dtype)

    def group(i, masked):
        first = i * CHUNK_GROUP
        scores(0, first)
        for n in range(CHUNK_GROUP):
            if n + 1 < CHUNK_GROUP:
                scores((n + 1) % 2, first + n + 1)
            softmax(n % 2, first + n, masked and n < PAD // Q_BLOCK)
            values(n % 2, first + n)

    def masked_group(i, carry):
        group(i, True)
        return carry

    def plain_group(i, carry):
        group(i, False)
        return carry

    group_rows = Q_BLOCK * CHUNK_GROUP
    n_groups = tq // group_rows
    n_masked = jnp.clip((PAD - qi * tq + group_rows - 1) // group_rows, 0, n_groups)
    lax.fori_loop(0, n_masked, masked_group, 0)
    lax.fori_loop(n_masked, n_groups, plain_group, 0)


def _chunk_attention(qc, kc, vc, bias, tq):
    b, s, _ = qc.shape
    assert tq % (CHUNK_GROUP * Q_BLOCK) == 0
    return pl.pallas_call(
        functools.partial(_chunk_body, tq=tq),
        grid=(b, N_PAIR, s // tq),
        in_specs=[pl.BlockSpec((1, tq, LANES), lambda i, p, j: (i, j, p)),
                  pl.BlockSpec((1, s, LANES), lambda i, p, j: (i, 0, p)),
                  pl.BlockSpec((1, s, LANES), lambda i, p, j: (i, 0, p)),
                  pl.BlockSpec((1, 2 * Q_BLOCK, BAND), lambda i, p, j: (p, 0, 0))],
        out_specs=pl.BlockSpec((1, tq, LANES), lambda i, p, j: (i, j, p)),
        out_shape=jax.ShapeDtypeStruct((b, s, W_ATT), jnp.bfloat16),
        scratch_shapes=[pltpu.VMEM((PAD + s, LANES), jnp.bfloat16),
                        pltpu.VMEM((PAD + s, LANES), jnp.bfloat16),
                        pltpu.VMEM((2, 2 * Q_BLOCK, BAND), jnp.float32),
                        pltpu.VMEM((2, 2 * Q_BLOCK, BAND), jnp.bfloat16),
                        pltpu.VMEM((2, 2 * Q_BLOCK, LANES), jnp.float32)],
        compiler_params=pltpu.CompilerParams(
            dimension_semantics=("arbitrary", "arbitrary", "arbitrary"), vmem_limit_bytes=VMEM_LIMIT),
        name="chunk_attention",
    )(qc, kc, vc, bias)


def _out_ffn_body(x_ref, oa_ref, ob_ref, gate_ref, wa_ref, wb_ref, wo_ref, g2_ref, wu_ref, wd_ref, gf_ref,
                  o_ref, *, ff_chunk, final):
    d = x_ref.shape[-1]
    ya = _dot(oa_ref[0], wa_ref[...])
    yb = _dot(ob_ref[0], wb_ref[...])
    ga = gate_ref[0, :, :d].astype(jnp.float32)
    gb = gate_ref[0, :, d:].astype(jnp.float32)
    merged = (ga * ya + gb * yb).astype(jnp.bfloat16)
    x1 = x_ref[0] + _dot(merged, wo_ref[...])
    h2 = _rms(x1, g2_ref[...]).astype(jnp.bfloat16)
    acc = x1
    for c0 in range(0, wu_ref.shape[1], ff_chunk):
        u = jnp.maximum(_dot(h2, wu_ref[:, c0:c0 + ff_chunk]), 0.0)
        acc = acc + _dot((u * u).astype(jnp.bfloat16), wd_ref[c0:c0 + ff_chunk, :])
    if final:
        acc = _rms(acc, gf_ref[...])
    o_ref[0] = acc


def _out_ffn(x, oa, ob, gates, wa, wb, wo, g2, wu, wd, gf, tm, final):
    b, s, d = x.shape
    tok = lambda w: pl.BlockSpec((1, tm, w), lambda i, j: (i, j, 0))
    return pl.pallas_call(
        functools.partial(_out_ffn_body, ff_chunk=min(1024, wu.shape[1]), final=final),
        grid=(b, s // tm),
        in_specs=[tok(d), tok(W_ATT), tok(W_ATT), tok(2 * d)]
                 + [_const_spec(a.shape) for a in (wa, wb, wo, g2, wu, wd, gf)],
        out_specs=tok(d),
        out_shape=jax.ShapeDtypeStruct((b, s, d), jnp.float32),
        compiler_params=pltpu.CompilerParams(
            dimension_semantics=("arbitrary", "arbitrary"), vmem_limit_bytes=VMEM_LIMIT),
        name="out_ffn",
    )(x, oa, ob, gates, wa, wb, wo, g2, wu, wd, gf)


def _aug_placement():
    eq = [[0.0] * W_ATT for _ in range(LANES)]
    ek = [[0.0] * W_ATT for _ in range(LANES)]
    one = 3 * H_FOX
    for h in range(H_FOX):
        for piece in range(3):
            eq[piece * H_FOX + h][HEAD_DIM * h + piece] = 1.0
            eq[one][HEAD_DIM * h + 3 + piece] = 1.0
            ek[one][HEAD_DIM * h + piece] = 1.0
            ek[piece * H_FOX + h][HEAD_DIM * h + 3 + piece] = -1.0
    return jnp.array(eq, jnp.bfloat16), jnp.array(ek, jnp.bfloat16)


def kernel(x, norm1, w_in, forget_bias, rel_bias, w_branch_a, w_branch_b, w_out, norm2, w_up, w_down, final_norm):
    b, s, d = x.shape
    depth = w_in.shape[0]
    bf16 = jnp.bfloat16
    tm = min(512, s)
    tqc = s
    eq, ek = _aug_placement()
    head_of_col = lax.broadcasted_iota(jnp.int32, (W_ATT, LANES), 0) // HEAD_DIM
    hsum = (head_of_col == lax.broadcasted_iota(jnp.int32, (W_ATT, LANES), 1)).astype(bf16)
    gf = final_norm.reshape(1, d)
    o = 3 * W_ATT + H_FOX
    for l in range(depth):
        w = w_in[l]
        wq = (w[:, :W_ATT] * (SCALE * LOG2E)).astype(bf16)
        wk = w[:, W_ATT:2 * W_ATT].astype(bf16)
        wv = w[:, 2 * W_ATT:3 * W_ATT].astype(bf16)
        wf = jnp.pad(w[:, 3 * W_ATT:o], ((0, 0), (0, LANES - H_FOX))).astype(bf16)
        wc = jnp.concatenate([w[:, o:o + W_ATT] * (SCALE * LOG2E), w[:, o + W_ATT:o + 3 * W_ATT]],
                             axis=1).astype(bf16)
        wg = w[:, o + 3 * W_ATT:].astype(bf16)
        bf = jnp.pad(forget_bias[l], (0, LANES - H_FOX)).reshape(1, LANES)
        qf, kf, vf, qc, kc, vc, gates, stat, wo, wu, wd = _in_proj(
            x, norm1[l].reshape(1, d), wq, wk, wv, wc, wg, wf, bf, eq, ek, hsum,
            (w_out, w_up, w_down), l, tm)

        stats = stat[:, :, :4, :H_FOX].transpose(2, 0, 3, 1).reshape(4, -1)
        o_a = _fox(tuple(stats), qf, kf, vf, tm)

        far = rel_bias[l][:, 2 * MAX_REL:]
        g_ext = jnp.concatenate([jnp.broadcast_to(far, (H_CHK, PAD - MAX_REL + 1)),
                                 rel_bias[l][:, 2 * MAX_REL - 1:0:-1],
                                 jnp.broadcast_to(far, (H_CHK, Q_BLOCK))], axis=1)
        bias = _band_bias(g_ext.reshape(H_CHK, 1, -1)).reshape(N_PAIR, 2 * Q_BLOCK, BAND)
        o_b = _chunk_attention(qc, kc, vc, bias, tqc)

        x = _out_ffn(x, o_a, o_b, gates, w_branch_a[l].astype(bf16), w_branch_b[l].astype(bf16),
                     wo, norm2[l].reshape(1, d), wu, wd, gf, tm, final=(l == depth - 1))
    return x
```

```python
import functools
import math

import jax
import jax.numpy as jnp
from jax import lax
from jax.experimental import pallas as pl
from jax.experimental.pallas import tpu as pltpu

HEAD_DIM = 64
H_FOX = 8
H_CHK = 8
N_PAIR = 4
W_ATT = H_FOX * HEAD_DIM
CHUNK = 64
Q_BLOCK = 128
LEFT_CHUNKS = 8
PAD = LEFT_CHUNKS * CHUNK
BAND = PAD + Q_BLOCK
MAX_REL = 128
EPS = 1e-6
NEG = -1e30
LANES = 128
SCALE = 1.0 / math.sqrt(HEAD_DIM)
LOG2E = math.log2(math.e)
F32_BIG = 3.0e38
PROJ_SPLIT = 2
FOX_ROWS = 32
FOX_TRIP = 8
FOX_AHEAD = 2
CHUNK_ROWS = 32
CHUNK_GROUP = 64
SKIP_MARGIN = 30.0
NORM_SLACK = 1.02
VMEM_LIMIT = 60 * 1024 * 1024

_NT = (((1,), (1,)), ((), ()))


def _dot(a, b):
    return jnp.dot(a, b, preferred_element_type=jnp.float32)


def _dot_nt(a, b):
    return lax.dot_general(a, b, _NT, preferred_element_type=jnp.float32)


def _rms(x, g):
    ms = jnp.mean(x * x, axis=-1, keepdims=True)
    return x * lax.rsqrt(ms + EPS) * g


def _const_spec(shape):
    nd = len(shape)
    return pl.BlockSpec(shape, lambda *_: (0,) * nd, pipeline_mode=pl.Buffered(1))


def _in_proj_body(x_ref, g_ref, wq_ref, wk_ref, wv_ref, wc_ref, wg_ref, wf_ref, bf_ref, eq_ref, ek_ref, hsum_ref,
                  wo32_ref, wu32_ref, wd32_ref,
                  qf_ref, kf_ref, vf_ref, qc_ref, kc_ref, vc_ref, gate_ref, stat_ref, wo_ref, wu_ref, wd_ref,
                  carry_ref):
    tm = x_ref.shape[1]
    th = tm // PROJ_SPLIT

    wo_ref[...] = wo32_ref[...].astype(wo_ref.dtype)
    wu_ref[...] = wu32_ref[...].astype(wu_ref.dtype)
    wd_ref[...] = wd32_ref[...].astype(wd_ref.dtype)

    @pl.when(pl.program_id(1) == 0)
    def _():
        carry_ref[...] = jnp.zeros_like(carry_ref)

    lane = lax.broadcasted_iota(jnp.int32, (th, LANES), 1)
    row = lax.broadcasted_iota(jnp.int32, (th, LANES), 0)

    def max_norm(t):
        t = t.astype(jnp.float32)
        sq = _dot((t * t).astype(jnp.bfloat16), hsum_ref[...])
        return jnp.max(sq, axis=0, keepdims=True)

    c_first, c_last, q_sq, k_sq = None, None, None, None
    for part in range(PROJ_SPLIT):
        rows = slice(part * th, (part + 1) * th)
        h = _rms(x_ref[0, rows, :], g_ref[...]).astype(jnp.bfloat16)

        logf = jax.nn.log_sigmoid(_dot(h, wf_ref[...]) + bf_ref[...]) * LOG2E
        c = jnp.where(lane < H_FOX, logf, 0.0)
        k = 1
        while k < th:
            c = c + jnp.where(row >= k, pltpu.roll(c, k, 0), 0.0)
            k *= 2
        c = c + carry_ref[...]
        carry_ref[...] = c[th - 1:th, :]
        c_first = c[0:1, :] if part == 0 else c_first
        c_last = c[th - 1:th, :]

        hi = c.astype(jnp.bfloat16).astype(jnp.float32)
        r1 = c - hi
        mid = r1.astype(jnp.bfloat16).astype(jnp.float32)
        lo = (r1 - mid).astype(jnp.bfloat16).astype(jnp.float32)
        pieces = hi + pltpu.roll(mid, H_FOX, 1) + pltpu.roll(lo, 2 * H_FOX, 1)
        pieces = jnp.where(lane == 3 * H_FOX, 1.0, pieces).astype(jnp.bfloat16)
        aug_q = _dot(pieces, eq_ref[...]).astype(jnp.bfloat16)
        aug_k = _dot(pieces, ek_ref[...]).astype(jnp.bfloat16)

        q = _dot(h, wq_ref[...]).astype(jnp.bfloat16)
        kk = _dot(h, wk_ref[...]).astype(jnp.bfloat16)
        for p in range(N_PAIR):
            src = slice(p * LANES, (p + 1) * LANES)
            qf_ref[0, rows, 2 * p * LANES:(2 * p + 1) * LANES] = q[:, src]
            qf_ref[0, rows, (2 * p + 1) * LANES:(2 * p + 2) * LANES] = aug_q[:, src]
            kf_ref[0, rows, 2 * p * LANES:(2 * p + 1) * LANES] = kk[:, src]
            kf_ref[0, rows, (2 * p + 1) * LANES:(2 * p + 2) * LANES] = aug_k[:, src]
        vf_ref[0, rows, :] = _dot(h, wv_ref[...]).astype(jnp.bfloat16)
        q_sq = max_norm(q) if part == 0 else jnp.maximum(q_sq, max_norm(q))
        k_sq = max_norm(kk) if part == 0 else jnp.maximum(k_sq, max_norm(kk))

        pc = _dot(h, wc_ref[...])
        qc_ref[0, rows, :] = pc[:, :W_ATT].astype(jnp.bfloat16)
        kc_ref[0, rows, :] = pc[:, W_ATT:2 * W_ATT].astype(jnp.bfloat16)
        vc_ref[0, rows, :] = pc[:, 2 * W_ATT:].astype(jnp.bfloat16)
        gate_ref[0, rows, :] = jax.nn.sigmoid(_dot(h, wg_ref[...])).astype(jnp.bfloat16)

    sub = lax.broadcasted_iota(jnp.int32, (8, LANES), 0)
    stat = jnp.where(sub == 0, c_first, 0.0)
    stat = jnp.where(sub == 1, c_last, stat)
    stat = jnp.where(sub == 2, jnp.sqrt(q_sq), stat)
    stat_ref[0, 0] = jnp.where(sub == 3, jnp.sqrt(k_sq), stat)


def _in_proj(x, g, wq, wk, wv, wc, wg, wf, bf, eq, ek, hsum, to_cast, layer, tm):
    b, s, d = x.shape
    n_steps = b * (s // tm)
    tok = lambda width: pl.BlockSpec((1, tm, width), lambda i, j: (i, j, 0))
    strip_in = lambda w: pl.BlockSpec((None, w.shape[1] // n_steps, w.shape[2]),
                                      lambda i, j: (layer, i * (s // tm) + j, 0))
    strip_out = lambda w: pl.BlockSpec((w.shape[1] // n_steps, w.shape[2]), lambda i, j: (i * (s // tm) + j, 0))
    bf16 = jnp.bfloat16
    assert all(w.shape[1] % (16 * n_steps) == 0 for w in to_cast)
    out_shape = [jax.ShapeDtypeStruct((b, s, width), bf16)
                 for width in (2 * W_ATT, 2 * W_ATT, W_ATT, W_ATT, W_ATT, W_ATT, 2 * d)]
    out_shape.append(jax.ShapeDtypeStruct((b, s // tm, 8, LANES), jnp.float32))
    out_shape += [jax.ShapeDtypeStruct(w.shape[1:], bf16) for w in to_cast]
    consts = (g, wq, wk, wv, wc, wg, wf, bf, eq, ek, hsum)
    return pl.pallas_call(
        _in_proj_body,
        grid=(b, s // tm),
        in_specs=[tok(d)] + [_const_spec(a.shape) for a in consts] + [strip_in(w) for w in to_cast],
        out_specs=[tok(sh.shape[-1]) for sh in out_shape[:7]]
                  + [pl.BlockSpec((1, 1, 8, LANES), lambda i, j: (i, j, 0, 0))]
                  + [strip_out(w) for w in to_cast],
        out_shape=out_shape,
        scratch_shapes=[pltpu.VMEM((1, LANES), jnp.float32)],
        compiler_params=pltpu.CompilerParams(
            dimension_semantics=("arbitrary", "arbitrary"), vmem_limit_bytes=VMEM_LIMIT),
        name="in_proj",
    )(x, *consts, *to_cast)


def _fox_body(cfirst_ref, clast_ref, qnorm_ref, knorm_ref, q_ref, k_ref, v_ref, o_ref,
              s_ref, p_ref, m_ref, l_ref, acc_ref, bad_ref, *, tq):
    n_tiles = q_ref.shape[1] // tq
    lane2 = lax.broadcasted_iota(jnp.int32, (1, 2 * LANES), 1) % LANES
    head_a = lax.broadcasted_iota(jnp.int32, (1, LANES), 1) < HEAD_DIM
    base = [(pl.program_id(0) * H_FOX + 2 * pl.program_id(1) + head) * n_tiles for head in range(2)]

    def key_rows(j):
        return pl.ds(pl.multiple_of(j * tq, tq), tq)

    def scores(slot, q_head, j):
        s_ref[slot] = _dot_nt(q_head, k_ref[0, key_rows(j), :])

    def softmax(slot, head, mode):
        for r in range(0, tq, FOX_ROWS):
            rows = slice(r, r + FOX_ROWS)
            s = s_ref[slot, rows, :]
            if mode in ("diagonal", "diagonal_self"):
                row = lax.broadcasted_iota(jnp.int32, (FOX_ROWS, tq), 0) + r
                col = lax.broadcasted_iota(jnp.int32, (FOX_ROWS, tq), 1)
                if mode == "diagonal":
                    s = jnp.where(col <= row, s, NEG)
                    m = jnp.max(s, axis=-1, keepdims=True)
                else:
                    own = slice(r // LANES * LANES, (r // LANES + 1) * LANES)
                    lane = lax.broadcasted_iota(jnp.int32, (FOX_ROWS, LANES), 1)
                    sub = lax.broadcasted_iota(jnp.int32, (FOX_ROWS, LANES), 0)
                    m = jnp.sum(jnp.where(lane == sub + r % LANES, s[:, own], 0.0), axis=-1, keepdims=True)
                    s = jnp.where(col <= row, s, NEG)
                m = jnp.broadcast_to(m, (FOX_ROWS, LANES))
                m_ref[head, rows, :] = m
            elif mode == "exact":
                m_old = m_ref[head, rows, :]
                m = jnp.maximum(m_old, jnp.max(s, axis=-1, keepdims=True))
                alpha = jnp.exp2(m_old - m)
                m_ref[head, rows, :] = m
            else:
                m = m_ref[head, rows, :]
            lsum = None
            for g in range(tq // LANES):
                cols = slice(g * LANES, (g + 1) * LANES)
                pg = jnp.exp2(s[:, cols] - m)
                p_ref[slot, rows, cols] = pg.astype(p_ref.dtype)
                lsum = pg if lsum is None else lsum + pg
            if mode in ("diagonal", "diagonal_self"):
                l_ref[head, rows, :] = lsum
            elif mode == "exact":
                l_ref[head, rows, :] = l_ref[head, rows, :] * alpha + lsum
                acc_ref[head, rows, :] = acc_ref[head, rows, :] * alpha
            else:
                l_ref[head, rows, :] = l_ref[head, rows, :] + lsum

    def values(slot, head, j, mode):
        pv = _dot(p_ref[slot], v_ref[0, key_rows(j), :])
        acc_ref[head] = pv if mode in ("diagonal", "diagonal_self") else acc_ref[head] + pv

    def unit_run(q_of, units):
        n_slots = s_ref.shape[0]
        for n in range(min(FOX_AHEAD, len(units))):
            scores(n, q_of(units[n][1]), units[n][0])
        for n, (tile, head, mode) in enumerate(units):
            softmax(n % n_slots, head, mode)
            if n + FOX_AHEAD < len(units):
                scores((n + FOX_AHEAD) % n_slots, q_of(units[n + FOX_AHEAD][1]), units[n + FOX_AHEAD][0])
            values(n % n_slots, head, tile, mode)

    def run_tiles(q_of, make_units, first, total):
        def full(t, carry):
            unit_run(q_of, make_units(first - FOX_TRIP * t, FOX_TRIP))
            return carry

        n_full = total // FOX_TRIP
        lax.fori_loop(0, n_full, full, 0)
        done = n_full * FOX_TRIP
        count = FOX_TRIP // 2
        while count >= 1:
            fits = (total - done) >= count

            @pl.when(fits)
            def _(start=first - done, count=count):
                unit_run(q_of, make_units(start, count))

            done = jnp.where(fits, done + count, done)
            count //= 2

    def query_tile(qi, kmax, exact):
        q_rows = pl.ds(pl.multiple_of(qi * tq, tq), tq)
        qcat = q_ref[0, q_rows, :]
        zero = jnp.zeros_like(qcat)
        qh = (jnp.where(lane2 < HEAD_DIM, qcat, zero), jnp.where(lane2 >= HEAD_DIM, qcat, zero))

        kmax = [jnp.maximum(kmax[head], knorm_ref[base[head] + qi]) for head in range(2)]
        reach = []
        for head in range(2):
            bound = (cfirst_ref[base[head] + qi] + 2.0 * NORM_SLACK * qnorm_ref[base[head] + qi] * kmax[head]
                     + SKIP_MARGIN * LOG2E)

            def reaches(n, head=head, bound=bound):
                t = jnp.maximum(qi - 1 - n, 0)
                return (n < qi) & (bound - clast_ref[base[head] + t] > 0.0)

            reach.append(lax.while_loop(reaches, lambda n: n + 1, 0))
        n_both, n_long = jnp.minimum(reach[0], reach[1]), jnp.maximum(reach[0], reach[1])

        def both_heads(j, count, first_mode="lazy"):
            return [(j - i, head, first_mode if i == 0 else "lazy") for i in range(count) for head in range(2)]

        if exact:
            unit_run(qh.__getitem__, both_heads(qi, 1, "diagonal"))

            def step(t, carry):
                unit_run(qh.__getitem__, both_heads(qi - 1 - t, 1, "exact"))
                return carry

            lax.fori_loop(0, n_long, step, 0)
        else:
            total = n_both + 1
            count, first_len = FOX_TRIP, 1
            while count >= 1:
                fits = total >= count
                if count < FOX_TRIP:
                    fits = fits & (total < 2 * count)

                @pl.when(fits)
                def _(count=count):
                    unit_run(qh.__getitem__, both_heads(qi, count, "diagonal_self"))

                first_len = jnp.where(fits, count, first_len)
                count //= 2
            run_tiles(qh.__getitem__, both_heads, qi - first_len, total - first_len)

            long_head = jnp.where(reach[1] > reach[0], 1, 0)
            q_long = jnp.where((lane2 < HEAD_DIM) == (long_head == 0), qcat, zero)
            run_tiles(lambda head: q_long, lambda j, count: [(j - i, long_head, "lazy") for i in range(count)],
                      qi - total, n_long - n_both)

        outs = []
        for head in range(2):
            acc, lp = acc_ref[head], l_ref[head]
            outs.append(acc / jnp.sum(lp, axis=-1, keepdims=True))
            if not exact:
                finite = (jnp.abs(acc) <= F32_BIG) & (lp <= F32_BIG)
                bad_ref[...] = jnp.maximum(bad_ref[...], jnp.where(finite, 0.0, 1.0))
        o_ref[0, q_rows, :] = jnp.where(head_a, outs[0], outs[1]).astype(o_ref.dtype)
        return kmax

    def sweep(exact):
        def body(qi, kmax):
            return tuple(query_tile(qi, kmax, exact))

        lax.fori_loop(0, n_tiles, body, (jnp.float32(0.0), jnp.float32(0.0)))

    bad_ref[...] = jnp.zeros_like(bad_ref)
    sweep(exact=False)

    @pl.when(jnp.max(bad_ref[...]) > 0.0)
    def _():
        sweep(exact=True)


def _fox(stats, qf, kf, vf, tq):
    b, s, _ = qf.shape
    smem = pl.BlockSpec(memory_space=pltpu.SMEM)
    return pl.pallas_call(
        functools.partial(_fox_body, tq=tq),
        grid=(b, N_PAIR),
        in_specs=[smem, smem, smem, smem,
                  pl.BlockSpec((1, s, 2 * LANES), lambda i, p: (i, 0, p)),
                  pl.BlockSpec((1, s, 2 * LANES), lambda i, p: (i, 0, p)),
                  pl.BlockSpec((1, s, LANES), lambda i, p: (i, 0, p))],
        out_specs=pl.BlockSpec((1, s, LANES), lambda i, p: (i, 0, p)),
        out_shape=jax.ShapeDtypeStruct((b, s, W_ATT), jnp.bfloat16),
        scratch_shapes=[pltpu.VMEM((2 * FOX_AHEAD, tq, tq), jnp.float32),
                        pltpu.VMEM((2 * FOX_AHEAD, tq, tq), jnp.bfloat16),
                        pltpu.VMEM((2, tq, LANES), jnp.float32),
                        pltpu.VMEM((2, tq, LANES), jnp.float32),
                        pltpu.VMEM((2, tq, LANES), jnp.float32),
                        pltpu.VMEM((tq, LANES), jnp.float32)],
        compiler_params=pltpu.CompilerParams(
            dimension_semantics=("arbitrary", "arbitrary"), vmem_limit_bytes=VMEM_LIMIT),
        name="fox_attention",
    )(*stats, qf, kf, vf)


def _band_bias_body(g_ref, o_ref):
    width = g_ref.shape[-1]
    base = jnp.broadcast_to(g_ref[0], (Q_BLOCK, width))
    toeplitz = pltpu.roll(base, 0, 1, stride=1, stride_axis=0)[:, :BAND]
    qrow = lax.broadcasted_iota(jnp.int32, (Q_BLOCK, BAND), 0)
    kcol = lax.broadcasted_iota(jnp.int32, (Q_BLOCK, BAND), 1)
    cq = qrow // CHUNK
    ck = kcol // CHUNK - LEFT_CHUNKS
    valid = (ck <= cq) & (ck >= cq - LEFT_CHUNKS)
    o_ref[0] = jnp.where(valid, toeplitz * LOG2E, NEG)


def _band_bias(g_ext):
    h, _, width = g_ext.shape
    return pl.pallas_call(
        _band_bias_body,
        grid=(h,),
        in_specs=[pl.BlockSpec((1, 1, width), lambda i: (i, 0, 0))],
        out_specs=pl.BlockSpec((1, Q_BLOCK, BAND), lambda i: (i, 0, 0)),
        out_shape=jax.ShapeDtypeStruct((h, Q_BLOCK, BAND), jnp.float32),
        name="band_bias",
    )(g_ext)


def _chunk_body(q_ref, k_ref, v_ref, bias_ref, o_ref, kpad_ref, vpad_ref, s_ref, p_ref, l_ref, *, tq):
    qi = pl.program_id(2)
    s_len = k_ref.shape[1]

    @pl.when(qi == 0)
    def _():
        zeros = jnp.zeros((PAD, LANES), kpad_ref.dtype)
        kpad_ref[:PAD, :] = zeros
        vpad_ref[:PAD, :] = zeros
        kpad_ref[PAD:PAD + s_len, :] = k_ref[0]
        vpad_ref[PAD:PAD + s_len, :] = v_ref[0]

    head_a = lax.broadcasted_iota(jnp.int32, (1, LANES), 1) < HEAD_DIM
    kcol = lax.broadcasted_iota(jnp.int32, (1, BAND), 1)

    def offsets(blk):
        r0 = pl.multiple_of(blk * Q_BLOCK, Q_BLOCK)
        return r0, pl.multiple_of(qi * tq + r0, Q_BLOCK)

    def scores(slot, blk):
        r0, p0 = offsets(blk)
        q2 = q_ref[0, pl.ds(r0, Q_BLOCK), :]
        zero = jnp.zeros_like(q2)
        qs = jnp.concatenate([jnp.where(head_a, q2, zero), jnp.where(head_a, zero, q2)], axis=0)
        s_ref[slot] = _dot_nt(qs, kpad_ref[pl.ds(p0, BAND), :])

    def softmax(slot, blk, masked):
        _, p0 = offsets(blk)
        for r in range(0, 2 * Q_BLOCK, CHUNK_ROWS):
            rows = slice(r, r + CHUNK_ROWS)
            s = s_ref[slot, rows, :] + bias_ref[0, rows, :]
            if masked:
                s = jnp.where(kcol + p0 >= PAD, s, NEG)
            p = jnp.exp2(s - jnp.max(s, axis=-1, keepdims=True))
            l_ref[slot, rows, :] = jnp.broadcast_to(jnp.sum(p, axis=-1, keepdims=True), (CHUNK_ROWS, LANES))
            p_ref[slot, rows, :] = p.astype(p_ref.dtype)

    def values(slot, blk):
        r0, p0 = offsets(blk)
        o = _dot(p_ref[slot], vpad_ref[pl.ds(p0, BAND), :]) / l_ref[slot]
        o_ref[0, pl.ds(r0, Q_BLOCK), :] = jnp.where(head_a, o[:Q_BLOCK], o[Q_BLOCK:]).astype(o_ref.dtype)

    group_blocks = min(CHUNK_GROUP, tq // Q_BLOCK)

    def group(i, masked):
        first = i * group_blocks
        scores(0, first)
        for n in range(group_blocks):
            if n + 1 < group_blocks:
                scores((n + 1) % 2, first + n + 1)
            softmax(n % 2, first + n, masked and n < PAD // Q_BLOCK)
            values(n % 2, first + n)

    def masked_group(i, carry):
        group(i, True)
        return carry

    def plain_group(i, carry):
        group(i, False)
        return carry

    group_rows = Q_BLOCK * group_blocks
    n_groups = tq // group_rows
    n_masked = jnp.clip((PAD - qi * tq + group_rows - 1) // group_rows, 0, n_groups)
    lax.fori_loop(0, n_masked, masked_group, 0)
    lax.fori_loop(n_masked, n_groups, plain_group, 0)


def _chunk_attention(qc, kc, vc, bias, tq):
    b, s, _ = qc.shape
    assert tq % (min(CHUNK_GROUP, tq // Q_BLOCK) * Q_BLOCK) == 0
    return pl.pallas_call(
        functools.partial(_chunk_body, tq=tq),
        grid=(b, N_PAIR, s // tq),
        in_specs=[pl.BlockSpec((1, tq, LANES), lambda i, p, j: (i, j, p)),
                  pl.BlockSpec((1, s, LANES), lambda i, p, j: (i, 0, p)),
                  pl.BlockSpec((1, s, LANES), lambda i, p, j: (i, 0, p)),
                  pl.BlockSpec((1, 2 * Q_BLOCK, BAND), lambda i, p, j: (p, 0, 0))],
        out_specs=pl.BlockSpec((1, tq, LANES), lambda i, p, j: (i, j, p)),
        out_shape=jax.ShapeDtypeStruct((b, s, W_ATT), jnp.bfloat16),
        scratch_shapes=[pltpu.VMEM((PAD + s, LANES), jnp.bfloat16),
                        pltpu.VMEM((PAD + s, LANES), jnp.bfloat16),
                        pltpu.VMEM((2, 2 * Q_BLOCK, BAND), jnp.float32),
                        pltpu.VMEM((2, 2 * Q_BLOCK, BAND), jnp.bfloat16),
                        pltpu.VMEM((2, 2 * Q_BLOCK, LANES), jnp.float32)],
        compiler_params=pltpu.CompilerParams(
            dimension_semantics=("arbitrary", "arbitrary", "arbitrary"), vmem_limit_bytes=VMEM_LIMIT),
        name="chunk_attention",
    )(qc, kc, vc, bias)


def _out_ffn_body(x_ref, oa_ref, ob_ref, gate_ref, wa_ref, wb_ref, wo_ref, g2_ref, wu_ref, wd_ref, gf_ref,
                  o_ref, *, ff_chunk, final):
    d = x_ref.shape[-1]
    ya = _dot(oa_ref[0], wa_ref[...])
    yb = _dot(ob_ref[0], wb_ref[...])
    ga = gate_ref[0, :, :d].astype(jnp.float32)
    gb = gate_ref[0, :, d:].astype(jnp.float32)
    merged = (ga * ya + gb * yb).astype(jnp.bfloat16)
    x1 = x_ref[0] + _dot(merged, wo_ref[...])
    h2 = _rms(x1, g2_ref[...]).astype(jnp.bfloat16)
    acc = x1
    for c0 in range(0, wu_ref.shape[1], ff_chunk):
        u = jnp.maximum(_dot(h2, wu_ref[:, c0:c0 + ff_chunk]), 0.0)
        acc = acc + _dot((u * u).astype(jnp.bfloat16), wd_ref[c0:c0 + ff_chunk, :])
    if final:
        acc = _rms(acc, gf_ref[...])
    o_ref[0] = acc


def _out_ffn(x, oa, ob, gates, wa, wb, wo, g2, wu, wd, gf, tm, final):
    b, s, d = x.shape
    tok = lambda w: pl.BlockSpec((1, tm, w), lambda i, j: (i, j, 0))
    return pl.pallas_call(
        functools.partial(_out_ffn_body, ff_chunk=min(1024, wu.shape[1]), final=final),
        grid=(b, s // tm),
        in_specs=[tok(d), tok(W_ATT), tok(W_ATT), tok(2 * d)]
                 + [_const_spec(a.shape) for a in (wa, wb, wo, g2, wu, wd, gf)],
        out_specs=tok(d),
        out_shape=jax.ShapeDtypeStruct((b, s, d), jnp.float32),
        compiler_params=pltpu.CompilerParams(
            dimension_semantics=("arbitrary", "arbitrary"), vmem_limit_bytes=VMEM_LIMIT),
        name="out_ffn",
    )(x, oa, ob, gates, wa, wb, wo, g2, wu, wd, gf)


def _aug_placement():
    eq = [[0.0] * W_ATT for _ in range(LANES)]
    ek = [[0.0] * W_ATT for _ in range(LANES)]
    one = 3 * H_FOX
    for h in range(H_FOX):
        for piece in range(3):
            eq[piece * H_FOX + h][HEAD_DIM * h + piece] = 1.0
            eq[one][HEAD_DIM * h + 3 + piece] = 1.0
            ek[one][HEAD_DIM * h + piece] = 1.0
            ek[piece * H_FOX + h][HEAD_DIM * h + 3 + piece] = -1.0
    return jnp.array(eq, jnp.bfloat16), jnp.array(ek, jnp.bfloat16)


def kernel(x, norm1, w_in, forget_bias, rel_bias, w_branch_a, w_branch_b, w_out, norm2, w_up, w_down, final_norm):
    b, s, d = x.shape
    depth = w_in.shape[0]
    bf16 = jnp.bfloat16
    tm = min(512, s)
    tqc = s
    eq, ek = _aug_placement()
    head_of_col = lax.broadcasted_iota(jnp.int32, (W_ATT, LANES), 0) // HEAD_DIM
    hsum = (head_of_col == lax.broadcasted_iota(jnp.int32, (W_ATT, LANES), 1)).astype(bf16)
    gf = final_norm.reshape(1, d)
    o = 3 * W_ATT + H_FOX
    for l in range(depth):
        w = w_in[l]
        wq = (w[:, :W_ATT] * (SCALE * LOG2E)).astype(bf16)
        wk = w[:, W_ATT:2 * W_ATT].astype(bf16)
        wv = w[:, 2 * W_ATT:3 * W_ATT].astype(bf16)
        wf = jnp.pad(w[:, 3 * W_ATT:o], ((0, 0), (0, LANES - H_FOX))).astype(bf16)
        wc = jnp.concatenate([w[:, o:o + W_ATT] * (SCALE * LOG2E), w[:, o + W_ATT:o + 3 * W_ATT]],
                             axis=1).astype(bf16)
        wg = w[:, o + 3 * W_ATT:].astype(bf16)
        bf = jnp.pad(forget_bias[l], (0, LANES - H_FOX)).reshape(1, LANES)
        qf, kf, vf, qc, kc, vc, gates, stat, wo, wu, wd = _in_proj(
            x, norm1[l].reshape(1, d), wq, wk, wv, wc, wg, wf, bf, eq, ek, hsum,
            (w_out, w_up, w_down), l, tm)

        stats = stat[:, :, :4, :H_FOX].transpose(2, 0, 3, 1).reshape(4, -1)
        o_a = _fox(tuple(stats), qf, kf, vf, tm)

        far = rel_bias[l][:, 2 * MAX_REL:]
        g_ext = jnp.concatenate([jnp.broadcast_to(far, (H_CHK, PAD - MAX_REL + 1)),
                                 rel_bias[l][:, 2 * MAX_REL - 1:0:-1],
                                 jnp.broadcast_to(far, (H_CHK, Q_BLOCK))], axis=1)
        bias = _band_bias(g_ext.reshape(H_CHK, 1, -1)).reshape(N_PAIR, 2 * Q_BLOCK, BAND)
        o_b = _chunk_attention(qc, kc, vc, bias, tqc)

        x = _out_ffn(x, o_a, o_b, gates, w_branch_a[l].astype(bf16), w_branch_b[l].astype(bf16),
                     wo, norm2[l].reshape(1, d), wu, wd, gf, tm, final=(l == depth - 1))
    return x
```

```python
import functools
import math

import jax
import jax.numpy as jnp
from jax import lax
from jax.experimental import pallas as pl
from jax.experimental.pallas import tpu as pltpu

HEAD_DIM = 64
H_FOX = 8
H_CHK = 8
N_PAIR = 4
W_ATT = H_FOX * HEAD_DIM
CHUNK = 64
Q_BLOCK = 128
LEFT_CHUNKS = 8
PAD = LEFT_CHUNKS * CHUNK
BAND = PAD + Q_BLOCK
MAX_REL = 128
EPS = 1e-6
NEG = -1e30
LANES = 128
SCALE = 1.0 / math.sqrt(HEAD_DIM)
LOG2E = math.log2(math.e)
F32_BIG = 3.0e38
PROJ_SPLIT = 2
FOX_ROWS = 32
FOX_TRIP = 8
FOX_AHEAD = 2
CHUNK_ROWS = 32
CHUNK_GROUP = 32
SKIP_MARGIN = 30.0
NORM_SLACK = 1.02
VMEM_LIMIT = 60 * 1024 * 1024

_NT = (((1,), (1,)), ((), ()))


def _dot(a, b):
    return jnp.dot(a, b, preferred_element_type=jnp.float32)


def _dot_nt(a, b):
    return lax.dot_general(a, b, _NT, preferred_element_type=jnp.float32)


def _rms(x, g):
    ms = jnp.mean(x * x, axis=-1, keepdims=True)
    return x * lax.rsqrt(ms + EPS) * g


def _const_spec(shape):
    nd = len(shape)
    return pl.BlockSpec(shape, lambda *_: (0,) * nd, pipeline_mode=pl.Buffered(1))


def _in_proj_body(x_ref, g_ref, wq_ref, wk_ref, wv_ref, wc_ref, wg_ref, wf_ref, bf_ref, eq_ref, ek_ref, hsum_ref,
                  wo32_ref, wu32_ref, wd32_ref,
                  qf_ref, kf_ref, vf_ref, qc_ref, kc_ref, vc_ref, gate_ref, stat_ref, wo_ref, wu_ref, wd_ref,
                  carry_ref):
    tm = x_ref.shape[1]
    th = tm // PROJ_SPLIT

    wo_ref[...] = wo32_ref[...].astype(wo_ref.dtype)
    wu_ref[...] = wu32_ref[...].astype(wu_ref.dtype)
    wd_ref[...] = wd32_ref[...].astype(wd_ref.dtype)

    @pl.when(pl.program_id(1) == 0)
    def _():
        carry_ref[...] = jnp.zeros_like(carry_ref)

    lane = lax.broadcasted_iota(jnp.int32, (th, LANES), 1)
    row = lax.broadcasted_iota(jnp.int32, (th, LANES), 0)

    def max_norm(t):
        t = t.astype(jnp.float32)
        sq = _dot((t * t).astype(jnp.bfloat16), hsum_ref[...])
        return jnp.max(sq, axis=0, keepdims=True)

    c_first, c_last, q_sq, k_sq = None, None, None, None
    for part in range(PROJ_SPLIT):
        rows = slice(part * th, (part + 1) * th)
        h = _rms(x_ref[0, rows, :], g_ref[...]).astype(jnp.bfloat16)

        logf = jax.nn.log_sigmoid(_dot(h, wf_ref[...]) + bf_ref[...]) * LOG2E
        c = jnp.where(lane < H_FOX, logf, 0.0)
        k = 1
        while k < th:
            c = c + jnp.where(row >= k, pltpu.roll(c, k, 0), 0.0)
            k *= 2
        c = c + carry_ref[...]
        carry_ref[...] = c[th - 1:th, :]
        c_first = c[0:1, :] if part == 0 else c_first
        c_last = c[th - 1:th, :]

        hi = c.astype(jnp.bfloat16).astype(jnp.float32)
        r1 = c - hi
        mid = r1.astype(jnp.bfloat16).astype(jnp.float32)
        lo = (r1 - mid).astype(jnp.bfloat16).astype(jnp.float32)
        pieces = hi + pltpu.roll(mid, H_FOX, 1) + pltpu.roll(lo, 2 * H_FOX, 1)
        pieces = jnp.where(lane == 3 * H_FOX, 1.0, pieces).astype(jnp.bfloat16)
        aug_q = _dot(pieces, eq_ref[...]).astype(jnp.bfloat16)
        aug_k = _dot(pieces, ek_ref[...]).astype(jnp.bfloat16)

        q = _dot(h, wq_ref[...]).astype(jnp.bfloat16)
        kk = _dot(h, wk_ref[...]).astype(jnp.bfloat16)
        for p in range(N_PAIR):
            src = slice(p * LANES, (p + 1) * LANES)
            qf_ref[0, rows, 2 * p * LANES:(2 * p + 1) * LANES] = q[:, src]
            qf_ref[0, rows, (2 * p + 1) * LANES:(2 * p + 2) * LANES] = aug_q[:, src]
            kf_ref[0, rows, 2 * p * LANES:(2 * p + 1) * LANES] = kk[:, src]
            kf_ref[0, rows, (2 * p + 1) * LANES:(2 * p + 2) * LANES] = aug_k[:, src]
        vf_ref[0, rows, :] = _dot(h, wv_ref[...]).astype(jnp.bfloat16)
        q_sq = max_norm(q) if part == 0 else jnp.maximum(q_sq, max_norm(q))
        k_sq = max_norm(kk) if part == 0 else jnp.maximum(k_sq, max_norm(kk))

        pc = _dot(h, wc_ref[...])
        qc_ref[0, rows, :] = pc[:, :W_ATT].astype(jnp.bfloat16)
        kc_ref[0, rows, :] = pc[:, W_ATT:2 * W_ATT].astype(jnp.bfloat16)
        vc_ref[0, rows, :] = pc[:, 2 * W_ATT:].astype(jnp.bfloat16)
        gate_ref[0, rows, :] = jax.nn.sigmoid(_dot(h, wg_ref[...])).astype(jnp.bfloat16)

    sub = lax.broadcasted_iota(jnp.int32, (8, LANES), 0)
    stat = jnp.where(sub == 0, c_first, 0.0)
    stat = jnp.where(sub == 1, c_last, stat)
    stat = jnp.where(sub == 2, jnp.sqrt(q_sq), stat)
    stat_ref[0, 0] = jnp.where(sub == 3, jnp.sqrt(k_sq), stat)


def _in_proj(x, g, wq, wk, wv, wc, wg, wf, bf, eq, ek, hsum, to_cast, layer, tm):
    b, s, d = x.shape
    n_steps = b * (s // tm)
    tok = lambda width: pl.BlockSpec((1, tm, width), lambda i, j: (i, j, 0))
    strip_in = lambda w: pl.BlockSpec((None, w.shape[1] // n_steps, w.shape[2]),
                                      lambda i, j: (layer, i * (s // tm) + j, 0))
    strip_out = lambda w: pl.BlockSpec((w.shape[1] // n_steps, w.shape[2]), lambda i, j: (i * (s // tm) + j, 0))
    bf16 = jnp.bfloat16
    assert all(w.shape[1] % (16 * n_steps) == 0 for w in to_cast)
    out_shape = [jax.ShapeDtypeStruct((b, s, width), bf16)
                 for width in (2 * W_ATT, 2 * W_ATT, W_ATT, W_ATT, W_ATT, W_ATT, 2 * d)]
    out_shape.append(jax.ShapeDtypeStruct((b, s // tm, 8, LANES), jnp.float32))
    out_shape += [jax.ShapeDtypeStruct(w.shape[1:], bf16) for w in to_cast]
    consts = (g, wq, wk, wv, wc, wg, wf, bf, eq, ek, hsum)
    return pl.pallas_call(
        _in_proj_body,
        grid=(b, s // tm),
        in_specs=[tok(d)] + [_const_spec(a.shape) for a in consts] + [strip_in(w) for w in to_cast],
        out_specs=[tok(sh.shape[-1]) for sh in out_shape[:7]]
                  + [pl.BlockSpec((1, 1, 8, LANES), lambda i, j: (i, j, 0, 0))]
                  + [strip_out(w) for w in to_cast],
        out_shape=out_shape,
        scratch_shapes=[pltpu.VMEM((1, LANES), jnp.float32)],
        compiler_params=pltpu.CompilerParams(
            dimension_semantics=("arbitrary", "arbitrary"), vmem_limit_bytes=VMEM_LIMIT),
        name="in_proj",
    )(x, *consts, *to_cast)


def _fox_body(cfirst_ref, clast_ref, qnorm_ref, knorm_ref, q_ref, k_ref, v_ref, o_ref,
              s_ref, p_ref, m_ref, l_ref, acc_ref, bad_ref, *, tq):
    n_tiles = q_ref.shape[1] // tq
    lane2 = lax.broadcasted_iota(jnp.int32, (1, 2 * LANES), 1) % LANES
    head_a = lax.broadcasted_iota(jnp.int32, (1, LANES), 1) < HEAD_DIM
    base = [(pl.program_id(0) * H_FOX + 2 * pl.program_id(1) + head) * n_tiles for head in range(2)]

    def key_rows(j):
        return pl.ds(pl.multiple_of(j * tq, tq), tq)

    def scores(slot, q_head, j):
        s_ref[slot] = _dot_nt(q_head, k_ref[0, key_rows(j), :])

    def softmax(slot, head, mode):
        for r in range(0, tq, FOX_ROWS):
            rows = slice(r, r + FOX_ROWS)
            s = s_ref[slot, rows, :]
            if mode in ("diagonal", "diagonal_self"):
                row = lax.broadcasted_iota(jnp.int32, (FOX_ROWS, tq), 0) + r
                col = lax.broadcasted_iota(jnp.int32, (FOX_ROWS, tq), 1)
                if mode == "diagonal":
                    s = jnp.where(col <= row, s, NEG)
                    m = jnp.max(s, axis=-1, keepdims=True)
                else:
                    own = slice(r // LANES * LANES, (r // LANES + 1) * LANES)
                    lane = lax.broadcasted_iota(jnp.int32, (FOX_ROWS, LANES), 1)
                    sub = lax.broadcasted_iota(jnp.int32, (FOX_ROWS, LANES), 0)
                    m = jnp.sum(jnp.where(lane == sub + r % LANES, s[:, own], 0.0), axis=-1, keepdims=True)
                    s = jnp.where(col <= row, s, NEG)
                m = jnp.broadcast_to(m, (FOX_ROWS, LANES))
                m_ref[head, rows, :] = m
            elif mode == "exact":
                m_old = m_ref[head, rows, :]
                m = jnp.maximum(m_old, jnp.max(s, axis=-1, keepdims=True))
                alpha = jnp.exp2(m_old - m)
                m_ref[head, rows, :] = m
            else:
                m = m_ref[head, rows, :]
            lsum = None
            for g in range(tq // LANES):
                cols = slice(g * LANES, (g + 1) * LANES)
                pg = jnp.exp2(s[:, cols] - m)
                p_ref[slot, rows, cols] = pg.astype(p_ref.dtype)
                lsum = pg if lsum is None else lsum + pg
            if mode in ("diagonal", "diagonal_self"):
                l_ref[head, rows, :] = lsum
            elif mode == "exact":
                l_ref[head, rows, :] = l_ref[head, rows, :] * alpha + lsum
                acc_ref[head, rows, :] = acc_ref[head, rows, :] * alpha
            else:
                l_ref[head, rows, :] = l_ref[head, rows, :] + lsum

    def values(slot, head, j, mode):
        pv = _dot(p_ref[slot], v_ref[0, key_rows(j), :])
        acc_ref[head] = pv if mode in ("diagonal", "diagonal_self") else acc_ref[head] + pv

    def unit_run(q_of, units):
        n_slots = s_ref.shape[0]
        for n in range(min(FOX_AHEAD, len(units))):
            scores(n, q_of(units[n][1]), units[n][0])
        for n, (tile, head, mode) in enumerate(units):
            softmax(n % n_slots, head, mode)
            if n + FOX_AHEAD < len(units):
                scores((n + FOX_AHEAD) % n_slots, q_of(units[n + FOX_AHEAD][1]), units[n + FOX_AHEAD][0])
            values(n % n_slots, head, tile, mode)

    def run_tiles(q_of, make_units, first, total):
        def full(t, carry):
            unit_run(q_of, make_units(first - FOX_TRIP * t, FOX_TRIP))
            return carry

        n_full = total // FOX_TRIP
        lax.fori_loop(0, n_full, full, 0)
        done = n_full * FOX_TRIP
        count = FOX_TRIP // 2
        while count >= 1:
            fits = (total - done) >= count

            @pl.when(fits)
            def _(start=first - done, count=count):
                unit_run(q_of, make_units(start, count))

            done = jnp.where(fits, done + count, done)
            count //= 2

    def query_tile(qi, kmax, exact):
        q_rows = pl.ds(pl.multiple_of(qi * tq, tq), tq)
        qcat = q_ref[0, q_rows, :]
        zero = jnp.zeros_like(qcat)
        qh = (jnp.where(lane2 < HEAD_DIM, qcat, zero), jnp.where(lane2 >= HEAD_DIM, qcat, zero))

        kmax = [jnp.maximum(kmax[head], knorm_ref[base[head] + qi]) for head in range(2)]
        reach = []
        for head in range(2):
            bound = (cfirst_ref[base[head] + qi] + 2.0 * NORM_SLACK * qnorm_ref[base[head] + qi] * kmax[head]
                     + SKIP_MARGIN * LOG2E)

            def reaches(n, head=head, bound=bound):
                t = jnp.maximum(qi - 1 - n, 0)
                return (n < qi) & (bound - clast_ref[base[head] + t] > 0.0)

            reach.append(lax.while_loop(reaches, lambda n: n + 1, 0))
        n_both, n_long = jnp.minimum(reach[0], reach[1]), jnp.maximum(reach[0], reach[1])

        def both_heads(j, count, first_mode="lazy"):
            return [(j - i, head, first_mode if i == 0 else "lazy") for i in range(count) for head in range(2)]

        if exact:
            unit_run(qh.__getitem__, both_heads(qi, 1, "diagonal"))

            def step(t, carry):
                unit_run(qh.__getitem__, both_heads(qi - 1 - t, 1, "exact"))
                return carry

            lax.fori_loop(0, n_long, step, 0)
        else:
            total = n_both + 1
            count, first_len = FOX_TRIP, 1
            while count >= 1:
                fits = total >= count
                if count < FOX_TRIP:
                    fits = fits & (total < 2 * count)

                @pl.when(fits)
                def _(count=count):
                    unit_run(qh.__getitem__, both_heads(qi, count, "diagonal_self"))

                first_len = jnp.where(fits, count, first_len)
                count //= 2
            run_tiles(qh.__getitem__, both_heads, qi - first_len, total - first_len)

            long_head = jnp.where(reach[1] > reach[0], 1, 0)
            q_long = jnp.where((lane2 < HEAD_DIM) == (long_head == 0), qcat, zero)
            run_tiles(lambda head: q_long, lambda j, count: [(j - i, long_head, "lazy") for i in range(count)],
                      qi - total, n_long - n_both)

        outs = []
        for head in range(2):
            acc, lp = acc_ref[head], l_ref[head]
            outs.append(acc / jnp.sum(lp, axis=-1, keepdims=True))
            if not exact:
                finite = (jnp.abs(acc) <= F32_BIG) & (lp <= F32_BIG)
                bad_ref[...] = jnp.maximum(bad_ref[...], jnp.where(finite, 0.0, 1.0))
        o_ref[0, q_rows, :] = jnp.where(head_a, outs[0], outs[1]).astype(o_ref.dtype)
        return kmax

    def sweep(exact):
        def body(qi, kmax):
            return tuple(query_tile(qi, kmax, exact))

        lax.fori_loop(0, n_tiles, body, (jnp.float32(0.0), jnp.float32(0.0)))

    bad_ref[...] = jnp.zeros_like(bad_ref)
    sweep(exact=False)

    @pl.when(jnp.max(bad_ref[...]) > 0.0)
    def _():
        sweep(exact=True)


def _fox(stats, qf, kf, vf, tq):
    b, s, _ = qf.shape
    smem = pl.BlockSpec(memory_space=pltpu.SMEM)
    return pl.pallas_call(
        functools.partial(_fox_body, tq=tq),
        grid=(b, N_PAIR),
        in_specs=[smem, smem, smem, smem,
                  pl.BlockSpec((1, s, 2 * LANES), lambda i, p: (i, 0, p)),
                  pl.BlockSpec((1, s, 2 * LANES), lambda i, p: (i, 0, p)),
                  pl.BlockSpec((1, s, LANES), lambda i, p: (i, 0, p))],
        out_specs=pl.BlockSpec((1, s, LANES), lambda i, p: (i, 0, p)),
        out_shape=jax.ShapeDtypeStruct((b, s, W_ATT), jnp.bfloat16),
        scratch_shapes=[pltpu.VMEM((2 * FOX_AHEAD, tq, tq), jnp.float32),
                        pltpu.VMEM((2 * FOX_AHEAD, tq, tq), jnp.bfloat16),
                        pltpu.VMEM((2, tq, LANES), jnp.float32),
                        pltpu.VMEM((2, tq, LANES), jnp.float32),
                        pltpu.VMEM((2, tq, LANES), jnp.float32),
                        pltpu.VMEM((tq, LANES), jnp.float32)],
        compiler_params=pltpu.CompilerParams(
            dimension_semantics=("arbitrary", "arbitrary"), vmem_limit_bytes=VMEM_LIMIT),
        name="fox_attention",
    )(*stats, qf, kf, vf)


def _band_bias_body(g_ref, o_ref):
    width = g_ref.shape[-1]
    base = jnp.broadcast_to(g_ref[0], (Q_BLOCK, width))
    toeplitz = pltpu.roll(base, 0, 1, stride=1, stride_axis=0)[:, :BAND]
    qrow = lax.broadcasted_iota(jnp.int32, (Q_BLOCK, BAND), 0)
    kcol = lax.broadcasted_iota(jnp.int32, (Q_BLOCK, BAND), 1)
    cq = qrow // CHUNK
    ck = kcol // CHUNK - LEFT_CHUNKS
    valid = (ck <= cq) & (ck >= cq - LEFT_CHUNKS)
    o_ref[0] = jnp.where(valid, toeplitz * LOG2E, NEG)


def _band_bias(g_ext):
    h, _, width = g_ext.shape
    return pl.pallas_call(
        _band_bias_body,
        grid=(h,),
        in_specs=[pl.BlockSpec((1, 1, width), lambda i: (i, 0, 0))],
        out_specs=pl.BlockSpec((1, Q_BLOCK, BAND), lambda i: (i, 0, 0)),
        out_shape=jax.ShapeDtypeStruct((h, Q_BLOCK, BAND), jnp.float32),
        name="band_bias",
    )(g_ext)


def _chunk_body(q_ref, k_ref, v_ref, bias_ref, o_ref, kpad_ref, vpad_ref, s_ref, p_ref, l_ref, *, tq):
    qi = pl.program_id(2)
    s_len = k_ref.shape[1]

    @pl.when(qi == 0)
    def _():
        zeros = jnp.zeros((PAD, LANES), kpad_ref.dtype)
        kpad_ref[:PAD, :] = zeros
        vpad_ref[:PAD, :] = zeros
        kpad_ref[PAD:PAD + s_len, :] = k_ref[0]
        vpad_ref[PAD:PAD + s_len, :] = v_ref[0]

    head_a = lax.broadcasted_iota(jnp.int32, (1, LANES), 1) < HEAD_DIM
    kcol = lax.broadcasted_iota(jnp.int32, (1, BAND), 1)

    def offsets(blk):
        r0 = pl.multiple_of(blk * Q_BLOCK, Q_BLOCK)
        return r0, pl.multiple_of(qi * tq + r0, Q_BLOCK)

    def scores(slot, blk):
        r0, p0 = offsets(blk)
        q2 = q_ref[0, pl.ds(r0, Q_BLOCK), :]
        zero = jnp.zeros_like(q2)
        qs = jnp.concatenate([jnp.where(head_a, q2, zero), jnp.where(head_a, zero, q2)], axis=0)
        s_ref[slot] = _dot_nt(qs, kpad_ref[pl.ds(p0, BAND), :])

    def softmax(slot, blk, masked):
        _, p0 = offsets(blk)
        for r in range(0, 2 * Q_BLOCK, CHUNK_ROWS):
            rows = slice(r, r + CHUNK_ROWS)
            s = s_ref[slot, rows, :] + bias_ref[0, rows, :]
            if masked:
                s = jnp.where(kcol + p0 >= PAD, s, NEG)
            p = jnp.exp2(s - jnp.max(s, axis=-1, keepdims=True))
            l_ref[slot, rows, :] = jnp.broadcast_to(jnp.sum(p, axis=-1, keepdims=True), (CHUNK_ROWS, LANES))
            p_ref[slot, rows, :] = p.astype(p_ref.dtype)

    def values(slot, blk):
        r0, p0 = offsets(blk)
        o = _dot(p_ref[slot], vpad_ref[pl.ds(p0, BAND), :]) / l_ref[slot]
        o_ref[0, pl.ds(r0, Q_BLOCK), :] = jnp.where(head_a, o[:Q_BLOCK], o[Q_BLOCK:]).astype(o_ref.dtype)

    group_blocks = min(CHUNK_GROUP, tq // Q_BLOCK)

    def group(i, masked):
        first = i * group_blocks
        scores(0, first)
        for n in range(group_blocks):
            if n + 1 < group_blocks:
                scores((n + 1) % 2, first + n + 1)
            softmax(n % 2, first + n, masked and n < PAD // Q_BLOCK)
            values(n % 2, first + n)

    def masked_group(i, carry):
        group(i, True)
        return carry

    def plain_group(i, carry):
        group(i, False)
        return carry

    group_rows = Q_BLOCK * group_blocks
    n_groups = tq // group_rows
    n_masked = jnp.clip((PAD - qi * tq + group_rows - 1) // group_rows, 0, n_groups)
    lax.fori_loop(0, n_masked, masked_group, 0)
    lax.fori_loop(n_masked, n_groups, plain_group, 0)


def _chunk_attention(qc, kc, vc, bias, tq):
    b, s, _ = qc.shape
    assert tq % (min(CHUNK_GROUP, tq // Q_BLOCK) * Q_BLOCK) == 0
    return pl.pallas_call(
        functools.partial(_chunk_body, tq=tq),
        grid=(b, N_PAIR, s // tq),
        in_specs=[pl.BlockSpec((1, tq, LANES), lambda i, p, j: (i, j, p)),
                  pl.BlockSpec((1, s, LANES), lambda i, p, j: (i, 0, p)),
                  pl.BlockSpec((1, s, LANES), lambda i, p, j: (i, 0, p)),
                  pl.BlockSpec((1, 2 * Q_BLOCK, BAND), lambda i, p, j: (p, 0, 0))],
        out_specs=pl.BlockSpec((1, tq, LANES), lambda i, p, j: (i, j, p)),
        out_shape=jax.ShapeDtypeStruct((b, s, W_ATT), jnp.bfloat16),
        scratch_shapes=[pltpu.VMEM((PAD + s, LANES), jnp.bfloat16),
                        pltpu.VMEM((PAD + s, LANES), jnp.bfloat16),
                        pltpu.VMEM((2, 2 * Q_BLOCK, BAND), jnp.float32),
                        pltpu.VMEM((2, 2 * Q_BLOCK, BAND), jnp.bfloat16),
                        pltpu.VMEM((2, 2 * Q_BLOCK, LANES), jnp.float32)],
        compiler_params=pltpu.CompilerParams(
            dimension_semantics=("arbitrary", "arbitrary", "arbitrary"), vmem_limit_bytes=VMEM_LIMIT),
        name="chunk_attention",
    )(qc, kc, vc, bias)


def _out_ffn_body(x_ref, oa_ref, ob_ref, gate_ref, wa_ref, wb_ref, wo_ref, g2_ref, wu_ref, wd_ref, gf_ref,
                  o_ref, *, ff_chunk, final):
    d = x_ref.shape[-1]
    ya = _dot(oa_ref[0], wa_ref[...])
    yb = _dot(ob_ref[0], wb_ref[...])
    ga = gate_ref[0, :, :d].astype(jnp.float32)
    gb = gate_ref[0, :, d:].astype(jnp.float32)
    merged = (ga * ya + gb * yb).astype(jnp.bfloat16)
    x1 = x_ref[0] + _dot(merged, wo_ref[...])
    h2 = _rms(x1, g2_ref[...]).astype(jnp.bfloat16)
    acc = x1
    for c0 in range(0, wu_ref.shape[1], ff_chunk):
        u = jnp.maximum(_dot(h2, wu_ref[:, c0:c0 + ff_chunk]), 0.0)
        acc = acc + _dot((u * u).astype(jnp.bfloat16), wd_ref[c0:c0 + ff_chunk, :])
    if final:
        acc = _rms(acc, gf_ref[...])
    o_ref[0] = acc


def _out_ffn(x, oa, ob, gates, wa, wb, wo, g2, wu, wd, gf, tm, final):
    b, s, d = x.shape
    tok = lambda w: pl.BlockSpec((1, tm, w), lambda i, j: (i, j, 0))
    return pl.pallas_call(
        functools.partial(_out_ffn_body, ff_chunk=min(1024, wu.shape[1]), final=final),
        grid=(b, s // tm),
        in_specs=[tok(d), tok(W_ATT), tok(W_ATT), tok(2 * d)]
                 + [_const_spec(a.shape) for a in (wa, wb, wo, g2, wu, wd, gf)],
        out_specs=tok(d),
        out_shape=jax.ShapeDtypeStruct((b, s, d), jnp.float32),
        compiler_params=pltpu.CompilerParams(
            dimension_semantics=("arbitrary", "arbitrary"), vmem_limit_bytes=VMEM_LIMIT),
        name="out_ffn",
    )(x, oa, ob, gates, wa, wb, wo, g2, wu, wd, gf)


def _aug_placement():
    eq = [[0.0] * W_ATT for _ in range(LANES)]
    ek = [[0.0] * W_ATT for _ in range(LANES)]
    one = 3 * H_FOX
    for h in range(H_FOX):
        for piece in range(3):
            eq[piece * H_FOX + h][HEAD_DIM * h + piece] = 1.0
            eq[one][HEAD_DIM * h + 3 + piece] = 1.0
            ek[one][HEAD_DIM * h + piece] = 1.0
            ek[piece * H_FOX + h][HEAD_DIM * h + 3 + piece] = -1.0
    return jnp.array(eq, jnp.bfloat16), jnp.array(ek, jnp.bfloat16)


def kernel(x, norm1, w_in, forget_bias, rel_bias, w_branch_a, w_branch_b, w_out, norm2, w_up, w_down, final_norm):
    b, s, d = x.shape
    depth = w_in.shape[0]
    bf16 = jnp.bfloat16
    tm = min(512, s)
    tqc = s
    eq, ek = _aug_placement()
    head_of_col = lax.broadcasted_iota(jnp.int32, (W_ATT, LANES), 0) // HEAD_DIM
    hsum = (head_of_col == lax.broadcasted_iota(jnp.int32, (W_ATT, LANES), 1)).astype(bf16)
    gf = final_norm.reshape(1, d)
    o = 3 * W_ATT + H_FOX
    for l in range(depth):
        w = w_in[l]
        wq = (w[:, :W_ATT] * (SCALE * LOG2E)).astype(bf16)
        wk = w[:, W_ATT:2 * W_ATT].astype(bf16)
        wv = w[:, 2 * W_ATT:3 * W_ATT].astype(bf16)
        wf = jnp.pad(w[:, 3 * W_ATT:o], ((0, 0), (0, LANES - H_FOX))).astype(bf16)
        wc = jnp.concatenate([w[:, o:o + W_ATT] * (SCALE * LOG2E), w[:, o + W_ATT:o + 3 * W_ATT]],
                             axis=1).astype(bf16)
        wg = w[:, o + 3 * W_ATT:].astype(bf16)
        bf = jnp.pad(forget_bias[l], (0, LANES - H_FOX)).reshape(1, LANES)
        qf, kf, vf, qc, kc, vc, gates, stat, wo, wu, wd = _in_proj(
            x, norm1[l].reshape(1, d), wq, wk, wv, wc, wg, wf, bf, eq, ek, hsum,
            (w_out, w_up, w_down), l, tm)

        stats = stat[:, :, :4, :H_FOX].transpose(2, 0, 3, 1).reshape(4, -1)
        o_a = _fox(tuple(stats), qf, kf, vf, tm)

        far = rel_bias[l][:, 2 * MAX_REL:]
        g_ext = jnp.concatenate([jnp.broadcast_to(far, (H_CHK, PAD - MAX_REL + 1)),
                                 rel_bias[l][:, 2 * MAX_REL - 1:0:-1],
                                 jnp.broadcast_to(far, (H_CHK, Q_BLOCK))], axis=1)
        bias = _band_bias(g_ext.reshape(H_CHK, 1, -1)).reshape(N_PAIR, 2 * Q_BLOCK, BAND)
        o_b = _chunk_attention(qc, kc, vc, bias, tqc)

        x = _out_ffn(x, o_a, o_b, gates, w_branch_a[l].astype(bf16), w_branch_b[l].astype(bf16),
                     wo, norm2[l].reshape(1, d), wu, wd, gf, tm, final=(l == depth - 1))
    return x
```

```python
import functools
import math

import jax
import jax.numpy as jnp
from jax import lax
from jax.experimental import pallas as pl
from jax.experimental.pallas import tpu as pltpu

HEAD_DIM = 64
H_FOX = 8
H_CHK = 8
N_PAIR = 4
W_ATT = H_FOX * HEAD_DIM
CHUNK = 64
Q_BLOCK = 128
LEFT_CHUNKS = 8
PAD = LEFT_CHUNKS * CHUNK
BAND = PAD + Q_BLOCK
MAX_REL = 128
EPS = 1e-6
NEG = -1e30
LANES = 128
SCALE = 1.0 / math.sqrt(HEAD_DIM)
LOG2E = math.log2(math.e)
F32_BIG = 3.0e38
PROJ_SPLIT = 2
FOX_ROWS = 32
FOX_TRIP = 8
FOX_AHEAD = 2
CHUNK_ROWS = 32
CHUNK_GROUP = 16
SKIP_MARGIN = 30.0
NORM_SLACK = 1.02
VMEM_LIMIT = 60 * 1024 * 1024

_NT = (((1,), (1,)), ((), ()))


def _dot(a, b):
    return jnp.dot(a, b, preferred_element_type=jnp.float32)


def _dot_nt(a, b):
    return lax.dot_general(a, b, _NT, preferred_element_type=jnp.float32)


def _rms(x, g):
    ms = jnp.mean(x * x, axis=-1, keepdims=True)
    return x * lax.rsqrt(ms + EPS) * g


def _const_spec(shape):
    nd = len(shape)
    return pl.BlockSpec(shape, lambda *_: (0,) * nd, pipeline_mode=pl.Buffered(1))


def _in_proj_body(x_ref, g_ref, wq_ref, wk_ref, wv_ref, wc_ref, wg_ref, wf_ref, bf_ref, eq_ref, ek_ref, hsum_ref,
                  wo32_ref, wu32_ref, wd32_ref,
                  qf_ref, kf_ref, vf_ref, qc_ref, kc_ref, vc_ref, gate_ref, stat_ref, wo_ref, wu_ref, wd_ref,
                  carry_ref):
    tm = x_ref.shape[1]
    th = tm // PROJ_SPLIT

    wo_ref[...] = wo32_ref[...].astype(wo_ref.dtype)
    wu_ref[...] = wu32_ref[...].astype(wu_ref.dtype)
    wd_ref[...] = wd32_ref[...].astype(wd_ref.dtype)

    @pl.when(pl.program_id(1) == 0)
    def _():
        carry_ref[...] = jnp.zeros_like(carry_ref)

    lane = lax.broadcasted_iota(jnp.int32, (th, LANES), 1)
    row = lax.broadcasted_iota(jnp.int32, (th, LANES), 0)

    def max_norm(t):
        t = t.astype(jnp.float32)
        sq = _dot((t * t).astype(jnp.bfloat16), hsum_ref[...])
        return jnp.max(sq, axis=0, keepdims=True)

    c_first, c_last, q_sq, k_sq = None, None, None, None
    for part in range(PROJ_SPLIT):
        rows = slice(part * th, (part + 1) * th)
        h = _rms(x_ref[0, rows, :], g_ref[...]).astype(jnp.bfloat16)

        logf = jax.nn.log_sigmoid(_dot(h, wf_ref[...]) + bf_ref[...]) * LOG2E
        c = jnp.where(lane < H_FOX, logf, 0.0)
        k = 1
        while k < th:
            c = c + jnp.where(row >= k, pltpu.roll(c, k, 0), 0.0)
            k *= 2
        c = c + carry_ref[...]
        carry_ref[...] = c[th - 1:th, :]
        c_first = c[0:1, :] if part == 0 else c_first
        c_last = c[th - 1:th, :]

        hi = c.astype(jnp.bfloat16).astype(jnp.float32)
        r1 = c - hi
        mid = r1.astype(jnp.bfloat16).astype(jnp.float32)
        lo = (r1 - mid).astype(jnp.bfloat16).astype(jnp.float32)
        pieces = hi + pltpu.roll(mid, H_FOX, 1) + pltpu.roll(lo, 2 * H_FOX, 1)
        pieces = jnp.where(lane == 3 * H_FOX, 1.0, pieces).astype(jnp.bfloat16)
        aug_q = _dot(pieces, eq_ref[...]).astype(jnp.bfloat16)
        aug_k = _dot(pieces, ek_ref[...]).astype(jnp.bfloat16)

        q = _dot(h, wq_ref[...]).astype(jnp.bfloat16)
        kk = _dot(h, wk_ref[...]).astype(jnp.bfloat16)
        for p in range(N_PAIR):
            src = slice(p * LANES, (p + 1) * LANES)
            qf_ref[0, rows, 2 * p * LANES:(2 * p + 1) * LANES] = q[:, src]
            qf_ref[0, rows, (2 * p + 1) * LANES:(2 * p + 2) * LANES] = aug_q[:, src]
            kf_ref[0, rows, 2 * p * LANES:(2 * p + 1) * LANES] = kk[:, src]
            kf_ref[0, rows, (2 * p + 1) * LANES:(2 * p + 2) * LANES] = aug_k[:, src]
        vf_ref[0, rows, :] = _dot(h, wv_ref[...]).astype(jnp.bfloat16)
        q_sq = max_norm(q) if part == 0 else jnp.maximum(q_sq, max_norm(q))
        k_sq = max_norm(kk) if part == 0 else jnp.maximum(k_sq, max_norm(kk))

        pc = _dot(h, wc_ref[...])
        qc_ref[0, rows, :] = pc[:, :W_ATT].astype(jnp.bfloat16)
        kc_ref[0, rows, :] = pc[:, W_ATT:2 * W_ATT].astype(jnp.bfloat16)
        vc_ref[0, rows, :] = pc[:, 2 * W_ATT:].astype(jnp.bfloat16)
        gate_ref[0, rows, :] = jax.nn.sigmoid(_dot(h, wg_ref[...])).astype(jnp.bfloat16)

    sub = lax.broadcasted_iota(jnp.int32, (8, LANES), 0)
    stat = jnp.where(sub == 0, c_first, 0.0)
    stat = jnp.where(sub == 1, c_last, stat)
    stat = jnp.where(sub == 2, jnp.sqrt(q_sq), stat)
    stat_ref[0, 0] = jnp.where(sub == 3, jnp.sqrt(k_sq), stat)


def _in_proj(x, g, wq, wk, wv, wc, wg, wf, bf, eq, ek, hsum, to_cast, layer, tm):
    b, s, d = x.shape
    n_steps = b * (s // tm)
    tok = lambda width: pl.BlockSpec((1, tm, width), lambda i, j: (i, j, 0))
    strip_in = lambda w: pl.BlockSpec((None, w.shape[1] // n_steps, w.shape[2]),
                                      lambda i, j: (layer, i * (s // tm) + j, 0))
    strip_out = lambda w: pl.BlockSpec((w.shape[1] // n_steps, w.shape[2]), lambda i, j: (i * (s // tm) + j, 0))
    bf16 = jnp.bfloat16
    assert all(w.shape[1] % (16 * n_steps) == 0 for w in to_cast)
    out_shape = [jax.ShapeDtypeStruct((b, s, width), bf16)
                 for width in (2 * W_ATT, 2 * W_ATT, W_ATT, W_ATT, W_ATT, W_ATT, 2 * d)]
    out_shape.append(jax.ShapeDtypeStruct((b, s // tm, 8, LANES), jnp.float32))
    out_shape += [jax.ShapeDtypeStruct(w.shape[1:], bf16) for w in to_cast]
    consts = (g, wq, wk, wv, wc, wg, wf, bf, eq, ek, hsum)
    return pl.pallas_call(
        _in_proj_body,
        grid=(b, s // tm),
        in_specs=[tok(d)] + [_const_spec(a.shape) for a in consts] + [strip_in(w) for w in to_cast],
        out_specs=[tok(sh.shape[-1]) for sh in out_shape[:7]]
                  + [pl.BlockSpec((1, 1, 8, LANES), lambda i, j: (i, j, 0, 0))]
                  + [strip_out(w) for w in to_cast],
        out_shape=out_shape,
        scratch_shapes=[pltpu.VMEM((1, LANES), jnp.float32)],
        compiler_params=pltpu.CompilerParams(
            dimension_semantics=("arbitrary", "arbitrary"), vmem_limit_bytes=VMEM_LIMIT),
        name="in_proj",
    )(x, *consts, *to_cast)


def _fox_body(cfirst_ref, clast_ref, qnorm_ref, knorm_ref, q_ref, k_ref, v_ref, o_ref,
              s_ref, p_ref, m_ref, l_ref, acc_ref, bad_ref, *, tq):
    n_tiles = q_ref.shape[1] // tq
    lane2 = lax.broadcasted_iota(jnp.int32, (1, 2 * LANES), 1) % LANES
    head_a = lax.broadcasted_iota(jnp.int32, (1, LANES), 1) < HEAD_DIM
    base = [(pl.program_id(0) * H_FOX + 2 * pl.program_id(1) + head) * n_tiles for head in range(2)]

    def key_rows(j):
        return pl.ds(pl.multiple_of(j * tq, tq), tq)

    def scores(slot, q_head, j):
        s_ref[slot] = _dot_nt(q_head, k_ref[0, key_rows(j), :])

    def softmax(slot, head, mode):
        for r in range(0, tq, FOX_ROWS):
            rows = slice(r, r + FOX_ROWS)
            s = s_ref[slot, rows, :]
            if mode in ("diagonal", "diagonal_self"):
                row = lax.broadcasted_iota(jnp.int32, (FOX_ROWS, tq), 0) + r
                col = lax.broadcasted_iota(jnp.int32, (FOX_ROWS, tq), 1)
                if mode == "diagonal":
                    s = jnp.where(col <= row, s, NEG)
                    m = jnp.max(s, axis=-1, keepdims=True)
                else:
                    own = slice(r // LANES * LANES, (r // LANES + 1) * LANES)
                    lane = lax.broadcasted_iota(jnp.int32, (FOX_ROWS, LANES), 1)
                    sub = lax.broadcasted_iota(jnp.int32, (FOX_ROWS, LANES), 0)
                    m = jnp.sum(jnp.where(lane == sub + r % LANES, s[:, own], 0.0), axis=-1, keepdims=True)
                    s = jnp.where(col <= row, s, NEG)
                m = jnp.broadcast_to(m, (FOX_ROWS, LANES))
                m_ref[head, rows, :] = m
            elif mode == "exact":
                m_old = m_ref[head, rows, :]
                m = jnp.maximum(m_old, jnp.max(s, axis=-1, keepdims=True))
                alpha = jnp.exp2(m_old - m)
                m_ref[head, rows, :] = m
            else:
                m = m_ref[head, rows, :]
            lsum = None
            for g in range(tq // LANES):
                cols = slice(g * LANES, (g + 1) * LANES)
                pg = jnp.exp2(s[:, cols] - m)
                p_ref[slot, rows, cols] = pg.astype(p_ref.dtype)
                lsum = pg if lsum is None else lsum + pg
            if mode in ("diagonal", "diagonal_self"):
                l_ref[head, rows, :] = lsum
            elif mode == "exact":
                l_ref[head, rows, :] = l_ref[head, rows, :] * alpha + lsum
                acc_ref[head, rows, :] = acc_ref[head, rows, :] * alpha
            else:
                l_ref[head, rows, :] = l_ref[head, rows, :] + lsum

    def values(slot, head, j, mode):
        pv = _dot(p_ref[slot], v_ref[0, key_rows(j), :])
        acc_ref[head] = pv if mode in ("diagonal", "diagonal_self") else acc_ref[head] + pv

    half = tq // 2

    def half_rows(j, part):
        return pl.ds(pl.multiple_of(j * tq + part * half, half), half)

    def scores_diagonal(slot, q_head, j):
        s_ref[slot, :, :half] = _dot_nt(q_head, k_ref[0, half_rows(j, 0), :])
        s_ref[slot, half:, half:] = _dot_nt(q_head[half:], k_ref[0, half_rows(j, 1), :])

    def softmax_diagonal(slot, head):
        for r in range(0, tq, FOX_ROWS):
            rows = slice(r, r + FOX_ROWS)
            width = half if r < half else tq
            s = s_ref[slot, rows, :width]
            row = lax.broadcasted_iota(jnp.int32, (FOX_ROWS, width), 0) + r
            col = lax.broadcasted_iota(jnp.int32, (FOX_ROWS, width), 1)
            own = slice(r // LANES * LANES, (r // LANES + 1) * LANES)
            lane = lax.broadcasted_iota(jnp.int32, (FOX_ROWS, LANES), 1)
            sub = lax.broadcasted_iota(jnp.int32, (FOX_ROWS, LANES), 0)
            m = jnp.sum(jnp.where(lane == sub + r % LANES, s[:, own], 0.0), axis=-1, keepdims=True)
            m = jnp.broadcast_to(m, (FOX_ROWS, LANES))
            s = jnp.where(col <= row, s, NEG)
            m_ref[head, rows, :] = m
            lsum = None
            for g in range(width // LANES):
                cols = slice(g * LANES, (g + 1) * LANES)
                pg = jnp.exp2(s[:, cols] - m)
                p_ref[slot, rows, cols] = pg.astype(p_ref.dtype)
                lsum = pg if lsum is None else lsum + pg
            l_ref[head, rows, :] = lsum

    def values_diagonal(slot, head, j):
        acc_ref[head] = _dot(p_ref[slot, :, :half], v_ref[0, half_rows(j, 0), :])
        acc_ref[head, half:, :] = acc_ref[head, half:, :] + _dot(p_ref[slot, half:, half:],
                                                                 v_ref[0, half_rows(j, 1), :])

    def unit_run(q_of, units):
        n_slots = s_ref.shape[0]

        def issue_scores(n):
            tile, head, mode = units[n]
            (scores_diagonal if mode == "diagonal_self" else scores)(n % n_slots, q_of(head), tile)

        for n in range(min(FOX_AHEAD, len(units))):
            issue_scores(n)
        for n, (tile, head, mode) in enumerate(units):
            if mode == "diagonal_self":
                softmax_diagonal(n % n_slots, head)
            else:
                softmax(n % n_slots, head, mode)
            if n + FOX_AHEAD < len(units):
                issue_scores(n + FOX_AHEAD)
            if mode == "diagonal_self":
                values_diagonal(n % n_slots, head, tile)
            else:
                values(n % n_slots, head, tile, mode)

    def run_tiles(q_of, make_units, first, total):
        def full(t, carry):
            unit_run(q_of, make_units(first - FOX_TRIP * t, FOX_TRIP))
            return carry

        n_full = total // FOX_TRIP
        lax.fori_loop(0, n_full, full, 0)
        done = n_full * FOX_TRIP
        count = FOX_TRIP // 2
        while count >= 1:
            fits = (total - done) >= count

            @pl.when(fits)
            def _(start=first - done, count=count):
                unit_run(q_of, make_units(start, count))

            done = jnp.where(fits, done + count, done)
            count //= 2

    def query_tile(qi, kmax, exact):
        q_rows = pl.ds(pl.multiple_of(qi * tq, tq), tq)
        qcat = q_ref[0, q_rows, :]
        zero = jnp.zeros_like(qcat)
        qh = (jnp.where(lane2 < HEAD_DIM, qcat, zero), jnp.where(lane2 >= HEAD_DIM, qcat, zero))

        kmax = [jnp.maximum(kmax[head], knorm_ref[base[head] + qi]) for head in range(2)]
        reach = []
        for head in range(2):
            bound = (cfirst_ref[base[head] + qi] + 2.0 * NORM_SLACK * qnorm_ref[base[head] + qi] * kmax[head]
                     + SKIP_MARGIN * LOG2E)

            def reaches(n, head=head, bound=bound):
                t = jnp.maximum(qi - 1 - n, 0)
                return (n < qi) & (bound - clast_ref[base[head] + t] > 0.0)

            reach.append(lax.while_loop(reaches, lambda n: n + 1, 0))
        n_both, n_long = jnp.minimum(reach[0], reach[1]), jnp.maximum(reach[0], reach[1])

        def both_heads(j, count, first_mode="lazy"):
            return [(j - i, head, first_mode if i == 0 else "lazy") for i in range(count) for head in range(2)]

        if exact:
            unit_run(qh.__getitem__, both_heads(qi, 1, "diagonal"))

            def step(t, carry):
                unit_run(qh.__getitem__, both_heads(qi - 1 - t, 1, "exact"))
                return carry

            lax.fori_loop(0, n_long, step, 0)
        else:
            total = n_both + 1
            count, first_len = FOX_TRIP, 1
            while count >= 1:
                fits = total >= count
                if count < FOX_TRIP:
                    fits = fits & (total < 2 * count)

                @pl.when(fits)
                def _(count=count):
                    unit_run(qh.__getitem__, both_heads(qi, count, "diagonal_self"))

                first_len = jnp.where(fits, count, first_len)
                count //= 2
            run_tiles(qh.__getitem__, both_heads, qi - first_len, total - first_len)

            long_head = jnp.where(reach[1] > reach[0], 1, 0)
            q_long = jnp.where((lane2 < HEAD_DIM) == (long_head == 0), qcat, zero)
            run_tiles(lambda head: q_long, lambda j, count: [(j - i, long_head, "lazy") for i in range(count)],
                      qi - total, n_long - n_both)

        outs = []
        for head in range(2):
            acc, lp = acc_ref[head], l_ref[head]
            outs.append(acc / jnp.sum(lp, axis=-1, keepdims=True))
            if not exact:
                finite = (jnp.abs(acc) <= F32_BIG) & (lp <= F32_BIG)
                bad_ref[...] = jnp.maximum(bad_ref[...], jnp.where(finite, 0.0, 1.0))
        o_ref[0, q_rows, :] = jnp.where(head_a, outs[0], outs[1]).astype(o_ref.dtype)
        return kmax

    def sweep(exact):
        def body(qi, kmax):
            return tuple(query_tile(qi, kmax, exact))

        lax.fori_loop(0, n_tiles, body, (jnp.float32(0.0), jnp.float32(0.0)))

    bad_ref[...] = jnp.zeros_like(bad_ref)
    sweep(exact=False)

    @pl.when(jnp.max(bad_ref[...]) > 0.0)
    def _():
        sweep(exact=True)


def _fox(stats, qf, kf, vf, tq):
    b, s, _ = qf.shape
    smem = pl.BlockSpec(memory_space=pltpu.SMEM)
    return pl.pallas_call(
        functools.partial(_fox_body, tq=tq),
        grid=(b, N_PAIR),
        in_specs=[smem, smem, smem, smem,
                  pl.BlockSpec((1, s, 2 * LANES), lambda i, p: (i, 0, p)),
                  pl.BlockSpec((1, s, 2 * LANES), lambda i, p: (i, 0, p)),
                  pl.BlockSpec((1, s, LANES), lambda i, p: (i, 0, p))],
        out_specs=pl.BlockSpec((1, s, LANES), lambda i, p: (i, 0, p)),
        out_shape=jax.ShapeDtypeStruct((b, s, W_ATT), jnp.bfloat16),
        scratch_shapes=[pltpu.VMEM((2 * FOX_AHEAD, tq, tq), jnp.float32),
                        pltpu.VMEM((2 * FOX_AHEAD, tq, tq), jnp.bfloat16),
                        pltpu.VMEM((2, tq, LANES), jnp.float32),
                        pltpu.VMEM((2, tq, LANES), jnp.float32),
                        pltpu.VMEM((2, tq, LANES), jnp.float32),
                        pltpu.VMEM((tq, LANES), jnp.float32)],
        compiler_params=pltpu.CompilerParams(
            dimension_semantics=("arbitrary", "arbitrary"), vmem_limit_bytes=VMEM_LIMIT),
        name="fox_attention",
    )(*stats, qf, kf, vf)


def _band_bias_body(g_ref, o_ref):
    width = g_ref.shape[-1]
    base = jnp.broadcast_to(g_ref[0], (Q_BLOCK, width))
    toeplitz = pltpu.roll(base, 0, 1, stride=1, stride_axis=0)[:, :BAND]
    qrow = lax.broadcasted_iota(jnp.int32, (Q_BLOCK, BAND), 0)
    kcol = lax.broadcasted_iota(jnp.int32, (Q_BLOCK, BAND), 1)
    cq = qrow // CHUNK
    ck = kcol // CHUNK - LEFT_CHUNKS
    valid = (ck <= cq) & (ck >= cq - LEFT_CHUNKS)
    o_ref[0] = jnp.where(valid, toeplitz * LOG2E, NEG)


def _band_bias(g_ext):
    h, _, width = g_ext.shape
    return pl.pallas_call(
        _band_bias_body,
        grid=(h,),
        in_specs=[pl.BlockSpec((1, 1, width), lambda i: (i, 0, 0))],
        out_specs=pl.BlockSpec((1, Q_BLOCK, BAND), lambda i: (i, 0, 0)),
        out_shape=jax.ShapeDtypeStruct((h, Q_BLOCK, BAND), jnp.float32),
        name="band_bias",
    )(g_ext)


def _chunk_body(q_ref, k_ref, v_ref, bias_ref, o_ref, kpad_ref, vpad_ref, s_ref, p_ref, l_ref, *, tq):
    qi = pl.program_id(2)
    s_len = k_ref.shape[1]

    @pl.when(qi == 0)
    def _():
        zeros = jnp.zeros((PAD, LANES), kpad_ref.dtype)
        kpad_ref[:PAD, :] = zeros
        vpad_ref[:PAD, :] = zeros
        kpad_ref[PAD:PAD + s_len, :] = k_ref[0]
        vpad_ref[PAD:PAD + s_len, :] = v_ref[0]

    head_a = lax.broadcasted_iota(jnp.int32, (1, LANES), 1) < HEAD_DIM
    kcol = lax.broadcasted_iota(jnp.int32, (1, BAND), 1)

    def offsets(blk):
        r0 = pl.multiple_of(blk * Q_BLOCK, Q_BLOCK)
        return r0, pl.multiple_of(qi * tq + r0, Q_BLOCK)

    def scores(slot, blk):
        r0, p0 = offsets(blk)
        q2 = q_ref[0, pl.ds(r0, Q_BLOCK), :]
        zero = jnp.zeros_like(q2)
        qs = jnp.concatenate([jnp.where(head_a, q2, zero), jnp.where(head_a, zero, q2)], axis=0)
        s_ref[slot] = _dot_nt(qs, kpad_ref[pl.ds(p0, BAND), :])

    def softmax(slot, blk, masked):
        _, p0 = offsets(blk)
        for r in range(0, 2 * Q_BLOCK, CHUNK_ROWS):
            rows = slice(r, r + CHUNK_ROWS)
            s = s_ref[slot, rows, :] + bias_ref[0, rows, :]
            if masked:
                s = jnp.where(kcol + p0 >= PAD, s, NEG)
            p = jnp.exp2(s - jnp.max(s, axis=-1, keepdims=True))
            l_ref[slot, rows, :] = jnp.broadcast_to(jnp.sum(p, axis=-1, keepdims=True), (CHUNK_ROWS, LANES))
            p_ref[slot, rows, :] = p.astype(p_ref.dtype)

    def values(slot, blk):
        r0, p0 = offsets(blk)
        o = _dot(p_ref[slot], vpad_ref[pl.ds(p0, BAND), :]) / l_ref[slot]
        o_ref[0, pl.ds(r0, Q_BLOCK), :] = jnp.where(head_a, o[:Q_BLOCK], o[Q_BLOCK:]).astype(o_ref.dtype)

    def group(i, masked):
        first = i * CHUNK_GROUP
        scores(0, first)
        for n in range(CHUNK_GROUP):
            if n + 1 < CHUNK_GROUP:
                scores((n + 1) % 2, first + n + 1)
            softmax(n % 2, first + n, masked)
            values(n % 2, first + n)

    def masked_group(i, carry):
        group(i, True)
        return carry

    def plain_group(i, carry):
        group(i, False)
        return carry

    group_rows = Q_BLOCK * CHUNK_GROUP
    n_groups = tq // group_rows
    n_masked = jnp.clip((PAD - qi * tq + group_rows - 1) // group_rows, 0, n_groups)
    lax.fori_loop(0, n_masked, masked_group, 0)
    lax.fori_loop(n_masked, n_groups, plain_group, 0)


def _chunk_attention(qc, kc, vc, bias, tq):
    b, s, _ = qc.shape
    assert tq % (CHUNK_GROUP * Q_BLOCK) == 0
    return pl.pallas_call(
        functools.partial(_chunk_body, tq=tq),
        grid=(b, N_PAIR, s // tq),
        in_specs=[pl.BlockSpec((1, tq, LANES), lambda i, p, j: (i, j, p)),
                  pl.BlockSpec((1, s, LANES), lambda i, p, j: (i, 0, p)),
                  pl.BlockSpec((1, s, LANES), lambda i, p, j: (i, 0, p)),
                  pl.BlockSpec((1, 2 * Q_BLOCK, BAND), lambda i, p, j: (p, 0, 0))],
        out_specs=pl.BlockSpec((1, tq, LANES), lambda i, p, j: (i, j, p)),
        out_shape=jax.ShapeDtypeStruct((b, s, W_ATT), jnp.bfloat16),
        scratch_shapes=[pltpu.VMEM((PAD + s, LANES), jnp.bfloat16),
                        pltpu.VMEM((PAD + s, LANES), jnp.bfloat16),
                        pltpu.VMEM((2, 2 * Q_BLOCK, BAND), jnp.float32),
                        pltpu.VMEM((2, 2 * Q_BLOCK, BAND), jnp.bfloat16),
                        pltpu.VMEM((2, 2 * Q_BLOCK, LANES), jnp.float32)],
        compiler_params=pltpu.CompilerParams(
            dimension_semantics=("arbitrary", "arbitrary", "arbitrary"), vmem_limit_bytes=VMEM_LIMIT),
        name="chunk_attention",
    )(qc, kc, vc, bias)


def _out_ffn_body(x_ref, oa_ref, ob_ref, gate_ref, wa_ref, wb_ref, wo_ref, g2_ref, wu_ref, wd_ref, gf_ref,
                  o_ref, *, ff_chunk, final):
    d = x_ref.shape[-1]
    ya = _dot(oa_ref[0], wa_ref[...])
    yb = _dot(ob_ref[0], wb_ref[...])
    ga = gate_ref[0, :, :d].astype(jnp.float32)
    gb = gate_ref[0, :, d:].astype(jnp.float32)
    merged = (ga * ya + gb * yb).astype(jnp.bfloat16)
    x1 = x_ref[0] + _dot(merged, wo_ref[...])
    h2 = _rms(x1, g2_ref[...]).astype(jnp.bfloat16)
    acc = x1
    for c0 in range(0, wu_ref.shape[1], ff_chunk):
        u = jnp.maximum(_dot(h2, wu_ref[:, c0:c0 + ff_chunk]), 0.0)
        acc = acc + _dot((u * u).astype(jnp.bfloat16), wd_ref[c0:c0 + ff_chunk, :])
    if final:
        acc = _rms(acc, gf_ref[...])
    o_ref[0] = acc


def _out_ffn(x, oa, ob, gates, wa, wb, wo, g2, wu, wd, gf, tm, final):
    b, s, d = x.shape
    tok = lambda w: pl.BlockSpec((1, tm, w), lambda i, j: (i, j, 0))
    return pl.pallas_call(
        functools.partial(_out_ffn_body, ff_chunk=min(1024, wu.shape[1]), final=final),
        grid=(b, s // tm),
        in_specs=[tok(d), tok(W_ATT), tok(W_ATT), tok(2 * d)]
                 + [_const_spec(a.shape) for a in (wa, wb, wo, g2, wu, wd, gf)],
        out_specs=tok(d),
        out_shape=jax.ShapeDtypeStruct((b, s, d), jnp.float32),
        compiler_params=pltpu.CompilerParams(
            dimension_semantics=("arbitrary", "arbitrary"), vmem_limit_bytes=VMEM_LIMIT),
        name="out_ffn",
    )(x, oa, ob, gates, wa, wb, wo, g2, wu, wd, gf)


def _aug_placement():
    eq = [[0.0] * W_ATT for _ in range(LANES)]
    ek = [[0.0] * W_ATT for _ in range(LANES)]
    one = 3 * H_FOX
    for h in range(H_FOX):
        for piece in range(3):
            eq[piece * H_FOX + h][HEAD_DIM * h + piece] = 1.0
            eq[one][HEAD_DIM * h + 3 + piece] = 1.0
            ek[one][HEAD_DIM * h + piece] = 1.0
            ek[piece * H_FOX + h][HEAD_DIM * h + 3 + piece] = -1.0
    return jnp.array(eq, jnp.bfloat16), jnp.array(ek, jnp.bfloat16)


def kernel(x, norm1, w_in, forget_bias, rel_bias, w_branch_a, w_branch_b, w_out, norm2, w_up, w_down, final_norm):
    b, s, d = x.shape
    depth = w_in.shape[0]
    bf16 = jnp.bfloat16
    tm = min(512, s)
    tqc = s
    eq, ek = _aug_placement()
    head_of_col = lax.broadcasted_iota(jnp.int32, (W_ATT, LANES), 0) // HEAD_DIM
    hsum = (head_of_col == lax.broadcasted_iota(jnp.int32, (W_ATT, LANES), 1)).astype(bf16)
    gf = final_norm.reshape(1, d)
    o = 3 * W_ATT + H_FOX
    for l in range(depth):
        w = w_in[l]
        wq = (w[:, :W_ATT] * (SCALE * LOG2E)).astype(bf16)
        wk = w[:, W_ATT:2 * W_ATT].astype(bf16)
        wv = w[:, 2 * W_ATT:3 * W_ATT].astype(bf16)
        wf = jnp.pad(w[:, 3 * W_ATT:o], ((0, 0), (0, LANES - H_FOX))).astype(bf16)
        wc = jnp.concatenate([w[:, o:o + W_ATT] * (SCALE * LOG2E), w[:, o + W_ATT:o + 3 * W_ATT]],
                             axis=1).astype(bf16)
        wg = w[:, o + 3 * W_ATT:].astype(bf16)
        bf = jnp.pad(forget_bias[l], (0, LANES - H_FOX)).reshape(1, LANES)
        qf, kf, vf, qc, kc, vc, gates, stat, wo, wu, wd = _in_proj(
            x, norm1[l].reshape(1, d), wq, wk, wv, wc, wg, wf, bf, eq, ek, hsum,
            (w_out, w_up, w_down), l, tm)

        stats = stat[:, :, :4, :H_FOX].transpose(2, 0, 3, 1).reshape(4, -1)
        o_a = _fox(tuple(stats), qf, kf, vf, tm)

        far = rel_bias[l][:, 2 * MAX_REL:]
        g_ext = jnp.concatenate([jnp.broadcast_to(far, (H_CHK, PAD - MAX_REL + 1)),
                                 rel_bias[l][:, 2 * MAX_REL - 1:0:-1],
                                 jnp.broadcast_to(far, (H_CHK, Q_BLOCK))], axis=1)
        bias = _band_bias(g_ext.reshape(H_CHK, 1, -1)).reshape(N_PAIR, 2 * Q_BLOCK, BAND)
        o_b = _chunk_attention(qc, kc, vc, bias, tqc)

        x = _out_ffn(x, o_a, o_b, gates, w_branch_a[l].astype(bf16), w_branch_b[l].astype(bf16),
                     wo, norm2[l].reshape(1, d), wu, wd, gf, tm, final=(l == depth - 1))
    return x
```

```python
import functools
import math

import jax
import jax.numpy as jnp
from jax import lax
from jax.experimental import pallas as pl
from jax.experimental.pallas import tpu as pltpu

HEAD_DIM = 64
H_FOX = 8
H_CHK = 8
N_PAIR = 4
W_ATT = H_FOX * HEAD_DIM
CHUNK = 64
Q_BLOCK = 128
LEFT_CHUNKS = 8
PAD = LEFT_CHUNKS * CHUNK
BAND = PAD + Q_BLOCK
MAX_REL = 128
EPS = 1e-6
NEG = -1e30
LANES = 128
SCALE = 1.0 / math.sqrt(HEAD_DIM)
LOG2E = math.log2(math.e)
F32_BIG = 3.0e38
PROJ_SPLIT = 2
FOX_ROWS = 32
FOX_TRIP = 8
FOX_AHEAD = 2
CHUNK_ROWS = 32
CHUNK_GROUP = 32
SKIP_MARGIN = 30.0
NORM_SLACK = 1.02
VMEM_LIMIT = 60 * 1024 * 1024

_NT = (((1,), (1,)), ((), ()))


def _dot(a, b):
    return jnp.dot(a, b, preferred_element_type=jnp.float32)


def _dot_nt(a, b):
    return lax.dot_general(a, b, _NT, preferred_element_type=jnp.float32)


def _rms(x, g):
    ms = jnp.mean(x * x, axis=-1, keepdims=True)
    return x * lax.rsqrt(ms + EPS) * g


def _const_spec(shape):
    nd = len(shape)
    return pl.BlockSpec(shape, lambda *_: (0,) * nd, pipeline_mode=pl.Buffered(1))


def _in_proj_body(x_ref, g_ref, wq_ref, wk_ref, wv_ref, wc_ref, wg_ref, wf_ref, bf_ref, eq_ref, ek_ref, hsum_ref,
                  wo32_ref, wu32_ref, wd32_ref,
                  qf_ref, kf_ref, vf_ref, qc_ref, kc_ref, vc_ref, gate_ref, stat_ref, wo_ref, wu_ref, wd_ref,
                  carry_ref):
    tm = x_ref.shape[1]
    th = tm // PROJ_SPLIT

    wo_ref[...] = wo32_ref[...].astype(wo_ref.dtype)
    wu_ref[...] = wu32_ref[...].astype(wu_ref.dtype)
    wd_ref[...] = wd32_ref[...].astype(wd_ref.dtype)

    @pl.when(pl.program_id(1) == 0)
    def _():
        carry_ref[...] = jnp.zeros_like(carry_ref)

    lane = lax.broadcasted_iota(jnp.int32, (th, LANES), 1)
    row = lax.broadcasted_iota(jnp.int32, (th, LANES), 0)

    def max_norm(t):
        t = t.astype(jnp.float32)
        sq = _dot((t * t).astype(jnp.bfloat16), hsum_ref[...])
        return jnp.max(sq, axis=0, keepdims=True)

    c_first, c_last, q_sq, k_sq = None, None, None, None
    for part in range(PROJ_SPLIT):
        rows = slice(part * th, (part + 1) * th)
        h = _rms(x_ref[0, rows, :], g_ref[...]).astype(jnp.bfloat16)

        logf = jax.nn.log_sigmoid(_dot(h, wf_ref[...]) + bf_ref[...]) * LOG2E
        c = jnp.where(lane < H_FOX, logf, 0.0)
        k = 1
        while k < th:
            c = c + jnp.where(row >= k, pltpu.roll(c, k, 0), 0.0)
            k *= 2
        c = c + carry_ref[...]
        carry_ref[...] = c[th - 1:th, :]
        c_first = c[0:1, :] if part == 0 else c_first
        c_last = c[th - 1:th, :]

        hi = c.astype(jnp.bfloat16).astype(jnp.float32)
        r1 = c - hi
        mid = r1.astype(jnp.bfloat16).astype(jnp.float32)
        lo = (r1 - mid).astype(jnp.bfloat16).astype(jnp.float32)
        pieces = hi + pltpu.roll(mid, H_FOX, 1) + pltpu.roll(lo, 2 * H_FOX, 1)
        pieces = jnp.where(lane == 3 * H_FOX, 1.0, pieces).astype(jnp.bfloat16)
        aug_q = _dot(pieces, eq_ref[...]).astype(jnp.bfloat16)
        aug_k = _dot(pieces, ek_ref[...]).astype(jnp.bfloat16)

        q = _dot(h, wq_ref[...]).astype(jnp.bfloat16)
        kk = _dot(h, wk_ref[...]).astype(jnp.bfloat16)
        for p in range(N_PAIR):
            src = slice(p * LANES, (p + 1) * LANES)
            qf_ref[0, rows, 2 * p * LANES:(2 * p + 1) * LANES] = q[:, src]
            qf_ref[0, rows, (2 * p + 1) * LANES:(2 * p + 2) * LANES] = aug_q[:, src]
            kf_ref[0, rows, 2 * p * LANES:(2 * p + 1) * LANES] = kk[:, src]
            kf_ref[0, rows, (2 * p + 1) * LANES:(2 * p + 2) * LANES] = aug_k[:, src]
        vf_ref[0, rows, :] = _dot(h, wv_ref[...]).astype(jnp.bfloat16)
        q_sq = max_norm(q) if part == 0 else jnp.maximum(q_sq, max_norm(q))
        k_sq = max_norm(kk) if part == 0 else jnp.maximum(k_sq, max_norm(kk))

        pc = _dot(h, wc_ref[...])
        qc_ref[0, rows, :] = pc[:, :W_ATT].astype(jnp.bfloat16)
        kc_ref[0, rows, :] = pc[:, W_ATT:2 * W_ATT].astype(jnp.bfloat16)
        vc_ref[0, rows, :] = pc[:, 2 * W_ATT:].astype(jnp.bfloat16)
        gate_ref[0, rows, :] = jax.nn.sigmoid(_dot(h, wg_ref[...])).astype(jnp.bfloat16)

    sub = lax.broadcasted_iota(jnp.int32, (8, LANES), 0)
    stat = jnp.where(sub == 0, c_first, 0.0)
    stat = jnp.where(sub == 1, c_last, stat)
    stat = jnp.where(sub == 2, jnp.sqrt(q_sq), stat)
    stat_ref[0, 0] = jnp.where(sub == 3, jnp.sqrt(k_sq), stat)


def _in_proj(x, g, wq, wk, wv, wc, wg, wf, bf, eq, ek, hsum, to_cast, layer, tm):
    b, s, d = x.shape
    n_steps = b * (s // tm)
    tok = lambda width: pl.BlockSpec((1, tm, width), lambda i, j: (i, j, 0))
    strip_in = lambda w: pl.BlockSpec((None, w.shape[1] // n_steps, w.shape[2]),
                                      lambda i, j: (layer, i * (s // tm) + j, 0))
    strip_out = lambda w: pl.BlockSpec((w.shape[1] // n_steps, w.shape[2]), lambda i, j: (i * (s // tm) + j, 0))
    bf16 = jnp.bfloat16
    assert all(w.shape[1] % (16 * n_steps) == 0 for w in to_cast)
    out_shape = [jax.ShapeDtypeStruct((b, s, width), bf16)
                 for width in (2 * W_ATT, 2 * W_ATT, W_ATT, W_ATT, W_ATT, W_ATT, 2 * d)]
    out_shape.append(jax.ShapeDtypeStruct((b, s // tm, 8, LANES), jnp.float32))
    out_shape += [jax.ShapeDtypeStruct(w.shape[1:], bf16) for w in to_cast]
    consts = (g, wq, wk, wv, wc, wg, wf, bf, eq, ek, hsum)
    return pl.pallas_call(
        _in_proj_body,
        grid=(b, s // tm),
        in_specs=[tok(d)] + [_const_spec(a.shape) for a in consts] + [strip_in(w) for w in to_cast],
        out_specs=[tok(sh.shape[-1]) for sh in out_shape[:7]]
                  + [pl.BlockSpec((1, 1, 8, LANES), lambda i, j: (i, j, 0, 0))]
                  + [strip_out(w) for w in to_cast],
        out_shape=out_shape,
        scratch_shapes=[pltpu.VMEM((1, LANES), jnp.float32)],
        compiler_params=pltpu.CompilerParams(
            dimension_semantics=("arbitrary", "arbitrary"), vmem_limit_bytes=VMEM_LIMIT),
        name="in_proj",
    )(x, *consts, *to_cast)


def _fox_body(cfirst_ref, clast_ref, qnorm_ref, knorm_ref, q_ref, k_ref, v_ref, o_ref,
              s_ref, p_ref, m_ref, l_ref, acc_ref, bad_ref, *, tq):
    n_tiles = q_ref.shape[1] // tq
    lane2 = lax.broadcasted_iota(jnp.int32, (1, 2 * LANES), 1) % LANES
    head_a = lax.broadcasted_iota(jnp.int32, (1, LANES), 1) < HEAD_DIM
    base = [(pl.program_id(0) * H_FOX + 2 * pl.program_id(1) + head) * n_tiles for head in range(2)]

    def key_rows(j):
        return pl.ds(pl.multiple_of(j * tq, tq), tq)

    def scores(slot, q_head, j):
        s_ref[slot] = _dot_nt(q_head, k_ref[0, key_rows(j), :])

    def softmax(slot, head, mode):
        for r in range(0, tq, FOX_ROWS):
            rows = slice(r, r + FOX_ROWS)
            s = s_ref[slot, rows, :]
            if mode in ("diagonal", "diagonal_self"):
                row = lax.broadcasted_iota(jnp.int32, (FOX_ROWS, tq), 0) + r
                col = lax.broadcasted_iota(jnp.int32, (FOX_ROWS, tq), 1)
                if mode == "diagonal":
                    s = jnp.where(col <= row, s, NEG)
                    m = jnp.max(s, axis=-1, keepdims=True)
                else:
                    own = slice(r // LANES * LANES, (r // LANES + 1) * LANES)
                    lane = lax.broadcasted_iota(jnp.int32, (FOX_ROWS, LANES), 1)
                    sub = lax.broadcasted_iota(jnp.int32, (FOX_ROWS, LANES), 0)
                    m = jnp.sum(jnp.where(lane == sub + r % LANES, s[:, own], 0.0), axis=-1, keepdims=True)
                    s = jnp.where(col <= row, s, NEG)
                m = jnp.broadcast_to(m, (FOX_ROWS, LANES))
                m_ref[head, rows, :] = m
            elif mode == "exact":
                m_old = m_ref[head, rows, :]
                m = jnp.maximum(m_old, jnp.max(s, axis=-1, keepdims=True))
                alpha = jnp.exp2(m_old - m)
                m_ref[head, rows, :] = m
            else:
                m = m_ref[head, rows, :]
            lsum = None
            for g in range(tq // LANES):
                cols = slice(g * LANES, (g + 1) * LANES)
                pg = jnp.exp2(s[:, cols] - m)
                p_ref[slot, rows, cols] = pg.astype(p_ref.dtype)
                lsum = pg if lsum is None else lsum + pg
            if mode in ("diagonal", "diagonal_self"):
                l_ref[head, rows, :] = lsum
            elif mode == "exact":
                l_ref[head, rows, :] = l_ref[head, rows, :] * alpha + lsum
                acc_ref[head, rows, :] = acc_ref[head, rows, :] * alpha
            else:
                l_ref[head, rows, :] = l_ref[head, rows, :] + lsum

    def values(slot, head, j, mode):
        pv = _dot(p_ref[slot], v_ref[0, key_rows(j), :])
        acc_ref[head] = pv if mode in ("diagonal", "diagonal_self") else acc_ref[head] + pv

    half = tq // 2

    def half_rows(j, part):
        return pl.ds(pl.multiple_of(j * tq + part * half, half), half)

    def scores_diagonal(slot, q_head, j):
        s_ref[slot, :, :half] = _dot_nt(q_head, k_ref[0, half_rows(j, 0), :])
        s_ref[slot, half:, half:] = _dot_nt(q_head[half:], k_ref[0, half_rows(j, 1), :])

    def softmax_diagonal(slot, head):
        for r in range(0, tq, FOX_ROWS):
            rows = slice(r, r + FOX_ROWS)
            width = half if r < half else tq
            s = s_ref[slot, rows, :width]
            row = lax.broadcasted_iota(jnp.int32, (FOX_ROWS, width), 0) + r
            col = lax.broadcasted_iota(jnp.int32, (FOX_ROWS, width), 1)
            own = slice(r // LANES * LANES, (r // LANES + 1) * LANES)
            lane = lax.broadcasted_iota(jnp.int32, (FOX_ROWS, LANES), 1)
            sub = lax.broadcasted_iota(jnp.int32, (FOX_ROWS, LANES), 0)
            m = jnp.sum(jnp.where(lane == sub + r % LANES, s[:, own], 0.0), axis=-1, keepdims=True)
            m = jnp.broadcast_to(m, (FOX_ROWS, LANES))
            s = jnp.where(col <= row, s, NEG)
            m_ref[head, rows, :] = m
            lsum = None
            for g in range(width // LANES):
                cols = slice(g * LANES, (g + 1) * LANES)
                pg = jnp.exp2(s[:, cols] - m)
                p_ref[slot, rows, cols] = pg.astype(p_ref.dtype)
                lsum = pg if lsum is None else lsum + pg
            l_ref[head, rows, :] = lsum

    def values_diagonal(slot, head, j):
        acc_ref[head] = _dot(p_ref[slot, :, :half], v_ref[0, half_rows(j, 0), :])
        acc_ref[head, half:, :] = acc_ref[head, half:, :] + _dot(p_ref[slot, half:, half:],
                                                                 v_ref[0, half_rows(j, 1), :])

    def unit_run(q_of, units):
        n_slots = s_ref.shape[0]

        def issue_scores(n):
            tile, head, mode = units[n]
            (scores_diagonal if mode == "diagonal_self" else scores)(n % n_slots, q_of(head), tile)

        for n in range(min(FOX_AHEAD, len(units))):
            issue_scores(n)
        for n, (tile, head, mode) in enumerate(units):
            if mode == "diagonal_self":
                softmax_diagonal(n % n_slots, head)
            else:
                softmax(n % n_slots, head, mode)
            if n + FOX_AHEAD < len(units):
                issue_scores(n + FOX_AHEAD)
            if mode == "diagonal_self":
                values_diagonal(n % n_slots, head, tile)
            else:
                values(n % n_slots, head, tile, mode)

    def run_tiles(q_of, make_units, first, total):
        def full(t, carry):
            unit_run(q_of, make_units(first - FOX_TRIP * t, FOX_TRIP))
            return carry

        n_full = total // FOX_TRIP
        lax.fori_loop(0, n_full, full, 0)
        done = n_full * FOX_TRIP
        count = FOX_TRIP // 2
        while count >= 1:
            fits = (total - done) >= count

            @pl.when(fits)
            def _(start=first - done, count=count):
                unit_run(q_of, make_units(start, count))

            done = jnp.where(fits, done + count, done)
            count //= 2

    def query_tile(qi, kmax, exact):
        q_rows = pl.ds(pl.multiple_of(qi * tq, tq), tq)
        qcat = q_ref[0, q_rows, :]
        zero = jnp.zeros_like(qcat)
        qh = (jnp.where(lane2 < HEAD_DIM, qcat, zero), jnp.where(lane2 >= HEAD_DIM, qcat, zero))

        kmax = [jnp.maximum(kmax[head], knorm_ref[base[head] + qi]) for head in range(2)]
        reach = []
        for head in range(2):
            bound = (cfirst_ref[base[head] + qi] + 2.0 * NORM_SLACK * qnorm_ref[base[head] + qi] * kmax[head]
                     + SKIP_MARGIN * LOG2E)

            def reaches(n, head=head, bound=bound):
                t = jnp.maximum(qi - 1 - n, 0)
                return (n < qi) & (bound - clast_ref[base[head] + t] > 0.0)

            reach.append(lax.while_loop(reaches, lambda n: n + 1, 0))
        n_both, n_long = jnp.minimum(reach[0], reach[1]), jnp.maximum(reach[0], reach[1])

        def both_heads(j, count, first_mode="lazy"):
            return [(j - i, head, first_mode if i == 0 else "lazy") for i in range(count) for head in range(2)]

        if exact:
            unit_run(qh.__getitem__, both_heads(qi, 1, "diagonal"))

            def step(t, carry):
                unit_run(qh.__getitem__, both_heads(qi - 1 - t, 1, "exact"))
                return carry

            lax.fori_loop(0, n_long, step, 0)
        else:
            total = n_both + 1
            count, first_len = FOX_TRIP, 1
            while count >= 1:
                fits = total >= count
                if count < FOX_TRIP:
                    fits = fits & (total < 2 * count)

                @pl.when(fits)
                def _(count=count):
                    unit_run(qh.__getitem__, both_heads(qi, count, "diagonal_self"))

                first_len = jnp.where(fits, count, first_len)
                count //= 2
            run_tiles(qh.__getitem__, both_heads, qi - first_len, total - first_len)

            long_head = jnp.where(reach[1] > reach[0], 1, 0)
            q_long = jnp.where((lane2 < HEAD_DIM) == (long_head == 0), qcat, zero)
            run_tiles(lambda head: q_long, lambda j, count: [(j - i, long_head, "lazy") for i in range(count)],
                      qi - total, n_long - n_both)

        outs = []
        for head in range(2):
            acc, lp = acc_ref[head], l_ref[head]
            outs.append(acc / jnp.sum(lp, axis=-1, keepdims=True))
            if not exact:
                finite = (jnp.abs(acc) <= F32_BIG) & (lp <= F32_BIG)
                bad_ref[...] = jnp.maximum(bad_ref[...], jnp.where(finite, 0.0, 1.0))
        o_ref[0, q_rows, :] = jnp.where(head_a, outs[0], outs[1]).astype(o_ref.dtype)
        return kmax

    def sweep(exact):
        def body(qi, kmax):
            return tuple(query_tile(qi, kmax, exact))

        lax.fori_loop(0, n_tiles, body, (jnp.float32(0.0), jnp.float32(0.0)))

    bad_ref[...] = jnp.zeros_like(bad_ref)
    sweep(exact=False)

    @pl.when(jnp.max(bad_ref[...]) > 0.0)
    def _():
        sweep(exact=True)


def _fox(stats, qf, kf, vf, tq):
    b, s, _ = qf.shape
    smem = pl.BlockSpec(memory_space=pltpu.SMEM)
    return pl.pallas_call(
        functools.partial(_fox_body, tq=tq),
        grid=(b, N_PAIR),
        in_specs=[smem, smem, smem, smem,
                  pl.BlockSpec((1, s, 2 * LANES), lambda i, p: (i, 0, p)),
                  pl.BlockSpec((1, s, 2 * LANES), lambda i, p: (i, 0, p)),
                  pl.BlockSpec((1, s, LANES), lambda i, p: (i, 0, p))],
        out_specs=pl.BlockSpec((1, s, LANES), lambda i, p: (i, 0, p)),
        out_shape=jax.ShapeDtypeStruct((b, s, W_ATT), jnp.bfloat16),
        scratch_shapes=[pltpu.VMEM((2 * FOX_AHEAD, tq, tq), jnp.float32),
                        pltpu.VMEM((2 * FOX_AHEAD, tq, tq), jnp.bfloat16),
                        pltpu.VMEM((2, tq, LANES), jnp.float32),
                        pltpu.VMEM((2, tq, LANES), jnp.float32),
                        pltpu.VMEM((2, tq, LANES), jnp.float32),
                        pltpu.VMEM((tq, LANES), jnp.float32)],
        compiler_params=pltpu.CompilerParams(
            dimension_semantics=("arbitrary", "arbitrary"), vmem_limit_bytes=VMEM_LIMIT),
        name="fox_attention",
    )(*stats, qf, kf, vf)


def _band_bias_body(g_ref, o_ref):
    width = g_ref.shape[-1]
    base = jnp.broadcast_to(g_ref[0], (Q_BLOCK, width))
    toeplitz = pltpu.roll(base, 0, 1, stride=1, stride_axis=0)[:, :BAND]
    qrow = lax.broadcasted_iota(jnp.int32, (Q_BLOCK, BAND), 0)
    kcol = lax.broadcasted_iota(jnp.int32, (Q_BLOCK, BAND), 1)
    cq = qrow // CHUNK
    ck = kcol // CHUNK - LEFT_CHUNKS
    valid = (ck <= cq) & (ck >= cq - LEFT_CHUNKS)
    o_ref[0] = jnp.where(valid, toeplitz * LOG2E, NEG)


def _band_bias(g_ext):
    h, _, width = g_ext.shape
    return pl.pallas_call(
        _band_bias_body,
        grid=(h,),
        in_specs=[pl.BlockSpec((1, 1, width), lambda i: (i, 0, 0))],
        out_specs=pl.BlockSpec((1, Q_BLOCK, BAND), lambda i: (i, 0, 0)),
        out_shape=jax.ShapeDtypeStruct((h, Q_BLOCK, BAND), jnp.float32),
        name="band_bias",
    )(g_ext)


def _chunk_body(q_ref, k_ref, v_ref, bias_ref, o_ref, kpad_ref, vpad_ref, s_ref, p_ref, l_ref, *, tq):
    qi = pl.program_id(2)
    s_len = k_ref.shape[1]

    @pl.when(qi == 0)
    def _():
        zeros = jnp.zeros((PAD, LANES), kpad_ref.dtype)
        kpad_ref[:PAD, :] = zeros
        vpad_ref[:PAD, :] = zeros
        kpad_ref[PAD:PAD + s_len, :] = k_ref[0]
        vpad_ref[PAD:PAD + s_len, :] = v_ref[0]

    head_a = lax.broadcasted_iota(jnp.int32, (1, LANES), 1) < HEAD_DIM
    kcol = lax.broadcasted_iota(jnp.int32, (1, BAND), 1)

    def offsets(blk):
        r0 = pl.multiple_of(blk * Q_BLOCK, Q_BLOCK)
        return r0, pl.multiple_of(qi * tq + r0, Q_BLOCK)

    def scores(slot, blk):
        r0, p0 = offsets(blk)
        q2 = q_ref[0, pl.ds(r0, Q_BLOCK), :]
        zero = jnp.zeros_like(q2)
        qs = jnp.concatenate([jnp.where(head_a, q2, zero), jnp.where(head_a, zero, q2)], axis=0)
        s_ref[slot] = _dot_nt(qs, kpad_ref[pl.ds(p0, BAND), :])

    def softmax(slot, blk, masked):
        _, p0 = offsets(blk)
        for r in range(0, 2 * Q_BLOCK, CHUNK_ROWS):
            rows = slice(r, r + CHUNK_ROWS)
            s = s_ref[slot, rows, :] + bias_ref[0, rows, :]
            if masked:
                s = jnp.where(kcol + p0 >= PAD, s, NEG)
            p = jnp.exp2(s - jnp.max(s, axis=-1, keepdims=True))
            l_ref[slot, rows, :] = jnp.broadcast_to(jnp.sum(p, axis=-1, keepdims=True), (CHUNK_ROWS, LANES))
            p_ref[slot, rows, :] = p.astype(p_ref.dtype)

    def values(slot, blk):
        r0, p0 = offsets(blk)
        o = _dot(p_ref[slot], vpad_ref[pl.ds(p0, BAND), :]) / l_ref[slot]
        o_ref[0, pl.ds(r0, Q_BLOCK), :] = jnp.where(head_a, o[:Q_BLOCK], o[Q_BLOCK:]).astype(o_ref.dtype)

    group_blocks = min(CHUNK_GROUP, tq // Q_BLOCK)

    def group(i, masked):
        first = i * group_blocks
        scores(0, first)
        for n in range(group_blocks):
            if n + 1 < group_blocks:
                scores((n + 1) % 2, first + n + 1)
            softmax(n % 2, first + n, masked and n < PAD // Q_BLOCK)
            values(n % 2, first + n)

    def masked_group(i, carry):
        group(i, True)
        return carry

    def plain_group(i, carry):
        group(i, False)
        return carry

    group_rows = Q_BLOCK * group_blocks
    n_groups = tq // group_rows
    n_masked = jnp.clip((PAD - qi * tq + group_rows - 1) // group_rows, 0, n_groups)
    lax.fori_loop(0, n_masked, masked_group, 0)
    lax.fori_loop(n_masked, n_groups, plain_group, 0)


def _chunk_attention(qc, kc, vc, bias, tq):
    b, s, _ = qc.shape
    assert tq % (min(CHUNK_GROUP, tq // Q_BLOCK) * Q_BLOCK) == 0
    return pl.pallas_call(
        functools.partial(_chunk_body, tq=tq),
        grid=(b, N_PAIR, s // tq),
        in_specs=[pl.BlockSpec((1, tq, LANES), lambda i, p, j: (i, j, p)),
                  pl.BlockSpec((1, s, LANES), lambda i, p, j: (i, 0, p)),
                  pl.BlockSpec((1, s, LANES), lambda i, p, j: (i, 0, p)),
                  pl.BlockSpec((1, 2 * Q_BLOCK, BAND), lambda i, p, j: (p, 0, 0))],
        out_specs=pl.BlockSpec((1, tq, LANES), lambda i, p, j: (i, j, p)),
        out_shape=jax.ShapeDtypeStruct((b, s, W_ATT), jnp.bfloat16),
        scratch_shapes=[pltpu.VMEM((PAD + s, LANES), jnp.bfloat16),
                        pltpu.VMEM((PAD + s, LANES), jnp.bfloat16),
                        pltpu.VMEM((2, 2 * Q_BLOCK, BAND), jnp.float32),
                        pltpu.VMEM((2, 2 * Q_BLOCK, BAND), jnp.bfloat16),
                        pltpu.VMEM((2, 2 * Q_BLOCK, LANES), jnp.float32)],
        compiler_params=pltpu.CompilerParams(
            dimension_semantics=("arbitrary", "arbitrary", "arbitrary"), vmem_limit_bytes=VMEM_LIMIT),
        name="chunk_attention",
    )(qc, kc, vc, bias)


def _out_ffn_body(x_ref, oa_ref, ob_ref, gate_ref, wa_ref, wb_ref, wo_ref, g2_ref, wu_ref, wd_ref, gf_ref,
                  o_ref, *, ff_chunk, final):
    d = x_ref.shape[-1]
    ya = _dot(oa_ref[0], wa_ref[...])
    yb = _dot(ob_ref[0], wb_ref[...])
    ga = gate_ref[0, :, :d].astype(jnp.float32)
    gb = gate_ref[0, :, d:].astype(jnp.float32)
    merged = (ga * ya + gb * yb).astype(jnp.bfloat16)
    x1 = x_ref[0] + _dot(merged, wo_ref[...])
    h2 = _rms(x1, g2_ref[...]).astype(jnp.bfloat16)
    acc = x1
    for c0 in range(0, wu_ref.shape[1], ff_chunk):
        u = jnp.maximum(_dot(h2, wu_ref[:, c0:c0 + ff_chunk]), 0.0)
        acc = acc + _dot((u * u).astype(jnp.bfloat16), wd_ref[c0:c0 + ff_chunk, :])
    if final:
        acc = _rms(acc, gf_ref[...])
    o_ref[0] = acc


def _out_ffn(x, oa, ob, gates, wa, wb, wo, g2, wu, wd, gf, tm, final):
    b, s, d = x.shape
    tok = lambda w: pl.BlockSpec((1, tm, w), lambda i, j: (i, j, 0))
    return pl.pallas_call(
        functools.partial(_out_ffn_body, ff_chunk=min(1024, wu.shape[1]), final=final),
        grid=(b, s // tm),
        in_specs=[tok(d), tok(W_ATT), tok(W_ATT), tok(2 * d)]
                 + [_const_spec(a.shape) for a in (wa, wb, wo, g2, wu, wd, gf)],
        out_specs=tok(d),
        out_shape=jax.ShapeDtypeStruct((b, s, d), jnp.float32),
        compiler_params=pltpu.CompilerParams(
            dimension_semantics=("arbitrary", "arbitrary"), vmem_limit_bytes=VMEM_LIMIT),
        name="out_ffn",
    )(x, oa, ob, gates, wa, wb, wo, g2, wu, wd, gf)


def _aug_placement():
    eq = [[0.0] * W_ATT for _ in range(LANES)]
    ek = [[0.0] * W_ATT for _ in range(LANES)]
    one = 3 * H_FOX
    for h in range(H_FOX):
        for piece in range(3):
            eq[piece * H_FOX + h][HEAD_DIM * h + piece] = 1.0
            eq[one][HEAD_DIM * h + 3 + piece] = 1.0
            ek[one][HEAD_DIM * h + piece] = 1.0
            ek[piece * H_FOX + h][HEAD_DIM * h + 3 + piece] = -1.0
    return jnp.array(eq, jnp.bfloat16), jnp.array(ek, jnp.bfloat16)


def kernel(x, norm1, w_in, forget_bias, rel_bias, w_branch_a, w_branch_b, w_out, norm2, w_up, w_down, final_norm):
    b, s, d = x.shape
    depth = w_in.shape[0]
    bf16 = jnp.bfloat16
    tm = min(512, s)
    tqc = s
    eq, ek = _aug_placement()
    head_of_col = lax.broadcasted_iota(jnp.int32, (W_ATT, LANES), 0) // HEAD_DIM
    hsum = (head_of_col == lax.broadcasted_iota(jnp.int32, (W_ATT, LANES), 1)).astype(bf16)
    gf = final_norm.reshape(1, d)
    o = 3 * W_ATT + H_FOX
    for l in range(depth):
        w = w_in[l]
        wq = (w[:, :W_ATT] * (SCALE * LOG2E)).astype(bf16)
        wk = w[:, W_ATT:2 * W_ATT].astype(bf16)
        wv = w[:, 2 * W_ATT:3 * W_ATT].astype(bf16)
        wf = jnp.pad(w[:, 3 * W_ATT:o], ((0, 0), (0, LANES - H_FOX))).astype(bf16)
        wc = jnp.concatenate([w[:, o:o + W_ATT] * (SCALE * LOG2E), w[:, o + W_ATT:o + 3 * W_ATT]],
                             axis=1).astype(bf16)
        wg = w[:, o + 3 * W_ATT:].astype(bf16)
        bf = jnp.pad(forget_bias[l], (0, LANES - H_FOX)).reshape(1, LANES)
        qf, kf, vf, qc, kc, vc, gates, stat, wo, wu, wd = _in_proj(
            x, norm1[l].reshape(1, d), wq, wk, wv, wc, wg, wf, bf, eq, ek, hsum,
            (w_out, w_up, w_down), l, tm)

        stats = stat[:, :, :4, :H_FOX].transpose(2, 0, 3, 1).reshape(4, -1)
        o_a = _fox(tuple(stats), qf, kf, vf, tm)

        far = rel_bias[l][:, 2 * MAX_REL:]
        g_ext = jnp.concatenate([jnp.broadcast_to(far, (H_CHK, PAD - MAX_REL + 1)),
                                 rel_bias[l][:, 2 * MAX_REL - 1:0:-1],
                                 jnp.broadcast_to(far, (H_CHK, Q_BLOCK))], axis=1)
        bias = _band_bias(g_ext.reshape(H_CHK, 1, -1)).reshape(N_PAIR, 2 * Q_BLOCK, BAND)
        o_b = _chunk_attention(qc, kc, vc, bias, tqc)

        x = _out_ffn(x, o_a, o_b, gates, w_branch_a[l].astype(bf16), w_branch_b[l].astype(bf16),
                     wo, norm2[l].reshape(1, d), wu, wd, gf, tm, final=(l == depth - 1))
    return x
```

```python
import functools
import math

import jax
import jax.numpy as jnp
from jax import lax
from jax.experimental import pallas as pl
from jax.experimental.pallas import tpu as pltpu

HEAD_DIM = 64
H_FOX = 8
H_CHK = 8
N_PAIR = 4
W_ATT = H_FOX * HEAD_DIM
CHUNK = 64
Q_BLOCK = 128
LEFT_CHUNKS = 8
PAD = LEFT_CHUNKS * CHUNK
BAND = PAD + Q_BLOCK
MAX_REL = 128
EPS = 1e-6
NEG = -1e30
LANES = 128
SCALE = 1.0 / math.sqrt(HEAD_DIM)
LOG2E = math.log2(math.e)
F32_BIG = 3.0e38
PROJ_SPLIT = 2
FOX_ROWS = 32
FOX_TRIP = 8
FOX_AHEAD = 2
CHUNK_ROWS = 32
CHUNK_GROUP = 8
SKIP_MARGIN = 30.0
NORM_SLACK = 1.02
VMEM_LIMIT = 60 * 1024 * 1024

_NT = (((1,), (1,)), ((), ()))


def _dot(a, b):
    return jnp.dot(a, b, preferred_element_type=jnp.float32)


def _dot_nt(a, b):
    return lax.dot_general(a, b, _NT, preferred_element_type=jnp.float32)


def _rms(x, g):
    ms = jnp.mean(x * x, axis=-1, keepdims=True)
    return x * lax.rsqrt(ms + EPS) * g


def _const_spec(shape):
    nd = len(shape)
    return pl.BlockSpec(shape, lambda *_: (0,) * nd, pipeline_mode=pl.Buffered(1))


def _in_proj_body(x_ref, g_ref, wq_ref, wk_ref, wv_ref, wc_ref, wg_ref, wf_ref, bf_ref, eq_ref, ek_ref, hsum_ref,
                  wo32_ref, wu32_ref, wd32_ref,
                  qf_ref, kf_ref, vf_ref, qc_ref, kc_ref, vc_ref, gate_ref, stat_ref, wo_ref, wu_ref, wd_ref,
                  carry_ref):
    tm = x_ref.shape[1]
    th = tm // PROJ_SPLIT

    wo_ref[...] = wo32_ref[...].astype(wo_ref.dtype)
    wu_ref[...] = wu32_ref[...].astype(wu_ref.dtype)
    wd_ref[...] = wd32_ref[...].astype(wd_ref.dtype)

    @pl.when(pl.program_id(1) == 0)
    def _():
        carry_ref[...] = jnp.zeros_like(carry_ref)

    lane = lax.broadcasted_iota(jnp.int32, (th, LANES), 1)
    row = lax.broadcasted_iota(jnp.int32, (th, LANES), 0)

    def max_norm(t):
        t = t.astype(jnp.float32)
        sq = _dot((t * t).astype(jnp.bfloat16), hsum_ref[...])
        return jnp.max(sq, axis=0, keepdims=True)

    c_first, c_last, q_sq, k_sq = None, None, None, None
    for part in range(PROJ_SPLIT):
        rows = slice(part * th, (part + 1) * th)
        h = _rms(x_ref[0, rows, :], g_ref[...]).astype(jnp.bfloat16)

        logf = jax.nn.log_sigmoid(_dot(h, wf_ref[...]) + bf_ref[...]) * LOG2E
        c = jnp.where(lane < H_FOX, logf, 0.0)
        k = 1
        while k < th:
            c = c + jnp.where(row >= k, pltpu.roll(c, k, 0), 0.0)
            k *= 2
        c = c + carry_ref[...]
        carry_ref[...] = c[th - 1:th, :]
        c_first = c[0:1, :] if part == 0 else c_first
        c_last = c[th - 1:th, :]

        hi = c.astype(jnp.bfloat16).astype(jnp.float32)
        r1 = c - hi
        mid = r1.astype(jnp.bfloat16).astype(jnp.float32)
        lo = (r1 - mid).astype(jnp.bfloat16).astype(jnp.float32)
        pieces = hi + pltpu.roll(mid, H_FOX, 1) + pltpu.roll(lo, 2 * H_FOX, 1)
        pieces = jnp.where(lane == 3 * H_FOX, 1.0, pieces).astype(jnp.bfloat16)
        aug_q = _dot(pieces, eq_ref[...]).astype(jnp.bfloat16)
        aug_k = _dot(pieces, ek_ref[...]).astype(jnp.bfloat16)

        q = _dot(h, wq_ref[...]).astype(jnp.bfloat16)
        kk = _dot(h, wk_ref[...]).astype(jnp.bfloat16)
        for p in range(N_PAIR):
            src = slice(p * LANES, (p + 1) * LANES)
            qf_ref[0, rows, 2 * p * LANES:(2 * p + 1) * LANES] = q[:, src]
            qf_ref[0, rows, (2 * p + 1) * LANES:(2 * p + 2) * LANES] = aug_q[:, src]
            kf_ref[0, rows, 2 * p * LANES:(2 * p + 1) * LANES] = kk[:, src]
            kf_ref[0, rows, (2 * p + 1) * LANES:(2 * p + 2) * LANES] = aug_k[:, src]
        vf_ref[0, rows, :] = _dot(h, wv_ref[...]).astype(jnp.bfloat16)
        q_sq = max_norm(q) if part == 0 else jnp.maximum(q_sq, max_norm(q))
        k_sq = max_norm(kk) if part == 0 else jnp.maximum(k_sq, max_norm(kk))

        pc = _dot(h, wc_ref[...])
        qc_ref[0, rows, :] = pc[:, :W_ATT].astype(jnp.bfloat16)
        kc_ref[0, rows, :] = pc[:, W_ATT:2 * W_ATT].astype(jnp.bfloat16)
        vc_ref[0, rows, :] = pc[:, 2 * W_ATT:].astype(jnp.bfloat16)
        gate_ref[0, rows, :] = jax.nn.sigmoid(_dot(h, wg_ref[...])).astype(jnp.bfloat16)

    sub = lax.broadcasted_iota(jnp.int32, (8, LANES), 0)
    stat = jnp.where(sub == 0, c_first, 0.0)
    stat = jnp.where(sub == 1, c_last, stat)
    stat = jnp.where(sub == 2, jnp.sqrt(q_sq), stat)
    stat_ref[0, 0] = jnp.where(sub == 3, jnp.sqrt(k_sq), stat)


def _in_proj(x, g, wq, wk, wv, wc, wg, wf, bf, eq, ek, hsum, to_cast, layer, tm):
    b, s, d = x.shape
    n_steps = b * (s // tm)
    tok = lambda width: pl.BlockSpec((1, tm, width), lambda i, j: (i, j, 0))
    strip_in = lambda w: pl.BlockSpec((None, w.shape[1] // n_steps, w.shape[2]),
                                      lambda i, j: (layer, i * (s // tm) + j, 0))
    strip_out = lambda w: pl.BlockSpec((w.shape[1] // n_steps, w.shape[2]), lambda i, j: (i * (s // tm) + j, 0))
    bf16 = jnp.bfloat16
    assert all(w.shape[1] % (16 * n_steps) == 0 for w in to_cast)
    out_shape = [jax.ShapeDtypeStruct((b, s, width), bf16)
                 for width in (2 * W_ATT, 2 * W_ATT, W_ATT, W_ATT, W_ATT, W_ATT, 2 * d)]
    out_shape.append(jax.ShapeDtypeStruct((b, s // tm, 8, LANES), jnp.float32))
    out_shape += [jax.ShapeDtypeStruct(w.shape[1:], bf16) for w in to_cast]
    consts = (g, wq, wk, wv, wc, wg, wf, bf, eq, ek, hsum)
    return pl.pallas_call(
        _in_proj_body,
        grid=(b, s // tm),
        in_specs=[tok(d)] + [_const_spec(a.shape) for a in consts] + [strip_in(w) for w in to_cast],
        out_specs=[tok(sh.shape[-1]) for sh in out_shape[:7]]
                  + [pl.BlockSpec((1, 1, 8, LANES), lambda i, j: (i, j, 0, 0))]
                  + [strip_out(w) for w in to_cast],
        out_shape=out_shape,
        scratch_shapes=[pltpu.VMEM((1, LANES), jnp.float32)],
        compiler_params=pltpu.CompilerParams(
            dimension_semantics=("arbitrary", "arbitrary"), vmem_limit_bytes=VMEM_LIMIT),
        name="in_proj",
    )(x, *consts, *to_cast)


def _fox_body(cfirst_ref, clast_ref, qnorm_ref, knorm_ref, q_ref, k_ref, v_ref, o_ref,
              s_ref, p_ref, m_ref, l_ref, acc_ref, bad_ref, *, tq):
    n_tiles = q_ref.shape[1] // tq
    lane2 = lax.broadcasted_iota(jnp.int32, (1, 2 * LANES), 1) % LANES
    head_a = lax.broadcasted_iota(jnp.int32, (1, LANES), 1) < HEAD_DIM
    base = [(pl.program_id(0) * H_FOX + 2 * pl.program_id(1) + head) * n_tiles for head in range(2)]

    def key_rows(j):
        return pl.ds(pl.multiple_of(j * tq, tq), tq)

    def scores(slot, q_head, j):
        s_ref[slot] = _dot_nt(q_head, k_ref[0, key_rows(j), :])

    def softmax(slot, head, mode):
        for r in range(0, tq, FOX_ROWS):
            rows = slice(r, r + FOX_ROWS)
            s = s_ref[slot, rows, :]
            if mode in ("diagonal", "diagonal_self"):
                row = lax.broadcasted_iota(jnp.int32, (FOX_ROWS, tq), 0) + r
                col = lax.broadcasted_iota(jnp.int32, (FOX_ROWS, tq), 1)
                if mode == "diagonal":
                    s = jnp.where(col <= row, s, NEG)
                    m = jnp.max(s, axis=-1, keepdims=True)
                else:
                    own = slice(r // LANES * LANES, (r // LANES + 1) * LANES)
                    lane = lax.broadcasted_iota(jnp.int32, (FOX_ROWS, LANES), 1)
                    sub = lax.broadcasted_iota(jnp.int32, (FOX_ROWS, LANES), 0)
                    m = jnp.sum(jnp.where(lane == sub + r % LANES, s[:, own], 0.0), axis=-1, keepdims=True)
                    s = jnp.where(col <= row, s, NEG)
                m = jnp.broadcast_to(m, (FOX_ROWS, LANES))
                m_ref[head, rows, :] = m
            elif mode == "exact":
                m_old = m_ref[head, rows, :]
                m = jnp.maximum(m_old, jnp.max(s, axis=-1, keepdims=True))
                alpha = jnp.exp2(m_old - m)
                m_ref[head, rows, :] = m
            else:
                m = m_ref[head, rows, :]
            lsum = None
            for g in range(tq // LANES):
                cols = slice(g * LANES, (g + 1) * LANES)
                pg = jnp.exp2(s[:, cols] - m)
                p_ref[slot, rows, cols] = pg.astype(p_ref.dtype)
                lsum = pg if lsum is None else lsum + pg
            if mode in ("diagonal", "diagonal_self"):
                l_ref[head, rows, :] = lsum
            elif mode == "exact":
                l_ref[head, rows, :] = l_ref[head, rows, :] * alpha + lsum
                acc_ref[head, rows, :] = acc_ref[head, rows, :] * alpha
            else:
                l_ref[head, rows, :] = l_ref[head, rows, :] + lsum

    def values(slot, head, j, mode):
        pv = _dot(p_ref[slot], v_ref[0, key_rows(j), :])
        acc_ref[head] = pv if mode in ("diagonal", "diagonal_self") else acc_ref[head] + pv

    half = tq // 2

    def half_rows(j, part):
        return pl.ds(pl.multiple_of(j * tq + part * half, half), half)

    def scores_diagonal(slot, q_head, j):
        s_ref[slot, :, :half] = _dot_nt(q_head, k_ref[0, half_rows(j, 0), :])
        s_ref[slot, half:, half:] = _dot_nt(q_head[half:], k_ref[0, half_rows(j, 1), :])

    def softmax_diagonal(slot, head):
        for r in range(0, tq, FOX_ROWS):
            rows = slice(r, r + FOX_ROWS)
            width = half if r < half else tq
            s = s_ref[slot, rows, :width]
            row = lax.broadcasted_iota(jnp.int32, (FOX_ROWS, width), 0) + r
            col = lax.broadcasted_iota(jnp.int32, (FOX_ROWS, width), 1)
            own = slice(r // LANES * LANES, (r // LANES + 1) * LANES)
            lane = lax.broadcasted_iota(jnp.int32, (FOX_ROWS, LANES), 1)
            sub = lax.broadcasted_iota(jnp.int32, (FOX_ROWS, LANES), 0)
            m = jnp.sum(jnp.where(lane == sub + r % LANES, s[:, own], 0.0), axis=-1, keepdims=True)
            m = jnp.broadcast_to(m, (FOX_ROWS, LANES))
            s = jnp.where(col <= row, s, NEG)
            m_ref[head, rows, :] = m
            lsum = None
            for g in range(width // LANES):
                cols = slice(g * LANES, (g + 1) * LANES)
                pg = jnp.exp2(s[:, cols] - m)
                p_ref[slot, rows, cols] = pg.astype(p_ref.dtype)
                lsum = pg if lsum is None else lsum + pg
            l_ref[head, rows, :] = lsum

    def values_diagonal(slot, head, j):
        acc_ref[head] = _dot(p_ref[slot, :, :half], v_ref[0, half_rows(j, 0), :])
        acc_ref[head, half:, :] = acc_ref[head, half:, :] + _dot(p_ref[slot, half:, half:],
                                                                 v_ref[0, half_rows(j, 1), :])

    def unit_run(q_of, units):
        n_slots = s_ref.shape[0]

        def issue_scores(n):
            tile, head, mode = units[n]
            (scores_diagonal if mode == "diagonal_self" else scores)(n % n_slots, q_of(head), tile)

        for n in range(min(FOX_AHEAD, len(units))):
            issue_scores(n)
        for n, (tile, head, mode) in enumerate(units):
            if mode == "diagonal_self":
                softmax_diagonal(n % n_slots, head)
            else:
                softmax(n % n_slots, head, mode)
            if n + FOX_AHEAD < len(units):
                issue_scores(n + FOX_AHEAD)
            if mode == "diagonal_self":
                values_diagonal(n % n_slots, head, tile)
            else:
                values(n % n_slots, head, tile, mode)

    def run_tiles(q_of, make_units, first, total):
        def full(t, carry):
            unit_run(q_of, make_units(first - FOX_TRIP * t, FOX_TRIP))
            return carry

        n_full = total // FOX_TRIP
        lax.fori_loop(0, n_full, full, 0)
        done = n_full * FOX_TRIP
        count = FOX_TRIP // 2
        while count >= 1:
            fits = (total - done) >= count

            @pl.when(fits)
            def _(start=first - done, count=count):
                unit_run(q_of, make_units(start, count))

            done = jnp.where(fits, done + count, done)
            count //= 2

    def query_tile(qi, kmax, exact):
        q_rows = pl.ds(pl.multiple_of(qi * tq, tq), tq)
        qcat = q_ref[0, q_rows, :]
        zero = jnp.zeros_like(qcat)
        qh = (jnp.where(lane2 < HEAD_DIM, qcat, zero), jnp.where(lane2 >= HEAD_DIM, qcat, zero))

        kmax = [jnp.maximum(kmax[head], knorm_ref[base[head] + qi]) for head in range(2)]
        reach = []
        for head in range(2):
            bound = (cfirst_ref[base[head] + qi] + 2.0 * NORM_SLACK * qnorm_ref[base[head] + qi] * kmax[head]
                     + SKIP_MARGIN * LOG2E)

            def reaches(n, head=head, bound=bound):
                t = jnp.maximum(qi - 1 - n, 0)
                return (n < qi) & (bound - clast_ref[base[head] + t] > 0.0)

            reach.append(lax.while_loop(reaches, lambda n: n + 1, 0))
        n_both, n_long = jnp.minimum(reach[0], reach[1]), jnp.maximum(reach[0], reach[1])

        def both_heads(j, count, first_mode="lazy"):
            return [(j - i, head, first_mode if i == 0 else "lazy") for i in range(count) for head in range(2)]

        if exact:
            unit_run(qh.__getitem__, both_heads(qi, 1, "diagonal"))

            def step(t, carry):
                unit_run(qh.__getitem__, both_heads(qi - 1 - t, 1, "exact"))
                return carry

            lax.fori_loop(0, n_long, step, 0)
        else:
            total = n_both + 1
            count, first_len = FOX_TRIP, 1
            while count >= 1:
                fits = total >= count
                if count < FOX_TRIP:
                    fits = fits & (total < 2 * count)

                @pl.when(fits)
                def _(count=count):
                    unit_run(qh.__getitem__, both_heads(qi, count, "diagonal_self"))

                first_len = jnp.where(fits, count, first_len)
                count //= 2
            run_tiles(qh.__getitem__, both_heads, qi - first_len, total - first_len)

            long_head = jnp.where(reach[1] > reach[0], 1, 0)
            q_long = jnp.where((lane2 < HEAD_DIM) == (long_head == 0), qcat, zero)
            run_tiles(lambda head: q_long, lambda j, count: [(j - i, long_head, "lazy") for i in range(count)],
                      qi - total, n_long - n_both)

        outs = []
        for head in range(2):
            acc, lp = acc_ref[head], l_ref[head]
            outs.append(acc / jnp.sum(lp, axis=-1, keepdims=True))
            if not exact:
                finite = (jnp.abs(acc) <= F32_BIG) & (lp <= F32_BIG)
                bad_ref[...] = jnp.maximum(bad_ref[...], jnp.where(finite, 0.0, 1.0))
        o_ref[0, q_rows, :] = jnp.where(head_a, outs[0], outs[1]).astype(o_ref.dtype)
        return kmax

    def sweep(exact):
        def body(qi, kmax):
            return tuple(query_tile(qi, kmax, exact))

        lax.fori_loop(0, n_tiles, body, (jnp.float32(0.0), jnp.float32(0.0)))

    bad_ref[...] = jnp.zeros_like(bad_ref)
    sweep(exact=False)

    @pl.when(jnp.max(bad_ref[...]) > 0.0)
    def _():
        sweep(exact=True)


def _fox(stats, qf, kf, vf, tq):
    b, s, _ = qf.shape
    smem = pl.BlockSpec(memory_space=pltpu.SMEM)
    return pl.pallas_call(
        functools.partial(_fox_body, tq=tq),
        grid=(b, N_PAIR),
        in_specs=[smem, smem, smem, smem,
                  pl.BlockSpec((1, s, 2 * LANES), lambda i, p: (i, 0, p)),
                  pl.BlockSpec((1, s, 2 * LANES), lambda i, p: (i, 0, p)),
                  pl.BlockSpec((1, s, LANES), lambda i, p: (i, 0, p))],
        out_specs=pl.BlockSpec((1, s, LANES), lambda i, p: (i, 0, p)),
        out_shape=jax.ShapeDtypeStruct((b, s, W_ATT), jnp.bfloat16),
        scratch_shapes=[pltpu.VMEM((2 * FOX_AHEAD, tq, tq), jnp.float32),
                        pltpu.VMEM((2 * FOX_AHEAD, tq, tq), jnp.bfloat16),
                        pltpu.VMEM((2, tq, LANES), jnp.float32),
                        pltpu.VMEM((2, tq, LANES), jnp.float32),
                        pltpu.VMEM((2, tq, LANES), jnp.float32),
                        pltpu.VMEM((tq, LANES), jnp.float32)],
        compiler_params=pltpu.CompilerParams(
            dimension_semantics=("arbitrary", "arbitrary"), vmem_limit_bytes=VMEM_LIMIT),
        name="fox_attention",
    )(*stats, qf, kf, vf)


def _band_bias_body(g_ref, o_ref):
    width = g_ref.shape[-1]
    base = jnp.broadcast_to(g_ref[0], (Q_BLOCK, width))
    toeplitz = pltpu.roll(base, 0, 1, stride=1, stride_axis=0)[:, :BAND]
    qrow = lax.broadcasted_iota(jnp.int32, (Q_BLOCK, BAND), 0)
    kcol = lax.broadcasted_iota(jnp.int32, (Q_BLOCK, BAND), 1)
    cq = qrow // CHUNK
    ck = kcol // CHUNK - LEFT_CHUNKS
    valid = (ck <= cq) & (ck >= cq - LEFT_CHUNKS)
    o_ref[0] = jnp.where(valid, toeplitz * LOG2E, NEG)


def _band_bias(g_ext):
    h, _, width = g_ext.shape
    return pl.pallas_call(
        _band_bias_body,
        grid=(h,),
        in_specs=[pl.BlockSpec((1, 1, width), lambda i: (i, 0, 0))],
        out_specs=pl.BlockSpec((1, Q_BLOCK, BAND), lambda i: (i, 0, 0)),
        out_shape=jax.ShapeDtypeStruct((h, Q_BLOCK, BAND), jnp.float32),
        name="band_bias",
    )(g_ext)


def _chunk_body(q_ref, k_ref, v_ref, bias_ref, o_ref, kpad_ref, vpad_ref, s_ref, p_ref, l_ref, *, tq):
    qi = pl.program_id(2)
    s_len = k_ref.shape[1]

    @pl.when(qi == 0)
    def _():
        zeros = jnp.zeros((PAD, LANES), kpad_ref.dtype)
        kpad_ref[:PAD, :] = zeros
        vpad_ref[:PAD, :] = zeros
        kpad_ref[PAD:PAD + s_len, :] = k_ref[0]
        vpad_ref[PAD:PAD + s_len, :] = v_ref[0]

    head_a = lax.broadcasted_iota(jnp.int32, (1, LANES), 1) < HEAD_DIM
    kcol = lax.broadcasted_iota(jnp.int32, (1, BAND), 1)

    def offsets(blk):
        r0 = pl.multiple_of(blk * Q_BLOCK, Q_BLOCK)
        return r0, pl.multiple_of(qi * tq + r0, Q_BLOCK)

    def scores(slot, blk):
        r0, p0 = offsets(blk)
        q2 = q_ref[0, pl.ds(r0, Q_BLOCK), :]
        zero = jnp.zeros_like(q2)
        qs = jnp.concatenate([jnp.where(head_a, q2, zero), jnp.where(head_a, zero, q2)], axis=0)
        s_ref[slot] = _dot_nt(qs, kpad_ref[pl.ds(p0, BAND), :])

    def softmax(slot, blk, masked):
        _, p0 = offsets(blk)
        for r in range(0, 2 * Q_BLOCK, CHUNK_ROWS):
            rows = slice(r, r + CHUNK_ROWS)
            s = s_ref[slot, rows, :] + bias_ref[0, rows, :]
            if masked:
                s = jnp.where(kcol + p0 >= PAD, s, NEG)
            p = jnp.exp2(s - jnp.max(s, axis=-1, keepdims=True))
            l_ref[slot, rows, :] = jnp.broadcast_to(jnp.sum(p, axis=-1, keepdims=True), (CHUNK_ROWS, LANES))
            p_ref[slot, rows, :] = p.astype(p_ref.dtype)

    def values(slot, blk):
        r0, p0 = offsets(blk)
        o = _dot(p_ref[slot], vpad_ref[pl.ds(p0, BAND), :]) / l_ref[slot]
        o_ref[0, pl.ds(r0, Q_BLOCK), :] = jnp.where(head_a, o[:Q_BLOCK], o[Q_BLOCK:]).astype(o_ref.dtype)

    def group(i, masked):
        first = i * CHUNK_GROUP
        scores(0, first)
        for n in range(CHUNK_GROUP):
            if n + 1 < CHUNK_GROUP:
                scores((n + 1) % 2, first + n + 1)
            softmax(n % 2, first + n, masked)
            values(n % 2, first + n)

    def masked_group(i, carry):
        group(i, True)
        return carry

    def plain_group(i, carry):
        group(i, False)
        return carry

    group_rows = Q_BLOCK * CHUNK_GROUP
    n_groups = tq // group_rows
    n_masked = jnp.clip((PAD - qi * tq + group_rows - 1) // group_rows, 0, n_groups)
    lax.fori_loop(0, n_masked, masked_group, 0)
    lax.fori_loop(n_masked, n_groups, plain_group, 0)


def _chunk_attention(qc, kc, vc, bias, tq):
    b, s, _ = qc.shape
    assert tq % (CHUNK_GROUP * Q_BLOCK) == 0
    return pl.pallas_call(
        functools.partial(_chunk_body, tq=tq),
        grid=(b, N_PAIR, s // tq),
        in_specs=[pl.BlockSpec((1, tq, LANES), lambda i, p, j: (i, j, p)),
                  pl.BlockSpec((1, s, LANES), lambda i, p, j: (i, 0, p)),
                  pl.BlockSpec((1, s, LANES), lambda i, p, j: (i, 0, p)),
                  pl.BlockSpec((1, 2 * Q_BLOCK, BAND), lambda i, p, j: (p, 0, 0))],
        out_specs=pl.BlockSpec((1, tq, LANES), lambda i, p, j: (i, j, p)),
        out_shape=jax.ShapeDtypeStruct((b, s, W_ATT), jnp.bfloat16),
        scratch_shapes=[pltpu.VMEM((PAD + s, LANES), jnp.bfloat16),
                        pltpu.VMEM((PAD + s, LANES), jnp.bfloat16),
                        pltpu.VMEM((2, 2 * Q_BLOCK, BAND), jnp.float32),
                        pltpu.VMEM((2, 2 * Q_BLOCK, BAND), jnp.bfloat16),
                        pltpu.VMEM((2, 2 * Q_BLOCK, LANES), jnp.float32)],
        compiler_params=pltpu.CompilerParams(
            dimension_semantics=("arbitrary", "arbitrary", "arbitrary"), vmem_limit_bytes=VMEM_LIMIT),
        name="chunk_attention",
    )(qc, kc, vc, bias)


def _out_ffn_body(x_ref, oa_ref, ob_ref, gate_ref, wa_ref, wb_ref, wo_ref, g2_ref, wu_ref, wd_ref, gf_ref,
                  o_ref, *, ff_chunk, final):
    d = x_ref.shape[-1]
    ya = _dot(oa_ref[0], wa_ref[...])
    yb = _dot(ob_ref[0], wb_ref[...])
    ga = gate_ref[0, :, :d].astype(jnp.float32)
    gb = gate_ref[0, :, d:].astype(jnp.float32)
    merged = (ga * ya + gb * yb).astype(jnp.bfloat16)
    x1 = x_ref[0] + _dot(merged, wo_ref[...])
    h2 = _rms(x1, g2_ref[...]).astype(jnp.bfloat16)
    acc = x1
    for c0 in range(0, wu_ref.shape[1], ff_chunk):
        u = jnp.maximum(_dot(h2, wu_ref[:, c0:c0 + ff_chunk]), 0.0)
        acc = acc + _dot((u * u).astype(jnp.bfloat16), wd_ref[c0:c0 + ff_chunk, :])
    if final:
        acc = _rms(acc, gf_ref[...])
    o_ref[0] = acc


def _out_ffn(x, oa, ob, gates, wa, wb, wo, g2, wu, wd, gf, tm, final):
    b, s, d = x.shape
    tok = lambda w: pl.BlockSpec((1, tm, w), lambda i, j: (i, j, 0))
    return pl.pallas_call(
        functools.partial(_out_ffn_body, ff_chunk=min(1024, wu.shape[1]), final=final),
        grid=(b, s // tm),
        in_specs=[tok(d), tok(W_ATT), tok(W_ATT), tok(2 * d)]
                 + [_const_spec(a.shape) for a in (wa, wb, wo, g2, wu, wd, gf)],
        out_specs=tok(d),
        out_shape=jax.ShapeDtypeStruct((b, s, d), jnp.float32),
        compiler_params=pltpu.CompilerParams(
            dimension_semantics=("arbitrary", "arbitrary"), vmem_limit_bytes=VMEM_LIMIT),
        name="out_ffn",
    )(x, oa, ob, gates, wa, wb, wo, g2, wu, wd, gf)


def _aug_placement():
    eq = [[0.0] * W_ATT for _ in range(LANES)]
    ek = [[0.0] * W_ATT for _ in range(LANES)]
    one = 3 * H_FOX
    for h in range(H_FOX):
        for piece in range(3):
            eq[piece * H_FOX + h][HEAD_DIM * h + piece] = 1.0
            eq[one][HEAD_DIM * h + 3 + piece] = 1.0
            ek[one][HEAD_DIM * h + piece] = 1.0
            ek[piece * H_FOX + h][HEAD_DIM * h + 3 + piece] = -1.0
    return jnp.array(eq, jnp.bfloat16), jnp.array(ek, jnp.bfloat16)


def kernel(x, norm1, w_in, forget_bias, rel_bias, w_branch_a, w_branch_b, w_out, norm2, w_up, w_down, final_norm):
    b, s, d = x.shape
    depth = w_in.shape[0]
    bf16 = jnp.bfloat16
    tm = min(512, s)
    tqc = s
    eq, ek = _aug_placement()
    head_of_col = lax.broadcasted_iota(jnp.int32, (W_ATT, LANES), 0) // HEAD_DIM
    hsum = (head_of_col == lax.broadcasted_iota(jnp.int32, (W_ATT, LANES), 1)).astype(bf16)
    gf = final_norm.reshape(1, d)
    o = 3 * W_ATT + H_FOX
    for l in range(depth):
        w = w_in[l]
        wq = (w[:, :W_ATT] * (SCALE * LOG2E)).astype(bf16)
        wk = w[:, W_ATT:2 * W_ATT].astype(bf16)
        wv = w[:, 2 * W_ATT:3 * W_ATT].astype(bf16)
        wf = jnp.pad(w[:, 3 * W_ATT:o], ((0, 0), (0, LANES - H_FOX))).astype(bf16)
        wc = jnp.concatenate([w[:, o:o + W_ATT] * (SCALE * LOG2E), w[:, o + W_ATT:o + 3 * W_ATT]],
                             axis=1).astype(bf16)
        wg = w[:, o + 3 * W_ATT:].astype(bf16)
        bf = jnp.pad(forget_bias[l], (0, LANES - H_FOX)).reshape(1, LANES)
        qf, kf, vf, qc, kc, vc, gates, stat, wo, wu, wd = _in_proj(
            x, norm1[l].reshape(1, d), wq, wk, wv, wc, wg, wf, bf, eq, ek, hsum,
            (w_out, w_up, w_down), l, tm)

        stats = stat[:, :, :4, :H_FOX].transpose(2, 0, 3, 1).reshape(4, -1)
        o_a = _fox(tuple(stats), qf, kf, vf, tm)

        far = rel_bias[l][:, 2 * MAX_REL:]
        g_ext = jnp.concatenate([jnp.broadcast_to(far, (H_CHK, PAD - MAX_REL + 1)),
                                 rel_bias[l][:, 2 * MAX_REL - 1:0:-1],
                                 jnp.broadcast_to(far, (H_CHK, Q_BLOCK))], axis=1)
        bias = _band_bias(g_ext.reshape(H_CHK, 1, -1)).reshape(N_PAIR, 2 * Q_BLOCK, BAND)
        o_b = _chunk_attention(qc, kc, vc, bias, tqc)

        x = _out_ffn(x, o_a, o_b, gates, w_branch_a[l].astype(bf16), w_branch_b[l].astype(bf16),
                     wo, norm2[l].reshape(1, d), wu, wd, gf, tm, final=(l == depth - 1))
    return x
```

```python
import functools
import math

import jax
import jax.numpy as jnp
from jax import lax
from jax.experimental import pallas as pl
from jax.experimental.pallas import tpu as pltpu

HEAD_DIM = 64
H_FOX = 8
H_CHK = 8
N_PAIR = 4
W_ATT = H_FOX * HEAD_DIM
CHUNK = 64
Q_BLOCK = 128
LEFT_CHUNKS = 8
PAD = LEFT_CHUNKS * CHUNK
BAND = PAD + Q_BLOCK
MAX_REL = 128
EPS = 1e-6
NEG = -1e30
LANES = 128
SCALE = 1.0 / math.sqrt(HEAD_DIM)
LOG2E = math.log2(math.e)
F32_BIG = 3.0e38
PROJ_SPLIT = 2
FOX_ROWS = 32
FOX_TRIP = 8
FOX_AHEAD = 2
CHUNK_ROWS = 32
CHUNK_GROUP = 16
SKIP_MARGIN = 30.0
NORM_SLACK = 1.02
VMEM_LIMIT = 60 * 1024 * 1024

_NT = (((1,), (1,)), ((), ()))


def _dot(a, b):
    return jnp.dot(a, b, preferred_element_type=jnp.float32)


def _dot_nt(a, b):
    return lax.dot_general(a, b, _NT, preferred_element_type=jnp.float32)


def _rms(x, g):
    ms = jnp.mean(x * x, axis=-1, keepdims=True)
    return x * lax.rsqrt(ms + EPS) * g


def _const_spec(shape):
    nd = len(shape)
    return pl.BlockSpec(shape, lambda *_: (0,) * nd, pipeline_mode=pl.Buffered(1))


def _in_proj_body(x_ref, g_ref, wq_ref, wk_ref, wv_ref, wc_ref, wg_ref, wf_ref, bf_ref, eq_ref, ek_ref, hsum_ref,
                  wo32_ref, wu32_ref, wd32_ref,
                  qf_ref, kf_ref, vf_ref, qc_ref, kc_ref, vc_ref, gate_ref, stat_ref, wo_ref, wu_ref, wd_ref,
                  carry_ref):
    tm = x_ref.shape[1]
    th = tm // PROJ_SPLIT

    wo_ref[...] = wo32_ref[...].astype(wo_ref.dtype)
    wu_ref[...] = wu32_ref[...].astype(wu_ref.dtype)
    wd_ref[...] = wd32_ref[...].astype(wd_ref.dtype)

    @pl.when(pl.program_id(1) == 0)
    def _():
        carry_ref[...] = jnp.zeros_like(carry_ref)

    lane = lax.broadcasted_iota(jnp.int32, (th, LANES), 1)
    row = lax.broadcasted_iota(jnp.int32, (th, LANES), 0)

    def max_norm(t):
        t = t.astype(jnp.float32)
        sq = _dot((t * t).astype(jnp.bfloat16), hsum_ref[...])
        return jnp.max(sq, axis=0, keepdims=True)

    c_first, c_last, q_sq, k_sq = None, None, None, None
    for part in range(PROJ_SPLIT):
        rows = slice(part * th, (part + 1) * th)
        h = _rms(x_ref[0, rows, :], g_ref[...]).astype(jnp.bfloat16)

        logf = jax.nn.log_sigmoid(_dot(h, wf_ref[...]) + bf_ref[...]) * LOG2E
        c = jnp.where(lane < H_FOX, logf, 0.0)
        k = 1
        while k < th:
            c = c + jnp.where(row >= k, pltpu.roll(c, k, 0), 0.0)
            k *= 2
        c = c + carry_ref[...]
        carry_ref[...] = c[th - 1:th, :]
        c_first = c[0:1, :] if part == 0 else c_first
        c_last = c[th - 1:th, :]

        hi = c.astype(jnp.bfloat16).astype(jnp.float32)
        r1 = c - hi
        mid = r1.astype(jnp.bfloat16).astype(jnp.float32)
        lo = (r1 - mid).astype(jnp.bfloat16).astype(jnp.float32)
        pieces = hi + pltpu.roll(mid, H_FOX, 1) + pltpu.roll(lo, 2 * H_FOX, 1)
        pieces = jnp.where(lane == 3 * H_FOX, 1.0, pieces).astype(jnp.bfloat16)
        aug_q = _dot(pieces, eq_ref[...]).astype(jnp.bfloat16)
        aug_k = _dot(pieces, ek_ref[...]).astype(jnp.bfloat16)

        q = _dot(h, wq_ref[...]).astype(jnp.bfloat16)
        kk = _dot(h, wk_ref[...]).astype(jnp.bfloat16)
        for p in range(N_PAIR):
            src = slice(p * LANES, (p + 1) * LANES)
            qf_ref[0, rows, 2 * p * LANES:(2 * p + 1) * LANES] = q[:, src]
            qf_ref[0, rows, (2 * p + 1) * LANES:(2 * p + 2) * LANES] = aug_q[:, src]
            kf_ref[0, rows, 2 * p * LANES:(2 * p + 1) * LANES] = kk[:, src]
            kf_ref[0, rows, (2 * p + 1) * LANES:(2 * p + 2) * LANES] = aug_k[:, src]
        vf_ref[0, rows, :] = _dot(h, wv_ref[...]).astype(jnp.bfloat16)
        q_sq = max_norm(q) if part == 0 else jnp.maximum(q_sq, max_norm(q))
        k_sq = max_norm(kk) if part == 0 else jnp.maximum(k_sq, max_norm(kk))

        pc = _dot(h, wc_ref[...])
        qc_ref[0, rows, :] = pc[:, :W_ATT].astype(jnp.bfloat16)
        kc_ref[0, rows, :] = pc[:, W_ATT:2 * W_ATT].astype(jnp.bfloat16)
        vc_ref[0, rows, :] = pc[:, 2 * W_ATT:].astype(jnp.bfloat16)
        gate_ref[0, rows, :] = jax.nn.sigmoid(_dot(h, wg_ref[...])).astype(jnp.bfloat16)

    sub = lax.broadcasted_iota(jnp.int32, (8, LANES), 0)
    stat = jnp.where(sub == 0, c_first, 0.0)
    stat = jnp.where(sub == 1, c_last, stat)
    stat = jnp.where(sub == 2, jnp.sqrt(q_sq), stat)
    stat_ref[0, 0] = jnp.where(sub == 3, jnp.sqrt(k_sq), stat)


def _in_proj(x, g, wq, wk, wv, wc, wg, wf, bf, eq, ek, hsum, to_cast, layer, tm):
    b, s, d = x.shape
    n_steps = b * (s // tm)
    tok = lambda width: pl.BlockSpec((1, tm, width), lambda i, j: (i, j, 0))
    strip_in = lambda w: pl.BlockSpec((None, w.shape[1] // n_steps, w.shape[2]),
                                      lambda i, j: (layer, i * (s // tm) + j, 0))
    strip_out = lambda w: pl.BlockSpec((w.shape[1] // n_steps, w.shape[2]), lambda i, j: (i * (s // tm) + j, 0))
    bf16 = jnp.bfloat16
    assert all(w.shape[1] % (16 * n_steps) == 0 for w in to_cast)
    out_shape = [jax.ShapeDtypeStruct((b, s, width), bf16)
                 for width in (2 * W_ATT, 2 * W_ATT, W_ATT, W_ATT, W_ATT, W_ATT, 2 * d)]
    out_shape.append(jax.ShapeDtypeStruct((b, s // tm, 8, LANES), jnp.float32))
    out_shape += [jax.ShapeDtypeStruct(w.shape[1:], bf16) for w in to_cast]
    consts = (g, wq, wk, wv, wc, wg, wf, bf, eq, ek, hsum)
    return pl.pallas_call(
        _in_proj_body,
        grid=(b, s // tm),
        in_specs=[tok(d)] + [_const_spec(a.shape) for a in consts] + [strip_in(w) for w in to_cast],
        out_specs=[tok(sh.shape[-1]) for sh in out_shape[:7]]
                  + [pl.BlockSpec((1, 1, 8, LANES), lambda i, j: (i, j, 0, 0))]
                  + [strip_out(w) for w in to_cast],
        out_shape=out_shape,
        scratch_shapes=[pltpu.VMEM((1, LANES), jnp.float32)],
        compiler_params=pltpu.CompilerParams(
            dimension_semantics=("arbitrary", "arbitrary"), vmem_limit_bytes=VMEM_LIMIT,
            allow_input_fusion=[2 <= n <= 7 for n in range(1 + len(consts) + len(to_cast))]),
        name="in_proj",
    )(x, *consts, *to_cast)


def _fox_body(cfirst_ref, clast_ref, qnorm_ref, knorm_ref, q_ref, k_ref, v_ref, o_ref,
              s_ref, p_ref, m_ref, l_ref, acc_ref, bad_ref, *, tq):
    n_tiles = q_ref.shape[1] // tq
    lane2 = lax.broadcasted_iota(jnp.int32, (1, 2 * LANES), 1) % LANES
    head_a = lax.broadcasted_iota(jnp.int32, (1, LANES), 1) < HEAD_DIM
    base = [(pl.program_id(0) * H_FOX + 2 * pl.program_id(1) + head) * n_tiles for head in range(2)]

    def key_rows(j):
        return pl.ds(pl.multiple_of(j * tq, tq), tq)

    def scores(slot, q_head, j):
        s_ref[slot] = _dot_nt(q_head, k_ref[0, key_rows(j), :])

    def softmax(slot, head, mode):
        for r in range(0, tq, FOX_ROWS):
            rows = slice(r, r + FOX_ROWS)
            s = s_ref[slot, rows, :]
            if mode in ("diagonal", "diagonal_self"):
                row = lax.broadcasted_iota(jnp.int32, (FOX_ROWS, tq), 0) + r
                col = lax.broadcasted_iota(jnp.int32, (FOX_ROWS, tq), 1)
                if mode == "diagonal":
                    s = jnp.where(col <= row, s, NEG)
                    m = jnp.max(s, axis=-1, keepdims=True)
                else:
                    own = slice(r // LANES * LANES, (r // LANES + 1) * LANES)
                    lane = lax.broadcasted_iota(jnp.int32, (FOX_ROWS, LANES), 1)
                    sub = lax.broadcasted_iota(jnp.int32, (FOX_ROWS, LANES), 0)
                    m = jnp.sum(jnp.where(lane == sub + r % LANES, s[:, own], 0.0), axis=-1, keepdims=True)
                    s = jnp.where(col <= row, s, NEG)
                m = jnp.broadcast_to(m, (FOX_ROWS, LANES))
                m_ref[head, rows, :] = m
            elif mode == "exact":
                m_old = m_ref[head, rows, :]
                m = jnp.maximum(m_old, jnp.max(s, axis=-1, keepdims=True))
                alpha = jnp.exp2(m_old - m)
                m_ref[head, rows, :] = m
            else:
                m = m_ref[head, rows, :]
            lsum = None
            for g in range(tq // LANES):
                cols = slice(g * LANES, (g + 1) * LANES)
                pg = jnp.exp2(s[:, cols] - m)
                p_ref[slot, rows, cols] = pg.astype(p_ref.dtype)
                lsum = pg if lsum is None else lsum + pg
            if mode in ("diagonal", "diagonal_self"):
                l_ref[head, rows, :] = lsum
            elif mode == "exact":
                l_ref[head, rows, :] = l_ref[head, rows, :] * alpha + lsum
                acc_ref[head, rows, :] = acc_ref[head, rows, :] * alpha
            else:
                l_ref[head, rows, :] = l_ref[head, rows, :] + lsum

    def values(slot, head, j, mode):
        pv = _dot(p_ref[slot], v_ref[0, key_rows(j), :])
        acc_ref[head] = pv if mode in ("diagonal", "diagonal_self") else acc_ref[head] + pv

    half = tq // 2

    def half_rows(j, part):
        return pl.ds(pl.multiple_of(j * tq + part * half, half), half)

    def scores_diagonal(slot, q_head, j):
        s_ref[slot, :, :half] = _dot_nt(q_head, k_ref[0, half_rows(j, 0), :])
        s_ref[slot, half:, half:] = _dot_nt(q_head[half:], k_ref[0, half_rows(j, 1), :])

    def softmax_diagonal(slot, head):
        for r in range(0, tq, FOX_ROWS):
            rows = slice(r, r + FOX_ROWS)
            width = half if r < half else tq
            s = s_ref[slot, rows, :width]
            row = lax.broadcasted_iota(jnp.int32, (FOX_ROWS, width), 0) + r
            col = lax.broadcasted_iota(jnp.int32, (FOX_ROWS, width), 1)
            own = slice(r // LANES * LANES, (r // LANES + 1) * LANES)
            lane = lax.broadcasted_iota(jnp.int32, (FOX_ROWS, LANES), 1)
            sub = lax.broadcasted_iota(jnp.int32, (FOX_ROWS, LANES), 0)
            m = jnp.sum(jnp.where(lane == sub + r % LANES, s[:, own], 0.0), axis=-1, keepdims=True)
            m = jnp.broadcast_to(m, (FOX_ROWS, LANES))
            s = jnp.where(col <= row, s, NEG)
            m_ref[head, rows, :] = m
            lsum = None
            for g in range(width // LANES):
                cols = slice(g * LANES, (g + 1) * LANES)
                pg = jnp.exp2(s[:, cols] - m)
                p_ref[slot, rows, cols] = pg.astype(p_ref.dtype)
                lsum = pg if lsum is None else lsum + pg
            l_ref[head, rows, :] = lsum

    def values_diagonal(slot, head, j):
        acc_ref[head] = _dot(p_ref[slot, :, :half], v_ref[0, half_rows(j, 0), :])
        acc_ref[head, half:, :] = acc_ref[head, half:, :] + _dot(p_ref[slot, half:, half:],
                                                                 v_ref[0, half_rows(j, 1), :])

    def unit_run(q_of, units):
        n_slots = s_ref.shape[0]

        def issue_scores(n):
            tile, head, mode = units[n]
            (scores_diagonal if mode == "diagonal_self" else scores)(n % n_slots, q_of(head), tile)

        for n in range(min(FOX_AHEAD, len(units))):
            issue_scores(n)
        for n, (tile, head, mode) in enumerate(units):
            if mode == "diagonal_self":
                softmax_diagonal(n % n_slots, head)
            else:
                softmax(n % n_slots, head, mode)
            if n + FOX_AHEAD < len(units):
                issue_scores(n + FOX_AHEAD)
            if mode == "diagonal_self":
                values_diagonal(n % n_slots, head, tile)
            else:
                values(n % n_slots, head, tile, mode)

    def run_tiles(q_of, make_units, first, total):
        def full(t, carry):
            unit_run(q_of, make_units(first - FOX_TRIP * t, FOX_TRIP))
            return carry

        n_full = total // FOX_TRIP
        lax.fori_loop(0, n_full, full, 0)
        done = n_full * FOX_TRIP
        count = FOX_TRIP // 2
        while count >= 1:
            fits = (total - done) >= count

            @pl.when(fits)
            def _(start=first - done, count=count):
                unit_run(q_of, make_units(start, count))

            done = jnp.where(fits, done + count, done)
            count //= 2

    def query_tile(qi, kmax, exact):
        q_rows = pl.ds(pl.multiple_of(qi * tq, tq), tq)
        qcat = q_ref[0, q_rows, :]
        zero = jnp.zeros_like(qcat)
        qh = (jnp.where(lane2 < HEAD_DIM, qcat, zero), jnp.where(lane2 >= HEAD_DIM, qcat, zero))

        kmax = [jnp.maximum(kmax[head], knorm_ref[base[head] + qi]) for head in range(2)]
        reach = []
        for head in range(2):
            bound = (cfirst_ref[base[head] + qi] + 2.0 * NORM_SLACK * qnorm_ref[base[head] + qi] * kmax[head]
                     + SKIP_MARGIN * LOG2E)

            def reaches(n, head=head, bound=bound):
                t = jnp.maximum(qi - 1 - n, 0)
                return (n < qi) & (bound - clast_ref[base[head] + t] > 0.0)

            reach.append(lax.while_loop(reaches, lambda n: n + 1, 0))
        n_both, n_long = jnp.minimum(reach[0], reach[1]), jnp.maximum(reach[0], reach[1])

        def both_heads(j, count, first_mode="lazy"):
            return [(j - i, head, first_mode if i == 0 else "lazy") for i in range(count) for head in range(2)]

        if exact:
            unit_run(qh.__getitem__, both_heads(qi, 1, "diagonal"))

            def step(t, carry):
                unit_run(qh.__getitem__, both_heads(qi - 1 - t, 1, "exact"))
                return carry

            lax.fori_loop(0, n_long, step, 0)
        else:
            total = n_both + 1
            count, first_len = FOX_TRIP, 1
            while count >= 1:
                fits = total >= count
                if count < FOX_TRIP:
                    fits = fits & (total < 2 * count)

                @pl.when(fits)
                def _(count=count):
                    unit_run(qh.__getitem__, both_heads(qi, count, "diagonal_self"))

                first_len = jnp.where(fits, count, first_len)
                count //= 2
            run_tiles(qh.__getitem__, both_heads, qi - first_len, total - first_len)

            long_head = jnp.where(reach[1] > reach[0], 1, 0)
            q_long = jnp.where((lane2 < HEAD_DIM) == (long_head == 0), qcat, zero)
            run_tiles(lambda head: q_long, lambda j, count: [(j - i, long_head, "lazy") for i in range(count)],
                      qi - total, n_long - n_both)

        outs = []
        for head in range(2):
            acc, lp = acc_ref[head], l_ref[head]
            outs.append(acc / jnp.sum(lp, axis=-1, keepdims=True))
            if not exact:
                finite = (jnp.abs(acc) <= F32_BIG) & (lp <= F32_BIG)
                bad_ref[...] = jnp.maximum(bad_ref[...], jnp.where(finite, 0.0, 1.0))
        o_ref[0, q_rows, :] = jnp.where(head_a, outs[0], outs[1]).astype(o_ref.dtype)
        return kmax

    def sweep(exact):
        def body(qi, kmax):
            return tuple(query_tile(qi, kmax, exact))

        lax.fori_loop(0, n_tiles, body, (jnp.float32(0.0), jnp.float32(0.0)))

    bad_ref[...] = jnp.zeros_like(bad_ref)
    sweep(exact=False)

    @pl.when(jnp.max(bad_ref[...]) > 0.0)
    def _():
        sweep(exact=True)


def _fox(stats, qf, kf, vf, tq):
    b, s, _ = qf.shape
    smem = pl.BlockSpec(memory_space=pltpu.SMEM)
    return pl.pallas_call(
        functools.partial(_fox_body, tq=tq),
        grid=(b, N_PAIR),
        in_specs=[smem, smem, smem, smem,
                  pl.BlockSpec((1, s, 2 * LANES), lambda i, p: (i, 0, p)),
                  pl.BlockSpec((1, s, 2 * LANES), lambda i, p: (i, 0, p)),
                  pl.BlockSpec((1, s, LANES), lambda i, p: (i, 0, p))],
        out_specs=pl.BlockSpec((1, s, LANES), lambda i, p: (i, 0, p)),
        out_shape=jax.ShapeDtypeStruct((b, s, W_ATT), jnp.bfloat16),
        scratch_shapes=[pltpu.VMEM((2 * FOX_AHEAD, tq, tq), jnp.float32),
                        pltpu.VMEM((2 * FOX_AHEAD, tq, tq), jnp.bfloat16),
                        pltpu.VMEM((2, tq, LANES), jnp.float32),
                        pltpu.VMEM((2, tq, LANES), jnp.float32),
                        pltpu.VMEM((2, tq, LANES), jnp.float32),
                        pltpu.VMEM((tq, LANES), jnp.float32)],
        compiler_params=pltpu.CompilerParams(
            dimension_semantics=("arbitrary", "arbitrary"), vmem_limit_bytes=VMEM_LIMIT),
        name="fox_attention",
    )(*stats, qf, kf, vf)


def _band_bias_body(g_ref, o_ref):
    width = g_ref.shape[-1]
    base = jnp.broadcast_to(g_ref[0], (Q_BLOCK, width))
    toeplitz = pltpu.roll(base, 0, 1, stride=1, stride_axis=0)[:, :BAND]
    qrow = lax.broadcasted_iota(jnp.int32, (Q_BLOCK, BAND), 0)
    kcol = lax.broadcasted_iota(jnp.int32, (Q_BLOCK, BAND), 1)
    cq = qrow // CHUNK
    ck = kcol // CHUNK - LEFT_CHUNKS
    valid = (ck <= cq) & (ck >= cq - LEFT_CHUNKS)
    o_ref[0] = jnp.where(valid, toeplitz * LOG2E, NEG)


def _band_bias(g_ext):
    h, _, width = g_ext.shape
    return pl.pallas_call(
        _band_bias_body,
        grid=(h,),
        in_specs=[pl.BlockSpec((1, 1, width), lambda i: (i, 0, 0))],
        out_specs=pl.BlockSpec((1, Q_BLOCK, BAND), lambda i: (i, 0, 0)),
        out_shape=jax.ShapeDtypeStruct((h, Q_BLOCK, BAND), jnp.float32),
        name="band_bias",
    )(g_ext)


def _chunk_body(q_ref, k_ref, v_ref, bias_ref, o_ref, kpad_ref, vpad_ref, s_ref, p_ref, l_ref, *, tq):
    qi = pl.program_id(2)
    s_len = k_ref.shape[1]

    @pl.when(qi == 0)
    def _():
        zeros = jnp.zeros((PAD, LANES), kpad_ref.dtype)
        kpad_ref[:PAD, :] = zeros
        vpad_ref[:PAD, :] = zeros
        kpad_ref[PAD:PAD + s_len, :] = k_ref[0]
        vpad_ref[PAD:PAD + s_len, :] = v_ref[0]

    head_a = lax.broadcasted_iota(jnp.int32, (1, LANES), 1) < HEAD_DIM
    kcol = lax.broadcasted_iota(jnp.int32, (1, BAND), 1)

    def offsets(blk):
        r0 = pl.multiple_of(blk * Q_BLOCK, Q_BLOCK)
        return r0, pl.multiple_of(qi * tq + r0, Q_BLOCK)

    def scores(slot, blk):
        r0, p0 = offsets(blk)
        q2 = q_ref[0, pl.ds(r0, Q_BLOCK), :]
        zero = jnp.zeros_like(q2)
        qs = jnp.concatenate([jnp.where(head_a, q2, zero), jnp.where(head_a, zero, q2)], axis=0)
        s_ref[slot] = _dot_nt(qs, kpad_ref[pl.ds(p0, BAND), :])

    def softmax(slot, blk, masked):
        _, p0 = offsets(blk)
        for r in range(0, 2 * Q_BLOCK, CHUNK_ROWS):
            rows = slice(r, r + CHUNK_ROWS)
            s = s_ref[slot, rows, :] + bias_ref[0, rows, :]
            if masked:
                s = jnp.where(kcol + p0 >= PAD, s, NEG)
            p = jnp.exp2(s - jnp.max(s, axis=-1, keepdims=True))
            l_ref[slot, rows, :] = jnp.broadcast_to(jnp.sum(p, axis=-1, keepdims=True), (CHUNK_ROWS, LANES))
            p_ref[slot, rows, :] = p.astype(p_ref.dtype)

    def values(slot, blk):
        r0, p0 = offsets(blk)
        o = _dot(p_ref[slot], vpad_ref[pl.ds(p0, BAND), :]) / l_ref[slot]
        o_ref[0, pl.ds(r0, Q_BLOCK), :] = jnp.where(head_a, o[:Q_BLOCK], o[Q_BLOCK:]).astype(o_ref.dtype)

    def group(i, masked):
        first = i * CHUNK_GROUP
        scores(0, first)
        for n in range(CHUNK_GROUP):
            if n + 1 < CHUNK_GROUP:
                scores((n + 1) % 2, first + n + 1)
            softmax(n % 2, first + n, masked)
            values(n % 2, first + n)

    def masked_group(i, carry):
        group(i, True)
        return carry

    def plain_group(i, carry):
        group(i, False)
        return carry

    group_rows = Q_BLOCK * CHUNK_GROUP
    n_groups = tq // group_rows
    n_masked = jnp.clip((PAD - qi * tq + group_rows - 1) // group_rows, 0, n_groups)
    lax.fori_loop(0, n_masked, masked_group, 0)
    lax.fori_loop(n_masked, n_groups, plain_group, 0)


def _chunk_attention(qc, kc, vc, bias, tq):
    b, s, _ = qc.shape
    assert tq % (CHUNK_GROUP * Q_BLOCK) == 0
    return pl.pallas_call(
        functools.partial(_chunk_body, tq=tq),
        grid=(b, N_PAIR, s // tq),
        in_specs=[pl.BlockSpec((1, tq, LANES), lambda i, p, j: (i, j, p)),
                  pl.BlockSpec((1, s, LANES), lambda i, p, j: (i, 0, p)),
                  pl.BlockSpec((1, s, LANES), lambda i, p, j: (i, 0, p)),
                  pl.BlockSpec((1, 2 * Q_BLOCK, BAND), lambda i, p, j: (p, 0, 0))],
        out_specs=pl.BlockSpec((1, tq, LANES), lambda i, p, j: (i, j, p)),
        out_shape=jax.ShapeDtypeStruct((b, s, W_ATT), jnp.bfloat16),
        scratch_shapes=[pltpu.VMEM((PAD + s, LANES), jnp.bfloat16),
                        pltpu.VMEM((PAD + s, LANES), jnp.bfloat16),
                        pltpu.VMEM((2, 2 * Q_BLOCK, BAND), jnp.float32),
                        pltpu.VMEM((2, 2 * Q_BLOCK, BAND), jnp.bfloat16),
                        pltpu.VMEM((2, 2 * Q_BLOCK, LANES), jnp.float32)],
        compiler_params=pltpu.CompilerParams(
            dimension_semantics=("arbitrary", "arbitrary", "arbitrary"), vmem_limit_bytes=VMEM_LIMIT),
        name="chunk_attention",
    )(qc, kc, vc, bias)


def _out_ffn_body(x_ref, oa_ref, ob_ref, gate_ref, wa_ref, wb_ref, wo_ref, g2_ref, wu_ref, wd_ref, gf_ref,
                  o_ref, *, ff_chunk, final):
    d = x_ref.shape[-1]
    ya = _dot(oa_ref[0], wa_ref[...])
    yb = _dot(ob_ref[0], wb_ref[...])
    ga = gate_ref[0, :, :d].astype(jnp.float32)
    gb = gate_ref[0, :, d:].astype(jnp.float32)
    merged = (ga * ya + gb * yb).astype(jnp.bfloat16)
    x1 = x_ref[0] + _dot(merged, wo_ref[...])
    h2 = _rms(x1, g2_ref[...]).astype(jnp.bfloat16)
    acc = x1
    for c0 in range(0, wu_ref.shape[1], ff_chunk):
        u = jnp.maximum(_dot(h2, wu_ref[:, c0:c0 + ff_chunk]), 0.0)
        acc = acc + _dot((u * u).astype(jnp.bfloat16), wd_ref[c0:c0 + ff_chunk, :])
    if final:
        acc = _rms(acc, gf_ref[...])
    o_ref[0] = acc


def _out_ffn(x, oa, ob, gates, wa, wb, wo, g2, wu, wd, gf, tm, final):
    b, s, d = x.shape
    tok = lambda w: pl.BlockSpec((1, tm, w), lambda i, j: (i, j, 0))
    return pl.pallas_call(
        functools.partial(_out_ffn_body, ff_chunk=min(1024, wu.shape[1]), final=final),
        grid=(b, s // tm),
        in_specs=[tok(d), tok(W_ATT), tok(W_ATT), tok(2 * d)]
                 + [_const_spec(a.shape) for a in (wa, wb, wo, g2, wu, wd, gf)],
        out_specs=tok(d),
        out_shape=jax.ShapeDtypeStruct((b, s, d), jnp.float32),
        compiler_params=pltpu.CompilerParams(
            dimension_semantics=("arbitrary", "arbitrary"), vmem_limit_bytes=VMEM_LIMIT),
        name="out_ffn",
    )(x, oa, ob, gates, wa, wb, wo, g2, wu, wd, gf)


def _aug_placement():
    eq = [[0.0] * W_ATT for _ in range(LANES)]
    ek = [[0.0] * W_ATT for _ in range(LANES)]
    one = 3 * H_FOX
    for h in range(H_FOX):
        for piece in range(3):
            eq[piece * H_FOX + h][HEAD_DIM * h + piece] = 1.0
            eq[one][HEAD_DIM * h + 3 + piece] = 1.0
            ek[one][HEAD_DIM * h + piece] = 1.0
            ek[piece * H_FOX + h][HEAD_DIM * h + 3 + piece] = -1.0
    return jnp.array(eq, jnp.bfloat16), jnp.array(ek, jnp.bfloat16)


def kernel(x, norm1, w_in, forget_bias, rel_bias, w_branch_a, w_branch_b, w_out, norm2, w_up, w_down, final_norm):
    b, s, d = x.shape
    depth = w_in.shape[0]
    bf16 = jnp.bfloat16
    tm = min(512, s)
    tqc = s
    eq, ek = _aug_placement()
    head_of_col = lax.broadcasted_iota(jnp.int32, (W_ATT, LANES), 0) // HEAD_DIM
    hsum = (head_of_col == lax.broadcasted_iota(jnp.int32, (W_ATT, LANES), 1)).astype(bf16)
    gf = final_norm.reshape(1, d)
    o = 3 * W_ATT + H_FOX
    for l in range(depth):
        w = w_in[l]
        wq = (w[:, :W_ATT] * (SCALE * LOG2E)).astype(bf16)
        wk = w[:, W_ATT:2 * W_ATT].astype(bf16)
        wv = w[:, 2 * W_ATT:3 * W_ATT].astype(bf16)
        wf = jnp.pad(w[:, 3 * W_ATT:o], ((0, 0), (0, LANES - H_FOX))).astype(bf16)
        wc = jnp.concatenate([w[:, o:o + W_ATT] * (SCALE * LOG2E), w[:, o + W_ATT:o + 3 * W_ATT]],
                             axis=1).astype(bf16)
        wg = w[:, o + 3 * W_ATT:].astype(bf16)
        bf = jnp.pad(forget_bias[l], (0, LANES - H_FOX)).reshape(1, LANES)
        qf, kf, vf, qc, kc, vc, gates, stat, wo, wu, wd = _in_proj(
            x, norm1[l].reshape(1, d), wq, wk, wv, wc, wg, wf, bf, eq, ek, hsum,
            (w_out, w_up, w_down), l, tm)

        stats = stat[:, :, :4, :H_FOX].transpose(2, 0, 3, 1).reshape(4, -1)
        o_a = _fox(tuple(stats), qf, kf, vf, tm)

        far = rel_bias[l][:, 2 * MAX_REL:]
        g_ext = jnp.concatenate([jnp.broadcast_to(far, (H_CHK, PAD - MAX_REL + 1)),
                                 rel_bias[l][:, 2 * MAX_REL - 1:0:-1],
                                 jnp.broadcast_to(far, (H_CHK, Q_BLOCK))], axis=1)
        bias = _band_bias(g_ext.reshape(H_CHK, 1, -1)).reshape(N_PAIR, 2 * Q_BLOCK, BAND)
        o_b = _chunk_attention(qc, kc, vc, bias, tqc)

        x = _out_ffn(x, o_a, o_b, gates, w_branch_a[l].astype(bf16), w_branch_b[l].astype(bf16),
                     wo, norm2[l].reshape(1, d), wu, wd, gf, tm, final=(l == depth - 1))
    return x
```
